```python
import jax, jax.numpy as jnp
from jax import lax
import numpy as np

D_MODEL = 2048
BATCH = 2
SEQ = 16384
DEPTH = 2
DEC_BATCH = 16
DEC_SEQ = 16
PAST_LEN = 1024

CHUNK = 64
Q_BLOCK = 128
KEY_BLOCK = 128
HEAD_DIM = 128
H_A = 4
A_W = H_A * HEAD_DIM
H_B = 4
B_W = H_B * HEAD_DIM
H_IDX = 4
D_IDX = 64
TOPK_MAX = 256
H_C = 16
N_C = 64
C_W = H_C * N_C
SCAN_CHUNK = 32
D_DECAY_LORA = 96
D_AAA_LORA = 96
D_MV_LORA = 64
D_GATE_LORA = 256
C_COLS = 3 * C_W + D_DECAY_LORA + D_AAA_LORA + D_GATE_LORA
IN_SPLITS = (A_W, A_W, A_W, B_W, HEAD_DIM, HEAD_DIM, H_IDX * D_IDX, D_IDX, H_IDX, C_COLS, D_MODEL, D_MODEL, D_MODEL)
IN_COLS = sum(IN_SPLITS)
C_SPLITS = (C_W, C_W, C_W, D_DECAY_LORA, D_AAA_LORA, D_GATE_LORA)
D_FF = -(-8 * D_MODEL // (3 * 256)) * 256
PLE_DIM = 256
ROPE_THETA = 500000.0
ROT_FRACTION = 4
NORM_EPS = 1e-6
LNX_EPS = 64e-5
PAD_POS = 1 << 30

kernel_name = 'hybrid_stickbreak_dsa_rwkv7_stream_step'


def _split(x, sizes):
    offs, acc = [], 0
    for s in sizes[:-1]:
        acc += s
        offs.append(acc)
    return jnp.split(x, offs, axis=-1)


def rms_norm(x, g):
    xf = x.astype(jnp.float32)
    xf = xf * lax.rsqrt(jnp.mean(xf * xf, axis=-1, keepdims=True) + NORM_EPS)
    return xf.astype(x.dtype) * g


def rope_partial(x, pos):
    rot = x.shape[-1] // ROT_FRACTION
    half = rot // 2
    inv = ROPE_THETA ** (-jnp.arange(half, dtype=jnp.float32) / half)
    ang = pos.astype(jnp.float32)[:, None] * inv[None, :]
    cos = jnp.cos(ang)[:, None, :].astype(x.dtype)
    sin = jnp.sin(ang)[:, None, :].astype(x.dtype)
    x1, x2, rest = x[..., :half], x[..., half:rot], x[..., rot:]
    return jnp.concatenate([x1 * cos - x2 * sin, x2 * cos + x1 * sin, rest], axis=-1)


def causal_sweep(fn, q_pos, qs, ks, k_pos):
    T = q_pos.shape[0]
    n_past = k_pos.shape[0] - T
    if T % Q_BLOCK:
        return fn(q_pos, qs, ks, k_pos)
    nb = T // Q_BLOCK
    G = 8 if nb % 8 == 0 else 4 if nb % 4 == 0 else 2 if nb % 2 == 0 else 1
    tg = T // G
    nbg = tg // Q_BLOCK
    outs = []
    for g in range(G):
        lo, hi = g * tg, (g + 1) * tg
        n_keys = n_past + hi
        ks_g = tuple(a[:, :n_keys] for a in ks)
        kp_g = k_pos[:n_keys]
        blk = lambda a: jnp.moveaxis(a[:, lo:hi].reshape((a.shape[0], nbg, Q_BLOCK) + a.shape[2:]), 1, 0)
        out = lax.map(lambda args: fn(args[0], args[1:], ks_g, kp_g),
                      (q_pos[lo:hi].reshape(nbg, Q_BLOCK),) + tuple(blk(q) for q in qs))
        out = jnp.moveaxis(out, 0, 1)
        outs.append(out.reshape((out.shape[0], tg) + out.shape[3:]))
    return jnp.concatenate(outs, axis=1)


def stick_breaking_attend(q_pos, qs, ks, k_pos):
    (q,) = qs
    k, v = ks
    L = k.shape[1]
    Lp = -(-L // KEY_BLOCK) * KEY_BLOCK
    if Lp != L:
        pad = ((0, 0), (0, Lp - L), (0, 0), (0, 0))
        k, v = jnp.pad(k, pad), jnp.pad(v, pad)
        k_pos = jnp.concatenate([k_pos, jnp.full((Lp - L,), PAD_POS, k_pos.dtype)])
    nz = jnp.einsum('bthd,bshd->bhts', q, k, preferred_element_type=jnp.float32)
    B, H, Tq = nz.shape[0], nz.shape[1], nz.shape[2]
    nkb = Lp // KEY_BLOCK
    causal = (k_pos[None, :] < q_pos[:, None]).reshape(Tq, nkb, KEY_BLOCK)
    nz = nz.reshape(B, H, Tq, nkb, KEY_BLOCK)
    log_keep = jnp.where(causal, jnp.log(jax.nn.sigmoid(nz)), 0.0)
    within = jnp.einsum('bhtnj,js->bhtns', log_keep, jnp.tri(KEY_BLOCK, dtype=jnp.float32))
    later = jnp.einsum('bhtm,mn->bhtn', within[..., 0], jnp.tri(nkb, k=-1, dtype=jnp.float32))
    w = jnp.where(causal, jnp.exp(within + later[..., None] - nz), 0.0).reshape(B, H, Tq, Lp)
    return jnp.einsum('bhts,bshd->bthd', w.astype(v.dtype), v)


def dsa_attend(q_pos, qs, ks, k_pos, topk):
    q, qi, wi = qs
    k, v, ki = ks
    s_idx = jnp.einsum('bthd,bsd->bths', qi, ki, preferred_element_type=jnp.float32)
    score = jnp.einsum('bth,bths->bts', wi.astype(jnp.float32), jax.nn.relu(s_idx))
    admissible = (k_pos[None, :] // CHUNK) <= (q_pos[:, None] // CHUNK)
    score = jnp.where(admissible[None], score, -jnp.inf)
    top_score, top_idx = lax.top_k(score, min(topk, k.shape[1]))
    valid = jnp.isfinite(top_score)
    gather = jax.vmap(lambda kb, ib: kb[ib])
    k_sel = gather(k, top_idx)
    v_sel = gather(v, top_idx)
    logits = jnp.einsum('bthd,btjd->bhtj', q, k_sel, preferred_element_type=jnp.float32)
    logits = jnp.where(valid[:, None], logits, -jnp.inf)
    probs = jax.nn.softmax(logits, axis=-1)
    return jnp.einsum('bhtj,btjd->bthd', probs.astype(v.dtype), v_sel)


def wkv7_chunked(r, logw, k, v, a, b, s0):
    B, T = r.shape[0], r.shape[1]
    nc = -(-T // SCAN_CHUNK)
    Tp = nc * SCAN_CHUNK

    def blocks(t):
        t = jnp.pad(t.astype(jnp.float32), ((0, 0), (0, Tp - T), (0, 0), (0, 0)))
        return t.reshape(B, nc, SCAN_CHUNK, H_C, N_C).transpose(0, 1, 3, 2, 4)

    r, logw, k, v, a, b = (blocks(t) for t in (r, logw, k, v, a, b))
    g = lax.cumsum(logw, axis=3)
    g_end = g[:, :, :, -1:]
    a_t = a * jnp.exp(g - logw)
    r_t = r * jnp.exp(g)
    inv = jnp.exp(-g)
    b_h, k_h = b * inv, k * inv
    to_end = jnp.exp(g_end - g)
    b_e, k_e = b * to_end, k * to_end
    strict = jnp.tri(SCAN_CHUNK, k=-1, dtype=bool)
    incl = jnp.tri(SCAN_CHUNK, dtype=bool)
    pair = lambda x, y, m: jnp.where(m, jnp.einsum('bnhtk,bnhsk->bnhts', x, y), 0.0)
    a_ab, a_ak = pair(a_t, b_h, strict), pair(a_t, k_h, strict)
    a_rb, a_rk = pair(r_t, b_h, incl), pair(r_t, k_h, incl)
    rhs = jnp.concatenate([a_t, a_ak @ v], axis=-1)
    sol = lax.linalg.triangular_solve(-a_ab, rhs, left_side=True, lower=True, unit_diagonal=True)
    p_mat, q_mat = jnp.split(sol, 2, axis=-1)
    y_v = a_rk @ v
    s_v = jnp.einsum('bnhcv,bnhck->bnhvk', v, k_e)
    d_end = jnp.exp(g_end[:, :, :, 0])
    xs = tuple(jnp.moveaxis(t, 1, 0) for t in (p_mat, q_mat, r_t, a_rb, y_v, b_e, s_v, d_end))

    def step(S, inp):
        p_c, q_c, r_c, arb, yv, be, sv, de = inp
        u = jnp.einsum('bhck,bhvk->bhcv', p_c, S) + q_c
        y = jnp.einsum('bhck,bhvk->bhcv', r_c, S) + arb @ u + yv
        S = S * de[:, :, None, :] + jnp.einsum('bhcv,bhck->bhvk', u, be) + sv
        return S, y

    s_T, ys = lax.scan(step, s0.astype(jnp.float32), xs)
    y = jnp.moveaxis(ys, 0, 1).transpose(0, 1, 3, 2, 4).reshape(B, Tp, H_C, N_C)[:, :T]
    return y, s_T


def head_group_norm(y, g, b):
    mu = jnp.mean(y, axis=-1, keepdims=True)
    yc = y - mu
    yn = yc * lax.rsqrt(jnp.mean(yc * yc, axis=-1, keepdims=True) + LNX_EPS)
    B, T = y.shape[0], y.shape[1]
    return yn.reshape(B, T, C_W) * g + b


def rwkv7_time_mix(cz, shift0, wkv0, v_first, vres, mu, w0, w2, a0, a2, g2, k_k, k_a, r_k, ln_g, ln_b):
    B, T, _ = cz.shape
    dt = cz.dtype
    cz_prev = jnp.concatenate([shift0[:, None, :].astype(dt), cz[:, :-1]], axis=1)
    zz = cz + (cz_prev - cz) * mu
    r, k, v, xw, xa, xg = _split(zz, C_SPLITS)
    w = -jax.nn.softplus(-(w0 + jnp.tanh(xw) @ w2)) - 0.5
    a = jax.nn.sigmoid(a0 + xa @ a2)
    g = jax.nn.sigmoid(xg) @ g2
    if vres is None:
        v_first = v
    else:
        v0, v1, v2 = vres
        v = v + (v_first - v) * jax.nn.sigmoid(v0 + (v @ v1) @ v2)
    heads = lambda t: t.reshape(B, T, H_C, N_C)
    kk = heads(k * k_k).astype(jnp.float32)
    kk = kk * lax.rsqrt(jnp.maximum(jnp.sum(kk * kk, axis=-1, keepdims=True), 1e-24))
    k = k * (1 + (a - 1) * k_a)
    logw = -jnp.exp(w.astype(jnp.float32))
    y, wkv_T = wkv7_chunked(heads(r), heads(logw), heads(k), heads(v), -kk, kk * heads(a).astype(jnp.float32), wkv0)
    y = head_group_norm(y, ln_g, ln_b)
    bonus = jnp.sum(heads(r) * heads(k) * r_k, axis=-1, keepdims=True) * heads(v)
    y = (y + bonus.reshape(B, T, C_W)) * g
    return y.astype(dt), v_first, wkv_T, cz[:, -1]


def run_trunk(x, p, pos, past_pos, past_a_k, past_a_v, past_b_k, past_b_v, past_b_ki, wkv0, shift0, weights):
    (norm_mix, w_in, a_q_norm, a_k_norm, b_q_norm, b_k_norm, c_mu, c_w0, c_w2, c_a0, c_a2, c_g2,
     c_v0, c_v1, c_v2, c_k_k, c_k_a, c_r_k, c_ln_g, c_ln_b, w_br_a, w_br_b, w_br_c, w_out,
     norm_ffn, w_ffn_in, w_ffn_out, norm_ple, w_ple_gate, w_ple_proj) = weights
    B, T, _ = x.shape
    dt = x.dtype
    k_pos = jnp.concatenate([past_pos, pos])
    topk = min(TOPK_MAX, k_pos.shape[0] // 4)
    h = x
    v_first = None
    outs = ([], [], [], [], [], [], [])
    for i in range(DEPTH):
        u = rms_norm(h, norm_mix[i])
        (aq, ak, av, bq, bk, bv, bqi, bki, bwi, cz, ga, gb, gc) = _split(u @ w_in[i], IN_SPLITS)
        aq = rms_norm(aq.reshape(B, T, H_A, HEAD_DIM), a_q_norm[i]) * (-(HEAD_DIM ** -0.5))
        ak = rms_norm(ak.reshape(B, T, H_A, HEAD_DIM), a_k_norm[i])
        av = av.reshape(B, T, H_A, HEAD_DIM)
        ka = jnp.concatenate([past_a_k[i].astype(dt), ak], axis=1)
        va = jnp.concatenate([past_a_v[i].astype(dt), av], axis=1)
        o_a = causal_sweep(stick_breaking_attend, pos, (aq,), (ka, va), k_pos)
        bq = rope_partial(rms_norm(bq.reshape(B, T, H_B, HEAD_DIM), b_q_norm[i]), pos) * (HEAD_DIM ** -0.5)
        bk = rope_partial(rms_norm(bk[:, :, None, :], b_k_norm[i]), pos)[:, :, 0]
        bqi = rope_partial(bqi.reshape(B, T, H_IDX, D_IDX), pos) * (D_IDX ** -0.5)
        bki = rope_partial(bki[:, :, None, :], pos)[:, :, 0]
        bwi = bwi * (H_IDX ** -0.5)
        kb = jnp.concatenate([past_b_k[i].astype(dt), bk], axis=1)
        vb = jnp.concatenate([past_b_v[i].astype(dt), bv], axis=1)
        kib = jnp.concatenate([past_b_ki[i].astype(dt), bki], axis=1)
        o_b = causal_sweep(lambda qp, qs, ks, kp: dsa_attend(qp, qs, ks, kp, topk), pos, (bq, bqi, bwi), (kb, vb, kib), k_pos)
        vres = None if i == 0 else (c_v0[i - 1], c_v1[i - 1], c_v2[i - 1])
        o_c, v_first, wkv_T, shift_T = rwkv7_time_mix(cz, shift0[i], wkv0[i], v_first, vres, c_mu[i], c_w0[i], c_w2[i], c_a0[i], c_a2[i], c_g2[i], c_k_k[i], c_k_a[i], c_r_k[i], c_ln_g[i], c_ln_b[i])
        merged = (jax.nn.sigmoid(ga) * (o_a.reshape(B, T, A_W) @ w_br_a[i])
                  + jax.nn.sigmoid(gb) * (o_b.reshape(B, T, B_W) @ w_br_b[i])
                  + jax.nn.sigmoid(gc) * (o_c @ w_br_c[i]))
        h = h + merged @ w_out[i]
        gate, up = jnp.split(rms_norm(h, norm_ffn[i]) @ w_ffn_in[i], 2, axis=-1)
        h = h + (jax.nn.silu(gate) * up) @ w_ffn_out[i]
        h = h + jax.nn.sigmoid(rms_norm(h, norm_ple[i]) @ w_ple_gate[i]) * (p[i] @ w_ple_proj[i])
        for lst, val in zip(outs, (ak, av, bk, bv, bki, wkv_T, shift_T)):
            lst.append(val)
    return h, [jnp.stack(l) for l in outs]


def setup_inputs(seed: int = 0) -> dict:
    key = jax.random.key(seed)
    keys = jax.random.split(key, 64)
    cnt = [0]
    f32 = jnp.float32

    def nk():
        cnt[0] += 1
        return keys[cnt[0] - 1]

    def nrm(shape, scale=1.0):
        return jax.random.normal(nk(), shape, f32) * scale

    def near_one(shape, center=1.0):
        return center + 0.05 * jax.random.normal(nk(), shape, f32)

    def unif(shape, lo, hi):
        return jax.random.uniform(nk(), shape, f32, lo, hi)

    return {
        'x_prompt': nrm((BATCH, SEQ, D_MODEL)),
        'x_sample': nrm((DEC_BATCH, DEC_SEQ, D_MODEL)),
        'cache_a_k': nrm((DEPTH, DEC_BATCH, PAST_LEN, H_A, HEAD_DIM)),
        'cache_a_v': nrm((DEPTH, DEC_BATCH, PAST_LEN, H_A, HEAD_DIM)),
        'cache_b_k': nrm((DEPTH, DEC_BATCH, PAST_LEN, HEAD_DIM)),
        'cache_b_v': nrm((DEPTH, DEC_BATCH, PAST_LEN, HEAD_DIM)),
        'cache_b_kidx': nrm((DEPTH, DEC_BATCH, PAST_LEN, D_IDX)),
        'state_c_wkv': nrm((DEPTH, DEC_BATCH, H_C, N_C, N_C), 0.5),
        'state_c_shift': nrm((DEPTH, DEC_BATCH, C_COLS)),
        'p_prompt': nrm((DEPTH, BATCH, SEQ, PLE_DIM)),
        'p_sample': nrm((DEPTH, DEC_BATCH, DEC_SEQ, PLE_DIM)),
        'norm_mix': near_one((DEPTH, D_MODEL)),
        'w_in': nrm((DEPTH, D_MODEL, IN_COLS), D_MODEL ** -0.5),
        'a_q_norm': near_one((DEPTH, HEAD_DIM)),
        'a_k_norm': near_one((DEPTH, HEAD_DIM)),
        'b_q_norm': near_one((DEPTH, HEAD_DIM)),
        'b_k_norm': near_one((DEPTH, HEAD_DIM)),
        'c_mu': unif((DEPTH, C_COLS), 0.0, 1.0),
        'c_w0': unif((DEPTH, C_W), -6.0, -1.0),
        'c_w2': nrm((DEPTH, D_DECAY_LORA, C_W), 0.1 * D_DECAY_LORA ** -0.5),
        'c_a0': nrm((DEPTH, C_W), 0.1),
        'c_a2': nrm((DEPTH, D_AAA_LORA, C_W), 0.5 * D_AAA_LORA ** -0.5),
        'c_g2': nrm((DEPTH, D_GATE_LORA, C_W), D_GATE_LORA ** -0.5),
        'c_v0': nrm((DEPTH - 1, C_W), 0.1),
        'c_v1': nrm((DEPTH - 1, C_W, D_MV_LORA), C_W ** -0.5),
        'c_v2': nrm((DEPTH - 1, D_MV_LORA, C_W), 0.5 * D_MV_LORA ** -0.5),
        'c_k_k': near_one((DEPTH, C_W), 0.85),
        'c_k_a': near_one((DEPTH, C_W)),
        'c_r_k': nrm((DEPTH, H_C, N_C), 0.1),
        'c_ln_g': near_one((DEPTH, C_W)),
        'c_ln_b': nrm((DEPTH, C_W), 0.02),
        'w_br_a': nrm((DEPTH, A_W, D_MODEL), A_W ** -0.5),
        'w_br_b': nrm((DEPTH, B_W, D_MODEL), B_W ** -0.5),
        'w_br_c': nrm((DEPTH, C_W, D_MODEL), C_W ** -0.5),
        'w_out': nrm((DEPTH, D_MODEL, D_MODEL), D_MODEL ** -0.5),
        'norm_ffn': near_one((DEPTH, D_MODEL)),
        'w_ffn_in': nrm((DEPTH, D_MODEL, 2 * D_FF), D_MODEL ** -0.5),
        'w_ffn_out': nrm((DEPTH, D_FF, D_MODEL), D_FF ** -0.5),
        'norm_ple': near_one((DEPTH, D_MODEL)),
        'w_ple_gate': nrm((DEPTH, D_MODEL, D_MODEL), D_MODEL ** -0.5),
        'w_ple_proj': nrm((DEPTH, PLE_DIM, D_MODEL), PLE_DIM ** -0.5),
    }


def reference(x_prompt, x_sample, cache_a_k, cache_a_v, cache_b_k, cache_b_v, cache_b_kidx, state_c_wkv, state_c_shift, p_prompt, p_sample, norm_mix, w_in, a_q_norm, a_k_norm, b_q_norm, b_k_norm, c_mu, c_w0, c_w2, c_a0, c_a2, c_g2, c_v0, c_v1, c_v2, c_k_k, c_k_a, c_r_k, c_ln_g, c_ln_b, w_br_a, w_br_b, w_br_c, w_out, norm_ffn, w_ffn_in, w_ffn_out, norm_ple, w_ple_gate, w_ple_proj):
    weights = (norm_mix, w_in, a_q_norm, a_k_norm, b_q_norm, b_k_norm, c_mu, c_w0, c_w2, c_a0, c_a2, c_g2,
               c_v0, c_v1, c_v2, c_k_k, c_k_a, c_r_k, c_ln_g, c_ln_b, w_br_a, w_br_b, w_br_c, w_out,
               norm_ffn, w_ffn_in, w_ffn_out, norm_ple, w_ple_gate, w_ple_proj)
    dt = x_prompt.dtype
    bp, tp = x_prompt.shape[0], x_prompt.shape[1]
    y_prompt, (ak_p, av_p, bk_p, bv_p, bki_p, wkv_p, shift_p) = run_trunk(
        x_prompt, p_prompt, jnp.arange(tp), jnp.arange(0),
        jnp.zeros((DEPTH, bp, 0, H_A, HEAD_DIM), dt), jnp.zeros((DEPTH, bp, 0, H_A, HEAD_DIM), dt),
        jnp.zeros((DEPTH, bp, 0, HEAD_DIM), dt), jnp.zeros((DEPTH, bp, 0, HEAD_DIM), dt),
        jnp.zeros((DEPTH, bp, 0, D_IDX), dt),
        jnp.zeros((DEPTH, bp, H_C, N_C, N_C), jnp.float32), jnp.zeros((DEPTH, bp, C_COLS), dt),
        weights)
    past_len = cache_a_k.shape[2]
    y_sample, (ak_s, av_s, bk_s, bv_s, bki_s, wkv_s, shift_s) = run_trunk(
        x_sample, p_sample, past_len + jnp.arange(x_sample.shape[1]), jnp.arange(past_len),
        cache_a_k, cache_a_v, cache_b_k, cache_b_v, cache_b_kidx, state_c_wkv, state_c_shift,
        weights)
    return (y_prompt, y_sample, ak_p, av_p, bk_p, bv_p, bki_p, wkv_p, shift_p, ak_s, av_s, bk_s, bv_s, bki_s, wkv_s, shift_s)
```

```python
import functools
import math

import jax
import jax.numpy as jnp
from jax import lax
from jax.experimental import pallas as pl
from jax.experimental.pallas import tpu as pltpu

F32 = jnp.float32
BF16 = jnp.bfloat16

D_MODEL = 2048
DEPTH = 2
HEAD_DIM = 128
H_A = 4
A_W = H_A * HEAD_DIM
H_B = 4
B_W = H_B * HEAD_DIM
H_IDX = 4
D_IDX = 64
TOPK_MAX = 256
CHUNK = 64
H_C = 16
N_C = 64
C_W = H_C * N_C
SCAN_CHUNK = 32
D_DECAY_LORA = 96
D_AAA_LORA = 96
D_GATE_LORA = 256
C_COLS = 3 * C_W + D_DECAY_LORA + D_AAA_LORA + D_GATE_LORA
C_PAD = 3584
C_TAIL = C_PAD - 3 * C_W
ATT_W = 3 * A_W + B_W + 2 * HEAD_DIM + H_IDX * D_IDX + 128
ROPE_THETA = 500000.0
ROT_FRACTION = 4
NORM_EPS = 1e-6
LNX_EPS = 64e-5
NEG_BIG = -1e30

VMEM_LIMIT = 56 * 1024 * 1024
INT_MIN = -(2 ** 31)
KEY_NEG_INF = (0xFF800000 ^ 0x7FFFFFFF) - 2 ** 32


def _params(*sem):
    return pltpu.CompilerParams(dimension_semantics=sem, vmem_limit_bytes=VMEM_LIMIT)


def _row_tile(n, pref):
    t = min(n, pref)
    assert n % t == 0, (n, t)
    return t


def _nt_dot(a, b):
    return lax.dot_general(a, b, (((1,), (1,)), ((), ())), preferred_element_type=F32)


def _rms(x, gain):
    xn = x * lax.rsqrt(jnp.mean(x * x, axis=-1, keepdims=True) + NORM_EPS)
    return xn * gain


def _mm_norm_kernel(x_ref, g_ref, w_ref, o_ref, xn_ref):
    @pl.when(pl.program_id(1) == 0)
    def _():
        xn_ref[...] = _rms(x_ref[...], g_ref[...]).astype(BF16)

    o_ref[...] = jnp.dot(xn_ref[...], w_ref[...], preferred_element_type=F32)


def mm_norm(x, gain, w, tn):
    n, k = x.shape
    nc = w.shape[1]
    tm = _row_tile(n, 1024)
    assert nc % tn == 0
    return pl.pallas_call(
        _mm_norm_kernel,
        grid=(n // tm, nc // tn),
        in_specs=[pl.BlockSpec((tm, k), lambda i, j: (i, 0)),
                  pl.BlockSpec((1, k), lambda i, j: (0, 0)),
                  pl.BlockSpec((k, tn), lambda i, j: (0, j))],
        out_specs=pl.BlockSpec((tm, tn), lambda i, j: (i, j)),
        out_shape=jax.ShapeDtypeStruct((n, nc), F32),
        scratch_shapes=[pltpu.VMEM((tm, k), BF16)],
        compiler_params=_params("parallel", "arbitrary"),
        name="mm_norm",
    )(x, gain.reshape(1, k), w)


def _mm_res_kernel(x_ref, w_ref, h_ref, o_ref):
    o_ref[...] = h_ref[...] + jnp.dot(x_ref[...], w_ref[...], preferred_element_type=F32)


def mm_res(x, w, h):
    n, k = x.shape
    nc = w.shape[1]
    tm = _row_tile(n, 1024)
    tn = 1024
    return pl.pallas_call(
        _mm_res_kernel,
        grid=(n // tm, nc // tn),
        in_specs=[pl.BlockSpec((tm, k), lambda i, j: (i, 0)),
                  pl.BlockSpec((k, tn), lambda i, j: (0, j)),
                  pl.BlockSpec((tm, tn), lambda i, j: (i, j))],
        out_specs=pl.BlockSpec((tm, tn), lambda i, j: (i, j)),
        out_shape=jax.ShapeDtypeStruct((n, nc), F32),
        compiler_params=_params("parallel", "parallel"),
        name="mm_res",
    )(x, w, h)


def _mm_gated3_kernel(oa_ref, ob_ref, oc_ref, wa_ref, wb_ref, wc_ref, ga_ref, gb_ref, gc_ref, o_ref):
    ya = jnp.dot(oa_ref[...], wa_ref[...], preferred_element_type=F32)
    yb = jnp.dot(ob_ref[...], wb_ref[...], preferred_element_type=F32)
    yc = jnp.dot(oc_ref[...], wc_ref[...], preferred_element_type=F32)
    m = (jax.nn.sigmoid(ga_ref[...]) * ya + jax.nn.sigmoid(gb_ref[...]) * yb
         + jax.nn.sigmoid(gc_ref[...]) * yc)
    o_ref[...] = m.astype(o_ref.dtype)


def mm_gated3(oa, ob, oc, wa, wb, wc, gates):
    n = oa.shape[0]
    tm = _row_tile(n, 512)
    tn = 512
    nj = D_MODEL // tn
    row = lambda w: pl.BlockSpec((tm, w), lambda i, j: (i, 0))
    col = lambda k: pl.BlockSpec((k, tn), lambda i, j: (0, j))
    gate = lambda s: pl.BlockSpec((tm, tn), lambda i, j: (i, j + s * nj))
    return pl.pallas_call(
        _mm_gated3_kernel,
        grid=(n // tm, nj),
        in_specs=[row(A_W), row(B_W), row(C_W), col(A_W), col(B_W), col(C_W), gate(0), gate(1), gate(2)],
        out_specs=pl.BlockSpec((tm, tn), lambda i, j: (i, j)),
        out_shape=jax.ShapeDtypeStruct((n, D_MODEL), BF16),
        compiler_params=_params("parallel", "parallel"),
        name="mm_gated3",
    )(oa, ob, oc, wa, wb, wc, gates, gates, gates)


def _ffn_kernel(h_ref, g_ref, wg_ref, wu_ref, wo_ref, o_ref, xn_ref, acc_ref):
    f = pl.program_id(1)

    @pl.when(f == 0)
    def _():
        xn_ref[...] = _rms(h_ref[...], g_ref[...]).astype(BF16)
        acc_ref[...] = jnp.zeros_like(acc_ref)

    xn = xn_ref[...]
    gate = jnp.dot(xn, wg_ref[...], preferred_element_type=F32)
    up = jnp.dot(xn, wu_ref[...], preferred_element_type=F32)
    act = (jax.nn.silu(gate) * up).astype(BF16)
    acc_ref[...] += jnp.dot(act, wo_ref[...], preferred_element_type=F32)

    @pl.when(f == pl.num_programs(1) - 1)
    def _():
        o_ref[...] = h_ref[...] + acc_ref[...]


def ffn(h, gain, w_in, w_out):
    n, d = h.shape
    dff = w_out.shape[0]
    tm = _row_tile(n, 512)
    tf = 512
    assert dff % tf == 0
    nf = dff // tf
    return pl.pallas_call(
        _ffn_kernel,
        grid=(n // tm, nf),
        in_specs=[pl.BlockSpec((tm, d), lambda i, f: (i, 0)),
                  pl.BlockSpec((1, d), lambda i, f: (0, 0)),
                  pl.BlockSpec((d, tf), lambda i, f: (0, f)),
                  pl.BlockSpec((d, tf), lambda i, f: (0, f + nf)),
                  pl.BlockSpec((tf, d), lambda i, f: (f, 0))],
        out_specs=pl.BlockSpec((tm, d), lambda i, f: (i, 0)),
        out_shape=jax.ShapeDtypeStruct((n, d), F32),
        scratch_shapes=[pltpu.VMEM((tm, d), BF16), pltpu.VMEM((tm, d), F32)],
        compiler_params=_params("parallel", "arbitrary"),
        name="ffn",
    )(h, gain.reshape(1, d), w_in, w_in, w_out)


def _ple_kernel(h_ref, g_ref, wg_ref, p_ref, wp_ref, hj_ref, o_ref, xn_ref):
    @pl.when(pl.program_id(1) == 0)
    def _():
        xn_ref[...] = _rms(h_ref[...], g_ref[...]).astype(BF16)

    gate = jnp.dot(xn_ref[...], wg_ref[...], preferred_element_type=F32)
    emb = jnp.dot(p_ref[...], wp_ref[...], preferred_element_type=F32)
    o_ref[...] = hj_ref[...] + jax.nn.sigmoid(gate) * emb


def ple(h, gain, w_gate, p, w_proj):
    n, d = h.shape
    pd = p.shape[1]
    tm = _row_tile(n, 1024)
    tn = 512
    return pl.pallas_call(
        _ple_kernel,
        grid=(n // tm, d // tn),
        in_specs=[pl.BlockSpec((tm, d), lambda i, j: (i, 0)),
                  pl.BlockSpec((1, d), lambda i, j: (0, 0)),
                  pl.BlockSpec((d, tn), lambda i, j: (0, j)),
                  pl.BlockSpec((tm, pd), lambda i, j: (i, 0)),
                  pl.BlockSpec((pd, tn), lambda i, j: (0, j)),
                  pl.BlockSpec((tm, tn), lambda i, j: (i, j))],
        out_specs=pl.BlockSpec((tm, tn), lambda i, j: (i, j)),
        out_shape=jax.ShapeDtypeStruct((n, d), F32),
        scratch_shapes=[pltpu.VMEM((tm, d), BF16)],
        compiler_params=_params("parallel", "arbitrary"),
        name="ple",
    )(h, gain.reshape(1, d), w_gate, p, w_proj, h)


def _rope_tables(pos, width, period, rot, scale_lanes=None):
    half = rot // 2
    inv = ROPE_THETA ** (-jnp.arange(half, dtype=F32) / half)
    ang = pos.astype(F32)[:, None] * inv[None, :]
    cos, sin = jnp.cos(ang), jnp.sin(ang)
    lane = jnp.arange(width) % period
    first, second = lane < half, (lane >= half) & (lane < rot)
    idx = lane % half
    c = jnp.where((first | second)[None, :], cos[:, idx], 1.0)
    s_up = jnp.where(first[None, :], -sin[:, idx], 0.0)
    s_dn = jnp.where(second[None, :], sin[:, idx], 0.0)
    if scale_lanes is not None:
        only, mult = scale_lanes
        c = jnp.where(only[None, :], c, mult[None, :])
        s_up = jnp.where(only[None, :], s_up, 0.0)
        s_dn = jnp.where(only[None, :], s_dn, 0.0)
    return jnp.stack([c, s_up, s_dn]).astype(F32)


def _rope(x, tab, half):
    return (x * tab[0] + pltpu.roll(x, x.shape[1] - half, 1) * tab[1] + pltpu.roll(x, half, 1) * tab[2])


def _post_in_kernel(z_ref, t128_ref, t64_ref, tki_ref, gaq_ref, gak_ref, gbq_ref, gbk_ref,
                    aq_ref, akf_ref, akb_ref, avf_ref, avb_ref, bq_ref, bkf_ref, bkb_ref,
                    bvf_ref, bvb_ref, qi_ref, kiw_ref, kilo_ref, kihi_ref):
    hd = HEAD_DIM
    t128 = t128_ref[...]
    t64 = t64_ref[...]
    half128 = hd // ROT_FRACTION // 2
    half64 = D_IDX // ROT_FRACTION // 2
    for h in range(H_A):
        sl = slice(h * hd, (h + 1) * hd)
        aq = _rms(z_ref[:, sl], gaq_ref[...]) * (-(hd ** -0.5))
        aq_ref[:, sl] = aq.astype(BF16)
        ak = _rms(z_ref[:, A_W + h * hd:A_W + (h + 1) * hd], gak_ref[...])
        akf_ref[:, sl] = ak
        akb_ref[:, sl] = ak.astype(BF16)
        av = z_ref[:, 2 * A_W + h * hd:2 * A_W + (h + 1) * hd]
        avf_ref[:, sl] = av
        avb_ref[:, sl] = av.astype(BF16)
    o = 3 * A_W
    for h in range(H_B):
        sl = slice(h * hd, (h + 1) * hd)
        bq = _rope(_rms(z_ref[:, o + h * hd:o + (h + 1) * hd], gbq_ref[...]), t128, half128) * (hd ** -0.5)
        bq_ref[:, sl] = bq.astype(BF16)
    o += B_W
    bk = _rope(_rms(z_ref[:, o:o + hd], gbk_ref[...]), t128, half128)
    bkf_ref[...] = bk
    bkb_ref[...] = bk.astype(BF16)
    o += hd
    bv = z_ref[:, o:o + hd]
    bvf_ref[...] = bv
    bvb_ref[...] = bv.astype(BF16)
    o += hd
    for c in range(H_IDX * D_IDX // 128):
        qi = _rope(z_ref[:, o + c * 128:o + (c + 1) * 128], t64, half64) * (D_IDX ** -0.5)
        qi_ref[:, c * 128:(c + 1) * 128] = qi.astype(BF16)
    o += H_IDX * D_IDX
    kiw = _rope(z_ref[:, o:o + 128], tki_ref[...], half64)
    kiw_ref[...] = kiw
    lane = lax.broadcasted_iota(jnp.int32, kiw.shape, 1)
    ki = jnp.where(lane < D_IDX, kiw, 0.0)
    kilo_ref[...] = ki.astype(BF16)
    kihi_ref[...] = pltpu.roll(ki, D_IDX, 1).astype(BF16)


def post_in(z, tabs, gains, t_len):
    n = z.shape[0]
    tm = _row_tile(t_len, 512)
    nt = t_len // tm
    rows = lambda w: pl.BlockSpec((tm, w), lambda i: (i, 0))
    tab = pl.BlockSpec((3, tm, 128), lambda i: (0, i % nt, 0))
    gain = pl.BlockSpec((1, HEAD_DIM), lambda i: (0, 0))
    widths = [(A_W, BF16), (A_W, F32), (A_W, BF16), (A_W, F32), (A_W, BF16), (B_W, BF16),
              (HEAD_DIM, F32), (HEAD_DIM, BF16), (HEAD_DIM, F32), (HEAD_DIM, BF16),
              (H_IDX * D_IDX, BF16), (128, F32), (128, BF16), (128, BF16)]
    return pl.pallas_call(
        _post_in_kernel,
        grid=(n // tm,),
        in_specs=[rows(ATT_W), tab, tab, tab, gain, gain, gain, gain],
        out_specs=[rows(w) for w, _ in widths],
        out_shape=[jax.ShapeDtypeStruct((n, w), dt) for w, dt in widths],
        compiler_params=_params("parallel"),
        name="post_in",
    )(z, *tabs, *[g.reshape(1, HEAD_DIM) for g in gains])


def _attn_a_kernel(q_ref, k_ref, v_ref, tri_ref, o_ref, *, tq, tk, past):
    i = pl.program_id(2)
    q0 = past + i * tq
    q = q_ref[...]
    tri = tri_ref[...]
    n_full = q0 // tk
    n_all = (q0 + tq - 1 + tk - 1) // tk
    row = q0 + lax.broadcasted_iota(jnp.int32, (tq, tk), 0)
    col = lax.broadcasted_iota(jnp.int32, (tq, tk), 1)

    def step(j, carry, masked):
        later, acc = carry
        off = pl.multiple_of(j * tk, tk)
        kb = k_ref[pl.ds(off, tk), :]
        vb = v_ref[pl.ds(off, tk), :]
        nz = _nt_dot(q, kb)
        log_keep = jnp.minimum(nz, 0.0) - jnp.log(1.0 + jnp.exp(-jnp.abs(nz)))
        if masked:
            causal = (off + col) < row
            log_keep = jnp.where(causal, log_keep, 0.0)
        within = jnp.dot(log_keep.astype(BF16), tri, preferred_element_type=F32)
        w = jnp.exp(within + later - nz)
        if masked:
            w = jnp.where(causal, w, 0.0)
        acc = acc + jnp.dot(w.astype(BF16), vb, preferred_element_type=F32)
        return later + within[:, 0:1], acc

    carry = (jnp.zeros((tq, 1), F32), jnp.zeros((tq, HEAD_DIM), F32))
    carry = lax.fori_loop(0, n_all - n_full, lambda t, c: step(n_all - 1 - t, c, True), carry)
    carry = lax.fori_loop(0, n_full, lambda t, c: step(n_full - 1 - t, c, False), carry)
    o_ref[...] = carry[1].astype(o_ref.dtype)


def attn_a(q, k, v, past, tk):
    b, t, _ = q.shape
    lp = k.shape[1]
    tq = _row_tile(t, 512)
    tri = jnp.tri(tk, dtype=BF16)
    return pl.pallas_call(
        functools.partial(_attn_a_kernel, tq=tq, tk=tk, past=past),
        grid=(b, H_A, t // tq),
        in_specs=[pl.BlockSpec((None, tq, HEAD_DIM), lambda bi, h, i: (bi, i, h)),
                  pl.BlockSpec((None, lp, HEAD_DIM), lambda bi, h, i: (bi, 0, h)),
                  pl.BlockSpec((None, lp, HEAD_DIM), lambda bi, h, i: (bi, 0, h)),
                  pl.BlockSpec((tk, tk), lambda bi, h, i: (0, 0))],
        out_specs=pl.BlockSpec((None, tq, HEAD_DIM), lambda bi, h, i: (bi, i, h)),
        out_shape=jax.ShapeDtypeStruct((b, t, A_W), BF16),
        compiler_params=_params("parallel", "parallel", "parallel"),
        name="attn_a",
    )(q, k, v, tri)


def _dsa_kernel(qi_ref, kiw_ref, kilo_ref, kihi_ref, q_ref, k_ref, v_ref, triu_ref, o_ref, key_ref,
                *, tq, tk, past, n_keys, topk):
    i = pl.program_id(1)
    q0 = past + i * tq
    pos = q0 + lax.broadcasted_iota(jnp.int32, (tq, 1), 0)
    lim = jnp.minimum((pos // CHUNK + 1) * CHUNK, n_keys)
    n_adm = jnp.minimum(((q0 + tq - 1) // CHUNK + 1) * CHUNK, n_keys)
    nt = (n_adm + tk - 1) // tk
    col = lax.broadcasted_iota(jnp.int32, (tq, tk), 1)
    ngrp = tk // 128

    qi = qi_ref[...]
    kiw = kiw_ref[...]
    wi = [kiw[:, D_IDX + h:D_IDX + h + 1] for h in range(H_IDX)]

    def score_tile(t, _):
        off = pl.multiple_of(t * tk, tk)
        kis = (kilo_ref[pl.ds(off, tk), :], kihi_ref[pl.ds(off, tk), :])
        s = jnp.zeros((tq, tk), F32)
        for h in range(H_IDX):
            s_idx = _nt_dot(qi[:, (h // 2) * 128:(h // 2 + 1) * 128], kis[h % 2])
            s = s + wi[h] * jnp.maximum(s_idx, 0.0)
        s = jnp.where(s == 0.0, 0.0, s)
        s = jnp.where(off + col < lim, s, -jnp.inf)
        bits = lax.bitcast_convert_type(s, jnp.int32)
        key_ref[:, pl.ds(off, tk)] = bits ^ ((bits >> 31) & 0x7FFFFFFF)
        return 0

    lax.fori_loop(0, nt, score_tile, 0)

    def count_ge(cand):
        cand_b = jnp.broadcast_to(cand, (tq, 128))

        def body(t, acc):
            off = pl.multiple_of(t * tk, tk)
            kt = key_ref[:, pl.ds(off, tk)]
            for g in range(ngrp):
                acc = acc + jnp.where(kt[:, g * 128:(g + 1) * 128] >= cand_b, 1, 0)
            return acc

        acc = lax.fori_loop(0, nt, body, jnp.zeros((tq, 128), jnp.int32))
        return jnp.sum(acc, axis=1, keepdims=True)

    def bit_step(b, t_u):
        cand_u = t_u | lax.shift_left(jnp.int32(1), 31 - b)
        ok = count_ge(cand_u ^ INT_MIN) >= topk
        return jnp.where(ok, cand_u, t_u)

    t_u = lax.fori_loop(0, 32, bit_step, jnp.zeros((tq, 1), jnp.int32))
    tau = t_u ^ INT_MIN
    tau_next = jnp.where(tau == 2 ** 31 - 1, tau, tau + 1)
    cnt_gt = jnp.where(tau == 2 ** 31 - 1, 0, count_ge(tau_next))
    need = (topk - cnt_gt).astype(F32)

    triu = triu_ref[...]

    def select_tile(t, seen):
        off = pl.multiple_of(t * tk, tk)
        kt = key_ref[:, pl.ds(off, tk)]
        eq = kt == tau
        prefix = seen + jnp.dot(jnp.where(eq, 1.0, 0.0).astype(BF16), triu, preferred_element_type=F32)
        sel = (kt > tau) | (eq & (prefix <= need))
        sel = sel & (kt > KEY_NEG_INF)
        bias = jnp.where(sel, 0.0, NEG_BIG).astype(F32)
        key_ref[:, pl.ds(off, tk)] = lax.bitcast_convert_type(bias, jnp.int32)
        return prefix[:, tk - 1:tk]

    lax.fori_loop(0, nt, select_tile, jnp.zeros((tq, 1), F32))

    qs = jnp.concatenate([q_ref[:, h * HEAD_DIM:(h + 1) * HEAD_DIM] for h in range(H_B)], axis=0)

    def attend_tile(t, carry):
        m, l, acc = carry
        off = pl.multiple_of(t * tk, tk)
        kb = k_ref[pl.ds(off, tk), :]
        vb = v_ref[pl.ds(off, tk), :]
        bias = lax.bitcast_convert_type(key_ref[:, pl.ds(off, tk)], F32)
        logits = _nt_dot(qs, kb).reshape(H_B, tq, tk) + bias[None]
        logits = logits.reshape(H_B * tq, tk)
        m_new = jnp.maximum(m, jnp.max(logits, axis=1, keepdims=True))
        alpha = jnp.exp(m - m_new)
        p = jnp.exp(logits - m_new)
        l = alpha * l + jnp.sum(p, axis=1, keepdims=True)
        acc = alpha * acc + jnp.dot(p.astype(BF16), vb, preferred_element_type=F32)
        return m_new, l, acc

    init = (jnp.full((H_B * tq, 1), NEG_BIG, F32), jnp.zeros((H_B * tq, 1), F32),
            jnp.zeros((H_B * tq, HEAD_DIM), F32))
    _, l, acc = lax.fori_loop(0, nt, attend_tile, init)
    out = acc / l
    for h in range(H_B):
        o_ref[:, h * HEAD_DIM:(h + 1) * HEAD_DIM] = out[h * tq:(h + 1) * tq].astype(o_ref.dtype)


def dsa(qi, kiw, kilo, kihi, q, k, v, past, n_keys, tk):
    b, t, _ = q.shape
    lp = k.shape[1]
    tq = _row_tile(t, 128)
    topk = min(TOPK_MAX, n_keys // 4)
    assert lp % tk == 0 and lp >= topk
    triu = jnp.tri(tk, dtype=BF16).T
    qblk = lambda w: pl.BlockSpec((None, tq, w), lambda bi, i: (bi, i, 0))
    kblk = lambda w: pl.BlockSpec((None, lp, w), lambda bi, i: (bi, 0, 0))
    return pl.pallas_call(
        functools.partial(_dsa_kernel, tq=tq, tk=tk, past=past, n_keys=n_keys, topk=topk),
        grid=(b, t // tq),
        in_specs=[qblk(H_IDX * D_IDX), qblk(128), kblk(128), kblk(128), qblk(B_W),
                  kblk(HEAD_DIM), kblk(HEAD_DIM), pl.BlockSpec((tk, tk), lambda bi, i: (0, 0))],
        out_specs=qblk(B_W),
        out_shape=jax.ShapeDtypeStruct((b, t, B_W), BF16),
        scratch_shapes=[pltpu.VMEM((tq, lp), jnp.int32)],
        compiler_params=_params("parallel", "parallel"),
        name="dsa",
    )(qi, kiw, kilo, kihi, q, k, v, triu)


def _softplus(x):
    return jnp.maximum(x, 0.0) + jnp.log(1.0 + jnp.exp(-jnp.abs(x)))


def _c_prep_kernel(*refs, first_layer):
    if first_layer:
        (cz_ref, prev_ref, mu_ref, w0_ref, a0_ref, w2_ref, a2_ref, g2_ref, kk_ref, ka_ref,
         r_ref, kn_ref, k_ref, v_ref, lw_ref, a_ref, g_ref) = refs
    else:
        (cz_ref, prev_ref, mu_ref, w0_ref, a0_ref, w2_ref, a2_ref, g2_ref, kk_ref, ka_ref,
         vf_ref, v0_ref, v1_ref, v2_ref,
         r_ref, kn_ref, k_ref, v_ref, lw_ref, a_ref, g_ref) = refs
    cz = cz_ref[...]
    rowid = lax.broadcasted_iota(jnp.int32, cz.shape, 0)
    prev = jnp.where(rowid == 0, prev_ref[...], pltpu.roll(cz, 1, 0))
    zz = cz + (prev - cz) * mu_ref[...]
    r = zz[:, 0:C_W]
    k = zz[:, C_W:2 * C_W]
    v = zz[:, 2 * C_W:3 * C_W]
    tail = zz[:, 3 * C_W:C_PAD]
    w = w0_ref[...] + jnp.dot(jnp.tanh(tail).astype(BF16), w2_ref[...], preferred_element_type=F32)
    w = -_softplus(-w) - 0.5
    a = jax.nn.sigmoid(a0_ref[...] + jnp.dot(tail.astype(BF16), a2_ref[...], preferred_element_type=F32))
    g = jnp.dot(jax.nn.sigmoid(tail).astype(BF16), g2_ref[...], preferred_element_type=F32)
    if not first_layer:
        lora = jnp.dot(v.astype(BF16), v1_ref[...], preferred_element_type=F32)
        lora = jnp.dot(lora.astype(BF16), v2_ref[...], preferred_element_type=F32)
        v = v + (vf_ref[...] - v) * jax.nn.sigmoid(v0_ref[...] + lora)
    r_ref[...] = r
    kn_ref[...] = k * kk_ref[...]
    k_ref[...] = k * (1.0 + (a - 1.0) * ka_ref[...])
    v_ref[...] = v
    lw_ref[...] = -jnp.exp(w)
    a_ref[...] = a
    g_ref[...] = g


def c_prep(cz, prev_rows, vecs, mats, v_first, vres, tm):
    b, t, _ = cz.shape
    first_layer = vres is None
    blk = lambda w: pl.BlockSpec((None, tm, w), lambda bi, i: (bi, i, 0))
    vec = lambda w: pl.BlockSpec((1, w), lambda bi, i: (0, 0))
    mat = lambda m: pl.BlockSpec(m.shape, lambda bi, i: (0, 0))
    mu, w0, a0, k_k, k_a = vecs
    w2, a2, g2 = mats
    ins = [cz, prev_rows, mu, w0, a0, w2, a2, g2, k_k, k_a]
    specs = [blk(C_PAD), pl.BlockSpec((None, None, 1, C_PAD), lambda bi, i: (bi, i, 0, 0)),
             vec(C_PAD), vec(C_W), vec(C_W), mat(w2), mat(a2), mat(g2), vec(C_W), vec(C_W)]
    if not first_layer:
        v0, v1, v2 = vres
        ins += [v_first, v0, v1, v2]
        specs += [blk(C_W), vec(C_W), mat(v1), mat(v2)]
    return pl.pallas_call(
        functools.partial(_c_prep_kernel, first_layer=first_layer),
        grid=(b, t // tm),
        in_specs=specs,
        out_specs=[blk(C_W)] * 7,
        out_shape=[jax.ShapeDtypeStruct((b, t, C_W), F32)] * 7,
        compiler_params=_params("parallel", "parallel"),
        name="c_prep",
    )(*ins)


def _bdot(a, b, contract_a, contract_b):
    return lax.dot_general(a.astype(BF16), b.astype(BF16), (((contract_a,), (contract_b,)), ((0,), (0,))),
                           preferred_element_type=F32)


def _bdot3(a, b, contract_a, contract_b):
    ah = a.astype(BF16)
    bh = b.astype(BF16)
    al = (a - ah.astype(F32)).astype(BF16)
    bl = (b - bh.astype(F32)).astype(BF16)
    dn = (((contract_a,), (contract_b,)), ((0,), (0,)))
    dot = lambda x, y: lax.dot_general(x, y, dn, preferred_element_type=F32)
    return dot(ah, bh) + (dot(al, bh) + dot(ah, bl))


def _c_scan_kernel(r_ref, kn_ref, k_ref, v_ref, lw_ref, a_ref, g_ref, s0_ref, rk_ref, lng_ref, lnb_ref,
                   y_ref, st_ref, s_ref, *, nck):
    c_len = SCAN_CHUNK

    @pl.when(pl.program_id(1) == 0)
    def _():
        s_ref[...] = s0_ref[...]

    ti = lax.broadcasted_iota(jnp.int32, (H_C, c_len, c_len), 1)
    si = lax.broadcasted_iota(jnp.int32, (H_C, c_len, c_len), 2)
    incl = si <= ti
    strict = si < ti
    tri = jnp.where(incl, 1.0, 0.0)
    eye = jnp.where(si == ti, 1.0, 0.0)
    r_k = rk_ref[...]
    ln_g = lng_ref[...]
    ln_b = lnb_ref[...]

    def chunk(c, _):
        rows = pl.ds(pl.multiple_of(c * c_len, c_len), c_len)
        r = r_ref[:, rows, :]
        k = k_ref[:, rows, :]
        v = v_ref[:, rows, :]
        lw = lw_ref[:, rows, :]
        a = a_ref[:, rows, :]
        kn = kn_ref[:, rows, :]
        kn = kn * lax.rsqrt(jnp.maximum(jnp.sum(kn * kn, axis=-1, keepdims=True), 1e-24))
        aa = -kn
        bb = kn * a
        g = _bdot3(tri, lw, 2, 1)
        g_end = g[:, c_len - 1:c_len, :]
        a_t = aa * jnp.exp(g - lw)
        r_t = r * jnp.exp(g)
        inv = jnp.exp(-g)
        b_h = bb * inv
        k_h = k * inv
        to_end = jnp.exp(g_end - g)
        b_e = bb * to_end
        k_e = k * to_end
        d_end = jnp.exp(g_end)
        a_ab = jnp.where(strict, _bdot(a_t, b_h, 2, 2), 0.0)
        a_ak = jnp.where(strict, _bdot(a_t, k_h, 2, 2), 0.0)
        a_rb = jnp.where(incl, _bdot(r_t, b_h, 2, 2), 0.0)
        a_rk = jnp.where(incl, _bdot(r_t, k_h, 2, 2), 0.0)
        t_inv = eye + a_ab
        pw = a_ab
        for _ in range(4):
            pw = _bdot3(pw, pw, 2, 1)
            t_inv = t_inv + _bdot3(t_inv, pw, 2, 1)
        p_mat = _bdot3(t_inv, a_t, 2, 1)
        q_mat = _bdot3(t_inv, _bdot(a_ak, v, 2, 1), 2, 1)
        y_v = _bdot(a_rk, v, 2, 1)
        s = s_ref[...]
        u = _bdot(p_mat, s, 2, 2) + q_mat
        y = _bdot(r_t, s, 2, 2) + _bdot(a_rb, u, 2, 1) + y_v
        uv = jnp.concatenate([u, v], axis=1)
        bk = jnp.concatenate([b_e, k_e], axis=1)
        s_ref[...] = s * d_end + _bdot(uv, bk, 1, 1)
        mu = jnp.mean(y, axis=-1, keepdims=True)
        yc = y - mu
        yn = yc * lax.rsqrt(jnp.mean(yc * yc, axis=-1, keepdims=True) + LNX_EPS)
        yn = yn * ln_g + ln_b
        bonus = jnp.sum(r * k * r_k, axis=-1, keepdims=True) * v
        y_ref[:, rows, :] = (yn + bonus) * g_ref[:, rows, :]
        return 0

    lax.fori_loop(0, nck, chunk, 0)

    @pl.when(pl.program_id(1) == pl.num_programs(1) - 1)
    def _():
        st_ref[...] = s_ref[...]


def c_scan(r, kn, k, v, lw, a, g, s0, r_k, ln_g, ln_b, tm):
    b, _, t, _ = r.shape
    blk = pl.BlockSpec((None, H_C, tm, N_C), lambda bi, i: (bi, 0, i, 0))
    st = pl.BlockSpec((None, H_C, N_C, N_C), lambda bi, i: (bi, 0, 0, 0))
    vec = pl.BlockSpec((H_C, 1, N_C), lambda bi, i: (0, 0, 0))
    return pl.pallas_call(
        functools.partial(_c_scan_kernel, nck=tm // SCAN_CHUNK),
        grid=(b, t // tm),
        in_specs=[blk] * 7 + [st, vec, vec, vec],
        out_specs=[blk, st],
        out_shape=[jax.ShapeDtypeStruct((b, H_C, t, N_C), F32),
                   jax.ShapeDtypeStruct((b, H_C, N_C, N_C), F32)],
        scratch_shapes=[pltpu.VMEM((H_C, N_C, N_C), F32)],
        compiler_params=_params("parallel", "arbitrary"),
        name="c_scan",
    )(r, kn, k, v, lw, a, g, s0, r_k.reshape(H_C, 1, N_C), ln_g.reshape(H_C, 1, N_C), ln_b.reshape(H_C, 1, N_C))


def _pad_rows(x, rows, axis=1):
    pad = rows - x.shape[axis]
    if pad == 0:
        return x
    widths = [(0, 0)] * x.ndim
    widths[axis] = (0, pad)
    return jnp.pad(x, widths)


def _prep_weights(w):
    out = []
    for i in range(DEPTH):
        w_in = w["w_in"][i]
        o_c = 3 * A_W + B_W + 2 * HEAD_DIM + H_IDX * D_IDX + D_IDX + H_IDX
        o_g = o_c + C_COLS
        w_att = jnp.pad(w_in[:, :o_c], ((0, 0), (0, ATT_W - o_c)))
        w_c = jnp.pad(w_in[:, o_c:o_g], ((0, 0), (0, C_PAD - C_COLS)))
        w_g = w_in[:, o_g:]
        tail_rows = lambda m, off: jnp.pad(m, ((off, C_TAIL - off - m.shape[0]), (0, 0))).astype(BF16)
        lw = dict(
            w_att=w_att.astype(BF16), w_c=w_c.astype(BF16), w_g=w_g.astype(BF16),
            w2=tail_rows(w["c_w2"][i], 0),
            a2=tail_rows(w["c_a2"][i], D_DECAY_LORA),
            g2=tail_rows(w["c_g2"][i], D_DECAY_LORA + D_AAA_LORA),
            mu=jnp.pad(w["c_mu"][i], (0, C_PAD - C_COLS)).reshape(1, C_PAD),
            w_br_a=w["w_br_a"][i].astype(BF16), w_br_b=w["w_br_b"][i].astype(BF16),
            w_br_c=w["w_br_c"][i].astype(BF16), w_out=w["w_out"][i].astype(BF16),
            w_ffn_in=w["w_ffn_in"][i].astype(BF16), w_ffn_out=w["w_ffn_out"][i].astype(BF16),
            w_ple_gate=w["w_ple_gate"][i].astype(BF16), w_ple_proj=w["w_ple_proj"][i].astype(BF16),
        )
        if i > 0:
            lw["v1"] = w["c_v1"][i - 1].astype(BF16)
            lw["v2"] = w["c_v2"][i - 1].astype(BF16)
        out.append(lw)
    return out


def _run_trunk(x, p, past, caches, w, wl):
    b, t, _ = x.shape
    n = b * t
    n_keys = past + t
    pos = past + jnp.arange(t)
    lane = jnp.arange(128)
    tabs = (_rope_tables(pos, 128, HEAD_DIM, HEAD_DIM // ROT_FRACTION),
            _rope_tables(pos, 128, D_IDX, D_IDX // ROT_FRACTION),
            _rope_tables(pos, 128, 128, D_IDX // ROT_FRACTION,
                         scale_lanes=(lane < D_IDX, jnp.where(lane < D_IDX + H_IDX, H_IDX ** -0.5, 1.0))))
    tk_a = 256
    tk_b = 512
    lp_a = -(-n_keys // tk_a) * tk_a
    lp_b = -(-n_keys // tk_b) * tk_b
    t_c = -(-t // SCAN_CHUNK) * SCAN_CHUNK
    tm_c = _row_tile(t, 256)
    tm_s = _row_tile(t_c, 256)
    h = x.reshape(n, D_MODEL)
    outs = ([], [], [], [], [], [], [])
    v_first = None
    for i in range(DEPTH):
        lw = wl[i]
        z_att = mm_norm(h, w["norm_mix"][i], lw["w_att"], ATT_W // 3)
        z_c = mm_norm(h, w["norm_mix"][i], lw["w_c"], C_PAD // 4)
        z_g = mm_norm(h, w["norm_mix"][i], lw["w_g"], 1024)
        (aq, ak_f, ak_b, av_f, av_b, bq, bk_f, bk_b, bv_f, bv_b, qi, kiw, kilo, kihi) = post_in(
            z_att, tabs, (w["a_q_norm"][i], w["a_k_norm"][i], w["b_q_norm"][i], w["b_k_norm"][i]), t)
        seq = lambda a: a.reshape(b, t, a.shape[-1])

        def keys(new, old, lp):
            new = seq(new)
            if old is not None:
                new = jnp.concatenate([old.astype(BF16), new], axis=1)
            return _pad_rows(new, lp)

        c = caches
        past_of = lambda name: None if c is None else c[name][i]
        pa_k = None if c is None else c["a_k"][i].reshape(b, past, A_W)
        pa_v = None if c is None else c["a_v"][i].reshape(b, past, A_W)
        o_a = attn_a(seq(aq), keys(ak_b, pa_k, lp_a), keys(av_b, pa_v, lp_a), past, tk_a)
        if c is None:
            p_lo = p_hi = None
        else:
            p_lo = jnp.pad(c["b_ki"][i], ((0, 0), (0, 0), (0, 128 - D_IDX)))
            p_hi = jnp.pad(c["b_ki"][i], ((0, 0), (0, 0), (128 - D_IDX, 0)))
        o_b = dsa(seq(qi), seq(kiw), keys(kilo, p_lo, lp_b), keys(kihi, p_hi, lp_b), seq(bq),
                  keys(bk_b, past_of("b_k"), lp_b), keys(bv_b, past_of("b_v"), lp_b), past, n_keys, tk_b)
        cz = seq(z_c)
        shift0 = jnp.zeros((b, C_COLS), F32) if c is None else c["shift"][i]
        shift0 = jnp.pad(shift0, ((0, 0), (0, C_PAD - C_COLS)))
        prev_rows = jnp.concatenate([shift0[:, None, :], cz[:, tm_c - 1:t - 1:tm_c, :]], axis=1)
        prev_rows = prev_rows.reshape(b, t // tm_c, 1, C_PAD)
        row = lambda a: a.reshape(1, C_W)
        vres = None if i == 0 else (row(w["c_v0"][i - 1]), lw["v1"], lw["v2"])
        prep = c_prep(cz, prev_rows,
                      (lw["mu"], row(w["c_w0"][i]), row(w["c_a0"][i]), row(w["c_k_k"][i]), row(w["c_k_a"][i])),
                      (lw["w2"], lw["a2"], lw["g2"]), v_first, vres, tm_c)
        if i == 0:
            v_first = prep[3]
        heads = lambda a: _pad_rows(a.reshape(b, t, H_C, N_C).transpose(0, 2, 1, 3), t_c, axis=2)
        s0 = jnp.zeros((b, H_C, N_C, N_C), F32) if c is None else c["wkv"][i]
        y_c, wkv_t = c_scan(*[heads(a) for a in prep], s0, w["c_r_k"][i], w["c_ln_g"][i], w["c_ln_b"][i], tm_s)
        o_c = y_c[:, :, :t].transpose(0, 2, 1, 3).reshape(n, C_W).astype(BF16)
        merged = mm_gated3(o_a.reshape(n, A_W), o_b.reshape(n, B_W), o_c,
                           lw["w_br_a"], lw["w_br_b"], lw["w_br_c"], z_g)
        h = mm_res(merged, lw["w_out"], h)
        h = ffn(h, w["norm_ffn"][i], lw["w_ffn_in"], lw["w_ffn_out"])
        h = ple(h, w["norm_ple"][i], lw["w_ple_gate"], p[i].reshape(n, -1).astype(BF16), lw["w_ple_proj"])
        vals = (ak_f.reshape(b, t, H_A, HEAD_DIM), av_f.reshape(b, t, H_A, HEAD_DIM),
                seq(bk_f), seq(bv_f), seq(kiw)[:, :, :D_IDX], wkv_t, cz[:, -1, :C_COLS])
        for lst, val in zip(outs, vals):
            lst.append(val)
    return h.reshape(b, t, D_MODEL), [jnp.stack(l) for l in outs]


def kernel(x_prompt, x_sample, cache_a_k, cache_a_v, cache_b_k, cache_b_v, cache_b_kidx, state_c_wkv, state_c_shift, p_prompt, p_sample, norm_mix, w_in, a_q_norm, a_k_norm, b_q_norm, b_k_norm, c_mu, c_w0, c_w2, c_a0, c_a2, c_g2, c_v0, c_v1, c_v2, c_k_k, c_k_a, c_r_k, c_ln_g, c_ln_b, w_br_a, w_br_b, w_br_c, w_out, norm_ffn, w_ffn_in, w_ffn_out, norm_ple, w_ple_gate, w_ple_proj):
    w = dict(norm_mix=norm_mix, w_in=w_in, a_q_norm=a_q_norm, a_k_norm=a_k_norm, b_q_norm=b_q_norm,
             b_k_norm=b_k_norm, c_mu=c_mu, c_w0=c_w0, c_w2=c_w2, c_a0=c_a0, c_a2=c_a2, c_g2=c_g2,
             c_v0=c_v0, c_v1=c_v1, c_v2=c_v2, c_k_k=c_k_k, c_k_a=c_k_a, c_r_k=c_r_k, c_ln_g=c_ln_g,
             c_ln_b=c_ln_b, w_br_a=w_br_a, w_br_b=w_br_b, w_br_c=w_br_c, w_out=w_out, norm_ffn=norm_ffn,
             w_ffn_in=w_ffn_in, w_ffn_out=w_ffn_out, norm_ple=norm_ple, w_ple_gate=w_ple_gate,
             w_ple_proj=w_ple_proj)
    wl = _prep_weights(w)
    y_p, o_p = _run_trunk(x_prompt, p_prompt, 0, None, w, wl)
    caches = dict(a_k=cache_a_k, a_v=cache_a_v, b_k=cache_b_k, b_v=cache_b_v, b_ki=cache_b_kidx,
                  wkv=state_c_wkv, shift=state_c_shift)
    y_s, o_s = _run_trunk(x_sample, p_sample, cache_a_k.shape[2], caches, w, wl)
    return (y_p, y_s, *o_p, *o_s)
```

```python
import functools
import math

import jax
import jax.numpy as jnp
from jax import lax
from jax.experimental import pallas as pl
from jax.experimental.pallas import tpu as pltpu

F32 = jnp.float32
BF16 = jnp.bfloat16

D_MODEL = 2048
DEPTH = 2
HEAD_DIM = 128
H_A = 4
A_W = H_A * HEAD_DIM
H_B = 4
B_W = H_B * HEAD_DIM
H_IDX = 4
D_IDX = 64
TOPK_MAX = 256
CHUNK = 64
H_C = 16
N_C = 64
C_W = H_C * N_C
SCAN_CHUNK = 32
D_DECAY_LORA = 96
D_AAA_LORA = 96
D_GATE_LORA = 256
C_COLS = 3 * C_W + D_DECAY_LORA + D_AAA_LORA + D_GATE_LORA
C_PAD = 3584
C_TAIL = C_PAD - 3 * C_W
ATT_W = 3 * A_W + B_W + 2 * HEAD_DIM + H_IDX * D_IDX + 128
ROPE_THETA = 500000.0
ROT_FRACTION = 4
NORM_EPS = 1e-6
LNX_EPS = 64e-5
NEG_BIG = -1e30

VMEM_LIMIT = 56 * 1024 * 1024
INT_MIN = -(2 ** 31)
KEY_NEG_INF = (0xFF800000 ^ 0x7FFFFFFF) - 2 ** 32


def _params(*sem):
    return pltpu.CompilerParams(dimension_semantics=sem, vmem_limit_bytes=VMEM_LIMIT)


def _row_tile(n, pref):
    t = min(n, pref)
    assert n % t == 0, (n, t)
    return t


def _nt_dot(a, b):
    return lax.dot_general(a, b, (((1,), (1,)), ((), ())), preferred_element_type=F32)


def _rms(x, gain):
    xn = x * lax.rsqrt(jnp.mean(x * x, axis=-1, keepdims=True) + NORM_EPS)
    return xn * gain


def _mm_norm_kernel(x_ref, g_ref, w_ref, o_ref, xn_ref):
    @pl.when(pl.program_id(1) == 0)
    def _():
        xn_ref[...] = _rms(x_ref[...], g_ref[...]).astype(BF16)

    o_ref[...] = jnp.dot(xn_ref[...], w_ref[...], preferred_element_type=F32)


def mm_norm(x, gain, w, tn):
    n, k = x.shape
    nc = w.shape[1]
    tm = _row_tile(n, 1024)
    assert nc % tn == 0
    return pl.pallas_call(
        _mm_norm_kernel,
        grid=(n // tm, nc // tn),
        in_specs=[pl.BlockSpec((tm, k), lambda i, j: (i, 0)),
                  pl.BlockSpec((1, k), lambda i, j: (0, 0)),
                  pl.BlockSpec((k, tn), lambda i, j: (0, j))],
        out_specs=pl.BlockSpec((tm, tn), lambda i, j: (i, j)),
        out_shape=jax.ShapeDtypeStruct((n, nc), F32),
        scratch_shapes=[pltpu.VMEM((tm, k), BF16)],
        compiler_params=_params("parallel", "arbitrary"),
        name="mm_norm",
    )(x, gain.reshape(1, k), w)


def _mm_res_kernel(x_ref, w_ref, h_ref, o_ref):
    o_ref[...] = h_ref[...] + jnp.dot(x_ref[...], w_ref[...], preferred_element_type=F32)


def mm_res(x, w, h):
    n, k = x.shape
    nc = w.shape[1]
    tm = _row_tile(n, 1024)
    tn = 1024
    return pl.pallas_call(
        _mm_res_kernel,
        grid=(n // tm, nc // tn),
        in_specs=[pl.BlockSpec((tm, k), lambda i, j: (i, 0)),
                  pl.BlockSpec((k, tn), lambda i, j: (0, j)),
                  pl.BlockSpec((tm, tn), lambda i, j: (i, j))],
        out_specs=pl.BlockSpec((tm, tn), lambda i, j: (i, j)),
        out_shape=jax.ShapeDtypeStruct((n, nc), F32),
        compiler_params=_params("parallel", "parallel"),
        name="mm_res",
    )(x, w, h)


def _mm_gated3_kernel(oa_ref, ob_ref, oc_ref, wa_ref, wb_ref, wc_ref, ga_ref, gb_ref, gc_ref, o_ref):
    ya = jnp.dot(oa_ref[...], wa_ref[...], preferred_element_type=F32)
    yb = jnp.dot(ob_ref[...], wb_ref[...], preferred_element_type=F32)
    yc = jnp.dot(oc_ref[...], wc_ref[...], preferred_element_type=F32)
    m = (jax.nn.sigmoid(ga_ref[...]) * ya + jax.nn.sigmoid(gb_ref[...]) * yb
         + jax.nn.sigmoid(gc_ref[...]) * yc)
    o_ref[...] = m.astype(o_ref.dtype)


def mm_gated3(oa, ob, oc, wa, wb, wc, gates):
    n = oa.shape[0]
    tm = _row_tile(n, 512)
    tn = 512
    nj = D_MODEL // tn
    row = lambda w: pl.BlockSpec((tm, w), lambda i, j: (i, 0))
    col = lambda k: pl.BlockSpec((k, tn), lambda i, j: (0, j))
    gate = lambda s: pl.BlockSpec((tm, tn), lambda i, j: (i, j + s * nj))
    return pl.pallas_call(
        _mm_gated3_kernel,
        grid=(n // tm, nj),
        in_specs=[row(A_W), row(B_W), row(C_W), col(A_W), col(B_W), col(C_W), gate(0), gate(1), gate(2)],
        out_specs=pl.BlockSpec((tm, tn), lambda i, j: (i, j)),
        out_shape=jax.ShapeDtypeStruct((n, D_MODEL), BF16),
        compiler_params=_params("parallel", "parallel"),
        name="mm_gated3",
    )(oa, ob, oc, wa, wb, wc, gates, gates, gates)


def _ffn_kernel(h_ref, g_ref, wg_ref, wu_ref, wo_ref, o_ref, xn_ref, acc_ref):
    f = pl.program_id(1)

    @pl.when(f == 0)
    def _():
        xn_ref[...] = _rms(h_ref[...], g_ref[...]).astype(BF16)
        acc_ref[...] = jnp.zeros_like(acc_ref)

    xn = xn_ref[...]
    gate = jnp.dot(xn, wg_ref[...], preferred_element_type=F32)
    up = jnp.dot(xn, wu_ref[...], preferred_element_type=F32)
    act = (jax.nn.silu(gate) * up).astype(BF16)
    acc_ref[...] += jnp.dot(act, wo_ref[...], preferred_element_type=F32)

    @pl.when(f == pl.num_programs(1) - 1)
    def _():
        o_ref[...] = h_ref[...] + acc_ref[...]


def ffn(h, gain, w_in, w_out):
    n, d = h.shape
    dff = w_out.shape[0]
    tm = _row_tile(n, 512)
    tf = 512
    assert dff % tf == 0
    nf = dff // tf
    return pl.pallas_call(
        _ffn_kernel,
        grid=(n // tm, nf),
        in_specs=[pl.BlockSpec((tm, d), lambda i, f: (i, 0)),
                  pl.BlockSpec((1, d), lambda i, f: (0, 0)),
                  pl.BlockSpec((d, tf), lambda i, f: (0, f)),
                  pl.BlockSpec((d, tf), lambda i, f: (0, f + nf)),
                  pl.BlockSpec((tf, d), lambda i, f: (f, 0))],
        out_specs=pl.BlockSpec((tm, d), lambda i, f: (i, 0)),
        out_shape=jax.ShapeDtypeStruct((n, d), F32),
        scratch_shapes=[pltpu.VMEM((tm, d), BF16), pltpu.VMEM((tm, d), F32)],
        compiler_params=_params("parallel", "arbitrary"),
        name="ffn",
    )(h, gain.reshape(1, d), w_in, w_in, w_out)


def _ple_kernel(h_ref, g_ref, wg_ref, p_ref, wp_ref, hj_ref, o_ref, xn_ref):
    @pl.when(pl.program_id(1) == 0)
    def _():
        xn_ref[...] = _rms(h_ref[...], g_ref[...]).astype(BF16)

    gate = jnp.dot(xn_ref[...], wg_ref[...], preferred_element_type=F32)
    emb = jnp.dot(p_ref[...], wp_ref[...], preferred_element_type=F32)
    o_ref[...] = hj_ref[...] + jax.nn.sigmoid(gate) * emb


def ple(h, gain, w_gate, p, w_proj):
    n, d = h.shape
    pd = p.shape[1]
    tm = _row_tile(n, 1024)
    tn = 512
    return pl.pallas_call(
        _ple_kernel,
        grid=(n // tm, d // tn),
        in_specs=[pl.BlockSpec((tm, d), lambda i, j: (i, 0)),
                  pl.BlockSpec((1, d), lambda i, j: (0, 0)),
                  pl.BlockSpec((d, tn), lambda i, j: (0, j)),
                  pl.BlockSpec((tm, pd), lambda i, j: (i, 0)),
                  pl.BlockSpec((pd, tn), lambda i, j: (0, j)),
                  pl.BlockSpec((tm, tn), lambda i, j: (i, j))],
        out_specs=pl.BlockSpec((tm, tn), lambda i, j: (i, j)),
        out_shape=jax.ShapeDtypeStruct((n, d), F32),
        scratch_shapes=[pltpu.VMEM((tm, d), BF16)],
        compiler_params=_params("parallel", "arbitrary"),
        name="ple",
    )(h, gain.reshape(1, d), w_gate, p, w_proj, h)


def _rope_tables(pos, width, period, rot, scale_lanes=None):
    half = rot // 2
    inv = ROPE_THETA ** (-jnp.arange(half, dtype=F32) / half)
    ang = pos.astype(F32)[:, None] * inv[None, :]
    cos, sin = jnp.cos(ang), jnp.sin(ang)
    lane = jnp.arange(width) % period
    first, second = lane < half, (lane >= half) & (lane < rot)
    idx = lane % half
    c = jnp.where((first | second)[None, :], cos[:, idx], 1.0)
    s_up = jnp.where(first[None, :], -sin[:, idx], 0.0)
    s_dn = jnp.where(second[None, :], sin[:, idx], 0.0)
    if scale_lanes is not None:
        only, mult = scale_lanes
        c = jnp.where(only[None, :], c, mult[None, :])
        s_up = jnp.where(only[None, :], s_up, 0.0)
        s_dn = jnp.where(only[None, :], s_dn, 0.0)
    return jnp.stack([c, s_up, s_dn]).astype(F32)


def _rope(x, tab, half):
    return (x * tab[0] + pltpu.roll(x, x.shape[1] - half, 1) * tab[1] + pltpu.roll(x, half, 1) * tab[2])


def _post_in_kernel(z_ref, t128_ref, t64_ref, tki_ref, gaq_ref, gak_ref, gbq_ref, gbk_ref,
                    aq_ref, akf_ref, akb_ref, avf_ref, avb_ref, bq_ref, bkf_ref, bkb_ref,
                    bvf_ref, bvb_ref, qi_ref, kiw_ref, kilo_ref, kihi_ref):
    hd = HEAD_DIM
    t128 = t128_ref[...]
    t64 = t64_ref[...]
    half128 = hd // ROT_FRACTION // 2
    half64 = D_IDX // ROT_FRACTION // 2
    for h in range(H_A):
        sl = slice(h * hd, (h + 1) * hd)
        aq = _rms(z_ref[:, sl], gaq_ref[...]) * (-(hd ** -0.5))
        aq_ref[:, sl] = aq.astype(BF16)
        ak = _rms(z_ref[:, A_W + h * hd:A_W + (h + 1) * hd], gak_ref[...])
        akf_ref[:, sl] = ak
        akb_ref[:, sl] = ak.astype(BF16)
        av = z_ref[:, 2 * A_W + h * hd:2 * A_W + (h + 1) * hd]
        avf_ref[:, sl] = av
        avb_ref[:, sl] = av.astype(BF16)
    o = 3 * A_W
    for h in range(H_B):
        sl = slice(h * hd, (h + 1) * hd)
        bq = _rope(_rms(z_ref[:, o + h * hd:o + (h + 1) * hd], gbq_ref[...]), t128, half128) * (hd ** -0.5)
        bq_ref[:, sl] = bq.astype(BF16)
    o += B_W
    bk = _rope(_rms(z_ref[:, o:o + hd], gbk_ref[...]), t128, half128)
    bkf_ref[...] = bk
    bkb_ref[...] = bk.astype(BF16)
    o += hd
    bv = z_ref[:, o:o + hd]
    bvf_ref[...] = bv
    bvb_ref[...] = bv.astype(BF16)
    o += hd
    for c in range(H_IDX * D_IDX // 128):
        qi = _rope(z_ref[:, o + c * 128:o + (c + 1) * 128], t64, half64) * (D_IDX ** -0.5)
        qi_ref[:, c * 128:(c + 1) * 128] = qi.astype(BF16)
    o += H_IDX * D_IDX
    kiw = _rope(z_ref[:, o:o + 128], tki_ref[...], half64)
    kiw_ref[...] = kiw
    lane = lax.broadcasted_iota(jnp.int32, kiw.shape, 1)
    ki = jnp.where(lane < D_IDX, kiw, 0.0)
    kilo_ref[...] = ki.astype(BF16)
    kihi_ref[...] = pltpu.roll(ki, D_IDX, 1).astype(BF16)


def post_in(z, tabs, gains, t_len):
    n = z.shape[0]
    tm = _row_tile(t_len, 512)
    nt = t_len // tm
    rows = lambda w: pl.BlockSpec((tm, w), lambda i: (i, 0))
    tab = pl.BlockSpec((3, tm, 128), lambda i: (0, i % nt, 0))
    gain = pl.BlockSpec((1, HEAD_DIM), lambda i: (0, 0))
    widths = [(A_W, BF16), (A_W, F32), (A_W, BF16), (A_W, F32), (A_W, BF16), (B_W, BF16),
              (HEAD_DIM, F32), (HEAD_DIM, BF16), (HEAD_DIM, F32), (HEAD_DIM, BF16),
              (H_IDX * D_IDX, BF16), (128, F32), (128, BF16), (128, BF16)]
    return pl.pallas_call(
        _post_in_kernel,
        grid=(n // tm,),
        in_specs=[rows(ATT_W), tab, tab, tab, gain, gain, gain, gain],
        out_specs=[rows(w) for w, _ in widths],
        out_shape=[jax.ShapeDtypeStruct((n, w), dt) for w, dt in widths],
        compiler_params=_params("parallel"),
        name="post_in",
    )(z, *tabs, *[g.reshape(1, HEAD_DIM) for g in gains])


LOG2E = 1.4426950408889634
ATTN_A_ROW_CHUNK = 256
ATTN_A_BLOCKS_PER_STEP = 4


def _attn_a_kernel(q_ref, k_ref, v_ref, tri_ref, o_ref, *, tq, tk, rc, past):
    i = pl.program_id(2)
    q0 = past + i * tq
    tri = tri_ref[...]
    n_full = q0 // tk
    n_all = (q0 + tq - 1 + tk - 1) // tk
    nrc = tq // rc
    qs = [q_ref[c * rc:(c + 1) * rc, :] for c in range(nrc)]
    rows = [q0 + c * rc + lax.broadcasted_iota(jnp.int32, (rc, tk), 0) for c in range(nrc)]
    col = lax.broadcasted_iota(jnp.int32, (rc, tk), 1)

    def step(j, carry, masked, nsub):
        cs = range(nrc)
        ss = range(nsub)
        offs = [pl.multiple_of((j - s) * tk, tk) for s in ss]
        kb = [k_ref[pl.ds(o, tk), :] for o in offs]
        vb = [v_ref[pl.ds(o, tk), :] for o in offs]
        later = [carry[0][c * rc:(c + 1) * rc] for c in cs]
        acc = [carry[1][c * rc:(c + 1) * rc] for c in cs]
        nz = [[_nt_dot(qs[c], kb[s]) for c in cs] for s in ss]
        loss = [[jnp.log(1.0 + jnp.exp2(jnp.minimum(nz[s][c] * (-LOG2E), 126.0))) for c in cs] for s in ss]
        if masked:
            causal = [[(offs[s] + col) < rows[c] for c in cs] for s in ss]
            loss = [[jnp.where(causal[s][c], loss[s][c], 0.0) for c in cs] for s in ss]
        within = [[jnp.dot(loss[s][c].astype(BF16), tri, preferred_element_type=F32) for c in cs] for s in ss]
        for s in ss:
            w = [jnp.exp2((nz[s][c] + later[c] + within[s][c]) * (-LOG2E)) for c in cs]
            if masked:
                w = [jnp.where(causal[s][c], w[c], 0.0) for c in cs]
            acc = [acc[c] + jnp.dot(w[c].astype(BF16), vb[s], preferred_element_type=F32) for c in cs]
            later = [later[c] + within[s][c][:, 0:1] for c in cs]
        return jnp.concatenate(later, axis=0), jnp.concatenate(acc, axis=0)

    carry = (jnp.zeros((tq, 1), F32), jnp.zeros((tq, HEAD_DIM), F32))
    carry = lax.fori_loop(0, n_all - n_full, lambda t, c: step(n_all - 1 - t, c, True, 1), carry)
    n_multi = n_full // ATTN_A_BLOCKS_PER_STEP
    n_single = n_full - n_multi * ATTN_A_BLOCKS_PER_STEP
    carry = lax.fori_loop(0, n_single, lambda t, c: step(n_full - 1 - t, c, False, 1), carry)
    carry = lax.fori_loop(
        0, n_multi,
        lambda t, c: step(n_full - n_single - 1 - t * ATTN_A_BLOCKS_PER_STEP, c, False, ATTN_A_BLOCKS_PER_STEP),
        carry)
    o_ref[...] = carry[1].astype(o_ref.dtype)


def attn_a(q, k, v, past, tk):
    b, t, _ = q.shape
    lp = k.shape[1]
    tq = _row_tile(t, 512)
    rc = min(tq, ATTN_A_ROW_CHUNK)
    tri = jnp.tri(tk, dtype=BF16)
    return pl.pallas_call(
        functools.partial(_attn_a_kernel, tq=tq, tk=tk, rc=rc, past=past),
        grid=(b, H_A, t // tq),
        in_specs=[pl.BlockSpec((None, tq, HEAD_DIM), lambda bi, h, i: (bi, i, h)),
                  pl.BlockSpec((None, lp, HEAD_DIM), lambda bi, h, i: (bi, 0, h)),
                  pl.BlockSpec((None, lp, HEAD_DIM), lambda bi, h, i: (bi, 0, h)),
                  pl.BlockSpec((tk, tk), lambda bi, h, i: (0, 0))],
        out_specs=pl.BlockSpec((None, tq, HEAD_DIM), lambda bi, h, i: (bi, i, h)),
        out_shape=jax.ShapeDtypeStruct((b, t, A_W), BF16),
        compiler_params=_params("parallel", "parallel", "parallel"),
        name="attn_a",
    )(q, k, v, tri)


def _dsa_kernel(qi_ref, kiw_ref, kilo_ref, kihi_ref, q_ref, k_ref, v_ref, triu_ref, o_ref, key_ref,
                *, tq, tk, past, n_keys, topk):
    i = pl.program_id(1)
    q0 = past + i * tq
    pos = q0 + lax.broadcasted_iota(jnp.int32, (tq, 1), 0)
    lim = jnp.minimum((pos // CHUNK + 1) * CHUNK, n_keys)
    n_adm = jnp.minimum(((q0 + tq - 1) // CHUNK + 1) * CHUNK, n_keys)
    nt = (n_adm + tk - 1) // tk
    col = lax.broadcasted_iota(jnp.int32, (tq, tk), 1)
    ngrp = tk // 128

    qi = qi_ref[...]
    kiw = kiw_ref[...]
    wi = [kiw[:, D_IDX + h:D_IDX + h + 1] for h in range(H_IDX)]

    def score_tile(t, _):
        off = pl.multiple_of(t * tk, tk)
        kis = (kilo_ref[pl.ds(off, tk), :], kihi_ref[pl.ds(off, tk), :])
        s = jnp.zeros((tq, tk), F32)
        for h in range(H_IDX):
            s_idx = _nt_dot(qi[:, (h // 2) * 128:(h // 2 + 1) * 128], kis[h % 2])
            s = s + wi[h] * jnp.maximum(s_idx, 0.0)
        s = jnp.where(s == 0.0, 0.0, s)
        s = jnp.where(off + col < lim, s, -jnp.inf)
        bits = lax.bitcast_convert_type(s, jnp.int32)
        key_ref[:, pl.ds(off, tk)] = bits ^ ((bits >> 31) & 0x7FFFFFFF)
        return 0

    lax.fori_loop(0, nt, score_tile, 0)

    def count_ge(cand):
        cand_b = jnp.broadcast_to(cand, (tq, 128))

        def body(t, acc):
            off = pl.multiple_of(t * tk, tk)
            kt = key_ref[:, pl.ds(off, tk)]
            for g in range(ngrp):
                acc = acc + jnp.where(kt[:, g * 128:(g + 1) * 128] >= cand_b, 1, 0)
            return acc

        acc = lax.fori_loop(0, nt, body, jnp.zeros((tq, 128), jnp.int32))
        return jnp.sum(acc, axis=1, keepdims=True)

    def bit_step(b, t_u):
        cand_u = t_u | lax.shift_left(jnp.int32(1), 31 - b)
        ok = count_ge(cand_u ^ INT_MIN) >= topk
        return jnp.where(ok, cand_u, t_u)

    t_u = lax.fori_loop(0, 32, bit_step, jnp.zeros((tq, 1), jnp.int32))
    tau = t_u ^ INT_MIN
    tau_next = jnp.where(tau == 2 ** 31 - 1, tau, tau + 1)
    cnt_gt = jnp.where(tau == 2 ** 31 - 1, 0, count_ge(tau_next))
    need = (topk - cnt_gt).astype(F32)

    triu = triu_ref[...]

    def select_tile(t, seen):
        off = pl.multiple_of(t * tk, tk)
        kt = key_ref[:, pl.ds(off, tk)]
        eq = kt == tau
        prefix = seen + jnp.dot(jnp.where(eq, 1.0, 0.0).astype(BF16), triu, preferred_element_type=F32)
        sel = (kt > tau) | (eq & (prefix <= need))
        sel = sel & (kt > KEY_NEG_INF)
        bias = jnp.where(sel, 0.0, NEG_BIG).astype(F32)
        key_ref[:, pl.ds(off, tk)] = lax.bitcast_convert_type(bias, jnp.int32)
        return prefix[:, tk - 1:tk]

    lax.fori_loop(0, nt, select_tile, jnp.zeros((tq, 1), F32))

    qs = jnp.concatenate([q_ref[:, h * HEAD_DIM:(h + 1) * HEAD_DIM] for h in range(H_B)], axis=0)

    def attend_tile(t, carry):
        m, l, acc = carry
        off = pl.multiple_of(t * tk, tk)
        kb = k_ref[pl.ds(off, tk), :]
        vb = v_ref[pl.ds(off, tk), :]
        bias = lax.bitcast_convert_type(key_ref[:, pl.ds(off, tk)], F32)
        logits = _nt_dot(qs, kb).reshape(H_B, tq, tk) + bias[None]
        logits = logits.reshape(H_B * tq, tk)
        m_new = jnp.maximum(m, jnp.max(logits, axis=1, keepdims=True))
        alpha = jnp.exp(m - m_new)
        p = jnp.exp(logits - m_new)
        l = alpha * l + jnp.sum(p, axis=1, keepdims=True)
        acc = alpha * acc + jnp.dot(p.astype(BF16), vb, preferred_element_type=F32)
        return m_new, l, acc

    init = (jnp.full((H_B * tq, 1), NEG_BIG, F32), jnp.zeros((H_B * tq, 1), F32),
            jnp.zeros((H_B * tq, HEAD_DIM), F32))
    _, l, acc = lax.fori_loop(0, nt, attend_tile, init)
    out = acc / l
    for h in range(H_B):
        o_ref[:, h * HEAD_DIM:(h + 1) * HEAD_DIM] = out[h * tq:(h + 1) * tq].astype(o_ref.dtype)


def dsa(qi, kiw, kilo, kihi, q, k, v, past, n_keys, tk):
    b, t, _ = q.shape
    lp = k.shape[1]
    tq = _row_tile(t, 128)
    topk = min(TOPK_MAX, n_keys // 4)
    assert lp % tk == 0 and lp >= topk
    triu = jnp.tri(tk, dtype=BF16).T
    qblk = lambda w: pl.BlockSpec((None, tq, w), lambda bi, i: (bi, i, 0))
    kblk = lambda w: pl.BlockSpec((None, lp, w), lambda bi, i: (bi, 0, 0))
    return pl.pallas_call(
        functools.partial(_dsa_kernel, tq=tq, tk=tk, past=past, n_keys=n_keys, topk=topk),
        grid=(b, t // tq),
        in_specs=[qblk(H_IDX * D_IDX), qblk(128), kblk(128), kblk(128), qblk(B_W),
                  kblk(HEAD_DIM), kblk(HEAD_DIM), pl.BlockSpec((tk, tk), lambda bi, i: (0, 0))],
        out_specs=qblk(B_W),
        out_shape=jax.ShapeDtypeStruct((b, t, B_W), BF16),
        scratch_shapes=[pltpu.VMEM((tq, lp), jnp.int32)],
        compiler_params=_params("parallel", "parallel"),
        name="dsa",
    )(qi, kiw, kilo, kihi, q, k, v, triu)


def _softplus(x):
    return jnp.maximum(x, 0.0) + jnp.log(1.0 + jnp.exp(-jnp.abs(x)))


def _c_prep_kernel(*refs, first_layer):
    if first_layer:
        (cz_ref, prev_ref, mu_ref, w0_ref, a0_ref, w2_ref, a2_ref, g2_ref, kk_ref, ka_ref,
         r_ref, kn_ref, k_ref, v_ref, lw_ref, a_ref, g_ref) = refs
    else:
        (cz_ref, prev_ref, mu_ref, w0_ref, a0_ref, w2_ref, a2_ref, g2_ref, kk_ref, ka_ref,
         vf_ref, v0_ref, v1_ref, v2_ref,
         r_ref, kn_ref, k_ref, v_ref, lw_ref, a_ref, g_ref) = refs
    cz = cz_ref[...]
    rowid = lax.broadcasted_iota(jnp.int32, cz.shape, 0)
    prev = jnp.where(rowid == 0, prev_ref[...], pltpu.roll(cz, 1, 0))
    zz = cz + (prev - cz) * mu_ref[...]
    r = zz[:, 0:C_W]
    k = zz[:, C_W:2 * C_W]
    v = zz[:, 2 * C_W:3 * C_W]
    tail = zz[:, 3 * C_W:C_PAD]
    w = w0_ref[...] + jnp.dot(jnp.tanh(tail).astype(BF16), w2_ref[...], preferred_element_type=F32)
    w = -_softplus(-w) - 0.5
    a = jax.nn.sigmoid(a0_ref[...] + jnp.dot(tail.astype(BF16), a2_ref[...], preferred_element_type=F32))
    g = jnp.dot(jax.nn.sigmoid(tail).astype(BF16), g2_ref[...], preferred_element_type=F32)
    if not first_layer:
        lora = jnp.dot(v.astype(BF16), v1_ref[...], preferred_element_type=F32)
        lora = jnp.dot(lora.astype(BF16), v2_ref[...], preferred_element_type=F32)
        v = v + (vf_ref[...] - v) * jax.nn.sigmoid(v0_ref[...] + lora)
    r_ref[...] = r
    kn_ref[...] = k * kk_ref[...]
    k_ref[...] = k * (1.0 + (a - 1.0) * ka_ref[...])
    v_ref[...] = v
    lw_ref[...] = -jnp.exp(w)
    a_ref[...] = a
    g_ref[...] = g


def c_prep(cz, prev_rows, vecs, mats, v_first, vres, tm):
    b, t, _ = cz.shape
    first_layer = vres is None
    blk = lambda w: pl.BlockSpec((None, tm, w), lambda bi, i: (bi, i, 0))
    vec = lambda w: pl.BlockSpec((1, w), lambda bi, i: (0, 0))
    mat = lambda m: pl.BlockSpec(m.shape, lambda bi, i: (0, 0))
    mu, w0, a0, k_k, k_a = vecs
    w2, a2, g2 = mats
    ins = [cz, prev_rows, mu, w0, a0, w2, a2, g2, k_k, k_a]
    specs = [blk(C_PAD), pl.BlockSpec((None, None, 1, C_PAD), lambda bi, i: (bi, i, 0, 0)),
             vec(C_PAD), vec(C_W), vec(C_W), mat(w2), mat(a2), mat(g2), vec(C_W), vec(C_W)]
    if not first_layer:
        v0, v1, v2 = vres
        ins += [v_first, v0, v1, v2]
        specs += [blk(C_W), vec(C_W), mat(v1), mat(v2)]
    return pl.pallas_call(
        functools.partial(_c_prep_kernel, first_layer=first_layer),
        grid=(b, t // tm),
        in_specs=specs,
        out_specs=[blk(C_W)] * 7,
        out_shape=[jax.ShapeDtypeStruct((b, t, C_W), F32)] * 7,
        compiler_params=_params("parallel", "parallel"),
        name="c_prep",
    )(*ins)


def _bdot(a, b, contract_a, contract_b):
    return lax.dot_general(a.astype(BF16), b.astype(BF16), (((contract_a,), (contract_b,)), ((0,), (0,))),
                           preferred_element_type=F32)


def _bdot3(a, b, contract_a, contract_b):
    ah = a.astype(BF16)
    bh = b.astype(BF16)
    al = (a - ah.astype(F32)).astype(BF16)
    bl = (b - bh.astype(F32)).astype(BF16)
    dn = (((contract_a,), (contract_b,)), ((0,), (0,)))
    dot = lambda x, y: lax.dot_general(x, y, dn, preferred_element_type=F32)
    return dot(ah, bh) + (dot(al, bh) + dot(ah, bl))


def _c_scan_kernel(r_ref, kn_ref, k_ref, v_ref, lw_ref, a_ref, g_ref, s0_ref, rk_ref, lng_ref, lnb_ref,
                   y_ref, st_ref, s_ref, *, nck):
    c_len = SCAN_CHUNK

    @pl.when(pl.program_id(1) == 0)
    def _():
        s_ref[...] = s0_ref[...]

    ti = lax.broadcasted_iota(jnp.int32, (H_C, c_len, c_len), 1)
    si = lax.broadcasted_iota(jnp.int32, (H_C, c_len, c_len), 2)
    incl = si <= ti
    strict = si < ti
    tri = jnp.where(incl, 1.0, 0.0)
    eye = jnp.where(si == ti, 1.0, 0.0)
    r_k = rk_ref[...]
    ln_g = lng_ref[...]
    ln_b = lnb_ref[...]

    def chunk(c, _):
        rows = pl.ds(pl.multiple_of(c * c_len, c_len), c_len)
        r = r_ref[:, rows, :]
        k = k_ref[:, rows, :]
        v = v_ref[:, rows, :]
        lw = lw_ref[:, rows, :]
        a = a_ref[:, rows, :]
        kn = kn_ref[:, rows, :]
        kn = kn * lax.rsqrt(jnp.maximum(jnp.sum(kn * kn, axis=-1, keepdims=True), 1e-24))
        aa = -kn
        bb = kn * a
        g = _bdot3(tri, lw, 2, 1)
        g_end = g[:, c_len - 1:c_len, :]
        a_t = aa * jnp.exp(g - lw)
        r_t = r * jnp.exp(g)
        inv = jnp.exp(-g)
        b_h = bb * inv
        k_h = k * inv
        to_end = jnp.exp(g_end - g)
        b_e = bb * to_end
        k_e = k * to_end
        d_end = jnp.exp(g_end)
        a_ab = jnp.where(strict, _bdot(a_t, b_h, 2, 2), 0.0)
        a_ak = jnp.where(strict, _bdot(a_t, k_h, 2, 2), 0.0)
        a_rb = jnp.where(incl, _bdot(r_t, b_h, 2, 2), 0.0)
        a_rk = jnp.where(incl, _bdot(r_t, k_h, 2, 2), 0.0)
        t_inv = eye + a_ab
        pw = a_ab
        for _ in range(4):
            pw = _bdot3(pw, pw, 2, 1)
            t_inv = t_inv + _bdot3(t_inv, pw, 2, 1)
        p_mat = _bdot3(t_inv, a_t, 2, 1)
        q_mat = _bdot3(t_inv, _bdot(a_ak, v, 2, 1), 2, 1)
        y_v = _bdot(a_rk, v, 2, 1)
        s = s_ref[...]
        u = _bdot(p_mat, s, 2, 2) + q_mat
        y = _bdot(r_t, s, 2, 2) + _bdot(a_rb, u, 2, 1) + y_v
        uv = jnp.concatenate([u, v], axis=1)
        bk = jnp.concatenate([b_e, k_e], axis=1)
        s_ref[...] = s * d_end + _bdot(uv, bk, 1, 1)
        mu = jnp.mean(y, axis=-1, keepdims=True)
        yc = y - mu
        yn = yc * lax.rsqrt(jnp.mean(yc * yc, axis=-1, keepdims=True) + LNX_EPS)
        yn = yn * ln_g + ln_b
        bonus = jnp.sum(r * k * r_k, axis=-1, keepdims=True) * v
        y_ref[:, rows, :] = (yn + bonus) * g_ref[:, rows, :]
        return 0

    lax.fori_loop(0, nck, chunk, 0)

    @pl.when(pl.program_id(1) == pl.num_programs(1) - 1)
    def _():
        st_ref[...] = s_ref[...]


def c_scan(r, kn, k, v, lw, a, g, s0, r_k, ln_g, ln_b, tm):
    b, _, t, _ = r.shape
    blk = pl.BlockSpec((None, H_C, tm, N_C), lambda bi, i: (bi, 0, i, 0))
    st = pl.BlockSpec((None, H_C, N_C, N_C), lambda bi, i: (bi, 0, 0, 0))
    vec = pl.BlockSpec((H_C, 1, N_C), lambda bi, i: (0, 0, 0))
    return pl.pallas_call(
        functools.partial(_c_scan_kernel, nck=tm // SCAN_CHUNK),
        grid=(b, t // tm),
        in_specs=[blk] * 7 + [st, vec, vec, vec],
        out_specs=[blk, st],
        out_shape=[jax.ShapeDtypeStruct((b, H_C, t, N_C), F32),
                   jax.ShapeDtypeStruct((b, H_C, N_C, N_C), F32)],
        scratch_shapes=[pltpu.VMEM((H_C, N_C, N_C), F32)],
        compiler_params=_params("parallel", "arbitrary"),
        name="c_scan",
    )(r, kn, k, v, lw, a, g, s0, r_k.reshape(H_C, 1, N_C), ln_g.reshape(H_C, 1, N_C), ln_b.reshape(H_C, 1, N_C))


def _pad_rows(x, rows, axis=1):
    pad = rows - x.shape[axis]
    if pad == 0:
        return x
    widths = [(0, 0)] * x.ndim
    widths[axis] = (0, pad)
    return jnp.pad(x, widths)


def _prep_weights(w):
    out = []
    for i in range(DEPTH):
        w_in = w["w_in"][i]
        o_c = 3 * A_W + B_W + 2 * HEAD_DIM + H_IDX * D_IDX + D_IDX + H_IDX
        o_g = o_c + C_COLS
        w_att = jnp.pad(w_in[:, :o_c], ((0, 0), (0, ATT_W - o_c)))
        w_c = jnp.pad(w_in[:, o_c:o_g], ((0, 0), (0, C_PAD - C_COLS)))
        w_g = w_in[:, o_g:]
        tail_rows = lambda m, off: jnp.pad(m, ((off, C_TAIL - off - m.shape[0]), (0, 0))).astype(BF16)
        lw = dict(
            w_att=w_att.astype(BF16), w_c=w_c.astype(BF16), w_g=w_g.astype(BF16),
            w2=tail_rows(w["c_w2"][i], 0),
            a2=tail_rows(w["c_a2"][i], D_DECAY_LORA),
            g2=tail_rows(w["c_g2"][i], D_DECAY_LORA + D_AAA_LORA),
            mu=jnp.pad(w["c_mu"][i], (0, C_PAD - C_COLS)).reshape(1, C_PAD),
            w_br_a=w["w_br_a"][i].astype(BF16), w_br_b=w["w_br_b"][i].astype(BF16),
            w_br_c=w["w_br_c"][i].astype(BF16), w_out=w["w_out"][i].astype(BF16),
            w_ffn_in=w["w_ffn_in"][i].astype(BF16), w_ffn_out=w["w_ffn_out"][i].astype(BF16),
            w_ple_gate=w["w_ple_gate"][i].astype(BF16), w_ple_proj=w["w_ple_proj"][i].astype(BF16),
        )
        if i > 0:
            lw["v1"] = w["c_v1"][i - 1].astype(BF16)
            lw["v2"] = w["c_v2"][i - 1].astype(BF16)
        out.append(lw)
    return out


def _run_trunk(x, p, past, caches, w, wl):
    b, t, _ = x.shape
    n = b * t
    n_keys = past + t
    pos = past + jnp.arange(t)
    lane = jnp.arange(128)
    tabs = (_rope_tables(pos, 128, HEAD_DIM, HEAD_DIM // ROT_FRACTION),
            _rope_tables(pos, 128, D_IDX, D_IDX // ROT_FRACTION),
            _rope_tables(pos, 128, 128, D_IDX // ROT_FRACTION,
                         scale_lanes=(lane < D_IDX, jnp.where(lane < D_IDX + H_IDX, H_IDX ** -0.5, 1.0))))
    tk_a = 256
    tk_b = 512
    lp_a = -(-n_keys // tk_a) * tk_a
    lp_b = -(-n_keys // tk_b) * tk_b
    t_c = -(-t // SCAN_CHUNK) * SCAN_CHUNK
    tm_c = _row_tile(t, 256)
    tm_s = _row_tile(t_c, 256)
    h = x.reshape(n, D_MODEL)
    outs = ([], [], [], [], [], [], [])
    v_first = None
    for i in range(DEPTH):
        lw = wl[i]
        z_att = mm_norm(h, w["norm_mix"][i], lw["w_att"], ATT_W // 3)
        z_c = mm_norm(h, w["norm_mix"][i], lw["w_c"], C_PAD // 4)
        z_g = mm_norm(h, w["norm_mix"][i], lw["w_g"], 1024)
        (aq, ak_f, ak_b, av_f, av_b, bq, bk_f, bk_b, bv_f, bv_b, qi, kiw, kilo, kihi) = post_in(
            z_att, tabs, (w["a_q_norm"][i], w["a_k_norm"][i], w["b_q_norm"][i], w["b_k_norm"][i]), t)
        seq = lambda a: a.reshape(b, t, a.shape[-1])

        def keys(new, old, lp):
            new = seq(new)
            if old is not None:
                new = jnp.concatenate([old.astype(BF16), new], axis=1)
            return _pad_rows(new, lp)

        c = caches
        past_of = lambda name: None if c is None else c[name][i]
        pa_k = None if c is None else c["a_k"][i].reshape(b, past, A_W)
        pa_v = None if c is None else c["a_v"][i].reshape(b, past, A_W)
        o_a = attn_a(seq(aq), keys(ak_b, pa_k, lp_a), keys(av_b, pa_v, lp_a), past, tk_a)
        if c is None:
            p_lo = p_hi = None
        else:
            p_lo = jnp.pad(c["b_ki"][i], ((0, 0), (0, 0), (0, 128 - D_IDX)))
            p_hi = jnp.pad(c["b_ki"][i], ((0, 0), (0, 0), (128 - D_IDX, 0)))
        o_b = dsa(seq(qi), seq(kiw), keys(kilo, p_lo, lp_b), keys(kihi, p_hi, lp_b), seq(bq),
                  keys(bk_b, past_of("b_k"), lp_b), keys(bv_b, past_of("b_v"), lp_b), past, n_keys, tk_b)
        cz = seq(z_c)
        shift0 = jnp.zeros((b, C_COLS), F32) if c is None else c["shift"][i]
        shift0 = jnp.pad(shift0, ((0, 0), (0, C_PAD - C_COLS)))
        prev_rows = jnp.concatenate([shift0[:, None, :], cz[:, tm_c - 1:t - 1:tm_c, :]], axis=1)
        prev_rows = prev_rows.reshape(b, t // tm_c, 1, C_PAD)
        row = lambda a: a.reshape(1, C_W)
        vres = None if i == 0 else (row(w["c_v0"][i - 1]), lw["v1"], lw["v2"])
        prep = c_prep(cz, prev_rows,
                      (lw["mu"], row(w["c_w0"][i]), row(w["c_a0"][i]), row(w["c_k_k"][i]), row(w["c_k_a"][i])),
                      (lw["w2"], lw["a2"], lw["g2"]), v_first, vres, tm_c)
        if i == 0:
            v_first = prep[3]
        heads = lambda a: _pad_rows(a.reshape(b, t, H_C, N_C).transpose(0, 2, 1, 3), t_c, axis=2)
        s0 = jnp.zeros((b, H_C, N_C, N_C), F32) if c is None else c["wkv"][i]
        y_c, wkv_t = c_scan(*[heads(a) for a in prep], s0, w["c_r_k"][i], w["c_ln_g"][i], w["c_ln_b"][i], tm_s)
        o_c = y_c[:, :, :t].transpose(0, 2, 1, 3).reshape(n, C_W).astype(BF16)
        merged = mm_gated3(o_a.reshape(n, A_W), o_b.reshape(n, B_W), o_c,
                           lw["w_br_a"], lw["w_br_b"], lw["w_br_c"], z_g)
        h = mm_res(merged, lw["w_out"], h)
        h = ffn(h, w["norm_ffn"][i], lw["w_ffn_in"], lw["w_ffn_out"])
        h = ple(h, w["norm_ple"][i], lw["w_ple_gate"], p[i].reshape(n, -1).astype(BF16), lw["w_ple_proj"])
        vals = (ak_f.reshape(b, t, H_A, HEAD_DIM), av_f.reshape(b, t, H_A, HEAD_DIM),
                seq(bk_f), seq(bv_f), seq(kiw)[:, :, :D_IDX], wkv_t, cz[:, -1, :C_COLS])
        for lst, val in zip(outs, vals):
            lst.append(val)
    return h.reshape(b, t, D_MODEL), [jnp.stack(l) for l in outs]


def kernel(x_prompt, x_sample, cache_a_k, cache_a_v, cache_b_k, cache_b_v, cache_b_kidx, state_c_wkv, state_c_shift, p_prompt, p_sample, norm_mix, w_in, a_q_norm, a_k_norm, b_q_norm, b_k_norm, c_mu, c_w0, c_w2, c_a0, c_a2, c_g2, c_v0, c_v1, c_v2, c_k_k, c_k_a, c_r_k, c_ln_g, c_ln_b, w_br_a, w_br_b, w_br_c, w_out, norm_ffn, w_ffn_in, w_ffn_out, norm_ple, w_ple_gate, w_ple_proj):
    w = dict(norm_mix=norm_mix, w_in=w_in, a_q_norm=a_q_norm, a_k_norm=a_k_norm, b_q_norm=b_q_norm,
             b_k_norm=b_k_norm, c_mu=c_mu, c_w0=c_w0, c_w2=c_w2, c_a0=c_a0, c_a2=c_a2, c_g2=c_g2,
             c_v0=c_v0, c_v1=c_v1, c_v2=c_v2, c_k_k=c_k_k, c_k_a=c_k_a, c_r_k=c_r_k, c_ln_g=c_ln_g,
             c_ln_b=c_ln_b, w_br_a=w_br_a, w_br_b=w_br_b, w_br_c=w_br_c, w_out=w_out, norm_ffn=norm_ffn,
             w_ffn_in=w_ffn_in, w_ffn_out=w_ffn_out, norm_ple=norm_ple, w_ple_gate=w_ple_gate,
             w_ple_proj=w_ple_proj)
    wl = _prep_weights(w)
    y_p, o_p = _run_trunk(x_prompt, p_prompt, 0, None, w, wl)
    caches = dict(a_k=cache_a_k, a_v=cache_a_v, b_k=cache_b_k, b_v=cache_b_v, b_ki=cache_b_kidx,
                  wkv=state_c_wkv, shift=state_c_shift)
    y_s, o_s = _run_trunk(x_sample, p_sample, cache_a_k.shape[2], caches, w, wl)
    return (y_p, y_s, *o_p, *o_s)
```

```python
import functools
import math

import jax
import jax.numpy as jnp
from jax import lax
from jax.experimental import pallas as pl
from jax.experimental.pallas import tpu as pltpu

F32 = jnp.float32
BF16 = jnp.bfloat16

D_MODEL = 2048
DEPTH = 2
HEAD_DIM = 128
H_A = 4
A_W = H_A * HEAD_DIM
H_B = 4
B_W = H_B * HEAD_DIM
H_IDX = 4
D_IDX = 64
TOPK_MAX = 256
CHUNK = 64
H_C = 16
N_C = 64
C_W = H_C * N_C
SCAN_CHUNK = 32
D_DECAY_LORA = 96
D_AAA_LORA = 96
D_GATE_LORA = 256
C_COLS = 3 * C_W + D_DECAY_LORA + D_AAA_LORA + D_GATE_LORA
C_PAD = 3584
C_TAIL = C_PAD - 3 * C_W
ATT_W = 3 * A_W + B_W + 2 * HEAD_DIM + H_IDX * D_IDX + 128
ROPE_THETA = 500000.0
ROT_FRACTION = 4
NORM_EPS = 1e-6
LNX_EPS = 64e-5
NEG_BIG = -1e30

VMEM_LIMIT = 56 * 1024 * 1024
INT_MIN = -(2 ** 31)
KEY_NEG_INF = (0xFF800000 ^ 0x7FFFFFFF) - 2 ** 32


def _params(*sem):
    return pltpu.CompilerParams(dimension_semantics=sem, vmem_limit_bytes=VMEM_LIMIT)


def _row_tile(n, pref):
    t = min(n, pref)
    assert n % t == 0, (n, t)
    return t


def _nt_dot(a, b):
    return lax.dot_general(a, b, (((1,), (1,)), ((), ())), preferred_element_type=F32)


def _rms(x, gain):
    xn = x * lax.rsqrt(jnp.mean(x * x, axis=-1, keepdims=True) + NORM_EPS)
    return xn * gain


def _mm_norm_kernel(x_ref, g_ref, w_ref, o_ref, xn_ref):
    @pl.when(pl.program_id(1) == 0)
    def _():
        xn_ref[...] = _rms(x_ref[...], g_ref[...]).astype(BF16)

    o_ref[...] = jnp.dot(xn_ref[...], w_ref[...], preferred_element_type=F32)


def mm_norm(x, gain, w, tn):
    n, k = x.shape
    nc = w.shape[1]
    tm = _row_tile(n, 1024)
    assert nc % tn == 0
    return pl.pallas_call(
        _mm_norm_kernel,
        grid=(n // tm, nc // tn),
        in_specs=[pl.BlockSpec((tm, k), lambda i, j: (i, 0)),
                  pl.BlockSpec((1, k), lambda i, j: (0, 0)),
                  pl.BlockSpec((k, tn), lambda i, j: (0, j))],
        out_specs=pl.BlockSpec((tm, tn), lambda i, j: (i, j)),
        out_shape=jax.ShapeDtypeStruct((n, nc), F32),
        scratch_shapes=[pltpu.VMEM((tm, k), BF16)],
        compiler_params=_params("parallel", "arbitrary"),
        name="mm_norm",
    )(x, gain.reshape(1, k), w)


def _mm_res_kernel(x_ref, w_ref, h_ref, o_ref):
    o_ref[...] = h_ref[...] + jnp.dot(x_ref[...], w_ref[...], preferred_element_type=F32)


def mm_res(x, w, h):
    n, k = x.shape
    nc = w.shape[1]
    tm = _row_tile(n, 1024)
    tn = 1024
    return pl.pallas_call(
        _mm_res_kernel,
        grid=(n // tm, nc // tn),
        in_specs=[pl.BlockSpec((tm, k), lambda i, j: (i, 0)),
                  pl.BlockSpec((k, tn), lambda i, j: (0, j)),
                  pl.BlockSpec((tm, tn), lambda i, j: (i, j))],
        out_specs=pl.BlockSpec((tm, tn), lambda i, j: (i, j)),
        out_shape=jax.ShapeDtypeStruct((n, nc), F32),
        compiler_params=_params("parallel", "parallel"),
        name="mm_res",
    )(x, w, h)


def _mm_gated3_kernel(oa_ref, ob_ref, oc_ref, wa_ref, wb_ref, wc_ref, ga_ref, gb_ref, gc_ref, o_ref):
    ya = jnp.dot(oa_ref[...], wa_ref[...], preferred_element_type=F32)
    yb = jnp.dot(ob_ref[...], wb_ref[...], preferred_element_type=F32)
    yc = jnp.dot(oc_ref[...], wc_ref[...], preferred_element_type=F32)
    m = (jax.nn.sigmoid(ga_ref[...]) * ya + jax.nn.sigmoid(gb_ref[...]) * yb
         + jax.nn.sigmoid(gc_ref[...]) * yc)
    o_ref[...] = m.astype(o_ref.dtype)


def mm_gated3(oa, ob, oc, wa, wb, wc, gates):
    n = oa.shape[0]
    tm = _row_tile(n, 512)
    tn = 512
    nj = D_MODEL // tn
    row = lambda w: pl.BlockSpec((tm, w), lambda i, j: (i, 0))
    col = lambda k: pl.BlockSpec((k, tn), lambda i, j: (0, j))
    gate = lambda s: pl.BlockSpec((tm, tn), lambda i, j: (i, j + s * nj))
    return pl.pallas_call(
        _mm_gated3_kernel,
        grid=(n // tm, nj),
        in_specs=[row(A_W), row(B_W), row(C_W), col(A_W), col(B_W), col(C_W), gate(0), gate(1), gate(2)],
        out_specs=pl.BlockSpec((tm, tn), lambda i, j: (i, j)),
        out_shape=jax.ShapeDtypeStruct((n, D_MODEL), BF16),
        compiler_params=_params("parallel", "parallel"),
        name="mm_gated3",
    )(oa, ob, oc, wa, wb, wc, gates, gates, gates)


def _ffn_kernel(h_ref, g_ref, wg_ref, wu_ref, wo_ref, o_ref, xn_ref, acc_ref):
    f = pl.program_id(1)

    @pl.when(f == 0)
    def _():
        xn_ref[...] = _rms(h_ref[...], g_ref[...]).astype(BF16)
        acc_ref[...] = jnp.zeros_like(acc_ref)

    xn = xn_ref[...]
    gate = jnp.dot(xn, wg_ref[...], preferred_element_type=F32)
    up = jnp.dot(xn, wu_ref[...], preferred_element_type=F32)
    act = (jax.nn.silu(gate) * up).astype(BF16)
    acc_ref[...] += jnp.dot(act, wo_ref[...], preferred_element_type=F32)

    @pl.when(f == pl.num_programs(1) - 1)
    def _():
        o_ref[...] = h_ref[...] + acc_ref[...]


def ffn(h, gain, w_in, w_out):
    n, d = h.shape
    dff = w_out.shape[0]
    tm = _row_tile(n, 512)
    tf = 512
    assert dff % tf == 0
    nf = dff // tf
    return pl.pallas_call(
        _ffn_kernel,
        grid=(n // tm, nf),
        in_specs=[pl.BlockSpec((tm, d), lambda i, f: (i, 0)),
                  pl.BlockSpec((1, d), lambda i, f: (0, 0)),
                  pl.BlockSpec((d, tf), lambda i, f: (0, f)),
                  pl.BlockSpec((d, tf), lambda i, f: (0, f + nf)),
                  pl.BlockSpec((tf, d), lambda i, f: (f, 0))],
        out_specs=pl.BlockSpec((tm, d), lambda i, f: (i, 0)),
        out_shape=jax.ShapeDtypeStruct((n, d), F32),
        scratch_shapes=[pltpu.VMEM((tm, d), BF16), pltpu.VMEM((tm, d), F32)],
        compiler_params=_params("parallel", "arbitrary"),
        name="ffn",
    )(h, gain.reshape(1, d), w_in, w_in, w_out)


def _ple_kernel(h_ref, g_ref, wg_ref, p_ref, wp_ref, hj_ref, o_ref, xn_ref):
    @pl.when(pl.program_id(1) == 0)
    def _():
        xn_ref[...] = _rms(h_ref[...], g_ref[...]).astype(BF16)

    gate = jnp.dot(xn_ref[...], wg_ref[...], preferred_element_type=F32)
    emb = jnp.dot(p_ref[...], wp_ref[...], preferred_element_type=F32)
    o_ref[...] = hj_ref[...] + jax.nn.sigmoid(gate) * emb


def ple(h, gain, w_gate, p, w_proj):
    n, d = h.shape
    pd = p.shape[1]
    tm = _row_tile(n, 1024)
    tn = 512
    return pl.pallas_call(
        _ple_kernel,
        grid=(n // tm, d // tn),
        in_specs=[pl.BlockSpec((tm, d), lambda i, j: (i, 0)),
                  pl.BlockSpec((1, d), lambda i, j: (0, 0)),
                  pl.BlockSpec((d, tn), lambda i, j: (0, j)),
                  pl.BlockSpec((tm, pd), lambda i, j: (i, 0)),
                  pl.BlockSpec((pd, tn), lambda i, j: (0, j)),
                  pl.BlockSpec((tm, tn), lambda i, j: (i, j))],
        out_specs=pl.BlockSpec((tm, tn), lambda i, j: (i, j)),
        out_shape=jax.ShapeDtypeStruct((n, d), F32),
        scratch_shapes=[pltpu.VMEM((tm, d), BF16)],
        compiler_params=_params("parallel", "arbitrary"),
        name="ple",
    )(h, gain.reshape(1, d), w_gate, p, w_proj, h)


def _rope_tables(pos, width, period, rot, scale_lanes=None):
    half = rot // 2
    inv = ROPE_THETA ** (-jnp.arange(half, dtype=F32) / half)
    ang = pos.astype(F32)[:, None] * inv[None, :]
    cos, sin = jnp.cos(ang), jnp.sin(ang)
    lane = jnp.arange(width) % period
    first, second = lane < half, (lane >= half) & (lane < rot)
    idx = lane % half
    c = jnp.where((first | second)[None, :], cos[:, idx], 1.0)
    s_up = jnp.where(first[None, :], -sin[:, idx], 0.0)
    s_dn = jnp.where(second[None, :], sin[:, idx], 0.0)
    if scale_lanes is not None:
        only, mult = scale_lanes
        c = jnp.where(only[None, :], c, mult[None, :])
        s_up = jnp.where(only[None, :], s_up, 0.0)
        s_dn = jnp.where(only[None, :], s_dn, 0.0)
    return jnp.stack([c, s_up, s_dn]).astype(F32)


def _rope(x, tab, half):
    return (x * tab[0] + pltpu.roll(x, x.shape[1] - half, 1) * tab[1] + pltpu.roll(x, half, 1) * tab[2])


def _post_in_kernel(z_ref, t128_ref, t64_ref, tki_ref, gaq_ref, gak_ref, gbq_ref, gbk_ref,
                    aq_ref, akf_ref, akb_ref, avf_ref, avb_ref, bq_ref, bkf_ref, bkb_ref,
                    bvf_ref, bvb_ref, qi_ref, kiw_ref, kilo_ref, kihi_ref):
    hd = HEAD_DIM
    t128 = t128_ref[...]
    t64 = t64_ref[...]
    half128 = hd // ROT_FRACTION // 2
    half64 = D_IDX // ROT_FRACTION // 2
    for h in range(H_A):
        sl = slice(h * hd, (h + 1) * hd)
        aq = _rms(z_ref[:, sl], gaq_ref[...]) * (-(hd ** -0.5))
        aq_ref[:, sl] = aq.astype(BF16)
        ak = _rms(z_ref[:, A_W + h * hd:A_W + (h + 1) * hd], gak_ref[...])
        akf_ref[:, sl] = ak
        akb_ref[:, sl] = ak.astype(BF16)
        av = z_ref[:, 2 * A_W + h * hd:2 * A_W + (h + 1) * hd]
        avf_ref[:, sl] = av
        avb_ref[:, sl] = av.astype(BF16)
    o = 3 * A_W
    for h in range(H_B):
        sl = slice(h * hd, (h + 1) * hd)
        bq = _rope(_rms(z_ref[:, o + h * hd:o + (h + 1) * hd], gbq_ref[...]), t128, half128) * (hd ** -0.5)
        bq_ref[:, sl] = bq.astype(BF16)
    o += B_W
    bk = _rope(_rms(z_ref[:, o:o + hd], gbk_ref[...]), t128, half128)
    bkf_ref[...] = bk
    bkb_ref[...] = bk.astype(BF16)
    o += hd
    bv = z_ref[:, o:o + hd]
    bvf_ref[...] = bv
    bvb_ref[...] = bv.astype(BF16)
    o += hd
    for c in range(H_IDX * D_IDX // 128):
        qi = _rope(z_ref[:, o + c * 128:o + (c + 1) * 128], t64, half64) * (D_IDX ** -0.5)
        qi_ref[:, c * 128:(c + 1) * 128] = qi.astype(BF16)
    o += H_IDX * D_IDX
    kiw = _rope(z_ref[:, o:o + 128], tki_ref[...], half64)
    kiw_ref[...] = kiw
    lane = lax.broadcasted_iota(jnp.int32, kiw.shape, 1)
    ki = jnp.where(lane < D_IDX, kiw, 0.0)
    kilo_ref[...] = ki.astype(BF16)
    kihi_ref[...] = pltpu.roll(ki, D_IDX, 1).astype(BF16)


def post_in(z, tabs, gains, t_len):
    n = z.shape[0]
    tm = _row_tile(t_len, 512)
    nt = t_len // tm
    rows = lambda w: pl.BlockSpec((tm, w), lambda i: (i, 0))
    tab = pl.BlockSpec((3, tm, 128), lambda i: (0, i % nt, 0))
    gain = pl.BlockSpec((1, HEAD_DIM), lambda i: (0, 0))
    widths = [(A_W, BF16), (A_W, F32), (A_W, BF16), (A_W, F32), (A_W, BF16), (B_W, BF16),
              (HEAD_DIM, F32), (HEAD_DIM, BF16), (HEAD_DIM, F32), (HEAD_DIM, BF16),
              (H_IDX * D_IDX, BF16), (128, F32), (128, BF16), (128, BF16)]
    return pl.pallas_call(
        _post_in_kernel,
        grid=(n // tm,),
        in_specs=[rows(ATT_W), tab, tab, tab, gain, gain, gain, gain],
        out_specs=[rows(w) for w, _ in widths],
        out_shape=[jax.ShapeDtypeStruct((n, w), dt) for w, dt in widths],
        compiler_params=_params("parallel"),
        name="post_in",
    )(z, *tabs, *[g.reshape(1, HEAD_DIM) for g in gains])


LOG2E = 1.4426950408889634
ATTN_A_ROW_CHUNK = 256
ATTN_A_BLOCKS_PER_STEP = 4


def _attn_a_kernel(q_ref, k_ref, v_ref, tri_ref, o_ref, *, tq, tk, rc, past):
    i = pl.program_id(2)
    q0 = past + i * tq
    tri = tri_ref[...]
    n_full = q0 // tk
    n_all = (q0 + tq - 1 + tk - 1) // tk
    nrc = tq // rc
    qs = [q_ref[c * rc:(c + 1) * rc, :] for c in range(nrc)]
    rows = [q0 + c * rc + lax.broadcasted_iota(jnp.int32, (rc, tk), 0) for c in range(nrc)]
    col = lax.broadcasted_iota(jnp.int32, (rc, tk), 1)

    def step(j, carry, masked, nsub):
        cs = range(nrc)
        ss = range(nsub)
        offs = [pl.multiple_of((j - s) * tk, tk) for s in ss]
        kb = [k_ref[pl.ds(o, tk), :] for o in offs]
        vb = [v_ref[pl.ds(o, tk), :] for o in offs]
        later = [carry[0][c * rc:(c + 1) * rc] for c in cs]
        acc = [carry[1][c * rc:(c + 1) * rc] for c in cs]
        nz = [[_nt_dot(qs[c], kb[s]) for c in cs] for s in ss]
        loss = [[jnp.log(1.0 + jnp.exp2(jnp.minimum(nz[s][c] * (-LOG2E), 126.0))) for c in cs] for s in ss]
        if masked:
            causal = [[(offs[s] + col) < rows[c] for c in cs] for s in ss]
            loss = [[jnp.where(causal[s][c], loss[s][c], 0.0) for c in cs] for s in ss]
        within = [[jnp.dot(loss[s][c].astype(BF16), tri, preferred_element_type=F32) for c in cs] for s in ss]
        for s in ss:
            w = [jnp.exp2((nz[s][c] + later[c] + within[s][c]) * (-LOG2E)) for c in cs]
            if masked:
                w = [jnp.where(causal[s][c], w[c], 0.0) for c in cs]
            acc = [acc[c] + jnp.dot(w[c].astype(BF16), vb[s], preferred_element_type=F32) for c in cs]
            later = [later[c] + within[s][c][:, 0:1] for c in cs]
        return jnp.concatenate(later, axis=0), jnp.concatenate(acc, axis=0)

    carry = (jnp.zeros((tq, 1), F32), jnp.zeros((tq, HEAD_DIM), F32))
    carry = lax.fori_loop(0, n_all - n_full, lambda t, c: step(n_all - 1 - t, c, True, 1), carry)
    n_multi = n_full // ATTN_A_BLOCKS_PER_STEP
    n_single = n_full - n_multi * ATTN_A_BLOCKS_PER_STEP
    carry = lax.fori_loop(0, n_single, lambda t, c: step(n_full - 1 - t, c, False, 1), carry)
    carry = lax.fori_loop(
        0, n_multi,
        lambda t, c: step(n_full - n_single - 1 - t * ATTN_A_BLOCKS_PER_STEP, c, False, ATTN_A_BLOCKS_PER_STEP),
        carry)
    o_ref[...] = carry[1].astype(o_ref.dtype)


def attn_a(q, k, v, past, tk):
    b, t, _ = q.shape
    lp = k.shape[1]
    tq = _row_tile(t, 512)
    rc = min(tq, ATTN_A_ROW_CHUNK)
    tri = jnp.tri(tk, dtype=BF16)
    return pl.pallas_call(
        functools.partial(_attn_a_kernel, tq=tq, tk=tk, rc=rc, past=past),
        grid=(b, H_A, t // tq),
        in_specs=[pl.BlockSpec((None, tq, HEAD_DIM), lambda bi, h, i: (bi, i, h)),
                  pl.BlockSpec((None, lp, HEAD_DIM), lambda bi, h, i: (bi, 0, h)),
                  pl.BlockSpec((None, lp, HEAD_DIM), lambda bi, h, i: (bi, 0, h)),
                  pl.BlockSpec((tk, tk), lambda bi, h, i: (0, 0))],
        out_specs=pl.BlockSpec((None, tq, HEAD_DIM), lambda bi, h, i: (bi, i, h)),
        out_shape=jax.ShapeDtypeStruct((b, t, A_W), BF16),
        compiler_params=_params("parallel", "parallel", "parallel"),
        name="attn_a",
    )(q, k, v, tri)


def _dsa_kernel(qi_ref, kiw_ref, kilo_ref, kihi_ref, q_ref, k_ref, v_ref, triu_ref, o_ref, key_ref,
                *, tq, tk, past, n_keys, topk):
    i = pl.program_id(1)
    q0 = past + i * tq
    pos = q0 + lax.broadcasted_iota(jnp.int32, (tq, 1), 0)
    lim = jnp.minimum((pos // CHUNK + 1) * CHUNK, n_keys)
    n_adm = jnp.minimum(((q0 + tq - 1) // CHUNK + 1) * CHUNK, n_keys)
    nt = (n_adm + tk - 1) // tk
    col = lax.broadcasted_iota(jnp.int32, (tq, tk), 1)
    ngrp = tk // 128

    qi = qi_ref[...]
    kiw = kiw_ref[...]
    wi = [kiw[:, D_IDX + h:D_IDX + h + 1] for h in range(H_IDX)]

    def to_key(x):
        bits = lax.bitcast_convert_type(x, jnp.int32)
        return bits ^ ((bits >> 31) & 0x7FFFFFFF)

    def score_tile(t, carry):
        m1, m2 = carry
        off = pl.multiple_of(t * tk, tk)
        kis = (kilo_ref[pl.ds(off, tk), :], kihi_ref[pl.ds(off, tk), :])
        s = jnp.zeros((tq, tk), F32)
        for h in range(H_IDX):
            s_idx = _nt_dot(qi[:, (h // 2) * 128:(h // 2 + 1) * 128], kis[h % 2])
            s = s + wi[h] * jnp.maximum(s_idx, 0.0)
        s = jnp.where(s == 0.0, 0.0, s)
        s = jnp.where(off + col < lim, s, -jnp.inf)
        for g in range(ngrp):
            x = s[:, g * 128:(g + 1) * 128]
            m2 = jnp.maximum(m2, jnp.minimum(m1, x))
            m1 = jnp.maximum(m1, x)
        key_ref[:, pl.ds(off, tk)] = to_key(s)
        return m1, m2

    ninf = jnp.full((tq, 128), -jnp.inf, F32)
    m1, m2 = lax.fori_loop(0, nt, score_tile, (ninf, ninf))
    lo_u = to_key(jnp.min(m2, axis=1, keepdims=True)) ^ INT_MIN
    hi_u = to_key(jnp.max(m2 if topk > 128 else m1, axis=1, keepdims=True)) ^ INT_MIN
    low_mask = lo_u ^ hi_u
    for sh in (1, 2, 4, 8, 16):
        low_mask = low_mask | lax.shift_right_logical(low_mask, sh)
    n_bits = jnp.max(lax.population_count(low_mask))

    def count_ge(cand):
        cand_b = jnp.broadcast_to(cand, (tq, 128))

        def body(t, acc):
            off = pl.multiple_of(t * tk, tk)
            kt = key_ref[:, pl.ds(off, tk)]
            for g in range(ngrp):
                acc = acc + jnp.where(kt[:, g * 128:(g + 1) * 128] >= cand_b, 1, 0)
            return acc

        acc = lax.fori_loop(0, nt, body, jnp.zeros((tq, 128), jnp.int32))
        return jnp.sum(acc, axis=1, keepdims=True)

    def undecided(cnt_t):
        return jnp.max(jnp.abs(cnt_t - topk)) > 0

    def bit_cond(c):
        b, _, cnt_t = c
        return (b < n_bits) & undecided(cnt_t)

    def bit_step(c):
        b, t_u, cnt_t = c
        bit = lax.shift_left(jnp.int32(1), n_bits - 1 - b)
        cand_u = t_u | bit
        cnt = count_ge(cand_u ^ INT_MIN)
        ok = ((low_mask & bit) != 0) & (cnt >= topk)
        return b + 1, jnp.where(ok, cand_u, t_u), jnp.where(ok, cnt, cnt_t)

    init = (jnp.int32(0), lo_u & ~low_mask, jnp.full((tq, 1), 2 ** 30, jnp.int32))
    _, t_u, cnt_t = lax.while_loop(bit_cond, bit_step, init)
    tau = t_u ^ INT_MIN
    has_ties = undecided(cnt_t)

    def store_bias(off, sel, kt):
        sel = sel & (kt > KEY_NEG_INF)
        bias = jnp.where(sel, 0.0, NEG_BIG).astype(F32)
        key_ref[:, pl.ds(off, tk)] = lax.bitcast_convert_type(bias, jnp.int32)

    @pl.when(jnp.logical_not(has_ties))
    def _():
        def select_tile(t, _):
            off = pl.multiple_of(t * tk, tk)
            kt = key_ref[:, pl.ds(off, tk)]
            store_bias(off, kt >= tau, kt)
            return 0

        lax.fori_loop(0, nt, select_tile, 0)

    @pl.when(has_ties)
    def _():
        tau_next = jnp.where(tau == 2 ** 31 - 1, tau, tau + 1)
        cnt_gt = jnp.where(tau == 2 ** 31 - 1, 0, count_ge(tau_next))
        need = (topk - cnt_gt).astype(F32)
        triu = triu_ref[...]

        def select_tile(t, seen):
            off = pl.multiple_of(t * tk, tk)
            kt = key_ref[:, pl.ds(off, tk)]
            eq = kt == tau
            prefix = seen + jnp.dot(jnp.where(eq, 1.0, 0.0).astype(BF16), triu, preferred_element_type=F32)
            store_bias(off, (kt > tau) | (eq & (prefix <= need)), kt)
            return prefix[:, tk - 1:tk]

        lax.fori_loop(0, nt, select_tile, jnp.zeros((tq, 1), F32))

    qs = jnp.concatenate([q_ref[:, h * HEAD_DIM:(h + 1) * HEAD_DIM] for h in range(H_B)], axis=0)

    def attend_tile(t, carry):
        m, l, acc = carry
        off = pl.multiple_of(t * tk, tk)
        kb = k_ref[pl.ds(off, tk), :]
        vb = v_ref[pl.ds(off, tk), :]
        bias = lax.bitcast_convert_type(key_ref[:, pl.ds(off, tk)], F32)
        logits = _nt_dot(qs, kb).reshape(H_B, tq, tk) + bias[None]
        logits = logits.reshape(H_B * tq, tk)
        m_new = jnp.maximum(m, jnp.max(logits, axis=1, keepdims=True))
        alpha = jnp.exp(m - m_new)
        p = jnp.exp(logits - m_new)
        l = alpha * l + jnp.sum(p, axis=1, keepdims=True)
        acc = alpha * acc + jnp.dot(p.astype(BF16), vb, preferred_element_type=F32)
        return m_new, l, acc

    init = (jnp.full((H_B * tq, 1), NEG_BIG, F32), jnp.zeros((H_B * tq, 1), F32),
            jnp.zeros((H_B * tq, HEAD_DIM), F32))
    _, l, acc = lax.fori_loop(0, nt, attend_tile, init)
    out = acc / l
    for h in range(H_B):
        o_ref[:, h * HEAD_DIM:(h + 1) * HEAD_DIM] = out[h * tq:(h + 1) * tq].astype(o_ref.dtype)


def dsa(qi, kiw, kilo, kihi, q, k, v, past, n_keys, tk):
    b, t, _ = q.shape
    lp = k.shape[1]
    tq = _row_tile(t, 128)
    topk = min(TOPK_MAX, n_keys // 4)
    assert lp % tk == 0 and lp >= topk
    triu = jnp.tri(tk, dtype=BF16).T
    qblk = lambda w: pl.BlockSpec((None, tq, w), lambda bi, i: (bi, i, 0))
    kblk = lambda w: pl.BlockSpec((None, lp, w), lambda bi, i: (bi, 0, 0))
    return pl.pallas_call(
        functools.partial(_dsa_kernel, tq=tq, tk=tk, past=past, n_keys=n_keys, topk=topk),
        grid=(b, t // tq),
        in_specs=[qblk(H_IDX * D_IDX), qblk(128), kblk(128), kblk(128), qblk(B_W),
                  kblk(HEAD_DIM), kblk(HEAD_DIM), pl.BlockSpec((tk, tk), lambda bi, i: (0, 0))],
        out_specs=qblk(B_W),
        out_shape=jax.ShapeDtypeStruct((b, t, B_W), BF16),
        scratch_shapes=[pltpu.VMEM((tq, lp), jnp.int32)],
        compiler_params=_params("parallel", "parallel"),
        name="dsa",
    )(qi, kiw, kilo, kihi, q, k, v, triu)


def _softplus(x):
    return jnp.maximum(x, 0.0) + jnp.log(1.0 + jnp.exp(-jnp.abs(x)))


def _c_prep_kernel(*refs, first_layer):
    if first_layer:
        (cz_ref, prev_ref, mu_ref, w0_ref, a0_ref, w2_ref, a2_ref, g2_ref, kk_ref, ka_ref,
         r_ref, kn_ref, k_ref, v_ref, lw_ref, a_ref, g_ref, vtok_ref) = refs
    else:
        (cz_ref, prev_ref, mu_ref, w0_ref, a0_ref, w2_ref, a2_ref, g2_ref, kk_ref, ka_ref,
         vf_ref, v0_ref, v1_ref, v2_ref,
         r_ref, kn_ref, k_ref, v_ref, lw_ref, a_ref, g_ref) = refs

    def put_heads(ref, x):
        for pair in range(H_C // 2):
            tile = x[:, pair * 128:(pair + 1) * 128]
            ref[2 * pair] = tile[:, :N_C]
            ref[2 * pair + 1] = pltpu.roll(tile, N_C, 1)[:, :N_C]

    cz = cz_ref[...]
    rowid = lax.broadcasted_iota(jnp.int32, cz.shape, 0)
    prev = jnp.where(rowid == 0, prev_ref[...], pltpu.roll(cz, 1, 0))
    zz = cz + (prev - cz) * mu_ref[...]
    r = zz[:, 0:C_W]
    k = zz[:, C_W:2 * C_W]
    v = zz[:, 2 * C_W:3 * C_W]
    tail = zz[:, 3 * C_W:C_PAD]
    w = w0_ref[...] + jnp.dot(jnp.tanh(tail).astype(BF16), w2_ref[...], preferred_element_type=F32)
    w = -_softplus(-w) - 0.5
    a = jax.nn.sigmoid(a0_ref[...] + jnp.dot(tail.astype(BF16), a2_ref[...], preferred_element_type=F32))
    g = jnp.dot(jax.nn.sigmoid(tail).astype(BF16), g2_ref[...], preferred_element_type=F32)
    if not first_layer:
        lora = jnp.dot(v.astype(BF16), v1_ref[...], preferred_element_type=F32)
        lora = jnp.dot(lora.astype(BF16), v2_ref[...], preferred_element_type=F32)
        v = v + (vf_ref[...] - v) * jax.nn.sigmoid(v0_ref[...] + lora)
    if first_layer:
        vtok_ref[...] = v
    put_heads(r_ref, r)
    put_heads(kn_ref, k * kk_ref[...])
    put_heads(k_ref, k * (1.0 + (a - 1.0) * ka_ref[...]))
    put_heads(v_ref, v)
    put_heads(lw_ref, -jnp.exp(w))
    put_heads(a_ref, a)
    put_heads(g_ref, g)


def c_prep(cz, prev_rows, vecs, mats, v_first, vres, tm):
    b, t, _ = cz.shape
    first_layer = vres is None
    blk = lambda w: pl.BlockSpec((None, tm, w), lambda bi, i: (bi, i, 0))
    vec = lambda w: pl.BlockSpec((1, w), lambda bi, i: (0, 0))
    mat = lambda m: pl.BlockSpec(m.shape, lambda bi, i: (0, 0))
    mu, w0, a0, k_k, k_a = vecs
    w2, a2, g2 = mats
    ins = [cz, prev_rows, mu, w0, a0, w2, a2, g2, k_k, k_a]
    specs = [blk(C_PAD), pl.BlockSpec((None, None, 1, C_PAD), lambda bi, i: (bi, i, 0, 0)),
             vec(C_PAD), vec(C_W), vec(C_W), mat(w2), mat(a2), mat(g2), vec(C_W), vec(C_W)]
    if not first_layer:
        v0, v1, v2 = vres
        ins += [v_first, v0, v1, v2]
        specs += [blk(C_W), vec(C_W), mat(v1), mat(v2)]
    heads = pl.BlockSpec((None, H_C, tm, N_C), lambda bi, i: (bi, 0, i, 0))
    out_specs = [heads] * 7
    out_shape = [jax.ShapeDtypeStruct((b, H_C, t, N_C), F32)] * 7
    if first_layer:
        out_specs.append(blk(C_W))
        out_shape.append(jax.ShapeDtypeStruct((b, t, C_W), F32))
    return pl.pallas_call(
        functools.partial(_c_prep_kernel, first_layer=first_layer),
        grid=(b, t // tm),
        in_specs=specs,
        out_specs=out_specs,
        out_shape=out_shape,
        compiler_params=_params("parallel", "parallel"),
        name="c_prep",
    )(*ins)


def _bdot(a, b, contract_a, contract_b):
    return lax.dot_general(a.astype(BF16), b.astype(BF16), (((contract_a,), (contract_b,)), ((0,), (0,))),
                           preferred_element_type=F32)


def _bdot3(a, b, contract_a, contract_b):
    ah = a.astype(BF16)
    bh = b.astype(BF16)
    al = (a - ah.astype(F32)).astype(BF16)
    bl = (b - bh.astype(F32)).astype(BF16)
    dn = (((contract_a,), (contract_b,)), ((0,), (0,)))
    dot = lambda x, y: lax.dot_general(x, y, dn, preferred_element_type=F32)
    return dot(ah, bh) + (dot(al, bh) + dot(ah, bl))


def _c_scan_kernel(r_ref, kn_ref, k_ref, v_ref, lw_ref, a_ref, g_ref, s0_ref, rk_ref, lng_ref, lnb_ref,
                   y_ref, st_ref, s_ref, *, nck):
    c_len = SCAN_CHUNK

    @pl.when(pl.program_id(1) == 0)
    def _():
        s_ref[...] = s0_ref[...]

    ti = lax.broadcasted_iota(jnp.int32, (H_C, c_len, c_len), 1)
    si = lax.broadcasted_iota(jnp.int32, (H_C, c_len, c_len), 2)
    incl = si <= ti
    strict = si < ti
    tri = jnp.where(incl, 1.0, 0.0)
    eye = jnp.where(si == ti, 1.0, 0.0)
    r_k = rk_ref[...]
    ln_g = lng_ref[...]
    ln_b = lnb_ref[...]

    def chunk(c, _):
        rows = pl.ds(pl.multiple_of(c * c_len, c_len), c_len)
        r = r_ref[:, rows, :]
        k = k_ref[:, rows, :]
        v = v_ref[:, rows, :]
        lw = lw_ref[:, rows, :]
        a = a_ref[:, rows, :]
        kn = kn_ref[:, rows, :]
        kn = kn * lax.rsqrt(jnp.maximum(jnp.sum(kn * kn, axis=-1, keepdims=True), 1e-24))
        aa = -kn
        bb = kn * a
        g = _bdot3(tri, lw, 2, 1)
        g_end = g[:, c_len - 1:c_len, :]
        a_t = aa * jnp.exp(g - lw)
        r_t = r * jnp.exp(g)
        inv = jnp.exp(-g)
        b_h = bb * inv
        k_h = k * inv
        to_end = jnp.exp(g_end - g)
        b_e = bb * to_end
        k_e = k * to_end
        d_end = jnp.exp(g_end)
        a_ab = jnp.where(strict, _bdot(a_t, b_h, 2, 2), 0.0)
        a_ak = jnp.where(strict, _bdot(a_t, k_h, 2, 2), 0.0)
        a_rb = jnp.where(incl, _bdot(r_t, b_h, 2, 2), 0.0)
        a_rk = jnp.where(incl, _bdot(r_t, k_h, 2, 2), 0.0)
        t_inv = eye + a_ab
        pw = a_ab
        for _ in range(4):
            pw = _bdot(pw, pw, 2, 1)
            t_inv = t_inv + _bdot(t_inv, pw, 2, 1)
        p_mat = _bdot(t_inv, a_t, 2, 1)
        q_mat = _bdot(t_inv, _bdot(a_ak, v, 2, 1), 2, 1)
        y_v = _bdot(a_rk, v, 2, 1)
        s = s_ref[...]
        u = _bdot(p_mat, s, 2, 2) + q_mat
        y = _bdot(r_t, s, 2, 2) + _bdot(a_rb, u, 2, 1) + y_v
        uv = jnp.concatenate([u, v], axis=1)
        bk = jnp.concatenate([b_e, k_e], axis=1)
        s_ref[...] = s * d_end + _bdot(uv, bk, 1, 1)
        mu = jnp.mean(y, axis=-1, keepdims=True)
        yc = y - mu
        yn = yc * lax.rsqrt(jnp.mean(yc * yc, axis=-1, keepdims=True) + LNX_EPS)
        yn = yn * ln_g + ln_b
        bonus = jnp.sum(r * k * r_k, axis=-1, keepdims=True) * v
        o = (yn + bonus) * g_ref[:, rows, :]
        y_ref[rows, :] = jnp.concatenate([o[h] for h in range(H_C)], axis=-1).astype(y_ref.dtype)
        return 0

    lax.fori_loop(0, nck, chunk, 0)

    @pl.when(pl.program_id(1) == pl.num_programs(1) - 1)
    def _():
        st_ref[...] = s_ref[...]


def c_scan(r, kn, k, v, lw, a, g, s0, r_k, ln_g, ln_b, tm):
    b, _, t, _ = r.shape
    blk = pl.BlockSpec((None, H_C, tm, N_C), lambda bi, i: (bi, 0, i, 0))
    st = pl.BlockSpec((None, H_C, N_C, N_C), lambda bi, i: (bi, 0, 0, 0))
    vec = pl.BlockSpec((H_C, 1, N_C), lambda bi, i: (0, 0, 0))
    return pl.pallas_call(
        functools.partial(_c_scan_kernel, nck=tm // SCAN_CHUNK),
        grid=(b, t // tm),
        in_specs=[blk] * 7 + [st, vec, vec, vec],
        out_specs=[pl.BlockSpec((None, tm, C_W), lambda bi, i: (bi, i, 0)), st],
        out_shape=[jax.ShapeDtypeStruct((b, t, C_W), BF16),
                   jax.ShapeDtypeStruct((b, H_C, N_C, N_C), F32)],
        scratch_shapes=[pltpu.VMEM((H_C, N_C, N_C), F32)],
        compiler_params=_params("parallel", "arbitrary"),
        name="c_scan",
    )(r, kn, k, v, lw, a, g, s0, r_k.reshape(H_C, 1, N_C), ln_g.reshape(H_C, 1, N_C), ln_b.reshape(H_C, 1, N_C))


def _pad_rows(x, rows, axis=1):
    pad = rows - x.shape[axis]
    if pad == 0:
        return x
    widths = [(0, 0)] * x.ndim
    widths[axis] = (0, pad)
    return jnp.pad(x, widths)


def _prep_weights(w):
    out = []
    for i in range(DEPTH):
        w_in = w["w_in"][i]
        o_c = 3 * A_W + B_W + 2 * HEAD_DIM + H_IDX * D_IDX + D_IDX + H_IDX
        o_g = o_c + C_COLS
        w_att = jnp.pad(w_in[:, :o_c], ((0, 0), (0, ATT_W - o_c)))
        w_c = jnp.pad(w_in[:, o_c:o_g], ((0, 0), (0, C_PAD - C_COLS)))
        w_g = w_in[:, o_g:]
        tail_rows = lambda m, off: jnp.pad(m, ((off, C_TAIL - off - m.shape[0]), (0, 0))).astype(BF16)
        lw = dict(
            w_att=w_att.astype(BF16), w_c=w_c.astype(BF16), w_g=w_g.astype(BF16),
            w2=tail_rows(w["c_w2"][i], 0),
            a2=tail_rows(w["c_a2"][i], D_DECAY_LORA),
            g2=tail_rows(w["c_g2"][i], D_DECAY_LORA + D_AAA_LORA),
            mu=jnp.pad(w["c_mu"][i], (0, C_PAD - C_COLS)).reshape(1, C_PAD),
            w_br_a=w["w_br_a"][i].astype(BF16), w_br_b=w["w_br_b"][i].astype(BF16),
            w_br_c=w["w_br_c"][i].astype(BF16), w_out=w["w_out"][i].astype(BF16),
            w_ffn_in=w["w_ffn_in"][i].astype(BF16), w_ffn_out=w["w_ffn_out"][i].astype(BF16),
            w_ple_gate=w["w_ple_gate"][i].astype(BF16), w_ple_proj=w["w_ple_proj"][i].astype(BF16),
        )
        if i > 0:
            lw["v1"] = w["c_v1"][i - 1].astype(BF16)
            lw["v2"] = w["c_v2"][i - 1].astype(BF16)
        out.append(lw)
    return out


def _run_trunk(x, p, past, caches, w, wl):
    b, t, _ = x.shape
    n = b * t
    n_keys = past + t
    pos = past + jnp.arange(t)
    lane = jnp.arange(128)
    tabs = (_rope_tables(pos, 128, HEAD_DIM, HEAD_DIM // ROT_FRACTION),
            _rope_tables(pos, 128, D_IDX, D_IDX // ROT_FRACTION),
            _rope_tables(pos, 128, 128, D_IDX // ROT_FRACTION,
                         scale_lanes=(lane < D_IDX, jnp.where(lane < D_IDX + H_IDX, H_IDX ** -0.5, 1.0))))
    tk_a = 256
    tk_b = 512
    lp_a = -(-n_keys // tk_a) * tk_a
    lp_b = -(-n_keys // tk_b) * tk_b
    t_c = -(-t // SCAN_CHUNK) * SCAN_CHUNK
    tm_c = _row_tile(t, 256)
    tm_s = _row_tile(t_c, 256)
    h = x.reshape(n, D_MODEL)
    outs = ([], [], [], [], [], [], [])
    v_first = None
    for i in range(DEPTH):
        lw = wl[i]
        z_att = mm_norm(h, w["norm_mix"][i], lw["w_att"], ATT_W // 3)
        z_c = mm_norm(h, w["norm_mix"][i], lw["w_c"], C_PAD // 4)
        z_g = mm_norm(h, w["norm_mix"][i], lw["w_g"], 1024)
        (aq, ak_f, ak_b, av_f, av_b, bq, bk_f, bk_b, bv_f, bv_b, qi, kiw, kilo, kihi) = post_in(
            z_att, tabs, (w["a_q_norm"][i], w["a_k_norm"][i], w["b_q_norm"][i], w["b_k_norm"][i]), t)
        seq = lambda a: a.reshape(b, t, a.shape[-1])

        def keys(new, old, lp):
            new = seq(new)
            if old is not None:
                new = jnp.concatenate([old.astype(BF16), new], axis=1)
            return _pad_rows(new, lp)

        c = caches
        past_of = lambda name: None if c is None else c[name][i]
        pa_k = None if c is None else c["a_k"][i].reshape(b, past, A_W)
        pa_v = None if c is None else c["a_v"][i].reshape(b, past, A_W)
        o_a = attn_a(seq(aq), keys(ak_b, pa_k, lp_a), keys(av_b, pa_v, lp_a), past, tk_a)
        if c is None:
            p_lo = p_hi = None
        else:
            p_lo = jnp.pad(c["b_ki"][i], ((0, 0), (0, 0), (0, 128 - D_IDX)))
            p_hi = jnp.pad(c["b_ki"][i], ((0, 0), (0, 0), (128 - D_IDX, 0)))
        o_b = dsa(seq(qi), seq(kiw), keys(kilo, p_lo, lp_b), keys(kihi, p_hi, lp_b), seq(bq),
                  keys(bk_b, past_of("b_k"), lp_b), keys(bv_b, past_of("b_v"), lp_b), past, n_keys, tk_b)
        cz = seq(z_c)
        shift0 = jnp.zeros((b, C_COLS), F32) if c is None else c["shift"][i]
        shift0 = jnp.pad(shift0, ((0, 0), (0, C_PAD - C_COLS)))
        prev_rows = jnp.concatenate([shift0[:, None, :], cz[:, tm_c - 1:t - 1:tm_c, :]], axis=1)
        prev_rows = prev_rows.reshape(b, t // tm_c, 1, C_PAD)
        row = lambda a: a.reshape(1, C_W)
        vres = None if i == 0 else (row(w["c_v0"][i - 1]), lw["v1"], lw["v2"])
        prep = c_prep(cz, prev_rows,
                      (lw["mu"], row(w["c_w0"][i]), row(w["c_a0"][i]), row(w["c_k_k"][i]), row(w["c_k_a"][i])),
                      (lw["w2"], lw["a2"], lw["g2"]), v_first, vres, tm_c)
        if i == 0:
            v_first = prep[7]
        s0 = jnp.zeros((b, H_C, N_C, N_C), F32) if c is None else c["wkv"][i]
        y_c, wkv_t = c_scan(*[_pad_rows(a, t_c, axis=2) for a in prep[:7]], s0,
                            w["c_r_k"][i], w["c_ln_g"][i], w["c_ln_b"][i], tm_s)
        o_c = y_c[:, :t].reshape(n, C_W)
        merged = mm_gated3(o_a.reshape(n, A_W), o_b.reshape(n, B_W), o_c,
                           lw["w_br_a"], lw["w_br_b"], lw["w_br_c"], z_g)
        h = mm_res(merged, lw["w_out"], h)
        h = ffn(h, w["norm_ffn"][i], lw["w_ffn_in"], lw["w_ffn_out"])
        h = ple(h, w["norm_ple"][i], lw["w_ple_gate"], p[i].reshape(n, -1).astype(BF16), lw["w_ple_proj"])
        vals = (ak_f.reshape(b, t, H_A, HEAD_DIM), av_f.reshape(b, t, H_A, HEAD_DIM),
                seq(bk_f), seq(bv_f), seq(kiw)[:, :, :D_IDX], wkv_t, cz[:, -1, :C_COLS])
        for lst, val in zip(outs, vals):
            lst.append(val)
    return h.reshape(b, t, D_MODEL), [jnp.stack(l) for l in outs]


def kernel(x_prompt, x_sample, cache_a_k, cache_a_v, cache_b_k, cache_b_v, cache_b_kidx, state_c_wkv, state_c_shift, p_prompt, p_sample, norm_mix, w_in, a_q_norm, a_k_norm, b_q_norm, b_k_norm, c_mu, c_w0, c_w2, c_a0, c_a2, c_g2, c_v0, c_v1, c_v2, c_k_k, c_k_a, c_r_k, c_ln_g, c_ln_b, w_br_a, w_br_b, w_br_c, w_out, norm_ffn, w_ffn_in, w_ffn_out, norm_ple, w_ple_gate, w_ple_proj):
    w = dict(norm_mix=norm_mix, w_in=w_in, a_q_norm=a_q_norm, a_k_norm=a_k_norm, b_q_norm=b_q_norm,
             b_k_norm=b_k_norm, c_mu=c_mu, c_w0=c_w0, c_w2=c_w2, c_a0=c_a0, c_a2=c_a2, c_g2=c_g2,
             c_v0=c_v0, c_v1=c_v1, c_v2=c_v2, c_k_k=c_k_k, c_k_a=c_k_a, c_r_k=c_r_k, c_ln_g=c_ln_g,
             c_ln_b=c_ln_b, w_br_a=w_br_a, w_br_b=w_br_b, w_br_c=w_br_c, w_out=w_out, norm_ffn=norm_ffn,
             w_ffn_in=w_ffn_in, w_ffn_out=w_ffn_out, norm_ple=norm_ple, w_ple_gate=w_ple_gate,
             w_ple_proj=w_ple_proj)
    wl = _prep_weights(w)
    y_p, o_p = _run_trunk(x_prompt, p_prompt, 0, None, w, wl)
    caches = dict(a_k=cache_a_k, a_v=cache_a_v, b_k=cache_b_k, b_v=cache_b_v, b_ki=cache_b_kidx,
                  wkv=state_c_wkv, shift=state_c_shift)
    y_s, o_s = _run_trunk(x_sample, p_sample, cache_a_k.shape[2], caches, w, wl)
    return (y_p, y_s, *o_p, *o_s)
```

```python
import functools
import math

import jax
import jax.numpy as jnp
from jax import lax
from jax.experimental import pallas as pl
from jax.experimental.pallas import tpu as pltpu

F32 = jnp.float32
BF16 = jnp.bfloat16

D_MODEL = 2048
DEPTH = 2
HEAD_DIM = 128
H_A = 4
A_W = H_A * HEAD_DIM
H_B = 4
B_W = H_B * HEAD_DIM
H_IDX = 4
D_IDX = 64
TOPK_MAX = 256
CHUNK = 64
H_C = 16
N_C = 64
C_W = H_C * N_C
SCAN_CHUNK = 32
D_DECAY_LORA = 96
D_AAA_LORA = 96
D_GATE_LORA = 256
C_COLS = 3 * C_W + D_DECAY_LORA + D_AAA_LORA + D_GATE_LORA
C_PAD = 3584
C_TAIL = C_PAD - 3 * C_W
ATT_W = 3 * A_W + B_W + 2 * HEAD_DIM + H_IDX * D_IDX + 128
ROPE_THETA = 500000.0
ROT_FRACTION = 4
NORM_EPS = 1e-6
LNX_EPS = 64e-5
NEG_BIG = -1e30

VMEM_LIMIT = 56 * 1024 * 1024
INT_MIN = -(2 ** 31)
KEY_NEG_INF = (0xFF800000 ^ 0x7FFFFFFF) - 2 ** 32
HALF = 2 ** 15


def _params(*sem):
    return pltpu.CompilerParams(dimension_semantics=sem, vmem_limit_bytes=VMEM_LIMIT)


def _row_tile(n, pref):
    t = min(n, pref)
    assert n % t == 0, (n, t)
    return t


def _nt_dot(a, b):
    return lax.dot_general(a, b, (((1,), (1,)), ((), ())), preferred_element_type=F32)


def _rms(x, gain):
    xn = x * lax.rsqrt(jnp.mean(x * x, axis=-1, keepdims=True) + NORM_EPS)
    return xn * gain


def _mm_norm_kernel(x_ref, g_ref, w_ref, o_ref, xn_ref):
    @pl.when(pl.program_id(1) == 0)
    def _():
        xn_ref[...] = _rms(x_ref[...], g_ref[...]).astype(BF16)

    o_ref[...] = jnp.dot(xn_ref[...], w_ref[...], preferred_element_type=F32)


def mm_norm(x, gain, w, tn):
    n, k = x.shape
    nc = w.shape[1]
    tm = _row_tile(n, 1024)
    assert nc % tn == 0
    return pl.pallas_call(
        _mm_norm_kernel,
        grid=(n // tm, nc // tn),
        in_specs=[pl.BlockSpec((tm, k), lambda i, j: (i, 0)),
                  pl.BlockSpec((1, k), lambda i, j: (0, 0)),
                  pl.BlockSpec((k, tn), lambda i, j: (0, j))],
        out_specs=pl.BlockSpec((tm, tn), lambda i, j: (i, j)),
        out_shape=jax.ShapeDtypeStruct((n, nc), F32),
        scratch_shapes=[pltpu.VMEM((tm, k), BF16)],
        compiler_params=_params("parallel", "arbitrary"),
        name="mm_norm",
    )(x, gain.reshape(1, k), w)


def _mm_res_kernel(x_ref, w_ref, h_ref, o_ref):
    o_ref[...] = h_ref[...] + jnp.dot(x_ref[...], w_ref[...], preferred_element_type=F32)


def mm_res(x, w, h):
    n, k = x.shape
    nc = w.shape[1]
    tm = _row_tile(n, 1024)
    tn = 1024
    return pl.pallas_call(
        _mm_res_kernel,
        grid=(n // tm, nc // tn),
        in_specs=[pl.BlockSpec((tm, k), lambda i, j: (i, 0)),
                  pl.BlockSpec((k, tn), lambda i, j: (0, j)),
                  pl.BlockSpec((tm, tn), lambda i, j: (i, j))],
        out_specs=pl.BlockSpec((tm, tn), lambda i, j: (i, j)),
        out_shape=jax.ShapeDtypeStruct((n, nc), F32),
        compiler_params=_params("parallel", "parallel"),
        name="mm_res",
    )(x, w, h)


def _mm_gated3_kernel(oa_ref, ob_ref, oc_ref, wa_ref, wb_ref, wc_ref, ga_ref, gb_ref, gc_ref, o_ref):
    ya = jnp.dot(oa_ref[...], wa_ref[...], preferred_element_type=F32)
    yb = jnp.dot(ob_ref[...], wb_ref[...], preferred_element_type=F32)
    yc = jnp.dot(oc_ref[...], wc_ref[...], preferred_element_type=F32)
    m = (jax.nn.sigmoid(ga_ref[...]) * ya + jax.nn.sigmoid(gb_ref[...]) * yb
         + jax.nn.sigmoid(gc_ref[...]) * yc)
    o_ref[...] = m.astype(o_ref.dtype)


def mm_gated3(oa, ob, oc, wa, wb, wc, gates):
    n = oa.shape[0]
    tm = _row_tile(n, 512)
    tn = 512
    nj = D_MODEL // tn
    row = lambda w: pl.BlockSpec((tm, w), lambda i, j: (i, 0))
    col = lambda k: pl.BlockSpec((k, tn), lambda i, j: (0, j))
    gate = lambda s: pl.BlockSpec((tm, tn), lambda i, j: (i, j + s * nj))
    return pl.pallas_call(
        _mm_gated3_kernel,
        grid=(n // tm, nj),
        in_specs=[row(A_W), row(B_W), row(C_W), col(A_W), col(B_W), col(C_W), gate(0), gate(1), gate(2)],
        out_specs=pl.BlockSpec((tm, tn), lambda i, j: (i, j)),
        out_shape=jax.ShapeDtypeStruct((n, D_MODEL), BF16),
        compiler_params=_params("parallel", "parallel"),
        name="mm_gated3",
    )(oa, ob, oc, wa, wb, wc, gates, gates, gates)


def _ffn_kernel(h_ref, g_ref, wg_ref, wu_ref, wo_ref, o_ref, xn_ref, acc_ref):
    f = pl.program_id(1)

    @pl.when(f == 0)
    def _():
        xn_ref[...] = _rms(h_ref[...], g_ref[...]).astype(BF16)
        acc_ref[...] = jnp.zeros_like(acc_ref)

    xn = xn_ref[...]
    gate = jnp.dot(xn, wg_ref[...], preferred_element_type=F32)
    up = jnp.dot(xn, wu_ref[...], preferred_element_type=F32)
    act = (jax.nn.silu(gate) * up).astype(BF16)
    acc_ref[...] += jnp.dot(act, wo_ref[...], preferred_element_type=F32)

    @pl.when(f == pl.num_programs(1) - 1)
    def _():
        o_ref[...] = h_ref[...] + acc_ref[...]


def ffn(h, gain, w_in, w_out):
    n, d = h.shape
    dff = w_out.shape[0]
    tm = _row_tile(n, 512)
    tf = 512
    assert dff % tf == 0
    nf = dff // tf
    return pl.pallas_call(
        _ffn_kernel,
        grid=(n // tm, nf),
        in_specs=[pl.BlockSpec((tm, d), lambda i, f: (i, 0)),
                  pl.BlockSpec((1, d), lambda i, f: (0, 0)),
                  pl.BlockSpec((d, tf), lambda i, f: (0, f)),
                  pl.BlockSpec((d, tf), lambda i, f: (0, f + nf)),
                  pl.BlockSpec((tf, d), lambda i, f: (f, 0))],
        out_specs=pl.BlockSpec((tm, d), lambda i, f: (i, 0)),
        out_shape=jax.ShapeDtypeStruct((n, d), F32),
        scratch_shapes=[pltpu.VMEM((tm, d), BF16), pltpu.VMEM((tm, d), F32)],
        compiler_params=_params("parallel", "arbitrary"),
        name="ffn",
    )(h, gain.reshape(1, d), w_in, w_in, w_out)


def _ple_kernel(h_ref, g_ref, wg_ref, p_ref, wp_ref, hj_ref, o_ref, xn_ref):
    @pl.when(pl.program_id(1) == 0)
    def _():
        xn_ref[...] = _rms(h_ref[...], g_ref[...]).astype(BF16)

    gate = jnp.dot(xn_ref[...], wg_ref[...], preferred_element_type=F32)
    emb = jnp.dot(p_ref[...], wp_ref[...], preferred_element_type=F32)
    o_ref[...] = hj_ref[...] + jax.nn.sigmoid(gate) * emb


def ple(h, gain, w_gate, p, w_proj):
    n, d = h.shape
    pd = p.shape[1]
    tm = _row_tile(n, 1024)
    tn = 512
    return pl.pallas_call(
        _ple_kernel,
        grid=(n // tm, d // tn),
        in_specs=[pl.BlockSpec((tm, d), lambda i, j: (i, 0)),
                  pl.BlockSpec((1, d), lambda i, j: (0, 0)),
                  pl.BlockSpec((d, tn), lambda i, j: (0, j)),
                  pl.BlockSpec((tm, pd), lambda i, j: (i, 0)),
                  pl.BlockSpec((pd, tn), lambda i, j: (0, j)),
                  pl.BlockSpec((tm, tn), lambda i, j: (i, j))],
        out_specs=pl.BlockSpec((tm, tn), lambda i, j: (i, j)),
        out_shape=jax.ShapeDtypeStruct((n, d), F32),
        scratch_shapes=[pltpu.VMEM((tm, d), BF16)],
        compiler_params=_params("parallel", "arbitrary"),
        name="ple",
    )(h, gain.reshape(1, d), w_gate, p, w_proj, h)


def _rope_tables(pos, width, period, rot, scale_lanes=None):
    half = rot // 2
    inv = ROPE_THETA ** (-jnp.arange(half, dtype=F32) / half)
    ang = pos.astype(F32)[:, None] * inv[None, :]
    cos, sin = jnp.cos(ang), jnp.sin(ang)
    lane = jnp.arange(width) % period
    first, second = lane < half, (lane >= half) & (lane < rot)
    idx = lane % half
    c = jnp.where((first | second)[None, :], cos[:, idx], 1.0)
    s_up = jnp.where(first[None, :], -sin[:, idx], 0.0)
    s_dn = jnp.where(second[None, :], sin[:, idx], 0.0)
    if scale_lanes is not None:
        only, mult = scale_lanes
        c = jnp.where(only[None, :], c, mult[None, :])
        s_up = jnp.where(only[None, :], s_up, 0.0)
        s_dn = jnp.where(only[None, :], s_dn, 0.0)
    return jnp.stack([c, s_up, s_dn]).astype(F32)


def _rope(x, tab, half):
    return (x * tab[0] + pltpu.roll(x, x.shape[1] - half, 1) * tab[1] + pltpu.roll(x, half, 1) * tab[2])


def _post_in_kernel(z_ref, t128_ref, t64_ref, tki_ref, gaq_ref, gak_ref, gbq_ref, gbk_ref,
                    aq_ref, akf_ref, akb_ref, avf_ref, avb_ref, bq_ref, bkf_ref, bkb_ref,
                    bvf_ref, bvb_ref, qi_ref, kiw_ref, kilo_ref, kihi_ref):
    hd = HEAD_DIM
    t128 = t128_ref[...]
    t64 = t64_ref[...]
    half128 = hd // ROT_FRACTION // 2
    half64 = D_IDX // ROT_FRACTION // 2
    for h in range(H_A):
        sl = slice(h * hd, (h + 1) * hd)
        aq = _rms(z_ref[:, sl], gaq_ref[...]) * (-(hd ** -0.5))
        aq_ref[:, sl] = aq.astype(BF16)
        ak = _rms(z_ref[:, A_W + h * hd:A_W + (h + 1) * hd], gak_ref[...])
        akf_ref[:, sl] = ak
        akb_ref[:, sl] = ak.astype(BF16)
        av = z_ref[:, 2 * A_W + h * hd:2 * A_W + (h + 1) * hd]
        avf_ref[:, sl] = av
        avb_ref[:, sl] = av.astype(BF16)
    o = 3 * A_W
    for h in range(H_B):
        sl = slice(h * hd, (h + 1) * hd)
        bq = _rope(_rms(z_ref[:, o + h * hd:o + (h + 1) * hd], gbq_ref[...]), t128, half128) * (hd ** -0.5)
        bq_ref[:, sl] = bq.astype(BF16)
    o += B_W
    bk = _rope(_rms(z_ref[:, o:o + hd], gbk_ref[...]), t128, half128)
    bkf_ref[...] = bk
    bkb_ref[...] = bk.astype(BF16)
    o += hd
    bv = z_ref[:, o:o + hd]
    bvf_ref[...] = bv
    bvb_ref[...] = bv.astype(BF16)
    o += hd
    for c in range(H_IDX * D_IDX // 128):
        qi = _rope(z_ref[:, o + c * 128:o + (c + 1) * 128], t64, half64) * (D_IDX ** -0.5)
        qi_ref[:, c * 128:(c + 1) * 128] = qi.astype(BF16)
    o += H_IDX * D_IDX
    kiw = _rope(z_ref[:, o:o + 128], tki_ref[...], half64)
    kiw_ref[...] = kiw
    lane = lax.broadcasted_iota(jnp.int32, kiw.shape, 1)
    ki = jnp.where(lane < D_IDX, kiw, 0.0)
    kilo_ref[...] = ki.astype(BF16)
    kihi_ref[...] = pltpu.roll(ki, D_IDX, 1).astype(BF16)


def post_in(z, tabs, gains, t_len):
    n = z.shape[0]
    tm = _row_tile(t_len, 512)
    nt = t_len // tm
    rows = lambda w: pl.BlockSpec((tm, w), lambda i: (i, 0))
    tab = pl.BlockSpec((3, tm, 128), lambda i: (0, i % nt, 0))
    gain = pl.BlockSpec((1, HEAD_DIM), lambda i: (0, 0))
    widths = [(A_W, BF16), (A_W, F32), (A_W, BF16), (A_W, F32), (A_W, BF16), (B_W, BF16),
              (HEAD_DIM, F32), (HEAD_DIM, BF16), (HEAD_DIM, F32), (HEAD_DIM, BF16),
              (H_IDX * D_IDX, BF16), (128, F32), (128, BF16), (128, BF16)]
    return pl.pallas_call(
        _post_in_kernel,
        grid=(n // tm,),
        in_specs=[rows(ATT_W), tab, tab, tab, gain, gain, gain, gain],
        out_specs=[rows(w) for w, _ in widths],
        out_shape=[jax.ShapeDtypeStruct((n, w), dt) for w, dt in widths],
        compiler_params=_params("parallel"),
        name="post_in",
    )(z, *tabs, *[g.reshape(1, HEAD_DIM) for g in gains])


LOG2E = 1.4426950408889634
ATTN_A_ROW_CHUNK = 256
ATTN_A_BLOCKS_PER_STEP = 4


def _attn_a_kernel(q_ref, k_ref, v_ref, tri_ref, o_ref, *, tq, tk, rc, past):
    i = pl.program_id(2)
    q0 = past + i * tq
    tri = tri_ref[...]
    n_full = q0 // tk
    n_all = (q0 + tq - 1 + tk - 1) // tk
    nrc = tq // rc
    qs = [q_ref[c * rc:(c + 1) * rc, :] for c in range(nrc)]
    rows = [q0 + c * rc + lax.broadcasted_iota(jnp.int32, (rc, tk), 0) for c in range(nrc)]
    col = lax.broadcasted_iota(jnp.int32, (rc, tk), 1)

    def step(j, carry, masked, nsub):
        cs = range(nrc)
        ss = range(nsub)
        offs = [pl.multiple_of((j - s) * tk, tk) for s in ss]
        kb = [k_ref[pl.ds(o, tk), :] for o in offs]
        vb = [v_ref[pl.ds(o, tk), :] for o in offs]
        later = [carry[0][c * rc:(c + 1) * rc] for c in cs]
        acc = [carry[1][c * rc:(c + 1) * rc] for c in cs]
        nz = [[_nt_dot(qs[c], kb[s]) for c in cs] for s in ss]
        loss = [[jnp.log(1.0 + jnp.exp2(jnp.minimum(nz[s][c] * (-LOG2E), 126.0))) for c in cs] for s in ss]
        if masked:
            causal = [[(offs[s] + col) < rows[c] for c in cs] for s in ss]
            loss = [[jnp.where(causal[s][c], loss[s][c], 0.0) for c in cs] for s in ss]
        within = [[jnp.dot(loss[s][c].astype(BF16), tri, preferred_element_type=F32) for c in cs] for s in ss]
        for s in ss:
            w = [jnp.exp2((nz[s][c] + later[c] + within[s][c]) * (-LOG2E)) for c in cs]
            if masked:
                w = [jnp.where(causal[s][c], w[c], 0.0) for c in cs]
            acc = [acc[c] + jnp.dot(w[c].astype(BF16), vb[s], preferred_element_type=F32) for c in cs]
            later = [later[c] + within[s][c][:, 0:1] for c in cs]
        return jnp.concatenate(later, axis=0), jnp.concatenate(acc, axis=0)

    carry = (jnp.zeros((tq, 1), F32), jnp.zeros((tq, HEAD_DIM), F32))
    carry = lax.fori_loop(0, n_all - n_full, lambda t, c: step(n_all - 1 - t, c, True, 1), carry)
    n_multi = n_full // ATTN_A_BLOCKS_PER_STEP
    n_single = n_full - n_multi * ATTN_A_BLOCKS_PER_STEP
    carry = lax.fori_loop(0, n_single, lambda t, c: step(n_full - 1 - t, c, False, 1), carry)
    carry = lax.fori_loop(
        0, n_multi,
        lambda t, c: step(n_full - n_single - 1 - t * ATTN_A_BLOCKS_PER_STEP, c, False, ATTN_A_BLOCKS_PER_STEP),
        carry)
    o_ref[...] = carry[1].astype(o_ref.dtype)


def attn_a(q, k, v, past, tk):
    b, t, _ = q.shape
    lp = k.shape[1]
    tq = _row_tile(t, 512)
    rc = min(tq, ATTN_A_ROW_CHUNK)
    tri = jnp.tri(tk, dtype=BF16)
    return pl.pallas_call(
        functools.partial(_attn_a_kernel, tq=tq, tk=tk, rc=rc, past=past),
        grid=(b, H_A, t // tq),
        in_specs=[pl.BlockSpec((None, tq, HEAD_DIM), lambda bi, h, i: (bi, i, h)),
                  pl.BlockSpec((None, lp, HEAD_DIM), lambda bi, h, i: (bi, 0, h)),
                  pl.BlockSpec((None, lp, HEAD_DIM), lambda bi, h, i: (bi, 0, h)),
                  pl.BlockSpec((tk, tk), lambda bi, h, i: (0, 0))],
        out_specs=pl.BlockSpec((None, tq, HEAD_DIM), lambda bi, h, i: (bi, i, h)),
        out_shape=jax.ShapeDtypeStruct((b, t, A_W), BF16),
        compiler_params=_params("parallel", "parallel", "parallel"),
        name="attn_a",
    )(q, k, v, tri)


def _dsa_kernel(qi_ref, kiw_ref, kilo_ref, kihi_ref, q_ref, k_ref, v_ref, triu_ref, o_ref, hi_ref, lo_ref,
                *, tq, tk, past, n_keys, topk):
    i = pl.program_id(1)
    q0 = past + i * tq
    pos = q0 + lax.broadcasted_iota(jnp.int32, (tq, 1), 0)
    lim = jnp.minimum((pos // CHUNK + 1) * CHUNK, n_keys)
    n_adm = jnp.minimum(((q0 + tq - 1) // CHUNK + 1) * CHUNK, n_keys)
    nt = (n_adm + tk - 1) // tk
    col = lax.broadcasted_iota(jnp.int32, (tq, tk), 1)
    ngrp = tk // 128

    qi = qi_ref[...]
    kiw = kiw_ref[...]
    wi = [kiw[:, D_IDX + h:D_IDX + h + 1] for h in range(H_IDX)]

    def to_key(x):
        bits = lax.bitcast_convert_type(x, jnp.int32)
        return bits ^ ((bits >> 31) & 0x7FFFFFFF)

    def score_tile(t, carry):
        m1, m2 = carry
        off = pl.multiple_of(t * tk, tk)
        kis = (kilo_ref[pl.ds(off, tk), :], kihi_ref[pl.ds(off, tk), :])
        s = jnp.zeros((tq, tk), F32)
        for h in range(H_IDX):
            s_idx = _nt_dot(qi[:, (h // 2) * 128:(h // 2 + 1) * 128], kis[h % 2])
            s = s + wi[h] * jnp.maximum(s_idx, 0.0)
        s = jnp.where(s == 0.0, 0.0, s)
        s = jnp.where(off + col < lim, s, -jnp.inf)
        for g in range(ngrp):
            x = s[:, g * 128:(g + 1) * 128]
            m2 = jnp.maximum(m2, jnp.minimum(m1, x))
            m1 = jnp.maximum(m1, x)
        key = to_key(s)
        hi_ref[:, pl.ds(off, tk)] = (key >> 16).astype(jnp.int16)
        lo_ref[:, pl.ds(off, tk)] = ((key & 0xFFFF) - HALF).astype(jnp.int16)
        return m1, m2

    ninf = jnp.full((tq, 128), -jnp.inf, F32)
    m1, m2 = lax.fori_loop(0, nt, score_tile, (ninf, ninf))
    lo_f = jnp.min(m2, axis=1, keepdims=True)
    hi_f = jnp.max(m2 if topk > 128 else m1, axis=1, keepdims=True)

    def smear(x):
        for sh in (1, 2, 4, 8):
            x = x | lax.shift_right_logical(x, sh)
        return x

    one16 = jnp.ones((tq, 128), jnp.int16)
    zero16 = jnp.zeros((tq, 128), jnp.int16)
    grp = lambda x, g: x[:, g * 128:(g + 1) * 128]
    lanes16 = lambda x: jnp.broadcast_to(x, (tq, 128)).astype(jnp.int16)

    def count_ge(ref, cand):
        cand_b = lanes16(cand)

        def body(t, acc):
            x = ref[:, pl.ds(pl.multiple_of(t * tk, tk), tk)]
            for g in range(ngrp):
                acc = acc + jnp.where(grp(x, g) >= cand_b, one16, zero16)
            return acc

        acc = lax.fori_loop(0, nt, body, zero16)
        return jnp.sum(acc.astype(jnp.int32), axis=1, keepdims=True)

    def largest_with_count(ref, want, t0, mask, n_bits):
        def step(b, c):
            t_u, cnt_t = c
            bit = lax.shift_left(jnp.int32(1), n_bits - 1 - b)
            cand_u = t_u | bit
            cnt = count_ge(ref, cand_u - HALF)
            ok = ((mask & bit) != 0) & (cnt >= want)
            return jnp.where(ok, cand_u, t_u), jnp.where(ok, cnt, cnt_t)

        return lax.fori_loop(0, n_bits, step, (t0, jnp.full((tq, 1), 2 ** 30, jnp.int32)))

    lo_h = (to_key(lo_f) >> 16) + HALF
    hi_h = (to_key(hi_f) >> 16) + HALF
    mask_h = smear(lo_h ^ hi_h)
    p_u, _ = largest_with_count(hi_ref, topk, lo_h & ~mask_h, mask_h, jnp.max(lax.population_count(mask_h)))
    p = p_u - HALF
    top16 = 2 ** 15 - 1
    cnt_gt = jnp.where(p == top16, 0, count_ge(hi_ref, jnp.minimum(p + 1, top16)))
    need_lo = topk - cnt_gt
    p16 = lanes16(p)

    def mask_low_halves(t, _):
        off = pl.multiple_of(t * tk, tk)
        h = hi_ref[:, pl.ds(off, tk)]
        l = lo_ref[:, pl.ds(off, tk)]
        lo_ref[:, pl.ds(off, tk)] = jnp.concatenate(
            [jnp.where(grp(h, g) == p16, grp(l, g), jnp.int16(-HALF)) for g in range(ngrp)], axis=1)
        return 0

    lax.fori_loop(0, nt, mask_low_halves, 0)
    tl_u, cnt_tl = largest_with_count(lo_ref, need_lo, jnp.zeros((tq, 1), jnp.int32),
                                      jnp.full((tq, 1), 0xFFFF, jnp.int32), 16)
    tl = tl_u - HALF
    tl16 = lanes16(tl)
    has_ties = jnp.max(jnp.abs(cnt_tl - need_lo)) > 0

    neg_inf_h = lanes16(jnp.full((tq, 1), KEY_NEG_INF >> 16, jnp.int32))
    zero_b = jnp.zeros((tq, 128), BF16)
    neg_b = jnp.full((tq, 128), NEG_BIG, BF16)

    def store_bias(off, sels, h):
        bias = [jnp.where(sels[g] & (grp(h, g) > neg_inf_h), zero_b, neg_b) for g in range(ngrp)]
        hi_ref[:, pl.ds(off, tk)] = lax.bitcast_convert_type(jnp.concatenate(bias, axis=1), jnp.int16)

    @pl.when(jnp.logical_not(has_ties))
    def _():
        def select_tile(t, _):
            off = pl.multiple_of(t * tk, tk)
            h = hi_ref[:, pl.ds(off, tk)]
            m = lo_ref[:, pl.ds(off, tk)]
            sels = [(grp(h, g) > p16) | ((grp(h, g) == p16) & (grp(m, g) >= tl16)) for g in range(ngrp)]
            store_bias(off, sels, h)
            return 0

        lax.fori_loop(0, nt, select_tile, 0)

    @pl.when(has_ties)
    def _():
        cnt_gt_lo = jnp.where(tl == top16, 0, count_ge(lo_ref, jnp.minimum(tl + 1, top16)))
        need_tie = (need_lo - cnt_gt_lo).astype(F32)
        triu = triu_ref[...]
        one_b = jnp.ones((tq, 128), BF16)

        def select_tile(t, seen):
            off = pl.multiple_of(t * tk, tk)
            h = hi_ref[:, pl.ds(off, tk)]
            m = lo_ref[:, pl.ds(off, tk)]
            eqs = [(grp(h, g) == p16) & (grp(m, g) == tl16) for g in range(ngrp)]
            eq_b = jnp.concatenate([jnp.where(eqs[g], one_b, zero_b) for g in range(ngrp)], axis=1)
            prefix = seen + jnp.dot(eq_b, triu, preferred_element_type=F32)
            in_rank = jnp.where(prefix <= need_tie, 1.0, 0.0).astype(BF16)
            sels = [(grp(h, g) > p16)
                    | ((grp(h, g) == p16) & (grp(m, g) > tl16))
                    | (eqs[g] & (grp(in_rank, g) > zero_b)) for g in range(ngrp)]
            store_bias(off, sels, h)
            return prefix[:, tk - 1:tk]

        lax.fori_loop(0, nt, select_tile, jnp.zeros((tq, 1), F32))

    qs = jnp.concatenate([q_ref[:, h * HEAD_DIM:(h + 1) * HEAD_DIM] for h in range(H_B)], axis=0)

    def attend_tile(t, carry):
        m, l, acc = carry
        off = pl.multiple_of(t * tk, tk)
        kb = k_ref[pl.ds(off, tk), :]
        vb = v_ref[pl.ds(off, tk), :]
        bias = lax.bitcast_convert_type(hi_ref[:, pl.ds(off, tk)], BF16).astype(F32)
        logits = _nt_dot(qs, kb).reshape(H_B, tq, tk) + bias[None]
        logits = logits.reshape(H_B * tq, tk)
        m_new = jnp.maximum(m, jnp.max(logits, axis=1, keepdims=True))
        alpha = jnp.exp(m - m_new)
        p = jnp.exp(logits - m_new)
        l = alpha * l + jnp.sum(p, axis=1, keepdims=True)
        acc = alpha * acc + jnp.dot(p.astype(BF16), vb, preferred_element_type=F32)
        return m_new, l, acc

    init = (jnp.full((H_B * tq, 1), NEG_BIG, F32), jnp.zeros((H_B * tq, 1), F32),
            jnp.zeros((H_B * tq, HEAD_DIM), F32))
    _, l, acc = lax.fori_loop(0, nt, attend_tile, init)
    out = acc / l
    for h in range(H_B):
        o_ref[:, h * HEAD_DIM:(h + 1) * HEAD_DIM] = out[h * tq:(h + 1) * tq].astype(o_ref.dtype)


def dsa(qi, kiw, kilo, kihi, q, k, v, past, n_keys, tk):
    b, t, _ = q.shape
    lp = k.shape[1]
    tq = _row_tile(t, 128)
    topk = min(TOPK_MAX, n_keys // 4)
    assert lp % tk == 0 and lp >= topk
    triu = jnp.tri(tk, dtype=BF16).T
    qblk = lambda w: pl.BlockSpec((None, tq, w), lambda bi, i: (bi, i, 0))
    kblk = lambda w: pl.BlockSpec((None, lp, w), lambda bi, i: (bi, 0, 0))
    return pl.pallas_call(
        functools.partial(_dsa_kernel, tq=tq, tk=tk, past=past, n_keys=n_keys, topk=topk),
        grid=(b, t // tq),
        in_specs=[qblk(H_IDX * D_IDX), qblk(128), kblk(128), kblk(128), qblk(B_W),
                  kblk(HEAD_DIM), kblk(HEAD_DIM), pl.BlockSpec((tk, tk), lambda bi, i: (0, 0))],
        out_specs=qblk(B_W),
        out_shape=jax.ShapeDtypeStruct((b, t, B_W), BF16),
        scratch_shapes=[pltpu.VMEM((tq, lp), jnp.int16), pltpu.VMEM((tq, lp), jnp.int16)],
        compiler_params=_params("parallel", "parallel"),
        name="dsa",
    )(qi, kiw, kilo, kihi, q, k, v, triu)


def _softplus(x):
    return jnp.maximum(x, 0.0) + jnp.log(1.0 + jnp.exp(-jnp.abs(x)))


def _c_prep_kernel(*refs, first_layer):
    if first_layer:
        (cz_ref, prev_ref, mu_ref, w0_ref, a0_ref, w2_ref, a2_ref, g2_ref, kk_ref, ka_ref,
         r_ref, kn_ref, k_ref, v_ref, lw_ref, a_ref, g_ref, vtok_ref) = refs
    else:
        (cz_ref, prev_ref, mu_ref, w0_ref, a0_ref, w2_ref, a2_ref, g2_ref, kk_ref, ka_ref,
         vf_ref, v0_ref, v1_ref, v2_ref,
         r_ref, kn_ref, k_ref, v_ref, lw_ref, a_ref, g_ref) = refs

    def put_heads(ref, x):
        for pair in range(H_C // 2):
            tile = x[:, pair * 128:(pair + 1) * 128]
            ref[2 * pair] = tile[:, :N_C]
            ref[2 * pair + 1] = pltpu.roll(tile, N_C, 1)[:, :N_C]

    cz = cz_ref[...]
    rowid = lax.broadcasted_iota(jnp.int32, cz.shape, 0)
    prev = jnp.where(rowid == 0, prev_ref[...], pltpu.roll(cz, 1, 0))
    zz = cz + (prev - cz) * mu_ref[...]
    r = zz[:, 0:C_W]
    k = zz[:, C_W:2 * C_W]
    v = zz[:, 2 * C_W:3 * C_W]
    tail = zz[:, 3 * C_W:C_PAD]
    w = w0_ref[...] + jnp.dot(jnp.tanh(tail).astype(BF16), w2_ref[...], preferred_element_type=F32)
    w = -_softplus(-w) - 0.5
    a = jax.nn.sigmoid(a0_ref[...] + jnp.dot(tail.astype(BF16), a2_ref[...], preferred_element_type=F32))
    g = jnp.dot(jax.nn.sigmoid(tail).astype(BF16), g2_ref[...], preferred_element_type=F32)
    if not first_layer:
        lora = jnp.dot(v.astype(BF16), v1_ref[...], preferred_element_type=F32)
        lora = jnp.dot(lora.astype(BF16), v2_ref[...], preferred_element_type=F32)
        v = v + (vf_ref[...] - v) * jax.nn.sigmoid(v0_ref[...] + lora)
    if first_layer:
        vtok_ref[...] = v
    put_heads(r_ref, r)
    put_heads(kn_ref, k * kk_ref[...])
    put_heads(k_ref, k * (1.0 + (a - 1.0) * ka_ref[...]))
    put_heads(v_ref, v)
    put_heads(lw_ref, -jnp.exp(w))
    put_heads(a_ref, a)
    put_heads(g_ref, g)


def c_prep(cz, prev_rows, vecs, mats, v_first, vres, tm):
    b, t, _ = cz.shape
    first_layer = vres is None
    blk = lambda w: pl.BlockSpec((None, tm, w), lambda bi, i: (bi, i, 0))
    vec = lambda w: pl.BlockSpec((1, w), lambda bi, i: (0, 0))
    mat = lambda m: pl.BlockSpec(m.shape, lambda bi, i: (0, 0))
    mu, w0, a0, k_k, k_a = vecs
    w2, a2, g2 = mats
    ins = [cz, prev_rows, mu, w0, a0, w2, a2, g2, k_k, k_a]
    specs = [blk(C_PAD), pl.BlockSpec((None, None, 1, C_PAD), lambda bi, i: (bi, i, 0, 0)),
             vec(C_PAD), vec(C_W), vec(C_W), mat(w2), mat(a2), mat(g2), vec(C_W), vec(C_W)]
    if not first_layer:
        v0, v1, v2 = vres
        ins += [v_first, v0, v1, v2]
        specs += [blk(C_W), vec(C_W), mat(v1), mat(v2)]
    heads = pl.BlockSpec((None, H_C, tm, N_C), lambda bi, i: (bi, 0, i, 0))
    out_specs = [heads] * 7
    out_shape = [jax.ShapeDtypeStruct((b, H_C, t, N_C), F32)] * 7
    if first_layer:
        out_specs.append(blk(C_W))
        out_shape.append(jax.ShapeDtypeStruct((b, t, C_W), F32))
    return pl.pallas_call(
        functools.partial(_c_prep_kernel, first_layer=first_layer),
        grid=(b, t // tm),
        in_specs=specs,
        out_specs=out_specs,
        out_shape=out_shape,
        compiler_params=_params("parallel", "parallel"),
        name="c_prep",
    )(*ins)


def _bdot(a, b, contract_a, contract_b):
    return lax.dot_general(a.astype(BF16), b.astype(BF16), (((contract_a,), (contract_b,)), ((0,), (0,))),
                           preferred_element_type=F32)


def _bdot3(a, b, contract_a, contract_b):
    ah = a.astype(BF16)
    bh = b.astype(BF16)
    al = (a - ah.astype(F32)).astype(BF16)
    bl = (b - bh.astype(F32)).astype(BF16)
    dn = (((contract_a,), (contract_b,)), ((0,), (0,)))
    dot = lambda x, y: lax.dot_general(x, y, dn, preferred_element_type=F32)
    return dot(ah, bh) + (dot(al, bh) + dot(ah, bl))


def _c_scan_kernel(r_ref, kn_ref, k_ref, v_ref, lw_ref, a_ref, g_ref, s0_ref, rk_ref, lng_ref, lnb_ref,
                   y_ref, st_ref, s_ref, *, nck):
    c_len = SCAN_CHUNK

    @pl.when(pl.program_id(1) == 0)
    def _():
        s_ref[...] = s0_ref[...]

    ti = lax.broadcasted_iota(jnp.int32, (H_C, c_len, c_len), 1)
    si = lax.broadcasted_iota(jnp.int32, (H_C, c_len, c_len), 2)
    incl = si <= ti
    strict = si < ti
    tri = jnp.where(incl, 1.0, 0.0)
    eye = jnp.where(si == ti, 1.0, 0.0)
    r_k = rk_ref[...]
    ln_g = lng_ref[...]
    ln_b = lnb_ref[...]

    def chunk(c, _):
        rows = pl.ds(pl.multiple_of(c * c_len, c_len), c_len)
        r = r_ref[:, rows, :]
        k = k_ref[:, rows, :]
        v = v_ref[:, rows, :]
        lw = lw_ref[:, rows, :]
        a = a_ref[:, rows, :]
        kn = kn_ref[:, rows, :]
        kn = kn * lax.rsqrt(jnp.maximum(jnp.sum(kn * kn, axis=-1, keepdims=True), 1e-24))
        aa = -kn
        bb = kn * a
        g = _bdot3(tri, lw, 2, 1)
        g_end = g[:, c_len - 1:c_len, :]
        a_t = aa * jnp.exp(g - lw)
        r_t = r * jnp.exp(g)
        inv = jnp.exp(-g)
        b_h = bb * inv
        k_h = k * inv
        to_end = jnp.exp(g_end - g)
        b_e = bb * to_end
        k_e = k * to_end
        d_end = jnp.exp(g_end)
        a_ab = jnp.where(strict, _bdot(a_t, b_h, 2, 2), 0.0)
        a_ak = jnp.where(strict, _bdot(a_t, k_h, 2, 2), 0.0)
        a_rb = jnp.where(incl, _bdot(r_t, b_h, 2, 2), 0.0)
        a_rk = jnp.where(incl, _bdot(r_t, k_h, 2, 2), 0.0)
        t_inv = eye + a_ab
        pw = a_ab
        for _ in range(4):
            pw = _bdot(pw, pw, 2, 1)
            t_inv = t_inv + _bdot(t_inv, pw, 2, 1)
        p_mat = _bdot(t_inv, a_t, 2, 1)
        q_mat = _bdot(t_inv, _bdot(a_ak, v, 2, 1), 2, 1)
        y_v = _bdot(a_rk, v, 2, 1)
        s = s_ref[...]
        u = _bdot(p_mat, s, 2, 2) + q_mat
        y = _bdot(r_t, s, 2, 2) + _bdot(a_rb, u, 2, 1) + y_v
        uv = jnp.concatenate([u, v], axis=1)
        bk = jnp.concatenate([b_e, k_e], axis=1)
        s_ref[...] = s * d_end + _bdot(uv, bk, 1, 1)
        mu = jnp.mean(y, axis=-1, keepdims=True)
        yc = y - mu
        yn = yc * lax.rsqrt(jnp.mean(yc * yc, axis=-1, keepdims=True) + LNX_EPS)
        yn = yn * ln_g + ln_b
        bonus = jnp.sum(r * k * r_k, axis=-1, keepdims=True) * v
        o = (yn + bonus) * g_ref[:, rows, :]
        y_ref[rows, :] = jnp.concatenate([o[h] for h in range(H_C)], axis=-1).astype(y_ref.dtype)
        return 0

    lax.fori_loop(0, nck, chunk, 0)

    @pl.when(pl.program_id(1) == pl.num_programs(1) - 1)
    def _():
        st_ref[...] = s_ref[...]


def c_scan(r, kn, k, v, lw, a, g, s0, r_k, ln_g, ln_b, tm):
    b, _, t, _ = r.shape
    blk = pl.BlockSpec((None, H_C, tm, N_C), lambda bi, i: (bi, 0, i, 0))
    st = pl.BlockSpec((None, H_C, N_C, N_C), lambda bi, i: (bi, 0, 0, 0))
    vec = pl.BlockSpec((H_C, 1, N_C), lambda bi, i: (0, 0, 0))
    return pl.pallas_call(
        functools.partial(_c_scan_kernel, nck=tm // SCAN_CHUNK),
        grid=(b, t // tm),
        in_specs=[blk] * 7 + [st, vec, vec, vec],
        out_specs=[pl.BlockSpec((None, tm, C_W), lambda bi, i: (bi, i, 0)), st],
        out_shape=[jax.ShapeDtypeStruct((b, t, C_W), BF16),
                   jax.ShapeDtypeStruct((b, H_C, N_C, N_C), F32)],
        scratch_shapes=[pltpu.VMEM((H_C, N_C, N_C), F32)],
        compiler_params=_params("parallel", "arbitrary"),
        name="c_scan",
    )(r, kn, k, v, lw, a, g, s0, r_k.reshape(H_C, 1, N_C), ln_g.reshape(H_C, 1, N_C), ln_b.reshape(H_C, 1, N_C))


def _pad_rows(x, rows, axis=1):
    pad = rows - x.shape[axis]
    if pad == 0:
        return x
    widths = [(0, 0)] * x.ndim
    widths[axis] = (0, pad)
    return jnp.pad(x, widths)


def _prep_weights(w):
    out = []
    for i in range(DEPTH):
        w_in = w["w_in"][i]
        o_c = 3 * A_W + B_W + 2 * HEAD_DIM + H_IDX * D_IDX + D_IDX + H_IDX
        o_g = o_c + C_COLS
        w_att = jnp.pad(w_in[:, :o_c], ((0, 0), (0, ATT_W - o_c)))
        w_c = jnp.pad(w_in[:, o_c:o_g], ((0, 0), (0, C_PAD - C_COLS)))
        w_g = w_in[:, o_g:]
        tail_rows = lambda m, off: jnp.pad(m, ((off, C_TAIL - off - m.shape[0]), (0, 0))).astype(BF16)
        lw = dict(
            w_att=w_att.astype(BF16), w_c=w_c.astype(BF16), w_g=w_g.astype(BF16),
            w2=tail_rows(w["c_w2"][i], 0),
            a2=tail_rows(w["c_a2"][i], D_DECAY_LORA),
            g2=tail_rows(w["c_g2"][i], D_DECAY_LORA + D_AAA_LORA),
            mu=jnp.pad(w["c_mu"][i], (0, C_PAD - C_COLS)).reshape(1, C_PAD),
            w_br_a=w["w_br_a"][i].astype(BF16), w_br_b=w["w_br_b"][i].astype(BF16),
            w_br_c=w["w_br_c"][i].astype(BF16), w_out=w["w_out"][i].astype(BF16),
            w_ffn_in=w["w_ffn_in"][i].astype(BF16), w_ffn_out=w["w_ffn_out"][i].astype(BF16),
            w_ple_gate=w["w_ple_gate"][i].astype(BF16), w_ple_proj=w["w_ple_proj"][i].astype(BF16),
        )
        if i > 0:
            lw["v1"] = w["c_v1"][i - 1].astype(BF16)
            lw["v2"] = w["c_v2"][i - 1].astype(BF16)
        out.append(lw)
    return out


def _run_trunk(x, p, past, caches, w, wl):
    b, t, _ = x.shape
    n = b * t
    n_keys = past + t
    pos = past + jnp.arange(t)
    lane = jnp.arange(128)
    tabs = (_rope_tables(pos, 128, HEAD_DIM, HEAD_DIM // ROT_FRACTION),
            _rope_tables(pos, 128, D_IDX, D_IDX // ROT_FRACTION),
            _rope_tables(pos, 128, 128, D_IDX // ROT_FRACTION,
                         scale_lanes=(lane < D_IDX, jnp.where(lane < D_IDX + H_IDX, H_IDX ** -0.5, 1.0))))
    tk_a = 256
    tk_b = 512
    lp_a = -(-n_keys // tk_a) * tk_a
    lp_b = -(-n_keys // tk_b) * tk_b
    t_c = -(-t // SCAN_CHUNK) * SCAN_CHUNK
    tm_c = _row_tile(t, 256)
    tm_s = _row_tile(t_c, 256)
    h = x.reshape(n, D_MODEL)
    outs = ([], [], [], [], [], [], [])
    v_first = None
    for i in range(DEPTH):
        lw = wl[i]
        z_att = mm_norm(h, w["norm_mix"][i], lw["w_att"], ATT_W // 3)
        z_c = mm_norm(h, w["norm_mix"][i], lw["w_c"], C_PAD // 4)
        z_g = mm_norm(h, w["norm_mix"][i], lw["w_g"], 1024)
        (aq, ak_f, ak_b, av_f, av_b, bq, bk_f, bk_b, bv_f, bv_b, qi, kiw, kilo, kihi) = post_in(
            z_att, tabs, (w["a_q_norm"][i], w["a_k_norm"][i], w["b_q_norm"][i], w["b_k_norm"][i]), t)
        seq = lambda a: a.reshape(b, t, a.shape[-1])

        def keys(new, old, lp):
            new = seq(new)
            if old is not None:
                new = jnp.concatenate([old.astype(BF16), new], axis=1)
            return _pad_rows(new, lp)

        c = caches
        past_of = lambda name: None if c is None else c[name][i]
        pa_k = None if c is None else c["a_k"][i].reshape(b, past, A_W)
        pa_v = None if c is None else c["a_v"][i].reshape(b, past, A_W)
        o_a = attn_a(seq(aq), keys(ak_b, pa_k, lp_a), keys(av_b, pa_v, lp_a), past, tk_a)
        if c is None:
            p_lo = p_hi = None
        else:
            p_lo = jnp.pad(c["b_ki"][i], ((0, 0), (0, 0), (0, 128 - D_IDX)))
            p_hi = jnp.pad(c["b_ki"][i], ((0, 0), (0, 0), (128 - D_IDX, 0)))
        o_b = dsa(seq(qi), seq(kiw), keys(kilo, p_lo, lp_b), keys(kihi, p_hi, lp_b), seq(bq),
                  keys(bk_b, past_of("b_k"), lp_b), keys(bv_b, past_of("b_v"), lp_b), past, n_keys, tk_b)
        cz = seq(z_c)
        shift0 = jnp.zeros((b, C_COLS), F32) if c is None else c["shift"][i]
        shift0 = jnp.pad(shift0, ((0, 0), (0, C_PAD - C_COLS)))
        prev_rows = jnp.concatenate([shift0[:, None, :], cz[:, tm_c - 1:t - 1:tm_c, :]], axis=1)
        prev_rows = prev_rows.reshape(b, t // tm_c, 1, C_PAD)
        row = lambda a: a.reshape(1, C_W)
        vres = None if i == 0 else (row(w["c_v0"][i - 1]), lw["v1"], lw["v2"])
        prep = c_prep(cz, prev_rows,
                      (lw["mu"], row(w["c_w0"][i]), row(w["c_a0"][i]), row(w["c_k_k"][i]), row(w["c_k_a"][i])),
                      (lw["w2"], lw["a2"], lw["g2"]), v_first, vres, tm_c)
        if i == 0:
            v_first = prep[7]
        s0 = jnp.zeros((b, H_C, N_C, N_C), F32) if c is None else c["wkv"][i]
        y_c, wkv_t = c_scan(*[_pad_rows(a, t_c, axis=2) for a in prep[:7]], s0,
                            w["c_r_k"][i], w["c_ln_g"][i], w["c_ln_b"][i], tm_s)
        o_c = y_c[:, :t].reshape(n, C_W)
        merged = mm_gated3(o_a.reshape(n, A_W), o_b.reshape(n, B_W), o_c,
                           lw["w_br_a"], lw["w_br_b"], lw["w_br_c"], z_g)
        h = mm_res(merged, lw["w_out"], h)
        h = ffn(h, w["norm_ffn"][i], lw["w_ffn_in"], lw["w_ffn_out"])
        h = ple(h, w["norm_ple"][i], lw["w_ple_gate"], p[i].reshape(n, -1).astype(BF16), lw["w_ple_proj"])
        vals = (ak_f.reshape(b, t, H_A, HEAD_DIM), av_f.reshape(b, t, H_A, HEAD_DIM),
                seq(bk_f), seq(bv_f), seq(kiw)[:, :, :D_IDX], wkv_t, cz[:, -1, :C_COLS])
        for lst, val in zip(outs, vals):
            lst.append(val)
    return h.reshape(b, t, D_MODEL), [jnp.stack(l) for l in outs]


def kernel(x_prompt, x_sample, cache_a_k, cache_a_v, cache_b_k, cache_b_v, cache_b_kidx, state_c_wkv, state_c_shift, p_prompt, p_sample, norm_mix, w_in, a_q_norm, a_k_norm, b_q_norm, b_k_norm, c_mu, c_w0, c_w2, c_a0, c_a2, c_g2, c_v0, c_v1, c_v2, c_k_k, c_k_a, c_r_k, c_ln_g, c_ln_b, w_br_a, w_br_b, w_br_c, w_out, norm_ffn, w_ffn_in, w_ffn_out, norm_ple, w_ple_gate, w_ple_proj):
    w = dict(norm_mix=norm_mix, w_in=w_in, a_q_norm=a_q_norm, a_k_norm=a_k_norm, b_q_norm=b_q_norm,
             b_k_norm=b_k_norm, c_mu=c_mu, c_w0=c_w0, c_w2=c_w2, c_a0=c_a0, c_a2=c_a2, c_g2=c_g2,
             c_v0=c_v0, c_v1=c_v1, c_v2=c_v2, c_k_k=c_k_k, c_k_a=c_k_a, c_r_k=c_r_k, c_ln_g=c_ln_g,
             c_ln_b=c_ln_b, w_br_a=w_br_a, w_br_b=w_br_b, w_br_c=w_br_c, w_out=w_out, norm_ffn=norm_ffn,
             w_ffn_in=w_ffn_in, w_ffn_out=w_ffn_out, norm_ple=norm_ple, w_ple_gate=w_ple_gate,
             w_ple_proj=w_ple_proj)
    wl = _prep_weights(w)
    y_p, o_p = _run_trunk(x_prompt, p_prompt, 0, None, w, wl)
    caches = dict(a_k=cache_a_k, a_v=cache_a_v, b_k=cache_b_k, b_v=cache_b_v, b_ki=cache_b_kidx,
                  wkv=state_c_wkv, shift=state_c_shift)
    y_s, o_s = _run_trunk(x_sample, p_sample, cache_a_k.shape[2], caches, w, wl)
    return (y_p, y_s, *o_p, *o_s)
```

```python
import functools
import math

import jax
import jax.numpy as jnp
from jax import lax
from jax.experimental import pallas as pl
from jax.experimental.pallas import tpu as pltpu

F32 = jnp.float32
BF16 = jnp.bfloat16

D_MODEL = 2048
DEPTH = 2
HEAD_DIM = 128
H_A = 4
A_W = H_A * HEAD_DIM
H_B = 4
B_W = H_B * HEAD_DIM
H_IDX = 4
D_IDX = 64
TOPK_MAX = 256
CHUNK = 64
H_C = 16
N_C = 64
C_W = H_C * N_C
SCAN_CHUNK = 32
D_DECAY_LORA = 96
D_AAA_LORA = 96
D_GATE_LORA = 256
C_COLS = 3 * C_W + D_DECAY_LORA + D_AAA_LORA + D_GATE_LORA
C_PAD = 3584
C_TAIL = C_PAD - 3 * C_W
ATT_W = 3 * A_W + B_W + 2 * HEAD_DIM + H_IDX * D_IDX + 128
ROPE_THETA = 500000.0
ROT_FRACTION = 4
NORM_EPS = 1e-6
LNX_EPS = 64e-5
NEG_BIG = -1e30

VMEM_LIMIT = 56 * 1024 * 1024
INT_MIN = -(2 ** 31)
KEY_NEG_INF = (0xFF800000 ^ 0x7FFFFFFF) - 2 ** 32


def _params(*sem):
    return pltpu.CompilerParams(dimension_semantics=sem, vmem_limit_bytes=VMEM_LIMIT)


def _row_tile(n, pref):
    t = min(n, pref)
    assert n % t == 0, (n, t)
    return t


def _nt_dot(a, b):
    return lax.dot_general(a, b, (((1,), (1,)), ((), ())), preferred_element_type=F32)


def _rms(x, gain):
    xn = x * lax.rsqrt(jnp.mean(x * x, axis=-1, keepdims=True) + NORM_EPS)
    return xn * gain


def _mm_norm_kernel(x_ref, g_ref, w_ref, o_ref, xn_ref):
    @pl.when(pl.program_id(1) == 0)
    def _():
        xn_ref[...] = _rms(x_ref[...], g_ref[...]).astype(BF16)

    o_ref[...] = jnp.dot(xn_ref[...], w_ref[...], preferred_element_type=F32)


def mm_norm(x, gain, w, tn):
    n, k = x.shape
    nc = w.shape[1]
    tm = _row_tile(n, 1024)
    assert nc % tn == 0
    return pl.pallas_call(
        _mm_norm_kernel,
        grid=(n // tm, nc // tn),
        in_specs=[pl.BlockSpec((tm, k), lambda i, j: (i, 0)),
                  pl.BlockSpec((1, k), lambda i, j: (0, 0)),
                  pl.BlockSpec((k, tn), lambda i, j: (0, j))],
        out_specs=pl.BlockSpec((tm, tn), lambda i, j: (i, j)),
        out_shape=jax.ShapeDtypeStruct((n, nc), F32),
        scratch_shapes=[pltpu.VMEM((tm, k), BF16)],
        compiler_params=_params("parallel", "arbitrary"),
        name="mm_norm",
    )(x, gain.reshape(1, k), w)


def _mm_res_kernel(x_ref, w_ref, h_ref, o_ref):
    o_ref[...] = h_ref[...] + jnp.dot(x_ref[...], w_ref[...], preferred_element_type=F32)


def mm_res(x, w, h):
    n, k = x.shape
    nc = w.shape[1]
    tm = _row_tile(n, 1024)
    tn = 1024
    return pl.pallas_call(
        _mm_res_kernel,
        grid=(n // tm, nc // tn),
        in_specs=[pl.BlockSpec((tm, k), lambda i, j: (i, 0)),
                  pl.BlockSpec((k, tn), lambda i, j: (0, j)),
                  pl.BlockSpec((tm, tn), lambda i, j: (i, j))],
        out_specs=pl.BlockSpec((tm, tn), lambda i, j: (i, j)),
        out_shape=jax.ShapeDtypeStruct((n, nc), F32),
        compiler_params=_params("parallel", "parallel"),
        name="mm_res",
    )(x, w, h)


def _mm_gated3_kernel(oa_ref, ob_ref, oc_ref, wa_ref, wb_ref, wc_ref, ga_ref, gb_ref, gc_ref, o_ref):
    ya = jnp.dot(oa_ref[...], wa_ref[...], preferred_element_type=F32)
    yb = jnp.dot(ob_ref[...], wb_ref[...], preferred_element_type=F32)
    yc = jnp.dot(oc_ref[...], wc_ref[...], preferred_element_type=F32)
    m = (jax.nn.sigmoid(ga_ref[...]) * ya + jax.nn.sigmoid(gb_ref[...]) * yb
         + jax.nn.sigmoid(gc_ref[...]) * yc)
    o_ref[...] = m.astype(o_ref.dtype)


def mm_gated3(oa, ob, oc, wa, wb, wc, gates):
    n = oa.shape[0]
    tm = _row_tile(n, 512)
    tn = 512
    nj = D_MODEL // tn
    row = lambda w: pl.BlockSpec((tm, w), lambda i, j: (i, 0))
    col = lambda k: pl.BlockSpec((k, tn), lambda i, j: (0, j))
    gate = lambda s: pl.BlockSpec((tm, tn), lambda i, j: (i, j + s * nj))
    return pl.pallas_call(
        _mm_gated3_kernel,
        grid=(n // tm, nj),
        in_specs=[row(A_W), row(B_W), row(C_W), col(A_W), col(B_W), col(C_W), gate(0), gate(1), gate(2)],
        out_specs=pl.BlockSpec((tm, tn), lambda i, j: (i, j)),
        out_shape=jax.ShapeDtypeStruct((n, D_MODEL), BF16),
        compiler_params=_params("parallel", "parallel"),
        name="mm_gated3",
    )(oa, ob, oc, wa, wb, wc, gates, gates, gates)


def _ffn_kernel(h_ref, g_ref, wg_ref, wu_ref, wo_ref, o_ref, xn_ref, acc_ref):
    f = pl.program_id(1)

    @pl.when(f == 0)
    def _():
        xn_ref[...] = _rms(h_ref[...], g_ref[...]).astype(BF16)
        acc_ref[...] = jnp.zeros_like(acc_ref)

    xn = xn_ref[...]
    gate = jnp.dot(xn, wg_ref[...], preferred_element_type=F32)
    up = jnp.dot(xn, wu_ref[...], preferred_element_type=F32)
    act = (jax.nn.silu(gate) * up).astype(BF16)
    acc_ref[...] += jnp.dot(act, wo_ref[...], preferred_element_type=F32)

    @pl.when(f == pl.num_programs(1) - 1)
    def _():
        o_ref[...] = h_ref[...] + acc_ref[...]


def ffn(h, gain, w_in, w_out):
    n, d = h.shape
    dff = w_out.shape[0]
    tm = _row_tile(n, 512)
    tf = 512
    assert dff % tf == 0
    nf = dff // tf
    return pl.pallas_call(
        _ffn_kernel,
        grid=(n // tm, nf),
        in_specs=[pl.BlockSpec((tm, d), lambda i, f: (i, 0)),
                  pl.BlockSpec((1, d), lambda i, f: (0, 0)),
                  pl.BlockSpec((d, tf), lambda i, f: (0, f)),
                  pl.BlockSpec((d, tf), lambda i, f: (0, f + nf)),
                  pl.BlockSpec((tf, d), lambda i, f: (f, 0))],
        out_specs=pl.BlockSpec((tm, d), lambda i, f: (i, 0)),
        out_shape=jax.ShapeDtypeStruct((n, d), F32),
        scratch_shapes=[pltpu.VMEM((tm, d), BF16), pltpu.VMEM((tm, d), F32)],
        compiler_params=_params("parallel", "arbitrary"),
        name="ffn",
    )(h, gain.reshape(1, d), w_in, w_in, w_out)


def _ple_kernel(h_ref, g_ref, wg_ref, p_ref, wp_ref, hj_ref, o_ref, xn_ref):
    @pl.when(pl.program_id(1) == 0)
    def _():
        xn_ref[...] = _rms(h_ref[...], g_ref[...]).astype(BF16)

    gate = jnp.dot(xn_ref[...], wg_ref[...], preferred_element_type=F32)
    emb = jnp.dot(p_ref[...], wp_ref[...], preferred_element_type=F32)
    o_ref[...] = hj_ref[...] + jax.nn.sigmoid(gate) * emb


def ple(h, gain, w_gate, p, w_proj):
    n, d = h.shape
    pd = p.shape[1]
    tm = _row_tile(n, 1024)
    tn = 512
    return pl.pallas_call(
        _ple_kernel,
        grid=(n // tm, d // tn),
        in_specs=[pl.BlockSpec((tm, d), lambda i, j: (i, 0)),
                  pl.BlockSpec((1, d), lambda i, j: (0, 0)),
                  pl.BlockSpec((d, tn), lambda i, j: (0, j)),
                  pl.BlockSpec((tm, pd), lambda i, j: (i, 0)),
                  pl.BlockSpec((pd, tn), lambda i, j: (0, j)),
                  pl.BlockSpec((tm, tn), lambda i, j: (i, j))],
        out_specs=pl.BlockSpec((tm, tn), lambda i, j: (i, j)),
        out_shape=jax.ShapeDtypeStruct((n, d), F32),
        scratch_shapes=[pltpu.VMEM((tm, d), BF16)],
        compiler_params=_params("parallel", "arbitrary"),
        name="ple",
    )(h, gain.reshape(1, d), w_gate, p, w_proj, h)


def _rope_tables(pos, width, period, rot, scale_lanes=None):
    half = rot // 2
    inv = ROPE_THETA ** (-jnp.arange(half, dtype=F32) / half)
    ang = pos.astype(F32)[:, None] * inv[None, :]
    cos, sin = jnp.cos(ang), jnp.sin(ang)
    lane = jnp.arange(width) % period
    first, second = lane < half, (lane >= half) & (lane < rot)
    idx = lane % half
    c = jnp.where((first | second)[None, :], cos[:, idx], 1.0)
    s_up = jnp.where(first[None, :], -sin[:, idx], 0.0)
    s_dn = jnp.where(second[None, :], sin[:, idx], 0.0)
    if scale_lanes is not None:
        only, mult = scale_lanes
        c = jnp.where(only[None, :], c, mult[None, :])
        s_up = jnp.where(only[None, :], s_up, 0.0)
        s_dn = jnp.where(only[None, :], s_dn, 0.0)
    return jnp.stack([c, s_up, s_dn]).astype(F32)


def _rope(x, tab, half):
    return (x * tab[0] + pltpu.roll(x, x.shape[1] - half, 1) * tab[1] + pltpu.roll(x, half, 1) * tab[2])


def _post_in_kernel(z_ref, t128_ref, t64_ref, tki_ref, gaq_ref, gak_ref, gbq_ref, gbk_ref,
                    aq_ref, akf_ref, akb_ref, avf_ref, avb_ref, bq_ref, bkf_ref, bkb_ref,
                    bvf_ref, bvb_ref, qi_ref, kiw_ref, kilo_ref, kihi_ref):
    hd = HEAD_DIM
    t128 = t128_ref[...]
    t64 = t64_ref[...]
    half128 = hd // ROT_FRACTION // 2
    half64 = D_IDX // ROT_FRACTION // 2
    for h in range(H_A):
        sl = slice(h * hd, (h + 1) * hd)
        aq = _rms(z_ref[:, sl], gaq_ref[...]) * (-(hd ** -0.5))
        aq_ref[:, sl] = aq.astype(BF16)
        ak = _rms(z_ref[:, A_W + h * hd:A_W + (h + 1) * hd], gak_ref[...])
        akf_ref[:, h, :] = ak
        akb_ref[:, sl] = ak.astype(BF16)
        av = z_ref[:, 2 * A_W + h * hd:2 * A_W + (h + 1) * hd]
        avf_ref[:, h, :] = av
        avb_ref[:, sl] = av.astype(BF16)
    o = 3 * A_W
    for h in range(H_B):
        sl = slice(h * hd, (h + 1) * hd)
        bq = _rope(_rms(z_ref[:, o + h * hd:o + (h + 1) * hd], gbq_ref[...]), t128, half128) * (hd ** -0.5)
        bq_ref[:, sl] = bq.astype(BF16)
    o += B_W
    bk = _rope(_rms(z_ref[:, o:o + hd], gbk_ref[...]), t128, half128)
    bkf_ref[...] = bk
    bkb_ref[...] = bk.astype(BF16)
    o += hd
    bv = z_ref[:, o:o + hd]
    bvf_ref[...] = bv
    bvb_ref[...] = bv.astype(BF16)
    o += hd
    for c in range(H_IDX * D_IDX // 128):
        qi = _rope(z_ref[:, o + c * 128:o + (c + 1) * 128], t64, half64) * (D_IDX ** -0.5)
        qi_ref[:, c * 128:(c + 1) * 128] = qi.astype(BF16)
    o += H_IDX * D_IDX
    kiw = _rope(z_ref[:, o:o + 128], tki_ref[...], half64)
    kiw_ref[...] = kiw
    lane = lax.broadcasted_iota(jnp.int32, kiw.shape, 1)
    ki = jnp.where(lane < D_IDX, kiw, 0.0)
    kilo_ref[...] = ki.astype(BF16)
    kihi_ref[...] = pltpu.roll(ki, D_IDX, 1).astype(BF16)


def post_in(z, tabs, gains, t_len):
    n = z.shape[0]
    tm = _row_tile(t_len, 512)
    nt = t_len // tm
    rows = lambda w: pl.BlockSpec((tm, w), lambda i: (i, 0))
    tab = pl.BlockSpec((3, tm, 128), lambda i: (0, i % nt, 0))
    gain = pl.BlockSpec((1, HEAD_DIM), lambda i: (0, 0))
    per_head = pl.BlockSpec((tm, H_A, HEAD_DIM), lambda i: (i, 0, 0))
    widths = [(A_W, BF16), (A_W, F32), (A_W, BF16), (A_W, F32), (A_W, BF16), (B_W, BF16),
              (HEAD_DIM, F32), (HEAD_DIM, BF16), (HEAD_DIM, F32), (HEAD_DIM, BF16),
              (H_IDX * D_IDX, BF16), (128, F32), (128, BF16), (128, BF16)]
    return pl.pallas_call(
        _post_in_kernel,
        grid=(n // tm,),
        in_specs=[rows(ATT_W), tab, tab, tab, gain, gain, gain, gain],
        out_specs=[per_head if i in (1, 3) else rows(w) for i, (w, _) in enumerate(widths)],
        out_shape=[jax.ShapeDtypeStruct((n, H_A, HEAD_DIM) if i in (1, 3) else (n, w), dt)
                   for i, (w, dt) in enumerate(widths)],
        compiler_params=_params("parallel"),
        name="post_in",
    )(z, *tabs, *[g.reshape(1, HEAD_DIM) for g in gains])


LOG2E = 1.4426950408889634
ATTN_A_ROW_CHUNK = 256
ATTN_A_BLOCKS_PER_STEP = 4


def _attn_a_kernel(q_ref, k_ref, v_ref, tri_ref, o_ref, *, tq, tk, rc, past):
    i = pl.program_id(2)
    q0 = past + i * tq
    tri = tri_ref[...]
    n_full = q0 // tk
    n_all = (q0 + tq - 1 + tk - 1) // tk
    nrc = tq // rc
    qs = [q_ref[c * rc:(c + 1) * rc, :] for c in range(nrc)]
    rows = [q0 + c * rc + lax.broadcasted_iota(jnp.int32, (rc, tk), 0) for c in range(nrc)]
    col = lax.broadcasted_iota(jnp.int32, (rc, tk), 1)

    def step(j, carry, masked, nsub):
        cs = range(nrc)
        ss = range(nsub)
        offs = [pl.multiple_of((j - s) * tk, tk) for s in ss]
        kb = [k_ref[pl.ds(o, tk), :] for o in offs]
        vb = [v_ref[pl.ds(o, tk), :] for o in offs]
        later = [carry[0][c * rc:(c + 1) * rc] for c in cs]
        acc = [carry[1][c * rc:(c + 1) * rc] for c in cs]
        nz = [[_nt_dot(qs[c], kb[s]) for c in cs] for s in ss]
        loss = [[jnp.log(1.0 + jnp.exp2(jnp.minimum(nz[s][c] * (-LOG2E), 126.0))) for c in cs] for s in ss]
        if masked:
            causal = [[(offs[s] + col) < rows[c] for c in cs] for s in ss]
            loss = [[jnp.where(causal[s][c], loss[s][c], 0.0) for c in cs] for s in ss]
        within = [[jnp.dot(loss[s][c].astype(BF16), tri, preferred_element_type=F32) for c in cs] for s in ss]
        for s in ss:
            w = [jnp.exp2((nz[s][c] + later[c] + within[s][c]) * (-LOG2E)) for c in cs]
            if masked:
                w = [jnp.where(causal[s][c], w[c], 0.0) for c in cs]
            acc = [acc[c] + jnp.dot(w[c].astype(BF16), vb[s], preferred_element_type=F32) for c in cs]
            later = [later[c] + within[s][c][:, 0:1] for c in cs]
        return jnp.concatenate(later, axis=0), jnp.concatenate(acc, axis=0)

    carry = (jnp.zeros((tq, 1), F32), jnp.zeros((tq, HEAD_DIM), F32))
    carry = lax.fori_loop(0, n_all - n_full, lambda t, c: step(n_all - 1 - t, c, True, 1), carry)
    n_multi = n_full // ATTN_A_BLOCKS_PER_STEP
    n_single = n_full - n_multi * ATTN_A_BLOCKS_PER_STEP
    carry = lax.fori_loop(0, n_single, lambda t, c: step(n_full - 1 - t, c, False, 1), carry)
    carry = lax.fori_loop(
        0, n_multi,
        lambda t, c: step(n_full - n_single - 1 - t * ATTN_A_BLOCKS_PER_STEP, c, False, ATTN_A_BLOCKS_PER_STEP),
        carry)
    o_ref[...] = carry[1].astype(o_ref.dtype)


def attn_a(q, k, v, past, tk):
    b, t, _ = q.shape
    lp = k.shape[1]
    tq = _row_tile(t, 512)
    rc = min(tq, ATTN_A_ROW_CHUNK)
    tri = jnp.tri(tk, dtype=BF16)
    return pl.pallas_call(
        functools.partial(_attn_a_kernel, tq=tq, tk=tk, rc=rc, past=past),
        grid=(b, H_A, t // tq),
        in_specs=[pl.BlockSpec((None, tq, HEAD_DIM), lambda bi, h, i: (bi, i, h)),
                  pl.BlockSpec((None, lp, HEAD_DIM), lambda bi, h, i: (bi, 0, h)),
                  pl.BlockSpec((None, lp, HEAD_DIM), lambda bi, h, i: (bi, 0, h)),
                  pl.BlockSpec((tk, tk), lambda bi, h, i: (0, 0))],
        out_specs=pl.BlockSpec((None, tq, HEAD_DIM), lambda bi, h, i: (bi, i, h)),
        out_shape=jax.ShapeDtypeStruct((b, t, A_W), BF16),
        compiler_params=_params("parallel", "parallel", "parallel"),
        name="attn_a",
    )(q, k, v, tri)


def _dsa_kernel(qi_ref, kiw_ref, kilo_ref, kihi_ref, q_ref, k_ref, v_ref, triu_ref, o_ref, key_ref,
                *, tq, tk, past, n_keys, topk):
    i = pl.program_id(1)
    q0 = past + i * tq
    pos = q0 + lax.broadcasted_iota(jnp.int32, (tq, 1), 0)
    lim = jnp.minimum((pos // CHUNK + 1) * CHUNK, n_keys)
    n_adm = jnp.minimum(((q0 + tq - 1) // CHUNK + 1) * CHUNK, n_keys)
    nt = (n_adm + tk - 1) // tk
    col = lax.broadcasted_iota(jnp.int32, (tq, tk), 1)
    ngrp = tk // 128

    qi = qi_ref[...]
    kiw = kiw_ref[...]
    wi = [kiw[:, D_IDX + h:D_IDX + h + 1] for h in range(H_IDX)]

    def score_tile(t, _):
        off = pl.multiple_of(t * tk, tk)
        kis = (kilo_ref[pl.ds(off, tk), :], kihi_ref[pl.ds(off, tk), :])
        s = jnp.zeros((tq, tk), F32)
        for h in range(H_IDX):
            s_idx = _nt_dot(qi[:, (h // 2) * 128:(h // 2 + 1) * 128], kis[h % 2])
            s = s + wi[h] * jnp.maximum(s_idx, 0.0)
        s = jnp.where(s == 0.0, 0.0, s)
        s = jnp.where(off + col < lim, s, -jnp.inf)
        bits = lax.bitcast_convert_type(s, jnp.int32)
        key_ref[:, pl.ds(off, tk)] = bits ^ ((bits >> 31) & 0x7FFFFFFF)
        return 0

    lax.fori_loop(0, nt, score_tile, 0)

    def count_ge(cand):
        cand_b = jnp.broadcast_to(cand, (tq, 128))

        def body(t, acc):
            off = pl.multiple_of(t * tk, tk)
            kt = key_ref[:, pl.ds(off, tk)]
            for g in range(ngrp):
                acc = acc + jnp.where(kt[:, g * 128:(g + 1) * 128] >= cand_b, 1, 0)
            return acc

        acc = lax.fori_loop(0, nt, body, jnp.zeros((tq, 128), jnp.int32))
        return jnp.sum(acc, axis=1, keepdims=True)

    def bit_step(b, t_u):
        cand_u = t_u | lax.shift_left(jnp.int32(1), 31 - b)
        ok = count_ge(cand_u ^ INT_MIN) >= topk
        return jnp.where(ok, cand_u, t_u)

    t_u = lax.fori_loop(0, 32, bit_step, jnp.zeros((tq, 1), jnp.int32))
    tau = t_u ^ INT_MIN
    tau_next = jnp.where(tau == 2 ** 31 - 1, tau, tau + 1)
    cnt_gt = jnp.where(tau == 2 ** 31 - 1, 0, count_ge(tau_next))
    need = (topk - cnt_gt).astype(F32)

    triu = triu_ref[...]

    def select_tile(t, seen):
        off = pl.multiple_of(t * tk, tk)
        kt = key_ref[:, pl.ds(off, tk)]
        eq = kt == tau
        prefix = seen + jnp.dot(jnp.where(eq, 1.0, 0.0).astype(BF16), triu, preferred_element_type=F32)
        sel = (kt > tau) | (eq & (prefix <= need))
        sel = sel & (kt > KEY_NEG_INF)
        bias = jnp.where(sel, 0.0, NEG_BIG).astype(F32)
        key_ref[:, pl.ds(off, tk)] = lax.bitcast_convert_type(bias, jnp.int32)
        return prefix[:, tk - 1:tk]

    lax.fori_loop(0, nt, select_tile, jnp.zeros((tq, 1), F32))

    qs = jnp.concatenate([q_ref[:, h * HEAD_DIM:(h + 1) * HEAD_DIM] for h in range(H_B)], axis=0)

    def attend_tile(t, carry):
        m, l, acc = carry
        off = pl.multiple_of(t * tk, tk)
        kb = k_ref[pl.ds(off, tk), :]
        vb = v_ref[pl.ds(off, tk), :]
        bias = lax.bitcast_convert_type(key_ref[:, pl.ds(off, tk)], F32)
        logits = _nt_dot(qs, kb).reshape(H_B, tq, tk) + bias[None]
        logits = logits.reshape(H_B * tq, tk)
        m_new = jnp.maximum(m, jnp.max(logits, axis=1, keepdims=True))
        alpha = jnp.exp(m - m_new)
        p = jnp.exp(logits - m_new)
        l = alpha * l + jnp.sum(p, axis=1, keepdims=True)
        acc = alpha * acc + jnp.dot(p.astype(BF16), vb, preferred_element_type=F32)
        return m_new, l, acc

    init = (jnp.full((H_B * tq, 1), NEG_BIG, F32), jnp.zeros((H_B * tq, 1), F32),
            jnp.zeros((H_B * tq, HEAD_DIM), F32))
    _, l, acc = lax.fori_loop(0, nt, attend_tile, init)
    out = acc / l
    for h in range(H_B):
        o_ref[:, h * HEAD_DIM:(h + 1) * HEAD_DIM] = out[h * tq:(h + 1) * tq].astype(o_ref.dtype)


def dsa(qi, kiw, kilo, kihi, q, k, v, past, n_keys, tk):
    b, t, _ = q.shape
    lp = k.shape[1]
    tq = _row_tile(t, 128)
    topk = min(TOPK_MAX, n_keys // 4)
    assert lp % tk == 0 and lp >= topk
    triu = jnp.tri(tk, dtype=BF16).T
    qblk = lambda w: pl.BlockSpec((None, tq, w), lambda bi, i: (bi, i, 0))
    kblk = lambda w: pl.BlockSpec((None, lp, w), lambda bi, i: (bi, 0, 0))
    return pl.pallas_call(
        functools.partial(_dsa_kernel, tq=tq, tk=tk, past=past, n_keys=n_keys, topk=topk),
        grid=(b, t // tq),
        in_specs=[qblk(H_IDX * D_IDX), qblk(128), kblk(128), kblk(128), qblk(B_W),
                  kblk(HEAD_DIM), kblk(HEAD_DIM), pl.BlockSpec((tk, tk), lambda bi, i: (0, 0))],
        out_specs=qblk(B_W),
        out_shape=jax.ShapeDtypeStruct((b, t, B_W), BF16),
        scratch_shapes=[pltpu.VMEM((tq, lp), jnp.int32)],
        compiler_params=_params("parallel", "parallel"),
        name="dsa",
    )(qi, kiw, kilo, kihi, q, k, v, triu)


def _softplus(x):
    return jnp.maximum(x, 0.0) + jnp.log(1.0 + jnp.exp(-jnp.abs(x)))


def _c_prep_kernel(*refs, first_layer):
    if first_layer:
        (cz_ref, prev_ref, mu_ref, w0_ref, a0_ref, w2_ref, a2_ref, g2_ref, kk_ref, ka_ref,
         r_ref, kn_ref, k_ref, v_ref, lw_ref, a_ref, g_ref, vtok_ref) = refs
    else:
        (cz_ref, prev_ref, mu_ref, w0_ref, a0_ref, w2_ref, a2_ref, g2_ref, kk_ref, ka_ref,
         vf_ref, v0_ref, v1_ref, v2_ref,
         r_ref, kn_ref, k_ref, v_ref, lw_ref, a_ref, g_ref) = refs

    def put_heads(ref, x):
        for pair in range(H_C // 2):
            tile = x[:, pair * 128:(pair + 1) * 128]
            ref[2 * pair] = tile[:, :N_C]
            ref[2 * pair + 1] = pltpu.roll(tile, N_C, 1)[:, :N_C]

    cz = cz_ref[...]
    rowid = lax.broadcasted_iota(jnp.int32, cz.shape, 0)
    prev = jnp.where(rowid == 0, prev_ref[...], pltpu.roll(cz, 1, 0))
    zz = cz + (prev - cz) * mu_ref[...]
    r = zz[:, 0:C_W]
    k = zz[:, C_W:2 * C_W]
    v = zz[:, 2 * C_W:3 * C_W]
    tail = zz[:, 3 * C_W:C_PAD]
    w = w0_ref[...] + jnp.dot(jnp.tanh(tail).astype(BF16), w2_ref[...], preferred_element_type=F32)
    w = -_softplus(-w) - 0.5
    a = jax.nn.sigmoid(a0_ref[...] + jnp.dot(tail.astype(BF16), a2_ref[...], preferred_element_type=F32))
    g = jnp.dot(jax.nn.sigmoid(tail).astype(BF16), g2_ref[...], preferred_element_type=F32)
    if not first_layer:
        lora = jnp.dot(v.astype(BF16), v1_ref[...], preferred_element_type=F32)
        lora = jnp.dot(lora.astype(BF16), v2_ref[...], preferred_element_type=F32)
        v = v + (vf_ref[...] - v) * jax.nn.sigmoid(v0_ref[...] + lora)
    if first_layer:
        vtok_ref[...] = v
    put_heads(r_ref, r)
    put_heads(kn_ref, k * kk_ref[...])
    put_heads(k_ref, k * (1.0 + (a - 1.0) * ka_ref[...]))
    put_heads(v_ref, v)
    put_heads(lw_ref, -jnp.exp(w))
    put_heads(a_ref, a)
    put_heads(g_ref, g)


def c_prep(cz, prev_rows, vecs, mats, v_first, vres, tm):
    b, t, _ = cz.shape
    first_layer = vres is None
    blk = lambda w: pl.BlockSpec((None, tm, w), lambda bi, i: (bi, i, 0))
    vec = lambda w: pl.BlockSpec((1, w), lambda bi, i: (0, 0))
    mat = lambda m: pl.BlockSpec(m.shape, lambda bi, i: (0, 0))
    mu, w0, a0, k_k, k_a = vecs
    w2, a2, g2 = mats
    ins = [cz, prev_rows, mu, w0, a0, w2, a2, g2, k_k, k_a]
    specs = [blk(C_PAD), pl.BlockSpec((None, None, 1, C_PAD), lambda bi, i: (bi, i, 0, 0)),
             vec(C_PAD), vec(C_W), vec(C_W), mat(w2), mat(a2), mat(g2), vec(C_W), vec(C_W)]
    if not first_layer:
        v0, v1, v2 = vres
        ins += [v_first, v0, v1, v2]
        specs += [blk(C_W), vec(C_W), mat(v1), mat(v2)]
    heads = pl.BlockSpec((None, H_C, tm, N_C), lambda bi, i: (bi, 0, i, 0))
    out_specs = [heads] * 7
    out_shape = [jax.ShapeDtypeStruct((b, H_C, t, N_C), F32)] * 7
    if first_layer:
        out_specs.append(blk(C_W))
        out_shape.append(jax.ShapeDtypeStruct((b, t, C_W), F32))
    return pl.pallas_call(
        functools.partial(_c_prep_kernel, first_layer=first_layer),
        grid=(b, t // tm),
        in_specs=specs,
        out_specs=out_specs,
        out_shape=out_shape,
        compiler_params=_params("parallel", "parallel"),
        name="c_prep",
    )(*ins)


def _bdot(a, b, contract_a, contract_b):
    return lax.dot_general(a.astype(BF16), b.astype(BF16), (((contract_a,), (contract_b,)), ((0,), (0,))),
                           preferred_element_type=F32)


def _bdot3(a, b, contract_a, contract_b):
    ah = a.astype(BF16)
    bh = b.astype(BF16)
    al = (a - ah.astype(F32)).astype(BF16)
    bl = (b - bh.astype(F32)).astype(BF16)
    dn = (((contract_a,), (contract_b,)), ((0,), (0,)))
    dot = lambda x, y: lax.dot_general(x, y, dn, preferred_element_type=F32)
    return dot(ah, bh) + (dot(al, bh) + dot(ah, bl))


def _c_scan_kernel(r_ref, kn_ref, k_ref, v_ref, lw_ref, a_ref, g_ref, s0_ref, rk_ref, lng_ref, lnb_ref,
                   y_ref, st_ref, s_ref, *, nck):
    c_len = SCAN_CHUNK

    @pl.when(pl.program_id(1) == 0)
    def _():
        s_ref[...] = s0_ref[...]

    ti = lax.broadcasted_iota(jnp.int32, (H_C, c_len, c_len), 1)
    si = lax.broadcasted_iota(jnp.int32, (H_C, c_len, c_len), 2)
    incl = si <= ti
    strict = si < ti
    tri = jnp.where(incl, 1.0, 0.0)
    eye = jnp.where(si == ti, 1.0, 0.0)
    r_k = rk_ref[...]
    ln_g = lng_ref[...]
    ln_b = lnb_ref[...]

    def chunk(c, _):
        rows = pl.ds(pl.multiple_of(c * c_len, c_len), c_len)
        r = r_ref[:, rows, :]
        k = k_ref[:, rows, :]
        v = v_ref[:, rows, :]
        lw = lw_ref[:, rows, :]
        a = a_ref[:, rows, :]
        kn = kn_ref[:, rows, :]
        kn = kn * lax.rsqrt(jnp.maximum(jnp.sum(kn * kn, axis=-1, keepdims=True), 1e-24))
        aa = -kn
        bb = kn * a
        g = _bdot3(tri, lw, 2, 1)
        g_end = g[:, c_len - 1:c_len, :]
        a_t = aa * jnp.exp(g - lw)
        r_t = r * jnp.exp(g)
        inv = jnp.exp(-g)
        b_h = bb * inv
        k_h = k * inv
        to_end = jnp.exp(g_end - g)
        b_e = bb * to_end
        k_e = k * to_end
        d_end = jnp.exp(g_end)
        a_ab = jnp.where(strict, _bdot(a_t, b_h, 2, 2), 0.0)
        a_ak = jnp.where(strict, _bdot(a_t, k_h, 2, 2), 0.0)
        a_rb = jnp.where(incl, _bdot(r_t, b_h, 2, 2), 0.0)
        a_rk = jnp.where(incl, _bdot(r_t, k_h, 2, 2), 0.0)
        t_inv = eye + a_ab
        pw = a_ab
        for _ in range(4):
            pw = _bdot(pw, pw, 2, 1)
            t_inv = t_inv + _bdot(t_inv, pw, 2, 1)
        p_mat = _bdot(t_inv, a_t, 2, 1)
        q_mat = _bdot(t_inv, _bdot(a_ak, v, 2, 1), 2, 1)
        y_v = _bdot(a_rk, v, 2, 1)
        s = s_ref[...]
        u = _bdot(p_mat, s, 2, 2) + q_mat
        y = _bdot(r_t, s, 2, 2) + _bdot(a_rb, u, 2, 1) + y_v
        uv = jnp.concatenate([u, v], axis=1)
        bk = jnp.concatenate([b_e, k_e], axis=1)
        s_ref[...] = s * d_end + _bdot(uv, bk, 1, 1)
        mu = jnp.mean(y, axis=-1, keepdims=True)
        yc = y - mu
        yn = yc * lax.rsqrt(jnp.mean(yc * yc, axis=-1, keepdims=True) + LNX_EPS)
        yn = yn * ln_g + ln_b
        bonus = jnp.sum(r * k * r_k, axis=-1, keepdims=True) * v
        o = (yn + bonus) * g_ref[:, rows, :]
        y_ref[rows, :] = jnp.concatenate([o[h] for h in range(H_C)], axis=-1).astype(y_ref.dtype)
        return 0

    lax.fori_loop(0, nck, chunk, 0)

    @pl.when(pl.program_id(1) == pl.num_programs(1) - 1)
    def _():
        st_ref[...] = s_ref[...]


def c_scan(r, kn, k, v, lw, a, g, s0, r_k, ln_g, ln_b, tm):
    b, _, t, _ = r.shape
    blk = pl.BlockSpec((None, H_C, tm, N_C), lambda bi, i: (bi, 0, i, 0))
    st = pl.BlockSpec((None, H_C, N_C, N_C), lambda bi, i: (bi, 0, 0, 0))
    vec = pl.BlockSpec((H_C, 1, N_C), lambda bi, i: (0, 0, 0))
    return pl.pallas_call(
        functools.partial(_c_scan_kernel, nck=tm // SCAN_CHUNK),
        grid=(b, t // tm),
        in_specs=[blk] * 7 + [st, vec, vec, vec],
        out_specs=[pl.BlockSpec((None, tm, C_W), lambda bi, i: (bi, i, 0)), st],
        out_shape=[jax.ShapeDtypeStruct((b, t, C_W), BF16),
                   jax.ShapeDtypeStruct((b, H_C, N_C, N_C), F32)],
        scratch_shapes=[pltpu.VMEM((H_C, N_C, N_C), F32)],
        compiler_params=_params("parallel", "arbitrary"),
        name="c_scan",
    )(r, kn, k, v, lw, a, g, s0, r_k.reshape(H_C, 1, N_C), ln_g.reshape(H_C, 1, N_C), ln_b.reshape(H_C, 1, N_C))


def _pad_rows(x, rows, axis=1):
    pad = rows - x.shape[axis]
    if pad == 0:
        return x
    widths = [(0, 0)] * x.ndim
    widths[axis] = (0, pad)
    return jnp.pad(x, widths)


def _prep_weights(w):
    out = []
    for i in range(DEPTH):
        w_in = w["w_in"][i]
        o_c = 3 * A_W + B_W + 2 * HEAD_DIM + H_IDX * D_IDX + D_IDX + H_IDX
        o_g = o_c + C_COLS
        w_att = jnp.pad(w_in[:, :o_c], ((0, 0), (0, ATT_W - o_c)))
        w_c = jnp.pad(w_in[:, o_c:o_g], ((0, 0), (0, C_PAD - C_COLS)))
        w_g = w_in[:, o_g:]
        tail_rows = lambda m, off: jnp.pad(m, ((off, C_TAIL - off - m.shape[0]), (0, 0))).astype(BF16)
        lw = dict(
            w_att=w_att.astype(BF16), w_c=w_c.astype(BF16), w_g=w_g.astype(BF16),
            w2=tail_rows(w["c_w2"][i], 0),
            a2=tail_rows(w["c_a2"][i], D_DECAY_LORA),
            g2=tail_rows(w["c_g2"][i], D_DECAY_LORA + D_AAA_LORA),
            mu=jnp.pad(w["c_mu"][i], (0, C_PAD - C_COLS)).reshape(1, C_PAD),
            w_br_a=w["w_br_a"][i].astype(BF16), w_br_b=w["w_br_b"][i].astype(BF16),
            w_br_c=w["w_br_c"][i].astype(BF16), w_out=w["w_out"][i].astype(BF16),
            w_ffn_in=w["w_ffn_in"][i].astype(BF16), w_ffn_out=w["w_ffn_out"][i].astype(BF16),
            w_ple_gate=w["w_ple_gate"][i].astype(BF16), w_ple_proj=w["w_ple_proj"][i].astype(BF16),
        )
        if i > 0:
            lw["v1"] = w["c_v1"][i - 1].astype(BF16)
            lw["v2"] = w["c_v2"][i - 1].astype(BF16)
        out.append(lw)
    return out


def _run_trunk(x, p, past, caches, w, wl):
    b, t, _ = x.shape
    n = b * t
    n_keys = past + t
    pos = past + jnp.arange(t)
    lane = jnp.arange(128)
    tabs = (_rope_tables(pos, 128, HEAD_DIM, HEAD_DIM // ROT_FRACTION),
            _rope_tables(pos, 128, D_IDX, D_IDX // ROT_FRACTION),
            _rope_tables(pos, 128, 128, D_IDX // ROT_FRACTION,
                         scale_lanes=(lane < D_IDX, jnp.where(lane < D_IDX + H_IDX, H_IDX ** -0.5, 1.0))))
    tk_a = 256
    tk_b = 512
    lp_a = -(-n_keys // tk_a) * tk_a
    lp_b = -(-n_keys // tk_b) * tk_b
    t_c = -(-t // SCAN_CHUNK) * SCAN_CHUNK
    tm_c = _row_tile(t, 256)
    tm_s = _row_tile(t_c, 256)
    h = x.reshape(n, D_MODEL)
    outs = ([], [], [], [], [], [], [])
    v_first = None
    for i in range(DEPTH):
        lw = wl[i]
        z_att = mm_norm(h, w["norm_mix"][i], lw["w_att"], ATT_W // 3)
        z_c = mm_norm(h, w["norm_mix"][i], lw["w_c"], C_PAD // 4)
        z_g = mm_norm(h, w["norm_mix"][i], lw["w_g"], 1024)
        (aq, ak_f, ak_b, av_f, av_b, bq, bk_f, bk_b, bv_f, bv_b, qi, kiw, kilo, kihi) = post_in(
            z_att, tabs, (w["a_q_norm"][i], w["a_k_norm"][i], w["b_q_norm"][i], w["b_k_norm"][i]), t)
        seq = lambda a: a.reshape(b, t, a.shape[-1])

        def keys(new, old, lp):
            new = seq(new)
            if old is not None:
                new = jnp.concatenate([old.astype(BF16), new], axis=1)
            return _pad_rows(new, lp)

        c = caches
        past_of = lambda name: None if c is None else c[name][i]
        pa_k = None if c is None else c["a_k"][i].reshape(b, past, A_W)
        pa_v = None if c is None else c["a_v"][i].reshape(b, past, A_W)
        o_a = attn_a(seq(aq), keys(ak_b, pa_k, lp_a), keys(av_b, pa_v, lp_a), past, tk_a)
        if c is None:
            p_lo = p_hi = None
        else:
            p_lo = jnp.pad(c["b_ki"][i], ((0, 0), (0, 0), (0, 128 - D_IDX)))
            p_hi = jnp.pad(c["b_ki"][i], ((0, 0), (0, 0), (128 - D_IDX, 0)))
        o_b = dsa(seq(qi), seq(kiw), keys(kilo, p_lo, lp_b), keys(kihi, p_hi, lp_b), seq(bq),
                  keys(bk_b, past_of("b_k"), lp_b), keys(bv_b, past_of("b_v"), lp_b), past, n_keys, tk_b)
        cz = seq(z_c)
        shift0 = jnp.zeros((b, C_COLS), F32) if c is None else c["shift"][i]
        shift0 = jnp.pad(shift0, ((0, 0), (0, C_PAD - C_COLS)))
        prev_rows = jnp.concatenate([shift0[:, None, :], cz[:, tm_c - 1:t - 1:tm_c, :]], axis=1)
        prev_rows = prev_rows.reshape(b, t // tm_c, 1, C_PAD)
        row = lambda a: a.reshape(1, C_W)
        vres = None if i == 0 else (row(w["c_v0"][i - 1]), lw["v1"], lw["v2"])
        prep = c_prep(cz, prev_rows,
                      (lw["mu"], row(w["c_w0"][i]), row(w["c_a0"][i]), row(w["c_k_k"][i]), row(w["c_k_a"][i])),
                      (lw["w2"], lw["a2"], lw["g2"]), v_first, vres, tm_c)
        if i == 0:
            v_first = prep[7]
        s0 = jnp.zeros((b, H_C, N_C, N_C), F32) if c is None else c["wkv"][i]
        y_c, wkv_t = c_scan(*[_pad_rows(a, t_c, axis=2) for a in prep[:7]], s0,
                            w["c_r_k"][i], w["c_ln_g"][i], w["c_ln_b"][i], tm_s)
        o_c = y_c[:, :t].reshape(n, C_W)
        merged = mm_gated3(o_a.reshape(n, A_W), o_b.reshape(n, B_W), o_c,
                           lw["w_br_a"], lw["w_br_b"], lw["w_br_c"], z_g)
        h = mm_res(merged, lw["w_out"], h)
        h = ffn(h, w["norm_ffn"][i], lw["w_ffn_in"], lw["w_ffn_out"])
        h = ple(h, w["norm_ple"][i], lw["w_ple_gate"], p[i].reshape(n, -1).astype(BF16), lw["w_ple_proj"])
        vals = (ak_f.reshape(b, t, H_A, HEAD_DIM), av_f.reshape(b, t, H_A, HEAD_DIM),
                seq(bk_f), seq(bv_f), seq(kiw)[:, :, :D_IDX], wkv_t, cz[:, -1, :C_COLS])
        for lst, val in zip(outs, vals):
            lst.append(val)
    return h.reshape(b, t, D_MODEL), [jnp.stack(l) for l in outs]


def kernel(x_prompt, x_sample, cache_a_k, cache_a_v, cache_b_k, cache_b_v, cache_b_kidx, state_c_wkv, state_c_shift, p_prompt, p_sample, norm_mix, w_in, a_q_norm, a_k_norm, b_q_norm, b_k_norm, c_mu, c_w0, c_w2, c_a0, c_a2, c_g2, c_v0, c_v1, c_v2, c_k_k, c_k_a, c_r_k, c_ln_g, c_ln_b, w_br_a, w_br_b, w_br_c, w_out, norm_ffn, w_ffn_in, w_ffn_out, norm_ple, w_ple_gate, w_ple_proj):
    w = dict(norm_mix=norm_mix, w_in=w_in, a_q_norm=a_q_norm, a_k_norm=a_k_norm, b_q_norm=b_q_norm,
             b_k_norm=b_k_norm, c_mu=c_mu, c_w0=c_w0, c_w2=c_w2, c_a0=c_a0, c_a2=c_a2, c_g2=c_g2,
             c_v0=c_v0, c_v1=c_v1, c_v2=c_v2, c_k_k=c_k_k, c_k_a=c_k_a, c_r_k=c_r_k, c_ln_g=c_ln_g,
             c_ln_b=c_ln_b, w_br_a=w_br_a, w_br_b=w_br_b, w_br_c=w_br_c, w_out=w_out, norm_ffn=norm_ffn,
             w_ffn_in=w_ffn_in, w_ffn_out=w_ffn_out, norm_ple=norm_ple, w_ple_gate=w_ple_gate,
             w_ple_proj=w_ple_proj)
    wl = _prep_weights(w)
    y_p, o_p = _run_trunk(x_prompt, p_prompt, 0, None, w, wl)
    caches = dict(a_k=cache_a_k, a_v=cache_a_v, b_k=cache_b_k, b_v=cache_b_v, b_ki=cache_b_kidx,
                  wkv=state_c_wkv, shift=state_c_shift)
    y_s, o_s = _run_trunk(x_sample, p_sample, cache_a_k.shape[2], caches, w, wl)
    return (y_p, y_s, *o_p, *o_s)
```

```python
import functools
import math

import jax
import jax.numpy as jnp
from jax import lax
from jax.experimental import pallas as pl
from jax.experimental.pallas import tpu as pltpu

F32 = jnp.float32
BF16 = jnp.bfloat16

D_MODEL = 2048
DEPTH = 2
HEAD_DIM = 128
H_A = 4
A_W = H_A * HEAD_DIM
H_B = 4
B_W = H_B * HEAD_DIM
H_IDX = 4
D_IDX = 64
TOPK_MAX = 256
CHUNK = 64
H_C = 16
N_C = 64
C_W = H_C * N_C
SCAN_CHUNK = 32
D_DECAY_LORA = 96
D_AAA_LORA = 96
D_GATE_LORA = 256
C_COLS = 3 * C_W + D_DECAY_LORA + D_AAA_LORA + D_GATE_LORA
C_PAD = 3584
C_TAIL = C_PAD - 3 * C_W
ATT_W = 3 * A_W + B_W + 2 * HEAD_DIM + H_IDX * D_IDX + 128
ROPE_THETA = 500000.0
ROT_FRACTION = 4
NORM_EPS = 1e-6
LNX_EPS = 64e-5
NEG_BIG = -1e30

VMEM_LIMIT = 56 * 1024 * 1024
INT_MIN = -(2 ** 31)
KEY_NEG_INF = (0xFF800000 ^ 0x7FFFFFFF) - 2 ** 32


def _params(*sem):
    return pltpu.CompilerParams(dimension_semantics=sem, vmem_limit_bytes=VMEM_LIMIT)


def _row_tile(n, pref):
    t = min(n, pref)
    assert n % t == 0, (n, t)
    return t


def _nt_dot(a, b):
    return lax.dot_general(a, b, (((1,), (1,)), ((), ())), preferred_element_type=F32)


def _rms(x, gain):
    xn = x * lax.rsqrt(jnp.mean(x * x, axis=-1, keepdims=True) + NORM_EPS)
    return xn * gain


def _mm_norm_kernel(x_ref, g_ref, w_ref, o_ref, xn_ref):
    @pl.when(pl.program_id(1) == 0)
    def _():
        xn_ref[...] = _rms(x_ref[...], g_ref[...]).astype(BF16)

    o_ref[...] = jnp.dot(xn_ref[...], w_ref[...], preferred_element_type=F32)


def mm_norm(x, gain, w, tn):
    n, k = x.shape
    nc = w.shape[1]
    tm = _row_tile(n, 1024)
    assert nc % tn == 0
    return pl.pallas_call(
        _mm_norm_kernel,
        grid=(n // tm, nc // tn),
        in_specs=[pl.BlockSpec((tm, k), lambda i, j: (i, 0)),
                  pl.BlockSpec((1, k), lambda i, j: (0, 0)),
                  pl.BlockSpec((k, tn), lambda i, j: (0, j))],
        out_specs=pl.BlockSpec((tm, tn), lambda i, j: (i, j)),
        out_shape=jax.ShapeDtypeStruct((n, nc), F32),
        scratch_shapes=[pltpu.VMEM((tm, k), BF16)],
        compiler_params=_params("parallel", "arbitrary"),
        name="mm_norm",
    )(x, gain.reshape(1, k), w)


def _mm_res_kernel(x_ref, w_ref, h_ref, o_ref):
    o_ref[...] = h_ref[...] + jnp.dot(x_ref[...], w_ref[...], preferred_element_type=F32)


def mm_res(x, w, h):
    n, k = x.shape
    nc = w.shape[1]
    tm = _row_tile(n, 1024)
    tn = 1024
    return pl.pallas_call(
        _mm_res_kernel,
        grid=(n // tm, nc // tn),
        in_specs=[pl.BlockSpec((tm, k), lambda i, j: (i, 0)),
                  pl.BlockSpec((k, tn), lambda i, j: (0, j)),
                  pl.BlockSpec((tm, tn), lambda i, j: (i, j))],
        out_specs=pl.BlockSpec((tm, tn), lambda i, j: (i, j)),
        out_shape=jax.ShapeDtypeStruct((n, nc), F32),
        compiler_params=_params("parallel", "parallel"),
        name="mm_res",
    )(x, w, h)


def _mm_gated3_kernel(oa_ref, ob_ref, oc_ref, wa_ref, wb_ref, wc_ref, ga_ref, gb_ref, gc_ref, o_ref):
    ya = jnp.dot(oa_ref[...], wa_ref[...], preferred_element_type=F32)
    yb = jnp.dot(ob_ref[...], wb_ref[...], preferred_element_type=F32)
    yc = jnp.dot(oc_ref[...], wc_ref[...], preferred_element_type=F32)
    m = (jax.nn.sigmoid(ga_ref[...]) * ya + jax.nn.sigmoid(gb_ref[...]) * yb
         + jax.nn.sigmoid(gc_ref[...]) * yc)
    o_ref[...] = m.astype(o_ref.dtype)


def mm_gated3(oa, ob, oc, wa, wb, wc, gates):
    n = oa.shape[0]
    tm = _row_tile(n, 512)
    tn = 512
    nj = D_MODEL // tn
    row = lambda w: pl.BlockSpec((tm, w), lambda i, j: (i, 0))
    col = lambda k: pl.BlockSpec((k, tn), lambda i, j: (0, j))
    gate = lambda s: pl.BlockSpec((tm, tn), lambda i, j: (i, j + s * nj))
    return pl.pallas_call(
        _mm_gated3_kernel,
        grid=(n // tm, nj),
        in_specs=[row(A_W), row(B_W), row(C_W), col(A_W), col(B_W), col(C_W), gate(0), gate(1), gate(2)],
        out_specs=pl.BlockSpec((tm, tn), lambda i, j: (i, j)),
        out_shape=jax.ShapeDtypeStruct((n, D_MODEL), BF16),
        compiler_params=_params("parallel", "parallel"),
        name="mm_gated3",
    )(oa, ob, oc, wa, wb, wc, gates, gates, gates)


def _ffn_kernel(h_ref, g_ref, wg_ref, wu_ref, wo_ref, o_ref, xn_ref, acc_ref):
    f = pl.program_id(1)

    @pl.when(f == 0)
    def _():
        xn_ref[...] = _rms(h_ref[...], g_ref[...]).astype(BF16)
        acc_ref[...] = jnp.zeros_like(acc_ref)

    xn = xn_ref[...]
    gate = jnp.dot(xn, wg_ref[...], preferred_element_type=F32)
    up = jnp.dot(xn, wu_ref[...], preferred_element_type=F32)
    act = (jax.nn.silu(gate) * up).astype(BF16)
    acc_ref[...] += jnp.dot(act, wo_ref[...], preferred_element_type=F32)

    @pl.when(f == pl.num_programs(1) - 1)
    def _():
        o_ref[...] = h_ref[...] + acc_ref[...]


def ffn(h, gain, w_in, w_out):
    n, d = h.shape
    dff = w_out.shape[0]
    tm = _row_tile(n, 512)
    tf = 512
    assert dff % tf == 0
    nf = dff // tf
    return pl.pallas_call(
        _ffn_kernel,
        grid=(n // tm, nf),
        in_specs=[pl.BlockSpec((tm, d), lambda i, f: (i, 0)),
                  pl.BlockSpec((1, d), lambda i, f: (0, 0)),
                  pl.BlockSpec((d, tf), lambda i, f: (0, f)),
                  pl.BlockSpec((d, tf), lambda i, f: (0, f + nf)),
                  pl.BlockSpec((tf, d), lambda i, f: (f, 0))],
        out_specs=pl.BlockSpec((tm, d), lambda i, f: (i, 0)),
        out_shape=jax.ShapeDtypeStruct((n, d), F32),
        scratch_shapes=[pltpu.VMEM((tm, d), BF16), pltpu.VMEM((tm, d), F32)],
        compiler_params=_params("parallel", "arbitrary"),
        name="ffn",
    )(h, gain.reshape(1, d), w_in, w_in, w_out)


def _ple_kernel(h_ref, g_ref, wg_ref, p_ref, wp_ref, hj_ref, o_ref, xn_ref):
    @pl.when(pl.program_id(1) == 0)
    def _():
        xn_ref[...] = _rms(h_ref[...], g_ref[...]).astype(BF16)

    gate = jnp.dot(xn_ref[...], wg_ref[...], preferred_element_type=F32)
    emb = jnp.dot(p_ref[...], wp_ref[...], preferred_element_type=F32)
    o_ref[...] = hj_ref[...] + jax.nn.sigmoid(gate) * emb


def ple(h, gain, w_gate, p, w_proj):
    n, d = h.shape
    pd = p.shape[1]
    tm = _row_tile(n, 1024)
    tn = 512
    return pl.pallas_call(
        _ple_kernel,
        grid=(n // tm, d // tn),
        in_specs=[pl.BlockSpec((tm, d), lambda i, j: (i, 0)),
                  pl.BlockSpec((1, d), lambda i, j: (0, 0)),
                  pl.BlockSpec((d, tn), lambda i, j: (0, j)),
                  pl.BlockSpec((tm, pd), lambda i, j: (i, 0)),
                  pl.BlockSpec((pd, tn), lambda i, j: (0, j)),
                  pl.BlockSpec((tm, tn), lambda i, j: (i, j))],
        out_specs=pl.BlockSpec((tm, tn), lambda i, j: (i, j)),
        out_shape=jax.ShapeDtypeStruct((n, d), F32),
        scratch_shapes=[pltpu.VMEM((tm, d), BF16)],
        compiler_params=_params("parallel", "arbitrary"),
        name="ple",
    )(h, gain.reshape(1, d), w_gate, p, w_proj, h)


def _rope_tables(pos, width, period, rot, scale_lanes=None):
    half = rot // 2
    inv = ROPE_THETA ** (-jnp.arange(half, dtype=F32) / half)
    lane = jnp.arange(width) % period
    first, second = lane < half, (lane >= half) & (lane < rot)
    ang = pos.astype(F32)[:, None] * inv[lane % half][None, :]
    cos, sin = jnp.cos(ang), jnp.sin(ang)
    c = jnp.where((first | second)[None, :], cos, 1.0)
    s_up = jnp.where(first[None, :], -sin, 0.0)
    s_dn = jnp.where(second[None, :], sin, 0.0)
    if scale_lanes is not None:
        only, mult = scale_lanes
        c = jnp.where(only[None, :], c, mult[None, :])
        s_up = jnp.where(only[None, :], s_up, 0.0)
        s_dn = jnp.where(only[None, :], s_dn, 0.0)
    return jnp.stack([c, s_up, s_dn]).astype(F32)


def _rope(x, tab, half):
    return (x * tab[0] + pltpu.roll(x, x.shape[1] - half, 1) * tab[1] + pltpu.roll(x, half, 1) * tab[2])


def _post_in_kernel(z_ref, t128_ref, t64_ref, tki_ref, gaq_ref, gak_ref, gbq_ref, gbk_ref,
                    aq_ref, akf_ref, akb_ref, avf_ref, avb_ref, bq_ref, bkf_ref, bkb_ref,
                    bvf_ref, bvb_ref, qi_ref, kiw_ref, kilo_ref, kihi_ref):
    hd = HEAD_DIM
    t128 = t128_ref[...]
    t64 = t64_ref[...]
    half128 = hd // ROT_FRACTION // 2
    half64 = D_IDX // ROT_FRACTION // 2
    for h in range(H_A):
        sl = slice(h * hd, (h + 1) * hd)
        aq = _rms(z_ref[:, sl], gaq_ref[...]) * (-(hd ** -0.5))
        aq_ref[:, sl] = aq.astype(BF16)
        ak = _rms(z_ref[:, A_W + h * hd:A_W + (h + 1) * hd], gak_ref[...])
        akf_ref[:, h, :] = ak
        akb_ref[:, sl] = ak.astype(BF16)
        av = z_ref[:, 2 * A_W + h * hd:2 * A_W + (h + 1) * hd]
        avf_ref[:, h, :] = av
        avb_ref[:, sl] = av.astype(BF16)
    o = 3 * A_W
    for h in range(H_B):
        sl = slice(h * hd, (h + 1) * hd)
        bq = _rope(_rms(z_ref[:, o + h * hd:o + (h + 1) * hd], gbq_ref[...]), t128, half128) * (hd ** -0.5)
        bq_ref[:, sl] = bq.astype(BF16)
    o += B_W
    bk = _rope(_rms(z_ref[:, o:o + hd], gbk_ref[...]), t128, half128)
    bkf_ref[...] = bk
    bkb_ref[...] = bk.astype(BF16)
    o += hd
    bv = z_ref[:, o:o + hd]
    bvf_ref[...] = bv
    bvb_ref[...] = bv.astype(BF16)
    o += hd
    for c in range(H_IDX * D_IDX // 128):
        qi = _rope(z_ref[:, o + c * 128:o + (c + 1) * 128], t64, half64) * (D_IDX ** -0.5)
        qi_ref[:, c * 128:(c + 1) * 128] = qi.astype(BF16)
    o += H_IDX * D_IDX
    kiw = _rope(z_ref[:, o:o + 128], tki_ref[...], half64)
    kiw_ref[...] = kiw
    lane = lax.broadcasted_iota(jnp.int32, kiw.shape, 1)
    ki = jnp.where(lane < D_IDX, kiw, 0.0)
    kilo_ref[...] = ki.astype(BF16)
    kihi_ref[...] = pltpu.roll(ki, D_IDX, 1).astype(BF16)


def post_in(z, tabs, gains, t_len):
    n = z.shape[0]
    tm = _row_tile(t_len, 512)
    nt = t_len // tm
    rows = lambda w: pl.BlockSpec((tm, w), lambda i: (i, 0))
    tab = pl.BlockSpec((3, tm, 128), lambda i: (0, i % nt, 0))
    gain = pl.BlockSpec((1, HEAD_DIM), lambda i: (0, 0))
    per_head = pl.BlockSpec((tm, H_A, HEAD_DIM), lambda i: (i, 0, 0))
    widths = [(A_W, BF16), (A_W, F32), (A_W, BF16), (A_W, F32), (A_W, BF16), (B_W, BF16),
              (HEAD_DIM, F32), (HEAD_DIM, BF16), (HEAD_DIM, F32), (HEAD_DIM, BF16),
              (H_IDX * D_IDX, BF16), (128, F32), (128, BF16), (128, BF16)]
    return pl.pallas_call(
        _post_in_kernel,
        grid=(n // tm,),
        in_specs=[rows(ATT_W), tab, tab, tab, gain, gain, gain, gain],
        out_specs=[per_head if i in (1, 3) else rows(w) for i, (w, _) in enumerate(widths)],
        out_shape=[jax.ShapeDtypeStruct((n, H_A, HEAD_DIM) if i in (1, 3) else (n, w), dt)
                   for i, (w, dt) in enumerate(widths)],
        compiler_params=_params("parallel"),
        name="post_in",
    )(z, *tabs, *[g.reshape(1, HEAD_DIM) for g in gains])


LOG2E = 1.4426950408889634
ATTN_A_ROW_CHUNK = 256
ATTN_A_BLOCKS_PER_STEP = 4


def _attn_a_kernel(q_ref, k_ref, v_ref, tri_ref, o_ref, *, tq, tk, rc, past):
    i = pl.program_id(2)
    q0 = past + i * tq
    tri = tri_ref[...]
    n_full = q0 // tk
    n_all = (q0 + tq - 1 + tk - 1) // tk
    nrc = tq // rc
    qs = [q_ref[c * rc:(c + 1) * rc, :] for c in range(nrc)]
    rows = [q0 + c * rc + lax.broadcasted_iota(jnp.int32, (rc, tk), 0) for c in range(nrc)]
    col = lax.broadcasted_iota(jnp.int32, (rc, tk), 1)

    def step(j, carry, masked, nsub):
        cs = range(nrc)
        ss = range(nsub)
        offs = [pl.multiple_of((j - s) * tk, tk) for s in ss]
        kb = [k_ref[pl.ds(o, tk), :] for o in offs]
        vb = [v_ref[pl.ds(o, tk), :] for o in offs]
        later = [carry[0][c * rc:(c + 1) * rc] for c in cs]
        acc = [carry[1][c * rc:(c + 1) * rc] for c in cs]
        nz = [[_nt_dot(qs[c], kb[s]) for c in cs] for s in ss]
        loss = [[jnp.log(1.0 + jnp.exp2(jnp.minimum(nz[s][c] * (-LOG2E), 126.0))) for c in cs] for s in ss]
        if masked:
            causal = [[(offs[s] + col) < rows[c] for c in cs] for s in ss]
            loss = [[jnp.where(causal[s][c], loss[s][c], 0.0) for c in cs] for s in ss]
        within = [[jnp.dot(loss[s][c].astype(BF16), tri, preferred_element_type=F32) for c in cs] for s in ss]
        for s in ss:
            w = [jnp.exp2((nz[s][c] + later[c] + within[s][c]) * (-LOG2E)) for c in cs]
            if masked:
                w = [jnp.where(causal[s][c], w[c], 0.0) for c in cs]
            acc = [acc[c] + jnp.dot(w[c].astype(BF16), vb[s], preferred_element_type=F32) for c in cs]
            later = [later[c] + within[s][c][:, 0:1] for c in cs]
        return jnp.concatenate(later, axis=0), jnp.concatenate(acc, axis=0)

    carry = (jnp.zeros((tq, 1), F32), jnp.zeros((tq, HEAD_DIM), F32))
    carry = lax.fori_loop(0, n_all - n_full, lambda t, c: step(n_all - 1 - t, c, True, 1), carry)
    n_multi = n_full // ATTN_A_BLOCKS_PER_STEP
    n_single = n_full - n_multi * ATTN_A_BLOCKS_PER_STEP
    carry = lax.fori_loop(0, n_single, lambda t, c: step(n_full - 1 - t, c, False, 1), carry)
    carry = lax.fori_loop(
        0, n_multi,
        lambda t, c: step(n_full - n_single - 1 - t * ATTN_A_BLOCKS_PER_STEP, c, False, ATTN_A_BLOCKS_PER_STEP),
        carry)
    o_ref[...] = carry[1].astype(o_ref.dtype)


def attn_a(q, k, v, past, tk):
    b, t, _ = q.shape
    lp = k.shape[1]
    tq = _row_tile(t, 512)
    rc = min(tq, ATTN_A_ROW_CHUNK)
    tri = jnp.tri(tk, dtype=BF16)
    return pl.pallas_call(
        functools.partial(_attn_a_kernel, tq=tq, tk=tk, rc=rc, past=past),
        grid=(b, H_A, t // tq),
        in_specs=[pl.BlockSpec((None, tq, HEAD_DIM), lambda bi, h, i: (bi, i, h)),
                  pl.BlockSpec((None, lp, HEAD_DIM), lambda bi, h, i: (bi, 0, h)),
                  pl.BlockSpec((None, lp, HEAD_DIM), lambda bi, h, i: (bi, 0, h)),
                  pl.BlockSpec((tk, tk), lambda bi, h, i: (0, 0))],
        out_specs=pl.BlockSpec((None, tq, HEAD_DIM), lambda bi, h, i: (bi, i, h)),
        out_shape=jax.ShapeDtypeStruct((b, t, A_W), BF16),
        compiler_params=_params("parallel", "parallel", "parallel"),
        name="attn_a",
    )(q, k, v, tri)


def _dsa_kernel(qi_ref, kiw_ref, kilo_ref, kihi_ref, q_ref, k_ref, v_ref, triu_ref, o_ref, key_ref,
                *, tq, tk, past, n_keys, topk):
    i = pl.program_id(1)
    q0 = past + i * tq
    pos = q0 + lax.broadcasted_iota(jnp.int32, (tq, 1), 0)
    lim = jnp.minimum((pos // CHUNK + 1) * CHUNK, n_keys)
    n_adm = jnp.minimum(((q0 + tq - 1) // CHUNK + 1) * CHUNK, n_keys)
    nt = (n_adm + tk - 1) // tk
    col = lax.broadcasted_iota(jnp.int32, (tq, tk), 1)
    ngrp = tk // 128

    qi = qi_ref[...]
    kiw = kiw_ref[...]
    wi = [kiw[:, D_IDX + h:D_IDX + h + 1] for h in range(H_IDX)]

    n_pair = (nt + 1) // 2
    pair = range(2)

    def score_pair(t, _):
        offs = [pl.multiple_of((2 * t + u) * tk, tk) for u in pair]
        kis = [(kilo_ref[pl.ds(o, tk), :], kihi_ref[pl.ds(o, tk), :]) for o in offs]
        s_idx = [[_nt_dot(qi[:, (h // 2) * 128:(h // 2 + 1) * 128], kis[u][h % 2]) for h in range(H_IDX)]
                 for u in pair]
        for u in pair:
            s = jnp.zeros((tq, tk), F32)
            for h in range(H_IDX):
                s = s + wi[h] * jnp.maximum(s_idx[u][h], 0.0)
            s = jnp.where(s == 0.0, 0.0, s)
            s = jnp.where(offs[u] + col < lim, s, -jnp.inf)
            bits = lax.bitcast_convert_type(s, jnp.int32)
            key_ref[:, pl.ds(offs[u], tk)] = bits ^ ((bits >> 31) & 0x7FFFFFFF)
        return 0

    lax.fori_loop(0, n_pair, score_pair, 0)

    def count_ge(cand):
        cand_b = jnp.broadcast_to(cand, (tq, 128))

        def body(t, acc):
            off = pl.multiple_of(t * tk, tk)
            kt = key_ref[:, pl.ds(off, tk)]
            for g in range(ngrp):
                acc = acc + jnp.where(kt[:, g * 128:(g + 1) * 128] >= cand_b, 1, 0)
            return acc

        acc = lax.fori_loop(0, nt, body, jnp.zeros((tq, 128), jnp.int32))
        return jnp.sum(acc, axis=1, keepdims=True)

    def bit_step(b, t_u):
        cand_u = t_u | lax.shift_left(jnp.int32(1), 31 - b)
        ok = count_ge(cand_u ^ INT_MIN) >= topk
        return jnp.where(ok, cand_u, t_u)

    t_u = lax.fori_loop(0, 32, bit_step, jnp.zeros((tq, 1), jnp.int32))
    tau = t_u ^ INT_MIN
    tau_next = jnp.where(tau == 2 ** 31 - 1, tau, tau + 1)
    cnt_gt = jnp.where(tau == 2 ** 31 - 1, 0, count_ge(tau_next))
    need = (topk - cnt_gt).astype(F32)

    triu = triu_ref[...]

    def select_tile(t, seen):
        off = pl.multiple_of(t * tk, tk)
        kt = key_ref[:, pl.ds(off, tk)]
        eq = kt == tau
        prefix = seen + jnp.dot(jnp.where(eq, 1.0, 0.0).astype(BF16), triu, preferred_element_type=F32)
        sel = (kt > tau) | (eq & (prefix <= need))
        sel = sel & (kt > KEY_NEG_INF)
        bias = jnp.where(sel, 0.0, NEG_BIG).astype(F32)
        key_ref[:, pl.ds(off, tk)] = lax.bitcast_convert_type(bias, jnp.int32)
        return prefix[:, tk - 1:tk]

    lax.fori_loop(0, 2 * n_pair, select_tile, jnp.zeros((tq, 1), F32))

    qs = jnp.concatenate([q_ref[:, h * HEAD_DIM:(h + 1) * HEAD_DIM] for h in range(H_B)], axis=0)

    def attend_pair(t, carry):
        m, l, acc = carry
        offs = [pl.multiple_of((2 * t + u) * tk, tk) for u in pair]
        kb = [k_ref[pl.ds(o, tk), :] for o in offs]
        vb = [v_ref[pl.ds(o, tk), :] for o in offs]
        bias = [lax.bitcast_convert_type(key_ref[:, pl.ds(o, tk)], F32) for o in offs]
        logits = [(_nt_dot(qs, kb[u]).reshape(H_B, tq, tk) + bias[u][None]).reshape(H_B * tq, tk) for u in pair]
        m_new = jnp.maximum(m, jnp.maximum(jnp.max(logits[0], axis=1, keepdims=True),
                                           jnp.max(logits[1], axis=1, keepdims=True)))
        alpha = jnp.exp(m - m_new)
        p = [jnp.exp(logits[u] - m_new) for u in pair]
        l = alpha * l + (jnp.sum(p[0], axis=1, keepdims=True) + jnp.sum(p[1], axis=1, keepdims=True))
        acc = alpha * acc + (jnp.dot(p[0].astype(BF16), vb[0], preferred_element_type=F32)
                             + jnp.dot(p[1].astype(BF16), vb[1], preferred_element_type=F32))
        return m_new, l, acc

    init = (jnp.full((H_B * tq, 1), NEG_BIG, F32), jnp.zeros((H_B * tq, 1), F32),
            jnp.zeros((H_B * tq, HEAD_DIM), F32))
    _, l, acc = lax.fori_loop(0, n_pair, attend_pair, init)
    out = acc / l
    for h in range(H_B):
        o_ref[:, h * HEAD_DIM:(h + 1) * HEAD_DIM] = out[h * tq:(h + 1) * tq].astype(o_ref.dtype)


def dsa(qi, kiw, kilo, kihi, q, k, v, past, n_keys, tk):
    b, t, _ = q.shape
    lp = k.shape[1]
    tq = _row_tile(t, 128)
    topk = min(TOPK_MAX, n_keys // 4)
    assert lp % (2 * tk) == 0 and lp >= topk
    triu = jnp.tri(tk, dtype=BF16).T
    qblk = lambda w: pl.BlockSpec((None, tq, w), lambda bi, i: (bi, i, 0))
    kblk = lambda w: pl.BlockSpec((None, lp, w), lambda bi, i: (bi, 0, 0))
    return pl.pallas_call(
        functools.partial(_dsa_kernel, tq=tq, tk=tk, past=past, n_keys=n_keys, topk=topk),
        grid=(b, t // tq),
        in_specs=[qblk(H_IDX * D_IDX), qblk(128), kblk(128), kblk(128), qblk(B_W),
                  kblk(HEAD_DIM), kblk(HEAD_DIM), pl.BlockSpec((tk, tk), lambda bi, i: (0, 0))],
        out_specs=qblk(B_W),
        out_shape=jax.ShapeDtypeStruct((b, t, B_W), BF16),
        scratch_shapes=[pltpu.VMEM((tq, lp), jnp.int32)],
        compiler_params=_params("parallel", "parallel"),
        name="dsa",
    )(qi, kiw, kilo, kihi, q, k, v, triu)


def _softplus(x):
    return jnp.maximum(x, 0.0) + jnp.log(1.0 + jnp.exp(-jnp.abs(x)))


def _c_prep_kernel(*refs, first_layer):
    if first_layer:
        (cz_ref, prev_ref, mu_ref, w0_ref, a0_ref, w2_ref, a2_ref, g2_ref, kk_ref, ka_ref,
         r_ref, kn_ref, k_ref, v_ref, lw_ref, a_ref, g_ref, vtok_ref) = refs
    else:
        (cz_ref, prev_ref, mu_ref, w0_ref, a0_ref, w2_ref, a2_ref, g2_ref, kk_ref, ka_ref,
         vf_ref, v0_ref, v1_ref, v2_ref,
         r_ref, kn_ref, k_ref, v_ref, lw_ref, a_ref, g_ref) = refs

    def put_heads(ref, x):
        for pair in range(H_C // 2):
            tile = x[:, pair * 128:(pair + 1) * 128]
            ref[2 * pair] = tile[:, :N_C]
            ref[2 * pair + 1] = pltpu.roll(tile, N_C, 1)[:, :N_C]

    cz = cz_ref[...]
    rowid = lax.broadcasted_iota(jnp.int32, cz.shape, 0)
    prev = jnp.where(rowid == 0, prev_ref[...], pltpu.roll(cz, 1, 0))
    zz = cz + (prev - cz) * mu_ref[...]
    r = zz[:, 0:C_W]
    k = zz[:, C_W:2 * C_W]
    v = zz[:, 2 * C_W:3 * C_W]
    tail = zz[:, 3 * C_W:C_PAD]
    w = w0_ref[...] + jnp.dot(jnp.tanh(tail).astype(BF16), w2_ref[...], preferred_element_type=F32)
    w = -_softplus(-w) - 0.5
    a = jax.nn.sigmoid(a0_ref[...] + jnp.dot(tail.astype(BF16), a2_ref[...], preferred_element_type=F32))
    g = jnp.dot(jax.nn.sigmoid(tail).astype(BF16), g2_ref[...], preferred_element_type=F32)
    if not first_layer:
        lora = jnp.dot(v.astype(BF16), v1_ref[...], preferred_element_type=F32)
        lora = jnp.dot(lora.astype(BF16), v2_ref[...], preferred_element_type=F32)
        v = v + (vf_ref[...] - v) * jax.nn.sigmoid(v0_ref[...] + lora)
    if first_layer:
        vtok_ref[...] = v
    put_heads(r_ref, r)
    put_heads(kn_ref, k * kk_ref[...])
    put_heads(k_ref, k * (1.0 + (a - 1.0) * ka_ref[...]))
    put_heads(v_ref, v)
    put_heads(lw_ref, -jnp.exp(w))
    put_heads(a_ref, a)
    put_heads(g_ref, g)


def c_prep(cz, prev_rows, vecs, mats, v_first, vres, tm):
    b, t, _ = cz.shape
    first_layer = vres is None
    blk = lambda w: pl.BlockSpec((None, tm, w), lambda bi, i: (bi, i, 0))
    vec = lambda w: pl.BlockSpec((1, w), lambda bi, i: (0, 0))
    mat = lambda m: pl.BlockSpec(m.shape, lambda bi, i: (0, 0))
    mu, w0, a0, k_k, k_a = vecs
    w2, a2, g2 = mats
    ins = [cz, prev_rows, mu, w0, a0, w2, a2, g2, k_k, k_a]
    specs = [blk(C_PAD), pl.BlockSpec((None, None, 1, C_PAD), lambda bi, i: (bi, i, 0, 0)),
             vec(C_PAD), vec(C_W), vec(C_W), mat(w2), mat(a2), mat(g2), vec(C_W), vec(C_W)]
    if not first_layer:
        v0, v1, v2 = vres
        ins += [v_first, v0, v1, v2]
        specs += [blk(C_W), vec(C_W), mat(v1), mat(v2)]
    heads = pl.BlockSpec((None, H_C, tm, N_C), lambda bi, i: (bi, 0, i, 0))
    out_specs = [heads] * 7
    out_shape = [jax.ShapeDtypeStruct((b, H_C, t, N_C), F32)] * 7
    if first_layer:
        out_specs.append(blk(C_W))
        out_shape.append(jax.ShapeDtypeStruct((b, t, C_W), F32))
    return pl.pallas_call(
        functools.partial(_c_prep_kernel, first_layer=first_layer),
        grid=(b, t // tm),
        in_specs=specs,
        out_specs=out_specs,
        out_shape=out_shape,
        compiler_params=_params("parallel", "parallel"),
        name="c_prep",
    )(*ins)


def _bdot(a, b, contract_a, contract_b):
    return lax.dot_general(a.astype(BF16), b.astype(BF16), (((contract_a,), (contract_b,)), ((0,), (0,))),
                           preferred_element_type=F32)


def _bdot3(a, b, contract_a, contract_b):
    ah = a.astype(BF16)
    bh = b.astype(BF16)
    al = (a - ah.astype(F32)).astype(BF16)
    bl = (b - bh.astype(F32)).astype(BF16)
    dn = (((contract_a,), (contract_b,)), ((0,), (0,)))
    dot = lambda x, y: lax.dot_general(x, y, dn, preferred_element_type=F32)
    return dot(ah, bh) + (dot(al, bh) + dot(ah, bl))


def _c_scan_kernel(r_ref, kn_ref, k_ref, v_ref, lw_ref, a_ref, g_ref, s0_ref, rk_ref, lng_ref, lnb_ref,
                   y_ref, st_ref, s_ref, *, nck):
    c_len = SCAN_CHUNK

    @pl.when(pl.program_id(1) == 0)
    def _():
        s_ref[...] = s0_ref[...]

    ti = lax.broadcasted_iota(jnp.int32, (H_C, c_len, c_len), 1)
    si = lax.broadcasted_iota(jnp.int32, (H_C, c_len, c_len), 2)
    incl = si <= ti
    strict = si < ti
    tri = jnp.where(incl, 1.0, 0.0)
    eye = jnp.where(si == ti, 1.0, 0.0)
    r_k = rk_ref[...]
    ln_g = lng_ref[...]
    ln_b = lnb_ref[...]

    def chunk(c, _):
        rows = pl.ds(pl.multiple_of(c * c_len, c_len), c_len)
        r = r_ref[:, rows, :]
        k = k_ref[:, rows, :]
        v = v_ref[:, rows, :]
        lw = lw_ref[:, rows, :]
        a = a_ref[:, rows, :]
        kn = kn_ref[:, rows, :]
        kn = kn * lax.rsqrt(jnp.maximum(jnp.sum(kn * kn, axis=-1, keepdims=True), 1e-24))
        aa = -kn
        bb = kn * a
        g = _bdot3(tri, lw, 2, 1)
        g_end = g[:, c_len - 1:c_len, :]
        a_t = aa * jnp.exp(g - lw)
        r_t = r * jnp.exp(g)
        inv = jnp.exp(-g)
        b_h = bb * inv
        k_h = k * inv
        to_end = jnp.exp(g_end - g)
        b_e = bb * to_end
        k_e = k * to_end
        d_end = jnp.exp(g_end)
        a_ab = jnp.where(strict, _bdot(a_t, b_h, 2, 2), 0.0)
        a_ak = jnp.where(strict, _bdot(a_t, k_h, 2, 2), 0.0)
        a_rb = jnp.where(incl, _bdot(r_t, b_h, 2, 2), 0.0)
        a_rk = jnp.where(incl, _bdot(r_t, k_h, 2, 2), 0.0)
        t_inv = eye + a_ab
        pw = a_ab
        for _ in range(4):
            pw = _bdot(pw, pw, 2, 1)
            t_inv = t_inv + _bdot(t_inv, pw, 2, 1)
        p_mat = _bdot(t_inv, a_t, 2, 1)
        q_mat = _bdot(t_inv, _bdot(a_ak, v, 2, 1), 2, 1)
        y_v = _bdot(a_rk, v, 2, 1)
        s = s_ref[...]
        u = _bdot(p_mat, s, 2, 2) + q_mat
        y = _bdot(r_t, s, 2, 2) + _bdot(a_rb, u, 2, 1) + y_v
        uv = jnp.concatenate([u, v], axis=1)
        bk = jnp.concatenate([b_e, k_e], axis=1)
        s_ref[...] = s * d_end + _bdot(uv, bk, 1, 1)
        mu = jnp.mean(y, axis=-1, keepdims=True)
        yc = y - mu
        yn = yc * lax.rsqrt(jnp.mean(yc * yc, axis=-1, keepdims=True) + LNX_EPS)
        yn = yn * ln_g + ln_b
        bonus = jnp.sum(r * k * r_k, axis=-1, keepdims=True) * v
        o = (yn + bonus) * g_ref[:, rows, :]
        y_ref[rows, :] = jnp.concatenate([o[h] for h in range(H_C)], axis=-1).astype(y_ref.dtype)
        return 0

    lax.fori_loop(0, nck, chunk, 0)

    @pl.when(pl.program_id(1) == pl.num_programs(1) - 1)
    def _():
        st_ref[...] = s_ref[...]


def c_scan(r, kn, k, v, lw, a, g, s0, r_k, ln_g, ln_b, tm):
    b, _, t, _ = r.shape
    blk = pl.BlockSpec((None, H_C, tm, N_C), lambda bi, i: (bi, 0, i, 0))
    st = pl.BlockSpec((None, H_C, N_C, N_C), lambda bi, i: (bi, 0, 0, 0))
    vec = pl.BlockSpec((H_C, 1, N_C), lambda bi, i: (0, 0, 0))
    return pl.pallas_call(
        functools.partial(_c_scan_kernel, nck=tm // SCAN_CHUNK),
        grid=(b, t // tm),
        in_specs=[blk] * 7 + [st, vec, vec, vec],
        out_specs=[pl.BlockSpec((None, tm, C_W), lambda bi, i: (bi, i, 0)), st],
        out_shape=[jax.ShapeDtypeStruct((b, t, C_W), BF16),
                   jax.ShapeDtypeStruct((b, H_C, N_C, N_C), F32)],
        scratch_shapes=[pltpu.VMEM((H_C, N_C, N_C), F32)],
        compiler_params=_params("parallel", "arbitrary"),
        name="c_scan",
    )(r, kn, k, v, lw, a, g, s0, r_k.reshape(H_C, 1, N_C), ln_g.reshape(H_C, 1, N_C), ln_b.reshape(H_C, 1, N_C))


def _pad_rows(x, rows, axis=1):
    pad = rows - x.shape[axis]
    if pad == 0:
        return x
    widths = [(0, 0)] * x.ndim
    widths[axis] = (0, pad)
    return jnp.pad(x, widths)


ATT_COLS = 3 * A_W + B_W + 2 * HEAD_DIM + H_IDX * D_IDX + D_IDX + H_IDX
IN_COLS = ATT_COLS + C_COLS + 3 * D_MODEL


def _regroup_kernel(w_ref, tail_ref, att_ref, c_ref, g_ref):
    lane = lax.broadcasted_iota(jnp.int32, (w_ref.shape[0], 128), 1)
    n_full = IN_COLS // 128

    def tile(j):
        return tail_ref[...] if j == n_full else w_ref[:, j * 128:(j + 1) * 128]

    def shifted(j0, shift, j):
        joined = jnp.where(lane >= shift, tile(j0 + j), tile(j0 + j + 1))
        return pltpu.roll(joined, 128 - shift, 1)

    for j in range(ATT_W // 128):
        x = tile(j)
        if (j + 1) * 128 > ATT_COLS:
            x = jnp.where(lane < ATT_COLS - j * 128, x, 0.0)
        att_ref[:, j * 128:(j + 1) * 128] = x.astype(BF16)
    for j in range(C_PAD // 128):
        x = shifted(ATT_COLS // 128, ATT_COLS % 128, j)
        if (j + 1) * 128 > C_COLS:
            x = jnp.where(lane < C_COLS - j * 128, x, 0.0)
        c_ref[:, j * 128:(j + 1) * 128] = x.astype(BF16)
    g0 = ATT_COLS + C_COLS
    for j in range(3 * D_MODEL // 128):
        g_ref[:, j * 128:(j + 1) * 128] = shifted(g0 // 128, g0 % 128, j).astype(BF16)


def regroup_w_in(w_in):
    depth, d, cols = w_in.shape
    assert cols == IN_COLS and (ATT_COLS + C_COLS) % 128 != 0
    n_full = cols // 128
    tail = jnp.pad(w_in[:, :, n_full * 128:], ((0, 0), (0, 0), (0, (n_full + 1) * 128 - cols)))
    tm = 128
    blk = lambda w: pl.BlockSpec((None, tm, w), lambda l, i: (l, i, 0))
    return pl.pallas_call(
        _regroup_kernel,
        grid=(depth, d // tm),
        in_specs=[blk(cols), blk(128)],
        out_specs=[blk(ATT_W), blk(C_PAD), blk(3 * D_MODEL)],
        out_shape=[jax.ShapeDtypeStruct((depth, d, w), BF16) for w in (ATT_W, C_PAD, 3 * D_MODEL)],
        compiler_params=_params("parallel", "parallel"),
        name="regroup_w_in",
    )(w_in, tail)


def _prep_weights(w):
    out = []
    w_att, w_c, w_g = regroup_w_in(w["w_in"])
    for i in range(DEPTH):
        tail_rows = lambda m, off: jnp.pad(m, ((off, C_TAIL - off - m.shape[0]), (0, 0))).astype(BF16)
        lw = dict(
            w_att=w_att[i], w_c=w_c[i], w_g=w_g[i],
            w2=tail_rows(w["c_w2"][i], 0),
            a2=tail_rows(w["c_a2"][i], D_DECAY_LORA),
            g2=tail_rows(w["c_g2"][i], D_DECAY_LORA + D_AAA_LORA),
            mu=jnp.pad(w["c_mu"][i], (0, C_PAD - C_COLS)).reshape(1, C_PAD),
            w_br_a=w["w_br_a"][i].astype(BF16), w_br_b=w["w_br_b"][i].astype(BF16),
            w_br_c=w["w_br_c"][i].astype(BF16), w_out=w["w_out"][i].astype(BF16),
            w_ffn_in=w["w_ffn_in"][i].astype(BF16), w_ffn_out=w["w_ffn_out"][i].astype(BF16),
            w_ple_gate=w["w_ple_gate"][i].astype(BF16), w_ple_proj=w["w_ple_proj"][i].astype(BF16),
        )
        if i > 0:
            lw["v1"] = w["c_v1"][i - 1].astype(BF16)
            lw["v2"] = w["c_v2"][i - 1].astype(BF16)
        out.append(lw)
    return out


def _run_trunk(x, p, past, caches, w, wl):
    b, t, _ = x.shape
    n = b * t
    n_keys = past + t
    pos = past + jnp.arange(t)
    lane = jnp.arange(128)
    tabs = (_rope_tables(pos, 128, HEAD_DIM, HEAD_DIM // ROT_FRACTION),
            _rope_tables(pos, 128, D_IDX, D_IDX // ROT_FRACTION),
            _rope_tables(pos, 128, 128, D_IDX // ROT_FRACTION,
                         scale_lanes=(lane < D_IDX, jnp.where(lane < D_IDX + H_IDX, H_IDX ** -0.5, 1.0))))
    tk_a = 256
    tk_b = 512
    lp_a = -(-n_keys // tk_a) * tk_a
    lp_b = -(-n_keys // (2 * tk_b)) * (2 * tk_b)
    t_c = -(-t // SCAN_CHUNK) * SCAN_CHUNK
    tm_c = _row_tile(t, 256)
    tm_s = _row_tile(t_c, 256)
    h = x.reshape(n, D_MODEL)
    outs = ([], [], [], [], [], [], [])
    v_first = None
    for i in range(DEPTH):
        lw = wl[i]
        z_att = mm_norm(h, w["norm_mix"][i], lw["w_att"], ATT_W // 3)
        z_c = mm_norm(h, w["norm_mix"][i], lw["w_c"], C_PAD // 4)
        z_g = mm_norm(h, w["norm_mix"][i], lw["w_g"], 1024)
        (aq, ak_f, ak_b, av_f, av_b, bq, bk_f, bk_b, bv_f, bv_b, qi, kiw, kilo, kihi) = post_in(
            z_att, tabs, (w["a_q_norm"][i], w["a_k_norm"][i], w["b_q_norm"][i], w["b_k_norm"][i]), t)
        seq = lambda a: a.reshape(b, t, a.shape[-1])

        def keys(new, old, lp):
            new = seq(new)
            if old is not None:
                new = jnp.concatenate([old.astype(BF16), new], axis=1)
            return _pad_rows(new, lp)

        c = caches
        past_of = lambda name: None if c is None else c[name][i]
        pa_k = None if c is None else c["a_k"][i].reshape(b, past, A_W)
        pa_v = None if c is None else c["a_v"][i].reshape(b, past, A_W)
        o_a = attn_a(seq(aq), keys(ak_b, pa_k, lp_a), keys(av_b, pa_v, lp_a), past, tk_a)
        if c is None:
            p_lo = p_hi = None
        else:
            p_lo = jnp.pad(c["b_ki"][i], ((0, 0), (0, 0), (0, 128 - D_IDX)))
            p_hi = jnp.pad(c["b_ki"][i], ((0, 0), (0, 0), (128 - D_IDX, 0)))
        o_b = dsa(seq(qi), seq(kiw), keys(kilo, p_lo, lp_b), keys(kihi, p_hi, lp_b), seq(bq),
                  keys(bk_b, past_of("b_k"), lp_b), keys(bv_b, past_of("b_v"), lp_b), past, n_keys, tk_b)
        cz = seq(z_c)
        shift0 = jnp.zeros((b, C_COLS), F32) if c is None else c["shift"][i]
        shift0 = jnp.pad(shift0, ((0, 0), (0, C_PAD - C_COLS)))
        prev_rows = jnp.concatenate([shift0[:, None, :], cz[:, tm_c - 1:t - 1:tm_c, :]], axis=1)
        prev_rows = prev_rows.reshape(b, t // tm_c, 1, C_PAD)
        row = lambda a: a.reshape(1, C_W)
        vres = None if i == 0 else (row(w["c_v0"][i - 1]), lw["v1"], lw["v2"])
        prep = c_prep(cz, prev_rows,
                      (lw["mu"], row(w["c_w0"][i]), row(w["c_a0"][i]), row(w["c_k_k"][i]), row(w["c_k_a"][i])),
                      (lw["w2"], lw["a2"], lw["g2"]), v_first, vres, tm_c)
        if i == 0:
            v_first = prep[7]
        s0 = jnp.zeros((b, H_C, N_C, N_C), F32) if c is None else c["wkv"][i]
        y_c, wkv_t = c_scan(*[_pad_rows(a, t_c, axis=2) for a in prep[:7]], s0,
                            w["c_r_k"][i], w["c_ln_g"][i], w["c_ln_b"][i], tm_s)
        o_c = y_c[:, :t].reshape(n, C_W)
        merged = mm_gated3(o_a.reshape(n, A_W), o_b.reshape(n, B_W), o_c,
                           lw["w_br_a"], lw["w_br_b"], lw["w_br_c"], z_g)
        h = mm_res(merged, lw["w_out"], h)
        h = ffn(h, w["norm_ffn"][i], lw["w_ffn_in"], lw["w_ffn_out"])
        h = ple(h, w["norm_ple"][i], lw["w_ple_gate"], p[i].reshape(n, -1).astype(BF16), lw["w_ple_proj"])
        vals = (ak_f.reshape(b, t, H_A, HEAD_DIM), av_f.reshape(b, t, H_A, HEAD_DIM),
                seq(bk_f), seq(bv_f), seq(kiw)[:, :, :D_IDX], wkv_t, cz[:, -1, :C_COLS])
        for lst, val in zip(outs, vals):
            lst.append(val)
    return h.reshape(b, t, D_MODEL), [jnp.stack(l) for l in outs]


def kernel(x_prompt, x_sample, cache_a_k, cache_a_v, cache_b_k, cache_b_v, cache_b_kidx, state_c_wkv, state_c_shift, p_prompt, p_sample, norm_mix, w_in, a_q_norm, a_k_norm, b_q_norm, b_k_norm, c_mu, c_w0, c_w2, c_a0, c_a2, c_g2, c_v0, c_v1, c_v2, c_k_k, c_k_a, c_r_k, c_ln_g, c_ln_b, w_br_a, w_br_b, w_br_c, w_out, norm_ffn, w_ffn_in, w_ffn_out, norm_ple, w_ple_gate, w_ple_proj):
    w = dict(norm_mix=norm_mix, w_in=w_in, a_q_norm=a_q_norm, a_k_norm=a_k_norm, b_q_norm=b_q_norm,
             b_k_norm=b_k_norm, c_mu=c_mu, c_w0=c_w0, c_w2=c_w2, c_a0=c_a0, c_a2=c_a2, c_g2=c_g2,
             c_v0=c_v0, c_v1=c_v1, c_v2=c_v2, c_k_k=c_k_k, c_k_a=c_k_a, c_r_k=c_r_k, c_ln_g=c_ln_g,
             c_ln_b=c_ln_b, w_br_a=w_br_a, w_br_b=w_br_b, w_br_c=w_br_c, w_out=w_out, norm_ffn=norm_ffn,
             w_ffn_in=w_ffn_in, w_ffn_out=w_ffn_out, norm_ple=norm_ple, w_ple_gate=w_ple_gate,
             w_ple_proj=w_ple_proj)
    wl = _prep_weights(w)
    y_p, o_p = _run_trunk(x_prompt, p_prompt, 0, None, w, wl)
    caches = dict(a_k=cache_a_k, a_v=cache_a_v, b_k=cache_b_k, b_v=cache_b_v, b_ki=cache_b_kidx,
                  wkv=state_c_wkv, shift=state_c_shift)
    y_s, o_s = _run_trunk(x_sample, p_sample, cache_a_k.shape[2], caches, w, wl)
    return (y_p, y_s, *o_p, *o_s)
```

```python
import functools
import math

import jax
import jax.numpy as jnp
from jax import lax
from jax.experimental import pallas as pl
from jax.experimental.pallas import tpu as pltpu

F32 = jnp.float32
BF16 = jnp.bfloat16

D_MODEL = 2048
DEPTH = 2
HEAD_DIM = 128
H_A = 4
A_W = H_A * HEAD_DIM
H_B = 4
B_W = H_B * HEAD_DIM
H_IDX = 4
D_IDX = 64
TOPK_MAX = 256
CHUNK = 64
H_C = 16
N_C = 64
C_W = H_C * N_C
SCAN_CHUNK = 32
D_DECAY_LORA = 96
D_AAA_LORA = 96
D_GATE_LORA = 256
C_COLS = 3 * C_W + D_DECAY_LORA + D_AAA_LORA + D_GATE_LORA
C_PAD = 3584
C_TAIL = C_PAD - 3 * C_W
ATT_W = 3 * A_W + B_W + 2 * HEAD_DIM + H_IDX * D_IDX + 128
ROPE_THETA = 500000.0
ROT_FRACTION = 4
NORM_EPS = 1e-6
LNX_EPS = 64e-5
NEG_BIG = -1e30

VMEM_LIMIT = 56 * 1024 * 1024
INT_MIN = -(2 ** 31)
KEY_NEG_INF = (0xFF800000 ^ 0x7FFFFFFF) - 2 ** 32


def _params(*sem):
    return pltpu.CompilerParams(dimension_semantics=sem, vmem_limit_bytes=VMEM_LIMIT)


def _row_tile(n, pref):
    t = min(n, pref)
    assert n % t == 0, (n, t)
    return t


def _nt_dot(a, b):
    return lax.dot_general(a, b, (((1,), (1,)), ((), ())), preferred_element_type=F32)


def _rms(x, gain):
    xn = x * lax.rsqrt(jnp.mean(x * x, axis=-1, keepdims=True) + NORM_EPS)
    return xn * gain


def _mm_norm_kernel(x_ref, g_ref, w_ref, o_ref, xn_ref):
    @pl.when(pl.program_id(1) == 0)
    def _():
        xn_ref[...] = _rms(x_ref[...], g_ref[...]).astype(BF16)

    o_ref[...] = jnp.dot(xn_ref[...], w_ref[...], preferred_element_type=F32)


def mm_norm(x, gain, w, tn):
    n, k = x.shape
    nc = w.shape[1]
    tm = _row_tile(n, 1024)
    assert nc % tn == 0
    return pl.pallas_call(
        _mm_norm_kernel,
        grid=(n // tm, nc // tn),
        in_specs=[pl.BlockSpec((tm, k), lambda i, j: (i, 0)),
                  pl.BlockSpec((1, k), lambda i, j: (0, 0)),
                  pl.BlockSpec((k, tn), lambda i, j: (0, j))],
        out_specs=pl.BlockSpec((tm, tn), lambda i, j: (i, j)),
        out_shape=jax.ShapeDtypeStruct((n, nc), F32),
        scratch_shapes=[pltpu.VMEM((tm, k), BF16)],
        compiler_params=_params("parallel", "arbitrary"),
        name="mm_norm",
    )(x, gain.reshape(1, k), w)


def _mm_res_kernel(x_ref, w_ref, h_ref, o_ref):
    o_ref[...] = h_ref[...] + jnp.dot(x_ref[...], w_ref[...], preferred_element_type=F32)


def mm_res(x, w, h):
    n, k = x.shape
    nc = w.shape[1]
    tm = _row_tile(n, 1024)
    tn = 1024
    return pl.pallas_call(
        _mm_res_kernel,
        grid=(n // tm, nc // tn),
        in_specs=[pl.BlockSpec((tm, k), lambda i, j: (i, 0)),
                  pl.BlockSpec((k, tn), lambda i, j: (0, j)),
                  pl.BlockSpec((tm, tn), lambda i, j: (i, j))],
        out_specs=pl.BlockSpec((tm, tn), lambda i, j: (i, j)),
        out_shape=jax.ShapeDtypeStruct((n, nc), F32),
        compiler_params=_params("parallel", "parallel"),
        name="mm_res",
    )(x, w, h)


def _mm_gated3_kernel(oa_ref, ob_ref, oc_ref, wa_ref, wb_ref, wc_ref, ga_ref, gb_ref, gc_ref, o_ref):
    ya = jnp.dot(oa_ref[...], wa_ref[...], preferred_element_type=F32)
    yb = jnp.dot(ob_ref[...], wb_ref[...], preferred_element_type=F32)
    yc = jnp.dot(oc_ref[...], wc_ref[...], preferred_element_type=F32)
    m = (jax.nn.sigmoid(ga_ref[...]) * ya + jax.nn.sigmoid(gb_ref[...]) * yb
         + jax.nn.sigmoid(gc_ref[...]) * yc)
    o_ref[...] = m.astype(o_ref.dtype)


def mm_gated3(oa, ob, oc, wa, wb, wc, gates):
    n = oa.shape[0]
    tm = _row_tile(n, 512)
    tn = 512
    nj = D_MODEL // tn
    row = lambda w: pl.BlockSpec((tm, w), lambda i, j: (i, 0))
    col = lambda k: pl.BlockSpec((k, tn), lambda i, j: (0, j))
    gate = lambda s: pl.BlockSpec((tm, tn), lambda i, j: (i, j + s * nj))
    return pl.pallas_call(
        _mm_gated3_kernel,
        grid=(n // tm, nj),
        in_specs=[row(A_W), row(B_W), row(C_W), col(A_W), col(B_W), col(C_W), gate(0), gate(1), gate(2)],
        out_specs=pl.BlockSpec((tm, tn), lambda i, j: (i, j)),
        out_shape=jax.ShapeDtypeStruct((n, D_MODEL), BF16),
        compiler_params=_params("parallel", "parallel"),
        name="mm_gated3",
    )(oa, ob, oc, wa, wb, wc, gates, gates, gates)


def _ffn_kernel(h_ref, g_ref, wg_ref, wu_ref, wo_ref, o_ref, xn_ref, acc_ref):
    f = pl.program_id(1)

    @pl.when(f == 0)
    def _():
        xn_ref[...] = _rms(h_ref[...], g_ref[...]).astype(BF16)
        acc_ref[...] = jnp.zeros_like(acc_ref)

    xn = xn_ref[...]
    gate = jnp.dot(xn, wg_ref[...], preferred_element_type=F32)
    up = jnp.dot(xn, wu_ref[...], preferred_element_type=F32)
    act = (jax.nn.silu(gate) * up).astype(BF16)
    acc_ref[...] += jnp.dot(act, wo_ref[...], preferred_element_type=F32)

    @pl.when(f == pl.num_programs(1) - 1)
    def _():
        o_ref[...] = h_ref[...] + acc_ref[...]


def ffn(h, gain, w_in, w_out):
    n, d = h.shape
    dff = w_out.shape[0]
    tm = _row_tile(n, 512)
    tf = 512
    assert dff % tf == 0
    nf = dff // tf
    return pl.pallas_call(
        _ffn_kernel,
        grid=(n // tm, nf),
        in_specs=[pl.BlockSpec((tm, d), lambda i, f: (i, 0)),
                  pl.BlockSpec((1, d), lambda i, f: (0, 0)),
                  pl.BlockSpec((d, tf), lambda i, f: (0, f)),
                  pl.BlockSpec((d, tf), lambda i, f: (0, f + nf)),
                  pl.BlockSpec((tf, d), lambda i, f: (f, 0))],
        out_specs=pl.BlockSpec((tm, d), lambda i, f: (i, 0)),
        out_shape=jax.ShapeDtypeStruct((n, d), F32),
        scratch_shapes=[pltpu.VMEM((tm, d), BF16), pltpu.VMEM((tm, d), F32)],
        compiler_params=_params("parallel", "arbitrary"),
        name="ffn",
    )(h, gain.reshape(1, d), w_in, w_in, w_out)


def _ple_kernel(h_ref, g_ref, wg_ref, p_ref, wp_ref, hj_ref, o_ref, xn_ref):
    @pl.when(pl.program_id(1) == 0)
    def _():
        xn_ref[...] = _rms(h_ref[...], g_ref[...]).astype(BF16)

    gate = jnp.dot(xn_ref[...], wg_ref[...], preferred_element_type=F32)
    emb = jnp.dot(p_ref[...], wp_ref[...], preferred_element_type=F32)
    o_ref[...] = hj_ref[...] + jax.nn.sigmoid(gate) * emb


def ple(h, gain, w_gate, p, w_proj):
    n, d = h.shape
    pd = p.shape[1]
    tm = _row_tile(n, 1024)
    tn = 512
    return pl.pallas_call(
        _ple_kernel,
        grid=(n // tm, d // tn),
        in_specs=[pl.BlockSpec((tm, d), lambda i, j: (i, 0)),
                  pl.BlockSpec((1, d), lambda i, j: (0, 0)),
                  pl.BlockSpec((d, tn), lambda i, j: (0, j)),
                  pl.BlockSpec((tm, pd), lambda i, j: (i, 0)),
                  pl.BlockSpec((pd, tn), lambda i, j: (0, j)),
                  pl.BlockSpec((tm, tn), lambda i, j: (i, j))],
        out_specs=pl.BlockSpec((tm, tn), lambda i, j: (i, j)),
        out_shape=jax.ShapeDtypeStruct((n, d), F32),
        scratch_shapes=[pltpu.VMEM((tm, d), BF16)],
        compiler_params=_params("parallel", "arbitrary"),
        name="ple",
    )(h, gain.reshape(1, d), w_gate, p, w_proj, h)


def _rope_tables(pos, width, period, rot, scale_lanes=None):
    half = rot // 2
    inv = ROPE_THETA ** (-jnp.arange(half, dtype=F32) / half)
    lane = jnp.arange(width) % period
    first, second = lane < half, (lane >= half) & (lane < rot)
    ang = pos.astype(F32)[:, None] * inv[lane % half][None, :]
    cos, sin = jnp.cos(ang), jnp.sin(ang)
    c = jnp.where((first | second)[None, :], cos, 1.0)
    s_up = jnp.where(first[None, :], -sin, 0.0)
    s_dn = jnp.where(second[None, :], sin, 0.0)
    if scale_lanes is not None:
        only, mult = scale_lanes
        c = jnp.where(only[None, :], c, mult[None, :])
        s_up = jnp.where(only[None, :], s_up, 0.0)
        s_dn = jnp.where(only[None, :], s_dn, 0.0)
    return jnp.stack([c, s_up, s_dn]).astype(F32)


def _rope(x, tab, half):
    return (x * tab[0] + pltpu.roll(x, x.shape[1] - half, 1) * tab[1] + pltpu.roll(x, half, 1) * tab[2])


def _post_in_kernel(z_ref, t128_ref, t64_ref, tki_ref, gaq_ref, gak_ref, gbq_ref, gbk_ref,
                    aq_ref, akf_ref, akb_ref, avf_ref, avb_ref, bq_ref, bkf_ref, bkb_ref,
                    bvf_ref, bvb_ref, qi_ref, kiw_ref, kilo_ref, kihi_ref):
    hd = HEAD_DIM
    t128 = t128_ref[...]
    t64 = t64_ref[...]
    half128 = hd // ROT_FRACTION // 2
    half64 = D_IDX // ROT_FRACTION // 2
    for h in range(H_A):
        sl = slice(h * hd, (h + 1) * hd)
        aq = _rms(z_ref[:, sl], gaq_ref[...]) * (-(hd ** -0.5))
        aq_ref[:, sl] = aq.astype(BF16)
        ak = _rms(z_ref[:, A_W + h * hd:A_W + (h + 1) * hd], gak_ref[...])
        akf_ref[:, h, :] = ak
        akb_ref[:, sl] = ak.astype(BF16)
        av = z_ref[:, 2 * A_W + h * hd:2 * A_W + (h + 1) * hd]
        avf_ref[:, h, :] = av
        avb_ref[:, sl] = av.astype(BF16)
    o = 3 * A_W
    for h in range(H_B):
        sl = slice(h * hd, (h + 1) * hd)
        bq = _rope(_rms(z_ref[:, o + h * hd:o + (h + 1) * hd], gbq_ref[...]), t128, half128) * (hd ** -0.5)
        bq_ref[:, sl] = bq.astype(BF16)
    o += B_W
    bk = _rope(_rms(z_ref[:, o:o + hd], gbk_ref[...]), t128, half128)
    bkf_ref[...] = bk
    bkb_ref[...] = bk.astype(BF16)
    o += hd
    bv = z_ref[:, o:o + hd]
    bvf_ref[...] = bv
    bvb_ref[...] = bv.astype(BF16)
    o += hd
    for c in range(H_IDX * D_IDX // 128):
        qi = _rope(z_ref[:, o + c * 128:o + (c + 1) * 128], t64, half64) * (D_IDX ** -0.5)
        qi_ref[:, c * 128:(c + 1) * 128] = qi.astype(BF16)
    o += H_IDX * D_IDX
    kiw = _rope(z_ref[:, o:o + 128], tki_ref[...], half64)
    kiw_ref[...] = kiw
    lane = lax.broadcasted_iota(jnp.int32, kiw.shape, 1)
    ki = jnp.where(lane < D_IDX, kiw, 0.0)
    kilo_ref[...] = ki.astype(BF16)
    kihi_ref[...] = pltpu.roll(ki, D_IDX, 1).astype(BF16)


def post_in(z, tabs, gains, t_len):
    n = z.shape[0]
    tm = _row_tile(t_len, 512)
    nt = t_len // tm
    rows = lambda w: pl.BlockSpec((tm, w), lambda i: (i, 0))
    tab = pl.BlockSpec((3, tm, 128), lambda i: (0, i % nt, 0))
    gain = pl.BlockSpec((1, HEAD_DIM), lambda i: (0, 0))
    per_head = pl.BlockSpec((tm, H_A, HEAD_DIM), lambda i: (i, 0, 0))
    widths = [(A_W, BF16), (A_W, F32), (A_W, BF16), (A_W, F32), (A_W, BF16), (B_W, BF16),
              (HEAD_DIM, F32), (HEAD_DIM, BF16), (HEAD_DIM, F32), (HEAD_DIM, BF16),
              (H_IDX * D_IDX, BF16), (128, F32), (128, BF16), (128, BF16)]
    return pl.pallas_call(
        _post_in_kernel,
        grid=(n // tm,),
        in_specs=[rows(ATT_W), tab, tab, tab, gain, gain, gain, gain],
        out_specs=[per_head if i in (1, 3) else rows(w) for i, (w, _) in enumerate(widths)],
        out_shape=[jax.ShapeDtypeStruct((n, H_A, HEAD_DIM) if i in (1, 3) else (n, w), dt)
                   for i, (w, dt) in enumerate(widths)],
        compiler_params=_params("parallel"),
        name="post_in",
    )(z, *tabs, *[g.reshape(1, HEAD_DIM) for g in gains])


LOG2E = 1.4426950408889634
ATTN_A_ROW_CHUNK = 256
ATTN_A_BLOCKS_PER_STEP = 4


def _attn_a_kernel(q_ref, k_ref, v_ref, tri_ref, o_ref, *, tq, tk, rc, past):
    i = pl.program_id(2)
    q0 = past + i * tq
    tri = tri_ref[...]
    n_full = q0 // tk
    n_all = (q0 + tq - 1 + tk - 1) // tk
    nrc = tq // rc
    qs = [q_ref[c * rc:(c + 1) * rc, :] for c in range(nrc)]
    rows = [q0 + c * rc + lax.broadcasted_iota(jnp.int32, (rc, tk), 0) for c in range(nrc)]
    col = lax.broadcasted_iota(jnp.int32, (rc, tk), 1)

    def step(j, carry, masked, nsub):
        cs = range(nrc)
        ss = range(nsub)
        offs = [pl.multiple_of((j - s) * tk, tk) for s in ss]
        kb = [k_ref[pl.ds(o, tk), :] for o in offs]
        vb = [v_ref[pl.ds(o, tk), :] for o in offs]
        later = [carry[0][c * rc:(c + 1) * rc] for c in cs]
        acc = [carry[1][c * rc:(c + 1) * rc] for c in cs]
        nz = [[_nt_dot(qs[c], kb[s]) for c in cs] for s in ss]
        loss = [[jnp.log(1.0 + jnp.exp2(jnp.minimum(nz[s][c] * (-LOG2E), 126.0))) for c in cs] for s in ss]
        if masked:
            causal = [[(offs[s] + col) < rows[c] for c in cs] for s in ss]
            loss = [[jnp.where(causal[s][c], loss[s][c], 0.0) for c in cs] for s in ss]
        within = [[jnp.dot(loss[s][c].astype(BF16), tri, preferred_element_type=F32) for c in cs] for s in ss]
        for s in ss:
            w = [jnp.exp2((nz[s][c] + later[c] + within[s][c]) * (-LOG2E)) for c in cs]
            if masked:
                w = [jnp.where(causal[s][c], w[c], 0.0) for c in cs]
            acc = [acc[c] + jnp.dot(w[c].astype(BF16), vb[s], preferred_element_type=F32) for c in cs]
            later = [later[c] + within[s][c][:, 0:1] for c in cs]
        return jnp.concatenate(later, axis=0), jnp.concatenate(acc, axis=0)

    carry = (jnp.zeros((tq, 1), F32), jnp.zeros((tq, HEAD_DIM), F32))
    carry = lax.fori_loop(0, n_all - n_full, lambda t, c: step(n_all - 1 - t, c, True, 1), carry)
    n_multi = n_full // ATTN_A_BLOCKS_PER_STEP
    n_single = n_full - n_multi * ATTN_A_BLOCKS_PER_STEP
    carry = lax.fori_loop(0, n_single, lambda t, c: step(n_full - 1 - t, c, False, 1), carry)
    carry = lax.fori_loop(
        0, n_multi,
        lambda t, c: step(n_full - n_single - 1 - t * ATTN_A_BLOCKS_PER_STEP, c, False, ATTN_A_BLOCKS_PER_STEP),
        carry)
    o_ref[...] = carry[1].astype(o_ref.dtype)


def attn_a(q, k, v, past, tk):
    b, t, _ = q.shape
    lp = k.shape[1]
    tq = _row_tile(t, 512)
    rc = min(tq, ATTN_A_ROW_CHUNK)
    tri = jnp.tri(tk, dtype=BF16)
    return pl.pallas_call(
        functools.partial(_attn_a_kernel, tq=tq, tk=tk, rc=rc, past=past),
        grid=(b, H_A, t // tq),
        in_specs=[pl.BlockSpec((None, tq, HEAD_DIM), lambda bi, h, i: (bi, i, h)),
                  pl.BlockSpec((None, lp, HEAD_DIM), lambda bi, h, i: (bi, 0, h)),
                  pl.BlockSpec((None, lp, HEAD_DIM), lambda bi, h, i: (bi, 0, h)),
                  pl.BlockSpec((tk, tk), lambda bi, h, i: (0, 0))],
        out_specs=pl.BlockSpec((None, tq, HEAD_DIM), lambda bi, h, i: (bi, i, h)),
        out_shape=jax.ShapeDtypeStruct((b, t, A_W), BF16),
        compiler_params=_params("parallel", "parallel", "parallel"),
        name="attn_a",
    )(q, k, v, tri)


def _dsa_kernel(qi_ref, kiw_ref, kilo_ref, kihi_ref, q_ref, k_ref, v_ref, triu_ref, o_ref, key_ref,
                *, tq, tk, past, n_keys, topk):
    i = pl.program_id(1)
    q0 = past + i * tq
    pos = q0 + lax.broadcasted_iota(jnp.int32, (tq, 1), 0)
    lim = jnp.minimum((pos // CHUNK + 1) * CHUNK, n_keys)
    n_adm = jnp.minimum(((q0 + tq - 1) // CHUNK + 1) * CHUNK, n_keys)
    nt = (n_adm + tk - 1) // tk
    col = lax.broadcasted_iota(jnp.int32, (tq, tk), 1)
    ngrp = tk // 128

    qi = qi_ref[...]
    kiw = kiw_ref[...]
    wi = [kiw[:, D_IDX + h:D_IDX + h + 1] for h in range(H_IDX)]

    n_pair = (nt + 1) // 2
    pair = range(2)

    def to_key(x):
        bits = lax.bitcast_convert_type(x, jnp.int32)
        return bits ^ ((bits >> 31) & 0x7FFFFFFF)

    def score_pair(t, carry):
        m1, m2 = carry
        offs = [pl.multiple_of((2 * t + u) * tk, tk) for u in pair]
        kis = [(kilo_ref[pl.ds(o, tk), :], kihi_ref[pl.ds(o, tk), :]) for o in offs]
        s_idx = [[_nt_dot(qi[:, (h // 2) * 128:(h // 2 + 1) * 128], kis[u][h % 2]) for h in range(H_IDX)]
                 for u in pair]
        for u in pair:
            s = jnp.zeros((tq, tk), F32)
            for h in range(H_IDX):
                s = s + wi[h] * jnp.maximum(s_idx[u][h], 0.0)
            s = jnp.where(s == 0.0, 0.0, s)
            s = jnp.where(offs[u] + col < lim, s, -jnp.inf)
            for g in range(ngrp):
                x = s[:, g * 128:(g + 1) * 128]
                m2 = jnp.maximum(m2, jnp.minimum(m1, x))
                m1 = jnp.maximum(m1, x)
            key_ref[:, pl.ds(offs[u], tk)] = to_key(s)
        return m1, m2

    ninf = jnp.full((tq, 128), -jnp.inf, F32)
    m1, m2 = lax.fori_loop(0, n_pair, score_pair, (ninf, ninf))
    lo_u = to_key(jnp.min(m2, axis=1, keepdims=True)) ^ INT_MIN
    hi_u = to_key(jnp.max(m2 if topk > 128 else m1, axis=1, keepdims=True)) ^ INT_MIN
    low_mask = lo_u ^ hi_u
    for sh in (1, 2, 4, 8, 16):
        low_mask = low_mask | lax.shift_right_logical(low_mask, sh)
    n_bits = jnp.max(lax.population_count(low_mask))

    def count_ge(cand):
        cand_b = jnp.broadcast_to(cand, (tq, 128))

        def body(t, acc):
            off = pl.multiple_of(t * tk, tk)
            kt = key_ref[:, pl.ds(off, tk)]
            for g in range(ngrp):
                acc = acc + jnp.where(kt[:, g * 128:(g + 1) * 128] >= cand_b, 1, 0)
            return acc

        acc = lax.fori_loop(0, nt, body, jnp.zeros((tq, 128), jnp.int32))
        return jnp.sum(acc, axis=1, keepdims=True)

    def bit_step(b, t_u):
        bit = lax.shift_left(jnp.int32(1), n_bits - 1 - b)
        cand_u = t_u | bit
        ok = ((low_mask & bit) != 0) & (count_ge(cand_u ^ INT_MIN) >= topk)
        return jnp.where(ok, cand_u, t_u)

    t_u = lax.fori_loop(0, n_bits, bit_step, lo_u & ~low_mask)
    tau = t_u ^ INT_MIN
    tau_next = jnp.where(tau == 2 ** 31 - 1, tau, tau + 1)
    cnt_gt = jnp.where(tau == 2 ** 31 - 1, 0, count_ge(tau_next))
    need = jnp.where(tau == KEY_NEG_INF, 0, topk - cnt_gt).astype(F32)

    triu = triu_ref[...]

    def select_tile(t, seen):
        off = pl.multiple_of(t * tk, tk)
        kt = key_ref[:, pl.ds(off, tk)]
        eq = kt == tau
        prefix = seen + jnp.dot(jnp.where(eq, 1.0, 0.0).astype(BF16), triu, preferred_element_type=F32)
        rank = jnp.where(eq, prefix, jnp.inf)
        bias = jnp.where(kt > tau, 0.0, jnp.where(rank <= need, 0.0, NEG_BIG)).astype(F32)
        key_ref[:, pl.ds(off, tk)] = lax.bitcast_convert_type(bias, jnp.int32)
        return prefix[:, tk - 1:tk]

    lax.fori_loop(0, 2 * n_pair, select_tile, jnp.zeros((tq, 1), F32))

    qs = jnp.concatenate([q_ref[:, h * HEAD_DIM:(h + 1) * HEAD_DIM] for h in range(H_B)], axis=0)

    def attend_pair(t, carry):
        m, l, acc = carry
        offs = [pl.multiple_of((2 * t + u) * tk, tk) for u in pair]
        kb = [k_ref[pl.ds(o, tk), :] for o in offs]
        vb = [v_ref[pl.ds(o, tk), :] for o in offs]
        bias = [lax.bitcast_convert_type(key_ref[:, pl.ds(o, tk)], F32) for o in offs]
        logits = [(_nt_dot(qs, kb[u]).reshape(H_B, tq, tk) + bias[u][None]).reshape(H_B * tq, tk) for u in pair]
        m_new = jnp.maximum(m, jnp.maximum(jnp.max(logits[0], axis=1, keepdims=True),
                                           jnp.max(logits[1], axis=1, keepdims=True)))
        alpha = jnp.exp(m - m_new)
        p = [jnp.exp(logits[u] - m_new) for u in pair]
        l = alpha * l + (jnp.sum(p[0], axis=1, keepdims=True) + jnp.sum(p[1], axis=1, keepdims=True))
        acc = alpha * acc + (jnp.dot(p[0].astype(BF16), vb[0], preferred_element_type=F32)
                             + jnp.dot(p[1].astype(BF16), vb[1], preferred_element_type=F32))
        return m_new, l, acc

    init = (jnp.full((H_B * tq, 1), NEG_BIG, F32), jnp.zeros((H_B * tq, 1), F32),
            jnp.zeros((H_B * tq, HEAD_DIM), F32))
    _, l, acc = lax.fori_loop(0, n_pair, attend_pair, init)
    out = acc / l
    for h in range(H_B):
        o_ref[:, h * HEAD_DIM:(h + 1) * HEAD_DIM] = out[h * tq:(h + 1) * tq].astype(o_ref.dtype)


def dsa(qi, kiw, kilo, kihi, q, k, v, past, n_keys, tk):
    b, t, _ = q.shape
    lp = k.shape[1]
    tq = _row_tile(t, 128)
    topk = min(TOPK_MAX, n_keys // 4)
    assert lp % (2 * tk) == 0 and lp >= topk
    triu = jnp.tri(tk, dtype=BF16).T
    qblk = lambda w: pl.BlockSpec((None, tq, w), lambda bi, i: (bi, i, 0))
    kblk = lambda w: pl.BlockSpec((None, lp, w), lambda bi, i: (bi, 0, 0))
    return pl.pallas_call(
        functools.partial(_dsa_kernel, tq=tq, tk=tk, past=past, n_keys=n_keys, topk=topk),
        grid=(b, t // tq),
        in_specs=[qblk(H_IDX * D_IDX), qblk(128), kblk(128), kblk(128), qblk(B_W),
                  kblk(HEAD_DIM), kblk(HEAD_DIM), pl.BlockSpec((tk, tk), lambda bi, i: (0, 0))],
        out_specs=qblk(B_W),
        out_shape=jax.ShapeDtypeStruct((b, t, B_W), BF16),
        scratch_shapes=[pltpu.VMEM((tq, lp), jnp.int32)],
        compiler_params=_params("parallel", "parallel"),
        name="dsa",
    )(qi, kiw, kilo, kihi, q, k, v, triu)


def _softplus(x):
    return jnp.maximum(x, 0.0) + jnp.log(1.0 + jnp.exp(-jnp.abs(x)))


def _c_prep_kernel(*refs, first_layer):
    if first_layer:
        (cz_ref, prev_ref, mu_ref, w0_ref, a0_ref, w2_ref, a2_ref, g2_ref, kk_ref, ka_ref,
         r_ref, kn_ref, k_ref, v_ref, lw_ref, a_ref, g_ref, vtok_ref) = refs
    else:
        (cz_ref, prev_ref, mu_ref, w0_ref, a0_ref, w2_ref, a2_ref, g2_ref, kk_ref, ka_ref,
         vf_ref, v0_ref, v1_ref, v2_ref,
         r_ref, kn_ref, k_ref, v_ref, lw_ref, a_ref, g_ref) = refs

    def put_heads(ref, x):
        for pair in range(H_C // 2):
            tile = x[:, pair * 128:(pair + 1) * 128]
            ref[2 * pair] = tile[:, :N_C]
            ref[2 * pair + 1] = pltpu.roll(tile, N_C, 1)[:, :N_C]

    cz = cz_ref[...]
    rowid = lax.broadcasted_iota(jnp.int32, cz.shape, 0)
    prev = jnp.where(rowid == 0, prev_ref[...], pltpu.roll(cz, 1, 0))
    zz = cz + (prev - cz) * mu_ref[...]
    r = zz[:, 0:C_W]
    k = zz[:, C_W:2 * C_W]
    v = zz[:, 2 * C_W:3 * C_W]
    tail = zz[:, 3 * C_W:C_PAD]
    w = w0_ref[...] + jnp.dot(jnp.tanh(tail).astype(BF16), w2_ref[...], preferred_element_type=F32)
    w = -_softplus(-w) - 0.5
    a = jax.nn.sigmoid(a0_ref[...] + jnp.dot(tail.astype(BF16), a2_ref[...], preferred_element_type=F32))
    g = jnp.dot(jax.nn.sigmoid(tail).astype(BF16), g2_ref[...], preferred_element_type=F32)
    if not first_layer:
        lora = jnp.dot(v.astype(BF16), v1_ref[...], preferred_element_type=F32)
        lora = jnp.dot(lora.astype(BF16), v2_ref[...], preferred_element_type=F32)
        v = v + (vf_ref[...] - v) * jax.nn.sigmoid(v0_ref[...] + lora)
    if first_layer:
        vtok_ref[...] = v
    put_heads(r_ref, r)
    put_heads(kn_ref, k * kk_ref[...])
    put_heads(k_ref, k * (1.0 + (a - 1.0) * ka_ref[...]))
    put_heads(v_ref, v)
    put_heads(lw_ref, -jnp.exp(w))
    put_heads(a_ref, a)
    put_heads(g_ref, g)


def c_prep(cz, prev_rows, vecs, mats, v_first, vres, tm):
    b, t, _ = cz.shape
    first_layer = vres is None
    blk = lambda w: pl.BlockSpec((None, tm, w), lambda bi, i: (bi, i, 0))
    vec = lambda w: pl.BlockSpec((1, w), lambda bi, i: (0, 0))
    mat = lambda m: pl.BlockSpec(m.shape, lambda bi, i: (0, 0))
    mu, w0, a0, k_k, k_a = vecs
    w2, a2, g2 = mats
    ins = [cz, prev_rows, mu, w0, a0, w2, a2, g2, k_k, k_a]
    specs = [blk(C_PAD), pl.BlockSpec((None, None, 1, C_PAD), lambda bi, i: (bi, i, 0, 0)),
             vec(C_PAD), vec(C_W), vec(C_W), mat(w2), mat(a2), mat(g2), vec(C_W), vec(C_W)]
    if not first_layer:
        v0, v1, v2 = vres
        ins += [v_first, v0, v1, v2]
        specs += [blk(C_W), vec(C_W), mat(v1), mat(v2)]
    heads = pl.BlockSpec((None, H_C, tm, N_C), lambda bi, i: (bi, 0, i, 0))
    out_specs = [heads] * 7
    out_shape = [jax.ShapeDtypeStruct((b, H_C, t, N_C), F32)] * 7
    if first_layer:
        out_specs.append(blk(C_W))
        out_shape.append(jax.ShapeDtypeStruct((b, t, C_W), F32))
    return pl.pallas_call(
        functools.partial(_c_prep_kernel, first_layer=first_layer),
        grid=(b, t // tm),
        in_specs=specs,
        out_specs=out_specs,
        out_shape=out_shape,
        compiler_params=_params("parallel", "parallel"),
        name="c_prep",
    )(*ins)


def _bdot(a, b, contract_a, contract_b):
    return lax.dot_general(a.astype(BF16), b.astype(BF16), (((contract_a,), (contract_b,)), ((0,), (0,))),
                           preferred_element_type=F32)


def _bdot3(a, b, contract_a, contract_b):
    ah = a.astype(BF16)
    bh = b.astype(BF16)
    al = (a - ah.astype(F32)).astype(BF16)
    bl = (b - bh.astype(F32)).astype(BF16)
    dn = (((contract_a,), (contract_b,)), ((0,), (0,)))
    dot = lambda x, y: lax.dot_general(x, y, dn, preferred_element_type=F32)
    return dot(ah, bh) + (dot(al, bh) + dot(ah, bl))


def _c_scan_kernel(r_ref, kn_ref, k_ref, v_ref, lw_ref, a_ref, g_ref, s0_ref, rk_ref, lng_ref, lnb_ref,
                   y_ref, st_ref, s_ref, *, nck):
    c_len = SCAN_CHUNK

    @pl.when(pl.program_id(1) == 0)
    def _():
        s_ref[...] = s0_ref[...]

    ti = lax.broadcasted_iota(jnp.int32, (H_C, c_len, c_len), 1)
    si = lax.broadcasted_iota(jnp.int32, (H_C, c_len, c_len), 2)
    incl = si <= ti
    strict = si < ti
    tri = jnp.where(incl, 1.0, 0.0)
    eye = jnp.where(si == ti, 1.0, 0.0)
    r_k = rk_ref[...]
    ln_g = lng_ref[...]
    ln_b = lnb_ref[...]

    def chunk(c, _):
        rows = pl.ds(pl.multiple_of(c * c_len, c_len), c_len)
        r = r_ref[:, rows, :]
        k = k_ref[:, rows, :]
        v = v_ref[:, rows, :]
        lw = lw_ref[:, rows, :]
        a = a_ref[:, rows, :]
        kn = kn_ref[:, rows, :]
        kn = kn * lax.rsqrt(jnp.maximum(jnp.sum(kn * kn, axis=-1, keepdims=True), 1e-24))
        aa = -kn
        bb = kn * a
        g = _bdot3(tri, lw, 2, 1)
        g_end = g[:, c_len - 1:c_len, :]
        a_t = aa * jnp.exp(g - lw)
        r_t = r * jnp.exp(g)
        inv = jnp.exp(-g)
        b_h = bb * inv
        k_h = k * inv
        to_end = jnp.exp(g_end - g)
        b_e = bb * to_end
        k_e = k * to_end
        d_end = jnp.exp(g_end)
        a_ab = jnp.where(strict, _bdot(a_t, b_h, 2, 2), 0.0)
        a_ak = jnp.where(strict, _bdot(a_t, k_h, 2, 2), 0.0)
        a_rb = jnp.where(incl, _bdot(r_t, b_h, 2, 2), 0.0)
        a_rk = jnp.where(incl, _bdot(r_t, k_h, 2, 2), 0.0)
        t_inv = eye + a_ab
        pw = a_ab
        for _ in range(4):
            pw = _bdot(pw, pw, 2, 1)
            t_inv = t_inv + _bdot(t_inv, pw, 2, 1)
        p_mat = _bdot(t_inv, a_t, 2, 1)
        q_mat = _bdot(t_inv, _bdot(a_ak, v, 2, 1), 2, 1)
        y_v = _bdot(a_rk, v, 2, 1)
        s = s_ref[...]
        u = _bdot(p_mat, s, 2, 2) + q_mat
        y = _bdot(r_t, s, 2, 2) + _bdot(a_rb, u, 2, 1) + y_v
        uv = jnp.concatenate([u, v], axis=1)
        bk = jnp.concatenate([b_e, k_e], axis=1)
        s_ref[...] = s * d_end + _bdot(uv, bk, 1, 1)
        mu = jnp.mean(y, axis=-1, keepdims=True)
        yc = y - mu
        yn = yc * lax.rsqrt(jnp.mean(yc * yc, axis=-1, keepdims=True) + LNX_EPS)
        yn = yn * ln_g + ln_b
        bonus = jnp.sum(r * k * r_k, axis=-1, keepdims=True) * v
        o = (yn + bonus) * g_ref[:, rows, :]
        y_ref[rows, :] = jnp.concatenate([o[h] for h in range(H_C)], axis=-1).astype(y_ref.dtype)
        return 0

    lax.fori_loop(0, nck, chunk, 0)

    @pl.when(pl.program_id(1) == pl.num_programs(1) - 1)
    def _():
        st_ref[...] = s_ref[...]


def c_scan(r, kn, k, v, lw, a, g, s0, r_k, ln_g, ln_b, tm):
    b, _, t, _ = r.shape
    blk = pl.BlockSpec((None, H_C, tm, N_C), lambda bi, i: (bi, 0, i, 0))
    st = pl.BlockSpec((None, H_C, N_C, N_C), lambda bi, i: (bi, 0, 0, 0))
    vec = pl.BlockSpec((H_C, 1, N_C), lambda bi, i: (0, 0, 0))
    return pl.pallas_call(
        functools.partial(_c_scan_kernel, nck=tm // SCAN_CHUNK),
        grid=(b, t // tm),
        in_specs=[blk] * 7 + [st, vec, vec, vec],
        out_specs=[pl.BlockSpec((None, tm, C_W), lambda bi, i: (bi, i, 0)), st],
        out_shape=[jax.ShapeDtypeStruct((b, t, C_W), BF16),
                   jax.ShapeDtypeStruct((b, H_C, N_C, N_C), F32)],
        scratch_shapes=[pltpu.VMEM((H_C, N_C, N_C), F32)],
        compiler_params=_params("parallel", "arbitrary"),
        name="c_scan",
    )(r, kn, k, v, lw, a, g, s0, r_k.reshape(H_C, 1, N_C), ln_g.reshape(H_C, 1, N_C), ln_b.reshape(H_C, 1, N_C))


def _pad_rows(x, rows, axis=1):
    pad = rows - x.shape[axis]
    if pad == 0:
        return x
    widths = [(0, 0)] * x.ndim
    widths[axis] = (0, pad)
    return jnp.pad(x, widths)


ATT_COLS = 3 * A_W + B_W + 2 * HEAD_DIM + H_IDX * D_IDX + D_IDX + H_IDX
IN_COLS = ATT_COLS + C_COLS + 3 * D_MODEL


def _regroup_kernel(w_ref, tail_ref, att_ref, c_ref, g_ref):
    lane = lax.broadcasted_iota(jnp.int32, (w_ref.shape[0], 128), 1)
    n_full = IN_COLS // 128

    def tile(j):
        return tail_ref[...] if j == n_full else w_ref[:, j * 128:(j + 1) * 128]

    def shifted(j0, shift, j):
        joined = jnp.where(lane >= shift, tile(j0 + j), tile(j0 + j + 1))
        return pltpu.roll(joined, 128 - shift, 1)

    for j in range(ATT_W // 128):
        x = tile(j)
        if (j + 1) * 128 > ATT_COLS:
            x = jnp.where(lane < ATT_COLS - j * 128, x, 0.0)
        att_ref[:, j * 128:(j + 1) * 128] = x.astype(BF16)
    for j in range(C_PAD // 128):
        x = shifted(ATT_COLS // 128, ATT_COLS % 128, j)
        if (j + 1) * 128 > C_COLS:
            x = jnp.where(lane < C_COLS - j * 128, x, 0.0)
        c_ref[:, j * 128:(j + 1) * 128] = x.astype(BF16)
    g0 = ATT_COLS + C_COLS
    for j in range(3 * D_MODEL // 128):
        g_ref[:, j * 128:(j + 1) * 128] = shifted(g0 // 128, g0 % 128, j).astype(BF16)


def regroup_w_in(w_in):
    depth, d, cols = w_in.shape
    assert cols == IN_COLS and (ATT_COLS + C_COLS) % 128 != 0
    n_full = cols // 128
    tail = jnp.pad(w_in[:, :, n_full * 128:], ((0, 0), (0, 0), (0, (n_full + 1) * 128 - cols)))
    tm = 128
    blk = lambda w: pl.BlockSpec((None, tm, w), lambda l, i: (l, i, 0))
    return pl.pallas_call(
        _regroup_kernel,
        grid=(depth, d // tm),
        in_specs=[blk(cols), blk(128)],
        out_specs=[blk(ATT_W), blk(C_PAD), blk(3 * D_MODEL)],
        out_shape=[jax.ShapeDtypeStruct((depth, d, w), BF16) for w in (ATT_W, C_PAD, 3 * D_MODEL)],
        compiler_params=_params("parallel", "parallel"),
        name="regroup_w_in",
    )(w_in, tail)


def _prep_weights(w):
    out = []
    w_att, w_c, w_g = regroup_w_in(w["w_in"])
    for i in range(DEPTH):
        tail_rows = lambda m, off: jnp.pad(m, ((off, C_TAIL - off - m.shape[0]), (0, 0))).astype(BF16)
        lw = dict(
            w_att=w_att[i], w_c=w_c[i], w_g=w_g[i],
            w2=tail_rows(w["c_w2"][i], 0),
            a2=tail_rows(w["c_a2"][i], D_DECAY_LORA),
            g2=tail_rows(w["c_g2"][i], D_DECAY_LORA + D_AAA_LORA),
            mu=jnp.pad(w["c_mu"][i], (0, C_PAD - C_COLS)).reshape(1, C_PAD),
            w_br_a=w["w_br_a"][i].astype(BF16), w_br_b=w["w_br_b"][i].astype(BF16),
            w_br_c=w["w_br_c"][i].astype(BF16), w_out=w["w_out"][i].astype(BF16),
            w_ffn_in=w["w_ffn_in"][i].astype(BF16), w_ffn_out=w["w_ffn_out"][i].astype(BF16),
            w_ple_gate=w["w_ple_gate"][i].astype(BF16), w_ple_proj=w["w_ple_proj"][i].astype(BF16),
        )
        if i > 0:
            lw["v1"] = w["c_v1"][i - 1].astype(BF16)
            lw["v2"] = w["c_v2"][i - 1].astype(BF16)
        out.append(lw)
    return out


def _run_trunk(x, p, past, caches, w, wl):
    b, t, _ = x.shape
    n = b * t
    n_keys = past + t
    pos = past + jnp.arange(t)
    lane = jnp.arange(128)
    tabs = (_rope_tables(pos, 128, HEAD_DIM, HEAD_DIM // ROT_FRACTION),
            _rope_tables(pos, 128, D_IDX, D_IDX // ROT_FRACTION),
            _rope_tables(pos, 128, 128, D_IDX // ROT_FRACTION,
                         scale_lanes=(lane < D_IDX, jnp.where(lane < D_IDX + H_IDX, H_IDX ** -0.5, 1.0))))
    tk_a = 256
    tk_b = 512
    lp_a = -(-n_keys // tk_a) * tk_a
    lp_b = -(-n_keys // (2 * tk_b)) * (2 * tk_b)
    t_c = -(-t // SCAN_CHUNK) * SCAN_CHUNK
    tm_c = _row_tile(t, 256)
    tm_s = _row_tile(t_c, 256)
    h = x.reshape(n, D_MODEL)
    outs = ([], [], [], [], [], [], [])
    v_first = None
    for i in range(DEPTH):
        lw = wl[i]
        z_att = mm_norm(h, w["norm_mix"][i], lw["w_att"], ATT_W // 3)
        z_c = mm_norm(h, w["norm_mix"][i], lw["w_c"], C_PAD // 4)
        z_g = mm_norm(h, w["norm_mix"][i], lw["w_g"], 1024)
        (aq, ak_f, ak_b, av_f, av_b, bq, bk_f, bk_b, bv_f, bv_b, qi, kiw, kilo, kihi) = post_in(
            z_att, tabs, (w["a_q_norm"][i], w["a_k_norm"][i], w["b_q_norm"][i], w["b_k_norm"][i]), t)
        seq = lambda a: a.reshape(b, t, a.shape[-1])

        def keys(new, old, lp):
            new = seq(new)
            if old is not None:
                new = jnp.concatenate([old.astype(BF16), new], axis=1)
            return _pad_rows(new, lp)

        c = caches
        past_of = lambda name: None if c is None else c[name][i]
        pa_k = None if c is None else c["a_k"][i].reshape(b, past, A_W)
        pa_v = None if c is None else c["a_v"][i].reshape(b, past, A_W)
        o_a = attn_a(seq(aq), keys(ak_b, pa_k, lp_a), keys(av_b, pa_v, lp_a), past, tk_a)
        if c is None:
            p_lo = p_hi = None
        else:
            p_lo = jnp.pad(c["b_ki"][i], ((0, 0), (0, 0), (0, 128 - D_IDX)))
            p_hi = jnp.pad(c["b_ki"][i], ((0, 0), (0, 0), (128 - D_IDX, 0)))
        o_b = dsa(seq(qi), seq(kiw), keys(kilo, p_lo, lp_b), keys(kihi, p_hi, lp_b), seq(bq),
                  keys(bk_b, past_of("b_k"), lp_b), keys(bv_b, past_of("b_v"), lp_b), past, n_keys, tk_b)
        cz = seq(z_c)
        shift0 = jnp.zeros((b, C_COLS), F32) if c is None else c["shift"][i]
        shift0 = jnp.pad(shift0, ((0, 0), (0, C_PAD - C_COLS)))
        prev_rows = jnp.concatenate([shift0[:, None, :], cz[:, tm_c - 1:t - 1:tm_c, :]], axis=1)
        prev_rows = prev_rows.reshape(b, t // tm_c, 1, C_PAD)
        row = lambda a: a.reshape(1, C_W)
        vres = None if i == 0 else (row(w["c_v0"][i - 1]), lw["v1"], lw["v2"])
        prep = c_prep(cz, prev_rows,
                      (lw["mu"], row(w["c_w0"][i]), row(w["c_a0"][i]), row(w["c_k_k"][i]), row(w["c_k_a"][i])),
                      (lw["w2"], lw["a2"], lw["g2"]), v_first, vres, tm_c)
        if i == 0:
            v_first = prep[7]
        s0 = jnp.zeros((b, H_C, N_C, N_C), F32) if c is None else c["wkv"][i]
        y_c, wkv_t = c_scan(*[_pad_rows(a, t_c, axis=2) for a in prep[:7]], s0,
                            w["c_r_k"][i], w["c_ln_g"][i], w["c_ln_b"][i], tm_s)
        o_c = y_c[:, :t].reshape(n, C_W)
        merged = mm_gated3(o_a.reshape(n, A_W), o_b.reshape(n, B_W), o_c,
                           lw["w_br_a"], lw["w_br_b"], lw["w_br_c"], z_g)
        h = mm_res(merged, lw["w_out"], h)
        h = ffn(h, w["norm_ffn"][i], lw["w_ffn_in"], lw["w_ffn_out"])
        h = ple(h, w["norm_ple"][i], lw["w_ple_gate"], p[i].reshape(n, -1).astype(BF16), lw["w_ple_proj"])
        vals = (ak_f.reshape(b, t, H_A, HEAD_DIM), av_f.reshape(b, t, H_A, HEAD_DIM),
                seq(bk_f), seq(bv_f), seq(kiw)[:, :, :D_IDX], wkv_t, cz[:, -1, :C_COLS])
        for lst, val in zip(outs, vals):
            lst.append(val)
    return h.reshape(b, t, D_MODEL), [jnp.stack(l) for l in outs]


def kernel(x_prompt, x_sample, cache_a_k, cache_a_v, cache_b_k, cache_b_v, cache_b_kidx, state_c_wkv, state_c_shift, p_prompt, p_sample, norm_mix, w_in, a_q_norm, a_k_norm, b_q_norm, b_k_norm, c_mu, c_w0, c_w2, c_a0, c_a2, c_g2, c_v0, c_v1, c_v2, c_k_k, c_k_a, c_r_k, c_ln_g, c_ln_b, w_br_a, w_br_b, w_br_c, w_out, norm_ffn, w_ffn_in, w_ffn_out, norm_ple, w_ple_gate, w_ple_proj):
    w = dict(norm_mix=norm_mix, w_in=w_in, a_q_norm=a_q_norm, a_k_norm=a_k_norm, b_q_norm=b_q_norm,
             b_k_norm=b_k_norm, c_mu=c_mu, c_w0=c_w0, c_w2=c_w2, c_a0=c_a0, c_a2=c_a2, c_g2=c_g2,
             c_v0=c_v0, c_v1=c_v1, c_v2=c_v2, c_k_k=c_k_k, c_k_a=c_k_a, c_r_k=c_r_k, c_ln_g=c_ln_g,
             c_ln_b=c_ln_b, w_br_a=w_br_a, w_br_b=w_br_b, w_br_c=w_br_c, w_out=w_out, norm_ffn=norm_ffn,
             w_ffn_in=w_ffn_in, w_ffn_out=w_ffn_out, norm_ple=norm_ple, w_ple_gate=w_ple_gate,
             w_ple_proj=w_ple_proj)
    wl = _prep_weights(w)
    y_p, o_p = _run_trunk(x_prompt, p_prompt, 0, None, w, wl)
    caches = dict(a_k=cache_a_k, a_v=cache_a_v, b_k=cache_b_k, b_v=cache_b_v, b_ki=cache_b_kidx,
                  wkv=state_c_wkv, shift=state_c_shift)
    y_s, o_s = _run_trunk(x_sample, p_sample, cache_a_k.shape[2], caches, w, wl)
    return (y_p, y_s, *o_p, *o_s)
```

```python
import functools
import math

import jax
import jax.numpy as jnp
from jax import lax
from jax.experimental import pallas as pl
from jax.experimental.pallas import tpu as pltpu

F32 = jnp.float32
BF16 = jnp.bfloat16

D_MODEL = 2048
DEPTH = 2
HEAD_DIM = 128
H_A = 4
A_W = H_A * HEAD_DIM
H_B = 4
B_W = H_B * HEAD_DIM
H_IDX = 4
D_IDX = 64
TOPK_MAX = 256
CHUNK = 64
H_C = 16
N_C = 64
C_W = H_C * N_C
SCAN_CHUNK = 32
D_DECAY_LORA = 96
D_AAA_LORA = 96
D_GATE_LORA = 256
C_COLS = 3 * C_W + D_DECAY_LORA + D_AAA_LORA + D_GATE_LORA
C_PAD = 3584
C_TAIL = C_PAD - 3 * C_W
ATT_W = 3 * A_W + B_W + 2 * HEAD_DIM + H_IDX * D_IDX + 128
ROPE_THETA = 500000.0
ROT_FRACTION = 4
NORM_EPS = 1e-6
LNX_EPS = 64e-5
NEG_BIG = -1e30

VMEM_LIMIT = 56 * 1024 * 1024
INT_MIN = -(2 ** 31)
KEY_NEG_INF = (0xFF800000 ^ 0x7FFFFFFF) - 2 ** 32


def _params(*sem):
    return pltpu.CompilerParams(dimension_semantics=sem, vmem_limit_bytes=VMEM_LIMIT)


def _row_tile(n, pref):
    t = min(n, pref)
    assert n % t == 0, (n, t)
    return t


def _nt_dot(a, b):
    return lax.dot_general(a, b, (((1,), (1,)), ((), ())), preferred_element_type=F32)


def _rms(x, gain):
    xn = x * lax.rsqrt(jnp.mean(x * x, axis=-1, keepdims=True) + NORM_EPS)
    return xn * gain


def _mm_norm_kernel(x_ref, g_ref, w_ref, o_ref, xn_ref):
    @pl.when(pl.program_id(1) == 0)
    def _():
        xn_ref[...] = _rms(x_ref[...], g_ref[...]).astype(BF16)

    o_ref[...] = jnp.dot(xn_ref[...], w_ref[...], preferred_element_type=F32)


def mm_norm(x, gain, w, tn):
    n, k = x.shape
    nc = w.shape[1]
    tm = _row_tile(n, 1024)
    assert nc % tn == 0
    return pl.pallas_call(
        _mm_norm_kernel,
        grid=(n // tm, nc // tn),
        in_specs=[pl.BlockSpec((tm, k), lambda i, j: (i, 0)),
                  pl.BlockSpec((1, k), lambda i, j: (0, 0)),
                  pl.BlockSpec((k, tn), lambda i, j: (0, j))],
        out_specs=pl.BlockSpec((tm, tn), lambda i, j: (i, j)),
        out_shape=jax.ShapeDtypeStruct((n, nc), F32),
        scratch_shapes=[pltpu.VMEM((tm, k), BF16)],
        compiler_params=_params("parallel", "arbitrary"),
        name="mm_norm",
    )(x, gain.reshape(1, k), w)


def _mm_res_kernel(x_ref, w_ref, h_ref, o_ref):
    o_ref[...] = h_ref[...] + jnp.dot(x_ref[...], w_ref[...], preferred_element_type=F32)


def mm_res(x, w, h):
    n, k = x.shape
    nc = w.shape[1]
    tm = _row_tile(n, 1024)
    tn = 1024
    return pl.pallas_call(
        _mm_res_kernel,
        grid=(n // tm, nc // tn),
        in_specs=[pl.BlockSpec((tm, k), lambda i, j: (i, 0)),
                  pl.BlockSpec((k, tn), lambda i, j: (0, j)),
                  pl.BlockSpec((tm, tn), lambda i, j: (i, j))],
        out_specs=pl.BlockSpec((tm, tn), lambda i, j: (i, j)),
        out_shape=jax.ShapeDtypeStruct((n, nc), F32),
        compiler_params=_params("parallel", "parallel"),
        name="mm_res",
    )(x, w, h)


def _mm_gated3_kernel(oa_ref, ob_ref, oc_ref, wa_ref, wb_ref, wc_ref, ga_ref, gb_ref, gc_ref, o_ref):
    ya = jnp.dot(oa_ref[...], wa_ref[...], preferred_element_type=F32)
    yb = jnp.dot(ob_ref[...], wb_ref[...], preferred_element_type=F32)
    yc = jnp.dot(oc_ref[...], wc_ref[...], preferred_element_type=F32)
    m = (jax.nn.sigmoid(ga_ref[...]) * ya + jax.nn.sigmoid(gb_ref[...]) * yb
         + jax.nn.sigmoid(gc_ref[...]) * yc)
    o_ref[...] = m.astype(o_ref.dtype)


def mm_gated3(oa, ob, oc, wa, wb, wc, gates):
    n = oa.shape[0]
    tm = _row_tile(n, 512)
    tn = 512
    nj = D_MODEL // tn
    row = lambda w: pl.BlockSpec((tm, w), lambda i, j: (i, 0))
    col = lambda k: pl.BlockSpec((k, tn), lambda i, j: (0, j))
    gate = lambda s: pl.BlockSpec((tm, tn), lambda i, j: (i, j + s * nj))
    return pl.pallas_call(
        _mm_gated3_kernel,
        grid=(n // tm, nj),
        in_specs=[row(A_W), row(B_W), row(C_W), col(A_W), col(B_W), col(C_W), gate(0), gate(1), gate(2)],
        out_specs=pl.BlockSpec((tm, tn), lambda i, j: (i, j)),
        out_shape=jax.ShapeDtypeStruct((n, D_MODEL), BF16),
        compiler_params=_params("parallel", "parallel"),
        name="mm_gated3",
    )(oa, ob, oc, wa, wb, wc, gates, gates, gates)


def _ffn_kernel(h_ref, g_ref, wg_ref, wu_ref, wo_ref, o_ref, xn_ref, acc_ref):
    f = pl.program_id(1)

    @pl.when(f == 0)
    def _():
        xn_ref[...] = _rms(h_ref[...], g_ref[...]).astype(BF16)
        acc_ref[...] = jnp.zeros_like(acc_ref)

    xn = xn_ref[...]
    gate = jnp.dot(xn, wg_ref[...], preferred_element_type=F32)
    up = jnp.dot(xn, wu_ref[...], preferred_element_type=F32)
    act = (jax.nn.silu(gate) * up).astype(BF16)
    acc_ref[...] += jnp.dot(act, wo_ref[...], preferred_element_type=F32)

    @pl.when(f == pl.num_programs(1) - 1)
    def _():
        o_ref[...] = h_ref[...] + acc_ref[...]


def ffn(h, gain, w_in, w_out):
    n, d = h.shape
    dff = w_out.shape[0]
    tm = _row_tile(n, 512)
    tf = 512
    assert dff % tf == 0
    nf = dff // tf
    return pl.pallas_call(
        _ffn_kernel,
        grid=(n // tm, nf),
        in_specs=[pl.BlockSpec((tm, d), lambda i, f: (i, 0)),
                  pl.BlockSpec((1, d), lambda i, f: (0, 0)),
                  pl.BlockSpec((d, tf), lambda i, f: (0, f)),
                  pl.BlockSpec((d, tf), lambda i, f: (0, f + nf)),
                  pl.BlockSpec((tf, d), lambda i, f: (f, 0))],
        out_specs=pl.BlockSpec((tm, d), lambda i, f: (i, 0)),
        out_shape=jax.ShapeDtypeStruct((n, d), F32),
        scratch_shapes=[pltpu.VMEM((tm, d), BF16), pltpu.VMEM((tm, d), F32)],
        compiler_params=_params("parallel", "arbitrary"),
        name="ffn",
    )(h, gain.reshape(1, d), w_in, w_in, w_out)


def _ple_kernel(h_ref, g_ref, wg_ref, p_ref, wp_ref, hj_ref, o_ref, xn_ref):
    @pl.when(pl.program_id(1) == 0)
    def _():
        xn_ref[...] = _rms(h_ref[...], g_ref[...]).astype(BF16)

    gate = jnp.dot(xn_ref[...], wg_ref[...], preferred_element_type=F32)
    emb = jnp.dot(p_ref[...], wp_ref[...], preferred_element_type=F32)
    o_ref[...] = hj_ref[...] + jax.nn.sigmoid(gate) * emb


def ple(h, gain, w_gate, p, w_proj):
    n, d = h.shape
    pd = p.shape[1]
    tm = _row_tile(n, 1024)
    tn = 512
    return pl.pallas_call(
        _ple_kernel,
        grid=(n // tm, d // tn),
        in_specs=[pl.BlockSpec((tm, d), lambda i, j: (i, 0)),
                  pl.BlockSpec((1, d), lambda i, j: (0, 0)),
                  pl.BlockSpec((d, tn), lambda i, j: (0, j)),
                  pl.BlockSpec((tm, pd), lambda i, j: (i, 0)),
                  pl.BlockSpec((pd, tn), lambda i, j: (0, j)),
                  pl.BlockSpec((tm, tn), lambda i, j: (i, j))],
        out_specs=pl.BlockSpec((tm, tn), lambda i, j: (i, j)),
        out_shape=jax.ShapeDtypeStruct((n, d), F32),
        scratch_shapes=[pltpu.VMEM((tm, d), BF16)],
        compiler_params=_params("parallel", "arbitrary"),
        name="ple",
    )(h, gain.reshape(1, d), w_gate, p, w_proj, h)


def _rope_tables(pos, width, period, rot, scale_lanes=None):
    half = rot // 2
    inv = ROPE_THETA ** (-jnp.arange(half, dtype=F32) / half)
    lane = jnp.arange(width) % period
    first, second = lane < half, (lane >= half) & (lane < rot)
    ang = pos.astype(F32)[:, None] * inv[lane % half][None, :]
    cos, sin = jnp.cos(ang), jnp.sin(ang)
    c = jnp.where((first | second)[None, :], cos, 1.0)
    s_up = jnp.where(first[None, :], -sin, 0.0)
    s_dn = jnp.where(second[None, :], sin, 0.0)
    if scale_lanes is not None:
        only, mult = scale_lanes
        c = jnp.where(only[None, :], c, mult[None, :])
        s_up = jnp.where(only[None, :], s_up, 0.0)
        s_dn = jnp.where(only[None, :], s_dn, 0.0)
    return jnp.stack([c, s_up, s_dn]).astype(F32)


def _rope(x, tab, half):
    return (x * tab[0] + pltpu.roll(x, x.shape[1] - half, 1) * tab[1] + pltpu.roll(x, half, 1) * tab[2])


def _post_in_kernel(z_ref, t128_ref, t64_ref, tki_ref, gaq_ref, gak_ref, gbq_ref, gbk_ref,
                    aq_ref, akf_ref, akb_ref, avf_ref, avb_ref, bq_ref, bkf_ref, bkb_ref,
                    bvf_ref, bvb_ref, qi_ref, kiw_ref, kilo_ref, kihi_ref):
    hd = HEAD_DIM
    t128 = t128_ref[...]
    t64 = t64_ref[...]
    half128 = hd // ROT_FRACTION // 2
    half64 = D_IDX // ROT_FRACTION // 2
    for h in range(H_A):
        sl = slice(h * hd, (h + 1) * hd)
        aq = _rms(z_ref[:, sl], gaq_ref[...]) * (-(hd ** -0.5))
        aq_ref[:, sl] = aq.astype(BF16)
        ak = _rms(z_ref[:, A_W + h * hd:A_W + (h + 1) * hd], gak_ref[...])
        akf_ref[:, h, :] = ak
        akb_ref[:, sl] = ak.astype(BF16)
        av = z_ref[:, 2 * A_W + h * hd:2 * A_W + (h + 1) * hd]
        avf_ref[:, h, :] = av
        avb_ref[:, sl] = av.astype(BF16)
    o = 3 * A_W
    for h in range(H_B):
        sl = slice(h * hd, (h + 1) * hd)
        bq = _rope(_rms(z_ref[:, o + h * hd:o + (h + 1) * hd], gbq_ref[...]), t128, half128) * (hd ** -0.5)
        bq_ref[:, sl] = bq.astype(BF16)
    o += B_W
    bk = _rope(_rms(z_ref[:, o:o + hd], gbk_ref[...]), t128, half128)
    bkf_ref[...] = bk
    bkb_ref[...] = bk.astype(BF16)
    o += hd
    bv = z_ref[:, o:o + hd]
    bvf_ref[...] = bv
    bvb_ref[...] = bv.astype(BF16)
    o += hd
    for c in range(H_IDX * D_IDX // 128):
        qi = _rope(z_ref[:, o + c * 128:o + (c + 1) * 128], t64, half64) * (D_IDX ** -0.5)
        qi_ref[:, c * 128:(c + 1) * 128] = qi.astype(BF16)
    o += H_IDX * D_IDX
    kiw = _rope(z_ref[:, o:o + 128], tki_ref[...], half64)
    kiw_ref[...] = kiw
    lane = lax.broadcasted_iota(jnp.int32, kiw.shape, 1)
    ki = jnp.where(lane < D_IDX, kiw, 0.0)
    kilo_ref[...] = ki.astype(BF16)
    kihi_ref[...] = pltpu.roll(ki, D_IDX, 1).astype(BF16)


def post_in(z, tabs, gains, t_len):
    n = z.shape[0]
    tm = _row_tile(t_len, 512)
    nt = t_len // tm
    rows = lambda w: pl.BlockSpec((tm, w), lambda i: (i, 0))
    tab = pl.BlockSpec((3, tm, 128), lambda i: (0, i % nt, 0))
    gain = pl.BlockSpec((1, HEAD_DIM), lambda i: (0, 0))
    per_head = pl.BlockSpec((tm, H_A, HEAD_DIM), lambda i: (i, 0, 0))
    widths = [(A_W, BF16), (A_W, F32), (A_W, BF16), (A_W, F32), (A_W, BF16), (B_W, BF16),
              (HEAD_DIM, F32), (HEAD_DIM, BF16), (HEAD_DIM, F32), (HEAD_DIM, BF16),
              (H_IDX * D_IDX, BF16), (128, F32), (128, BF16), (128, BF16)]
    return pl.pallas_call(
        _post_in_kernel,
        grid=(n // tm,),
        in_specs=[rows(ATT_W), tab, tab, tab, gain, gain, gain, gain],
        out_specs=[per_head if i in (1, 3) else rows(w) for i, (w, _) in enumerate(widths)],
        out_shape=[jax.ShapeDtypeStruct((n, H_A, HEAD_DIM) if i in (1, 3) else (n, w), dt)
                   for i, (w, dt) in enumerate(widths)],
        compiler_params=_params("parallel"),
        name="post_in",
    )(z, *tabs, *[g.reshape(1, HEAD_DIM) for g in gains])


LOG2E = 1.4426950408889634
ATTN_A_ROW_CHUNK = 256
ATTN_A_BLOCKS_PER_STEP = 4


def _attn_a_kernel(q_ref, k_ref, v_ref, tri_ref, o_ref, *, tq, tk, rc, past):
    i = pl.program_id(2)
    q0 = past + i * tq
    tri = tri_ref[...]
    n_full = q0 // tk
    n_all = (q0 + tq - 1 + tk - 1) // tk
    nrc = tq // rc
    qs = [q_ref[c * rc:(c + 1) * rc, :] for c in range(nrc)]
    rows = [q0 + c * rc + lax.broadcasted_iota(jnp.int32, (rc, tk), 0) for c in range(nrc)]
    col = lax.broadcasted_iota(jnp.int32, (rc, tk), 1)

    def step(j, carry, masked, nsub):
        cs = range(nrc)
        ss = range(nsub)
        offs = [pl.multiple_of((j - s) * tk, tk) for s in ss]
        kb = [k_ref[pl.ds(o, tk), :] for o in offs]
        vb = [v_ref[pl.ds(o, tk), :] for o in offs]
        later = [carry[0][c * rc:(c + 1) * rc] for c in cs]
        acc = [carry[1][c * rc:(c + 1) * rc] for c in cs]
        nz = [[_nt_dot(qs[c], kb[s]) for c in cs] for s in ss]
        loss = [[jnp.log(1.0 + jnp.exp2(jnp.minimum(nz[s][c] * (-LOG2E), 126.0))) for c in cs] for s in ss]
        if masked:
            causal = [[(offs[s] + col) < rows[c] for c in cs] for s in ss]
            loss = [[jnp.where(causal[s][c], loss[s][c], 0.0) for c in cs] for s in ss]
        within = [[jnp.dot(loss[s][c].astype(BF16), tri, preferred_element_type=F32) for c in cs] for s in ss]
        for s in ss:
            w = [jnp.exp2((nz[s][c] + later[c] + within[s][c]) * (-LOG2E)) for c in cs]
            if masked:
                w = [jnp.where(causal[s][c], w[c], 0.0) for c in cs]
            acc = [acc[c] + jnp.dot(w[c].astype(BF16), vb[s], preferred_element_type=F32) for c in cs]
            later = [later[c] + within[s][c][:, 0:1] for c in cs]
        return jnp.concatenate(later, axis=0), jnp.concatenate(acc, axis=0)

    carry = (jnp.zeros((tq, 1), F32), jnp.zeros((tq, HEAD_DIM), F32))
    carry = lax.fori_loop(0, n_all - n_full, lambda t, c: step(n_all - 1 - t, c, True, 1), carry)
    n_multi = n_full // ATTN_A_BLOCKS_PER_STEP
    n_single = n_full - n_multi * ATTN_A_BLOCKS_PER_STEP
    carry = lax.fori_loop(0, n_single, lambda t, c: step(n_full - 1 - t, c, False, 1), carry)
    carry = lax.fori_loop(
        0, n_multi,
        lambda t, c: step(n_full - n_single - 1 - t * ATTN_A_BLOCKS_PER_STEP, c, False, ATTN_A_BLOCKS_PER_STEP),
        carry)
    o_ref[...] = carry[1].astype(o_ref.dtype)


def attn_a(q, k, v, past, tk):
    b, t, _ = q.shape
    lp = k.shape[1]
    tq = _row_tile(t, 512)
    rc = min(tq, ATTN_A_ROW_CHUNK)
    tri = jnp.tri(tk, dtype=BF16)
    return pl.pallas_call(
        functools.partial(_attn_a_kernel, tq=tq, tk=tk, rc=rc, past=past),
        grid=(b, H_A, t // tq),
        in_specs=[pl.BlockSpec((None, tq, HEAD_DIM), lambda bi, h, i: (bi, i, h)),
                  pl.BlockSpec((None, lp, HEAD_DIM), lambda bi, h, i: (bi, 0, h)),
                  pl.BlockSpec((None, lp, HEAD_DIM), lambda bi, h, i: (bi, 0, h)),
                  pl.BlockSpec((tk, tk), lambda bi, h, i: (0, 0))],
        out_specs=pl.BlockSpec((None, tq, HEAD_DIM), lambda bi, h, i: (bi, i, h)),
        out_shape=jax.ShapeDtypeStruct((b, t, A_W), BF16),
        compiler_params=_params("parallel", "parallel", "parallel"),
        name="attn_a",
    )(q, k, v, tri)


DSA_QUERY_ROWS = 256
DSA_ATTEND_ROWS = 128


def _dsa_kernel(qi_ref, kiw_ref, kilo_ref, kihi_ref, q_ref, k_ref, v_ref, triu_ref, o_ref, key_ref,
                *, tq, tk, past, n_keys, topk):
    i = pl.program_id(1)
    q0 = past + i * tq
    pos = q0 + lax.broadcasted_iota(jnp.int32, (tq, 1), 0)
    lim = jnp.minimum((pos // CHUNK + 1) * CHUNK, n_keys)
    n_adm = jnp.minimum(((q0 + tq - 1) // CHUNK + 1) * CHUNK, n_keys)
    nt = (n_adm + tk - 1) // tk
    col = lax.broadcasted_iota(jnp.int32, (tq, tk), 1)
    ngrp = tk // 128

    qi = qi_ref[...]
    kiw = kiw_ref[...]
    wi = [kiw[:, D_IDX + h:D_IDX + h + 1] for h in range(H_IDX)]

    n_pair = (nt + 1) // 2
    pair = range(2)

    def score_pair(t, _):
        offs = [pl.multiple_of((2 * t + u) * tk, tk) for u in pair]
        kis = [(kilo_ref[pl.ds(o, tk), :], kihi_ref[pl.ds(o, tk), :]) for o in offs]
        s_idx = [[_nt_dot(qi[:, (h // 2) * 128:(h // 2 + 1) * 128], kis[u][h % 2]) for h in range(H_IDX)]
                 for u in pair]
        for u in pair:
            s = jnp.zeros((tq, tk), F32)
            for h in range(H_IDX):
                s = s + wi[h] * jnp.maximum(s_idx[u][h], 0.0)
            s = jnp.where(s == 0.0, 0.0, s)
            s = jnp.where(offs[u] + col < lim, s, -jnp.inf)
            bits = lax.bitcast_convert_type(s, jnp.int32)
            key_ref[:, pl.ds(offs[u], tk)] = bits ^ ((bits >> 31) & 0x7FFFFFFF)
        return 0

    lax.fori_loop(0, n_pair, score_pair, 0)

    cr = min(tq, DSA_ATTEND_ROWS)

    def count_ge(cand):
        accs = []
        for r in range(tq // cr):
            cand_b = jnp.broadcast_to(cand[r * cr:(r + 1) * cr], (cr, 128))

            def body(t, acc):
                off = pl.multiple_of(t * tk, tk)
                kt = key_ref[r * cr:(r + 1) * cr, pl.ds(off, tk)]
                for g in range(ngrp):
                    acc = acc + jnp.where(kt[:, g * 128:(g + 1) * 128] >= cand_b, 1, 0)
                return acc

            accs.append(lax.fori_loop(0, nt, body, jnp.zeros((cr, 128), jnp.int32)))
        return jnp.sum(jnp.concatenate(accs, axis=0), axis=1, keepdims=True)

    def bit_step(b, t_u):
        cand_u = t_u | lax.shift_left(jnp.int32(1), 31 - b)
        ok = count_ge(cand_u ^ INT_MIN) >= topk
        return jnp.where(ok, cand_u, t_u)

    t_u = lax.fori_loop(0, 32, bit_step, jnp.zeros((tq, 1), jnp.int32))
    tau = t_u ^ INT_MIN
    tau_next = jnp.where(tau == 2 ** 31 - 1, tau, tau + 1)
    cnt_gt = jnp.where(tau == 2 ** 31 - 1, 0, count_ge(tau_next))
    need = jnp.where(tau == KEY_NEG_INF, 0, topk - cnt_gt).astype(F32)

    triu = triu_ref[...]

    def select_tile(t, seen):
        off = pl.multiple_of(t * tk, tk)
        kt = key_ref[:, pl.ds(off, tk)]
        eq = kt == tau
        prefix = seen + jnp.dot(jnp.where(eq, 1.0, 0.0).astype(BF16), triu, preferred_element_type=F32)
        rank = jnp.where(eq, prefix, jnp.inf)
        bias = jnp.where(kt > tau, 0.0, jnp.where(rank <= need, 0.0, NEG_BIG)).astype(F32)
        key_ref[:, pl.ds(off, tk)] = lax.bitcast_convert_type(bias, jnp.int32)
        return prefix[:, tk - 1:tk]

    lax.fori_loop(0, 2 * n_pair, select_tile, jnp.zeros((tq, 1), F32))

    rg = min(tq, DSA_ATTEND_ROWS)
    groups = range(tq // rg)
    gr = H_B * rg
    qs = [jnp.concatenate([q_ref[r * rg:(r + 1) * rg, h * HEAD_DIM:(h + 1) * HEAD_DIM] for h in range(H_B)], axis=0)
          for r in groups]

    def attend_pair(t, carry):
        offs = [pl.multiple_of((2 * t + u) * tk, tk) for u in pair]
        kb = [k_ref[pl.ds(o, tk), :] for o in offs]
        vb = [v_ref[pl.ds(o, tk), :] for o in offs]
        new = []
        for r in groups:
            m, l, acc = [c[r * gr:(r + 1) * gr] for c in carry]
            bias = [lax.bitcast_convert_type(key_ref[r * rg:(r + 1) * rg, pl.ds(o, tk)], F32) for o in offs]
            logits = [(_nt_dot(qs[r], kb[u]).reshape(H_B, rg, tk) + bias[u][None]).reshape(gr, tk) for u in pair]
            m_new = jnp.maximum(m, jnp.maximum(jnp.max(logits[0], axis=1, keepdims=True),
                                               jnp.max(logits[1], axis=1, keepdims=True)))
            alpha = jnp.exp(m - m_new)
            p = [jnp.exp(logits[u] - m_new) for u in pair]
            l = alpha * l + (jnp.sum(p[0], axis=1, keepdims=True) + jnp.sum(p[1], axis=1, keepdims=True))
            acc = alpha * acc + (jnp.dot(p[0].astype(BF16), vb[0], preferred_element_type=F32)
                                 + jnp.dot(p[1].astype(BF16), vb[1], preferred_element_type=F32))
            new.append((m_new, l, acc))
        return tuple(jnp.concatenate([n[k] for n in new], axis=0) for k in range(3))

    init = (jnp.full((H_B * tq, 1), NEG_BIG, F32), jnp.zeros((H_B * tq, 1), F32),
            jnp.zeros((H_B * tq, HEAD_DIM), F32))
    _, l, acc = lax.fori_loop(0, n_pair, attend_pair, init)
    out = acc / l
    for r in groups:
        for h in range(H_B):
            rows = out[r * gr + h * rg:r * gr + (h + 1) * rg]
            o_ref[r * rg:(r + 1) * rg, h * HEAD_DIM:(h + 1) * HEAD_DIM] = rows.astype(o_ref.dtype)


def dsa(qi, kiw, kilo, kihi, q, k, v, past, n_keys, tk):
    b, t, _ = q.shape
    lp = k.shape[1]
    tq = _row_tile(t, DSA_QUERY_ROWS)
    topk = min(TOPK_MAX, n_keys // 4)
    assert lp % (2 * tk) == 0 and lp >= topk
    triu = jnp.tri(tk, dtype=BF16).T
    qblk = lambda w: pl.BlockSpec((None, tq, w), lambda bi, i: (bi, i, 0))
    kblk = lambda w: pl.BlockSpec((None, lp, w), lambda bi, i: (bi, 0, 0))
    return pl.pallas_call(
        functools.partial(_dsa_kernel, tq=tq, tk=tk, past=past, n_keys=n_keys, topk=topk),
        grid=(b, t // tq),
        in_specs=[qblk(H_IDX * D_IDX), qblk(128), kblk(128), kblk(128), qblk(B_W),
                  kblk(HEAD_DIM), kblk(HEAD_DIM), pl.BlockSpec((tk, tk), lambda bi, i: (0, 0))],
        out_specs=qblk(B_W),
        out_shape=jax.ShapeDtypeStruct((b, t, B_W), BF16),
        scratch_shapes=[pltpu.VMEM((tq, lp), jnp.int32)],
        compiler_params=_params("parallel", "parallel"),
        name="dsa",
    )(qi, kiw, kilo, kihi, q, k, v, triu)


def _softplus(x):
    return jnp.maximum(x, 0.0) + jnp.log(1.0 + jnp.exp(-jnp.abs(x)))


def _c_prep_kernel(*refs, first_layer):
    if first_layer:
        (cz_ref, prev_ref, mu_ref, w0_ref, a0_ref, w2_ref, a2_ref, g2_ref, kk_ref, ka_ref,
         r_ref, kn_ref, k_ref, v_ref, lw_ref, a_ref, g_ref, vtok_ref) = refs
    else:
        (cz_ref, prev_ref, mu_ref, w0_ref, a0_ref, w2_ref, a2_ref, g2_ref, kk_ref, ka_ref,
         vf_ref, v0_ref, v1_ref, v2_ref,
         r_ref, kn_ref, k_ref, v_ref, lw_ref, a_ref, g_ref) = refs

    def put_heads(ref, x):
        for pair in range(H_C // 2):
            tile = x[:, pair * 128:(pair + 1) * 128]
            ref[2 * pair] = tile[:, :N_C]
            ref[2 * pair + 1] = pltpu.roll(tile, N_C, 1)[:, :N_C]

    cz = cz_ref[...]
    rowid = lax.broadcasted_iota(jnp.int32, cz.shape, 0)
    prev = jnp.where(rowid == 0, prev_ref[...], pltpu.roll(cz, 1, 0))
    zz = cz + (prev - cz) * mu_ref[...]
    r = zz[:, 0:C_W]
    k = zz[:, C_W:2 * C_W]
    v = zz[:, 2 * C_W:3 * C_W]
    tail = zz[:, 3 * C_W:C_PAD]
    w = w0_ref[...] + jnp.dot(jnp.tanh(tail).astype(BF16), w2_ref[...], preferred_element_type=F32)
    w = -_softplus(-w) - 0.5
    a = jax.nn.sigmoid(a0_ref[...] + jnp.dot(tail.astype(BF16), a2_ref[...], preferred_element_type=F32))
    g = jnp.dot(jax.nn.sigmoid(tail).astype(BF16), g2_ref[...], preferred_element_type=F32)
    if not first_layer:
        lora = jnp.dot(v.astype(BF16), v1_ref[...], preferred_element_type=F32)
        lora = jnp.dot(lora.astype(BF16), v2_ref[...], preferred_element_type=F32)
        v = v + (vf_ref[...] - v) * jax.nn.sigmoid(v0_ref[...] + lora)
    if first_layer:
        vtok_ref[...] = v
    put_heads(r_ref, r)
    put_heads(kn_ref, k * kk_ref[...])
    put_heads(k_ref, k * (1.0 + (a - 1.0) * ka_ref[...]))
    put_heads(v_ref, v)
    put_heads(lw_ref, -jnp.exp(w))
    put_heads(a_ref, a)
    put_heads(g_ref, g)


def c_prep(cz, prev_rows, vecs, mats, v_first, vres, tm):
    b, t, _ = cz.shape
    first_layer = vres is None
    blk = lambda w: pl.BlockSpec((None, tm, w), lambda bi, i: (bi, i, 0))
    vec = lambda w: pl.BlockSpec((1, w), lambda bi, i: (0, 0))
    mat = lambda m: pl.BlockSpec(m.shape, lambda bi, i: (0, 0))
    mu, w0, a0, k_k, k_a = vecs
    w2, a2, g2 = mats
    ins = [cz, prev_rows, mu, w0, a0, w2, a2, g2, k_k, k_a]
    specs = [blk(C_PAD), pl.BlockSpec((None, None, 1, C_PAD), lambda bi, i: (bi, i, 0, 0)),
             vec(C_PAD), vec(C_W), vec(C_W), mat(w2), mat(a2), mat(g2), vec(C_W), vec(C_W)]
    if not first_layer:
        v0, v1, v2 = vres
        ins += [v_first, v0, v1, v2]
        specs += [blk(C_W), vec(C_W), mat(v1), mat(v2)]
    heads = pl.BlockSpec((None, H_C, tm, N_C), lambda bi, i: (bi, 0, i, 0))
    out_specs = [heads] * 7
    out_shape = [jax.ShapeDtypeStruct((b, H_C, t, N_C), F32)] * 7
    if first_layer:
        out_specs.append(blk(C_W))
        out_shape.append(jax.ShapeDtypeStruct((b, t, C_W), F32))
    return pl.pallas_call(
        functools.partial(_c_prep_kernel, first_layer=first_layer),
        grid=(b, t // tm),
        in_specs=specs,
        out_specs=out_specs,
        out_shape=out_shape,
        compiler_params=_params("parallel", "parallel"),
        name="c_prep",
    )(*ins)


def _bdot(a, b, contract_a, contract_b):
    return lax.dot_general(a.astype(BF16), b.astype(BF16), (((contract_a,), (contract_b,)), ((0,), (0,))),
                           preferred_element_type=F32)


def _bdot3(a, b, contract_a, contract_b):
    ah = a.astype(BF16)
    bh = b.astype(BF16)
    al = (a - ah.astype(F32)).astype(BF16)
    bl = (b - bh.astype(F32)).astype(BF16)
    dn = (((contract_a,), (contract_b,)), ((0,), (0,)))
    dot = lambda x, y: lax.dot_general(x, y, dn, preferred_element_type=F32)
    return dot(ah, bh) + (dot(al, bh) + dot(ah, bl))


def _c_scan_kernel(r_ref, kn_ref, k_ref, v_ref, lw_ref, a_ref, g_ref, s0_ref, rk_ref, lng_ref, lnb_ref,
                   y_ref, st_ref, s_ref, *, nck):
    c_len = SCAN_CHUNK

    @pl.when(pl.program_id(1) == 0)
    def _():
        s_ref[...] = s0_ref[...]

    ti = lax.broadcasted_iota(jnp.int32, (H_C, c_len, c_len), 1)
    si = lax.broadcasted_iota(jnp.int32, (H_C, c_len, c_len), 2)
    incl = si <= ti
    strict = si < ti
    tri = jnp.where(incl, 1.0, 0.0)
    eye = jnp.where(si == ti, 1.0, 0.0)
    r_k = rk_ref[...]
    ln_g = lng_ref[...]
    ln_b = lnb_ref[...]

    def chunk(c, _):
        rows = pl.ds(pl.multiple_of(c * c_len, c_len), c_len)
        r = r_ref[:, rows, :]
        k = k_ref[:, rows, :]
        v = v_ref[:, rows, :]
        lw = lw_ref[:, rows, :]
        a = a_ref[:, rows, :]
        kn = kn_ref[:, rows, :]
        kn = kn * lax.rsqrt(jnp.maximum(jnp.sum(kn * kn, axis=-1, keepdims=True), 1e-24))
        aa = -kn
        bb = kn * a
        g = _bdot3(tri, lw, 2, 1)
        g_end = g[:, c_len - 1:c_len, :]
        a_t = aa * jnp.exp(g - lw)
        r_t = r * jnp.exp(g)
        inv = jnp.exp(-g)
        b_h = bb * inv
        k_h = k * inv
        to_end = jnp.exp(g_end - g)
        b_e = bb * to_end
        k_e = k * to_end
        d_end = jnp.exp(g_end)
        a_ab = jnp.where(strict, _bdot(a_t, b_h, 2, 2), 0.0)
        a_ak = jnp.where(strict, _bdot(a_t, k_h, 2, 2), 0.0)
        a_rb = jnp.where(incl, _bdot(r_t, b_h, 2, 2), 0.0)
        a_rk = jnp.where(incl, _bdot(r_t, k_h, 2, 2), 0.0)
        t_inv = eye + a_ab
        pw = a_ab
        for _ in range(4):
            pw = _bdot(pw, pw, 2, 1)
            t_inv = t_inv + _bdot(t_inv, pw, 2, 1)
        p_mat = _bdot(t_inv, a_t, 2, 1)
        q_mat = _bdot(t_inv, _bdot(a_ak, v, 2, 1), 2, 1)
        y_v = _bdot(a_rk, v, 2, 1)
        s = s_ref[...]
        u = _bdot(p_mat, s, 2, 2) + q_mat
        y = _bdot(r_t, s, 2, 2) + _bdot(a_rb, u, 2, 1) + y_v
        uv = jnp.concatenate([u, v], axis=1)
        bk = jnp.concatenate([b_e, k_e], axis=1)
        s_ref[...] = s * d_end + _bdot(uv, bk, 1, 1)
        mu = jnp.mean(y, axis=-1, keepdims=True)
        yc = y - mu
        yn = yc * lax.rsqrt(jnp.mean(yc * yc, axis=-1, keepdims=True) + LNX_EPS)
        yn = yn * ln_g + ln_b
        bonus = jnp.sum(r * k * r_k, axis=-1, keepdims=True) * v
        o = (yn + bonus) * g_ref[:, rows, :]
        y_ref[rows, :] = jnp.concatenate([o[h] for h in range(H_C)], axis=-1).astype(y_ref.dtype)
        return 0

    lax.fori_loop(0, nck, chunk, 0)

    @pl.when(pl.program_id(1) == pl.num_programs(1) - 1)
    def _():
        st_ref[...] = s_ref[...]


def c_scan(r, kn, k, v, lw, a, g, s0, r_k, ln_g, ln_b, tm):
    b, _, t, _ = r.shape
    blk = pl.BlockSpec((None, H_C, tm, N_C), lambda bi, i: (bi, 0, i, 0))
    st = pl.BlockSpec((None, H_C, N_C, N_C), lambda bi, i: (bi, 0, 0, 0))
    vec = pl.BlockSpec((H_C, 1, N_C), lambda bi, i: (0, 0, 0))
    return pl.pallas_call(
        functools.partial(_c_scan_kernel, nck=tm // SCAN_CHUNK),
        grid=(b, t // tm),
        in_specs=[blk] * 7 + [st, vec, vec, vec],
        out_specs=[pl.BlockSpec((None, tm, C_W), lambda bi, i: (bi, i, 0)), st],
        out_shape=[jax.ShapeDtypeStruct((b, t, C_W), BF16),
                   jax.ShapeDtypeStruct((b, H_C, N_C, N_C), F32)],
        scratch_shapes=[pltpu.VMEM((H_C, N_C, N_C), F32)],
        compiler_params=_params("parallel", "arbitrary"),
        name="c_scan",
    )(r, kn, k, v, lw, a, g, s0, r_k.reshape(H_C, 1, N_C), ln_g.reshape(H_C, 1, N_C), ln_b.reshape(H_C, 1, N_C))


def _pad_rows(x, rows, axis=1):
    pad = rows - x.shape[axis]
    if pad == 0:
        return x
    widths = [(0, 0)] * x.ndim
    widths[axis] = (0, pad)
    return jnp.pad(x, widths)


ATT_COLS = 3 * A_W + B_W + 2 * HEAD_DIM + H_IDX * D_IDX + D_IDX + H_IDX
IN_COLS = ATT_COLS + C_COLS + 3 * D_MODEL


def _regroup_kernel(w_ref, tail_ref, att_ref, c_ref, g_ref):
    lane = lax.broadcasted_iota(jnp.int32, (w_ref.shape[0], 128), 1)
    n_full = IN_COLS // 128

    def tile(j):
        return tail_ref[...] if j == n_full else w_ref[:, j * 128:(j + 1) * 128]

    def shifted(j0, shift, j):
        joined = jnp.where(lane >= shift, tile(j0 + j), tile(j0 + j + 1))
        return pltpu.roll(joined, 128 - shift, 1)

    for j in range(ATT_W // 128):
        x = tile(j)
        if (j + 1) * 128 > ATT_COLS:
            x = jnp.where(lane < ATT_COLS - j * 128, x, 0.0)
        att_ref[:, j * 128:(j + 1) * 128] = x.astype(BF16)
    for j in range(C_PAD // 128):
        x = shifted(ATT_COLS // 128, ATT_COLS % 128, j)
        if (j + 1) * 128 > C_COLS:
            x = jnp.where(lane < C_COLS - j * 128, x, 0.0)
        c_ref[:, j * 128:(j + 1) * 128] = x.astype(BF16)
    g0 = ATT_COLS + C_COLS
    for j in range(3 * D_MODEL // 128):
        g_ref[:, j * 128:(j + 1) * 128] = shifted(g0 // 128, g0 % 128, j).astype(BF16)


def regroup_w_in(w_in):
    depth, d, cols = w_in.shape
    assert cols == IN_COLS and (ATT_COLS + C_COLS) % 128 != 0
    n_full = cols // 128
    tail = jnp.pad(w_in[:, :, n_full * 128:], ((0, 0), (0, 0), (0, (n_full + 1) * 128 - cols)))
    tm = 128
    blk = lambda w: pl.BlockSpec((None, tm, w), lambda l, i: (l, i, 0))
    return pl.pallas_call(
        _regroup_kernel,
        grid=(depth, d // tm),
        in_specs=[blk(cols), blk(128)],
        out_specs=[blk(ATT_W), blk(C_PAD), blk(3 * D_MODEL)],
        out_shape=[jax.ShapeDtypeStruct((depth, d, w), BF16) for w in (ATT_W, C_PAD, 3 * D_MODEL)],
        compiler_params=_params("parallel", "parallel"),
        name="regroup_w_in",
    )(w_in, tail)


def _prep_weights(w):
    out = []
    w_att, w_c, w_g = regroup_w_in(w["w_in"])
    for i in range(DEPTH):
        tail_rows = lambda m, off: jnp.pad(m, ((off, C_TAIL - off - m.shape[0]), (0, 0))).astype(BF16)
        lw = dict(
            w_att=w_att[i], w_c=w_c[i], w_g=w_g[i],
            w2=tail_rows(w["c_w2"][i], 0),
            a2=tail_rows(w["c_a2"][i], D_DECAY_LORA),
            g2=tail_rows(w["c_g2"][i], D_DECAY_LORA + D_AAA_LORA),
            mu=jnp.pad(w["c_mu"][i], (0, C_PAD - C_COLS)).reshape(1, C_PAD),
            w_br_a=w["w_br_a"][i].astype(BF16), w_br_b=w["w_br_b"][i].astype(BF16),
            w_br_c=w["w_br_c"][i].astype(BF16), w_out=w["w_out"][i].astype(BF16),
            w_ffn_in=w["w_ffn_in"][i].astype(BF16), w_ffn_out=w["w_ffn_out"][i].astype(BF16),
            w_ple_gate=w["w_ple_gate"][i].astype(BF16), w_ple_proj=w["w_ple_proj"][i].astype(BF16),
        )
        if i > 0:
            lw["v1"] = w["c_v1"][i - 1].astype(BF16)
            lw["v2"] = w["c_v2"][i - 1].astype(BF16)
        out.append(lw)
    return out


def _run_trunk(x, p, past, caches, w, wl):
    b, t, _ = x.shape
    n = b * t
    n_keys = past + t
    pos = past + jnp.arange(t)
    lane = jnp.arange(128)
    tabs = (_rope_tables(pos, 128, HEAD_DIM, HEAD_DIM // ROT_FRACTION),
            _rope_tables(pos, 128, D_IDX, D_IDX // ROT_FRACTION),
            _rope_tables(pos, 128, 128, D_IDX // ROT_FRACTION,
                         scale_lanes=(lane < D_IDX, jnp.where(lane < D_IDX + H_IDX, H_IDX ** -0.5, 1.0))))
    tk_a = 256
    tk_b = 512
    lp_a = -(-n_keys // tk_a) * tk_a
    lp_b = -(-n_keys // (2 * tk_b)) * (2 * tk_b)
    t_c = -(-t // SCAN_CHUNK) * SCAN_CHUNK
    tm_c = _row_tile(t, 256)
    tm_s = _row_tile(t_c, 256)
    h = x.reshape(n, D_MODEL)
    outs = ([], [], [], [], [], [], [])
    v_first = None
    for i in range(DEPTH):
        lw = wl[i]
        z_att = mm_norm(h, w["norm_mix"][i], lw["w_att"], ATT_W // 3)
        z_c = mm_norm(h, w["norm_mix"][i], lw["w_c"], C_PAD // 4)
        z_g = mm_norm(h, w["norm_mix"][i], lw["w_g"], 1024)
        (aq, ak_f, ak_b, av_f, av_b, bq, bk_f, bk_b, bv_f, bv_b, qi, kiw, kilo, kihi) = post_in(
            z_att, tabs, (w["a_q_norm"][i], w["a_k_norm"][i], w["b_q_norm"][i], w["b_k_norm"][i]), t)
        seq = lambda a: a.reshape(b, t, a.shape[-1])

        def keys(new, old, lp):
            new = seq(new)
            if old is not None:
                new = jnp.concatenate([old.astype(BF16), new], axis=1)
            return _pad_rows(new, lp)

        c = caches
        past_of = lambda name: None if c is None else c[name][i]
        pa_k = None if c is None else c["a_k"][i].reshape(b, past, A_W)
        pa_v = None if c is None else c["a_v"][i].reshape(b, past, A_W)
        o_a = attn_a(seq(aq), keys(ak_b, pa_k, lp_a), keys(av_b, pa_v, lp_a), past, tk_a)
        if c is None:
            p_lo = p_hi = None
        else:
            p_lo = jnp.pad(c["b_ki"][i], ((0, 0), (0, 0), (0, 128 - D_IDX)))
            p_hi = jnp.pad(c["b_ki"][i], ((0, 0), (0, 0), (128 - D_IDX, 0)))
        o_b = dsa(seq(qi), seq(kiw), keys(kilo, p_lo, lp_b), keys(kihi, p_hi, lp_b), seq(bq),
                  keys(bk_b, past_of("b_k"), lp_b), keys(bv_b, past_of("b_v"), lp_b), past, n_keys, tk_b)
        cz = seq(z_c)
        shift0 = jnp.zeros((b, C_COLS), F32) if c is None else c["shift"][i]
        shift0 = jnp.pad(shift0, ((0, 0), (0, C_PAD - C_COLS)))
        prev_rows = jnp.concatenate([shift0[:, None, :], cz[:, tm_c - 1:t - 1:tm_c, :]], axis=1)
        prev_rows = prev_rows.reshape(b, t // tm_c, 1, C_PAD)
        row = lambda a: a.reshape(1, C_W)
        vres = None if i == 0 else (row(w["c_v0"][i - 1]), lw["v1"], lw["v2"])
        prep = c_prep(cz, prev_rows,
                      (lw["mu"], row(w["c_w0"][i]), row(w["c_a0"][i]), row(w["c_k_k"][i]), row(w["c_k_a"][i])),
                      (lw["w2"], lw["a2"], lw["g2"]), v_first, vres, tm_c)
        if i == 0:
            v_first = prep[7]
        s0 = jnp.zeros((b, H_C, N_C, N_C), F32) if c is None else c["wkv"][i]
        y_c, wkv_t = c_scan(*[_pad_rows(a, t_c, axis=2) for a in prep[:7]], s0,
                            w["c_r_k"][i], w["c_ln_g"][i], w["c_ln_b"][i], tm_s)
        o_c = y_c[:, :t].reshape(n, C_W)
        merged = mm_gated3(o_a.reshape(n, A_W), o_b.reshape(n, B_W), o_c,
                           lw["w_br_a"], lw["w_br_b"], lw["w_br_c"], z_g)
        h = mm_res(merged, lw["w_out"], h)
        h = ffn(h, w["norm_ffn"][i], lw["w_ffn_in"], lw["w_ffn_out"])
        h = ple(h, w["norm_ple"][i], lw["w_ple_gate"], p[i].reshape(n, -1).astype(BF16), lw["w_ple_proj"])
        vals = (ak_f.reshape(b, t, H_A, HEAD_DIM), av_f.reshape(b, t, H_A, HEAD_DIM),
                seq(bk_f), seq(bv_f), seq(kiw)[:, :, :D_IDX], wkv_t, cz[:, -1, :C_COLS])
        for lst, val in zip(outs, vals):
            lst.append(val)
    return h.reshape(b, t, D_MODEL), [jnp.stack(l) for l in outs]


def kernel(x_prompt, x_sample, cache_a_k, cache_a_v, cache_b_k, cache_b_v, cache_b_kidx, state_c_wkv, state_c_shift, p_prompt, p_sample, norm_mix, w_in, a_q_norm, a_k_norm, b_q_norm, b_k_norm, c_mu, c_w0, c_w2, c_a0, c_a2, c_g2, c_v0, c_v1, c_v2, c_k_k, c_k_a, c_r_k, c_ln_g, c_ln_b, w_br_a, w_br_b, w_br_c, w_out, norm_ffn, w_ffn_in, w_ffn_out, norm_ple, w_ple_gate, w_ple_proj):
    w = dict(norm_mix=norm_mix, w_in=w_in, a_q_norm=a_q_norm, a_k_norm=a_k_norm, b_q_norm=b_q_norm,
             b_k_norm=b_k_norm, c_mu=c_mu, c_w0=c_w0, c_w2=c_w2, c_a0=c_a0, c_a2=c_a2, c_g2=c_g2,
             c_v0=c_v0, c_v1=c_v1, c_v2=c_v2, c_k_k=c_k_k, c_k_a=c_k_a, c_r_k=c_r_k, c_ln_g=c_ln_g,
             c_ln_b=c_ln_b, w_br_a=w_br_a, w_br_b=w_br_b, w_br_c=w_br_c, w_out=w_out, norm_ffn=norm_ffn,
             w_ffn_in=w_ffn_in, w_ffn_out=w_ffn_out, norm_ple=norm_ple, w_ple_gate=w_ple_gate,
             w_ple_proj=w_ple_proj)
    wl = _prep_weights(w)
    y_p, o_p = _run_trunk(x_prompt, p_prompt, 0, None, w, wl)
    caches = dict(a_k=cache_a_k, a_v=cache_a_v, b_k=cache_b_k, b_v=cache_b_v, b_ki=cache_b_kidx,
                  wkv=state_c_wkv, shift=state_c_shift)
    y_s, o_s = _run_trunk(x_sample, p_sample, cache_a_k.shape[2], caches, w, wl)
    return (y_p, y_s, *o_p, *o_s)
```

```python
import functools
import math

import jax
import jax.numpy as jnp
from jax import lax
from jax.experimental import pallas as pl
from jax.experimental.pallas import tpu as pltpu

F32 = jnp.float32
BF16 = jnp.bfloat16

D_MODEL = 2048
DEPTH = 2
HEAD_DIM = 128
H_A = 4
A_W = H_A * HEAD_DIM
H_B = 4
B_W = H_B * HEAD_DIM
H_IDX = 4
D_IDX = 64
TOPK_MAX = 256
CHUNK = 64
H_C = 16
N_C = 64
C_W = H_C * N_C
SCAN_CHUNK = 32
D_DECAY_LORA = 96
D_AAA_LORA = 96
D_GATE_LORA = 256
C_COLS = 3 * C_W + D_DECAY_LORA + D_AAA_LORA + D_GATE_LORA
C_PAD = 3584
C_TAIL = C_PAD - 3 * C_W
ATT_W = 3 * A_W + B_W + 2 * HEAD_DIM + H_IDX * D_IDX + 128
ROPE_THETA = 500000.0
ROT_FRACTION = 4
NORM_EPS = 1e-6
LNX_EPS = 64e-5
NEG_BIG = -1e30

VMEM_LIMIT = 56 * 1024 * 1024
INT_MIN = -(2 ** 31)
KEY_NEG_INF = (0xFF800000 ^ 0x7FFFFFFF) - 2 ** 32


def _params(*sem):
    return pltpu.CompilerParams(dimension_semantics=sem, vmem_limit_bytes=VMEM_LIMIT)


def _row_tile(n, pref):
    t = min(n, pref)
    assert n % t == 0, (n, t)
    return t


def _nt_dot(a, b):
    return lax.dot_general(a, b, (((1,), (1,)), ((), ())), preferred_element_type=F32)


def _rms(x, gain):
    xn = x * lax.rsqrt(jnp.mean(x * x, axis=-1, keepdims=True) + NORM_EPS)
    return xn * gain


def _mm_norm_kernel(x_ref, g_ref, w_ref, o_ref, xn_ref):
    @pl.when(pl.program_id(1) == 0)
    def _():
        xn_ref[...] = _rms(x_ref[...], g_ref[...]).astype(BF16)

    o_ref[...] = jnp.dot(xn_ref[...], w_ref[...], preferred_element_type=F32)


def mm_norm(x, gain, w, tn):
    n, k = x.shape
    nc = w.shape[1]
    tm = _row_tile(n, 1024)
    assert nc % tn == 0
    return pl.pallas_call(
        _mm_norm_kernel,
        grid=(n // tm, nc // tn),
        in_specs=[pl.BlockSpec((tm, k), lambda i, j: (i, 0)),
                  pl.BlockSpec((1, k), lambda i, j: (0, 0)),
                  pl.BlockSpec((k, tn), lambda i, j: (0, j))],
        out_specs=pl.BlockSpec((tm, tn), lambda i, j: (i, j)),
        out_shape=jax.ShapeDtypeStruct((n, nc), F32),
        scratch_shapes=[pltpu.VMEM((tm, k), BF16)],
        compiler_params=_params("parallel", "arbitrary"),
        name="mm_norm",
    )(x, gain.reshape(1, k), w)


def _mm_res_kernel(x_ref, w_ref, h_ref, o_ref):
    o_ref[...] = h_ref[...] + jnp.dot(x_ref[...], w_ref[...], preferred_element_type=F32)


def mm_res(x, w, h):
    n, k = x.shape
    nc = w.shape[1]
    tm = _row_tile(n, 1024)
    tn = 1024
    return pl.pallas_call(
        _mm_res_kernel,
        grid=(n // tm, nc // tn),
        in_specs=[pl.BlockSpec((tm, k), lambda i, j: (i, 0)),
                  pl.BlockSpec((k, tn), lambda i, j: (0, j)),
                  pl.BlockSpec((tm, tn), lambda i, j: (i, j))],
        out_specs=pl.BlockSpec((tm, tn), lambda i, j: (i, j)),
        out_shape=jax.ShapeDtypeStruct((n, nc), F32),
        compiler_params=_params("parallel", "parallel"),
        name="mm_res",
    )(x, w, h)


def _mm_gated3_kernel(oa_ref, ob_ref, oc_ref, wa_ref, wb_ref, wc_ref, ga_ref, gb_ref, gc_ref, o_ref):
    ya = jnp.dot(oa_ref[...], wa_ref[...], preferred_element_type=F32)
    yb = jnp.dot(ob_ref[...], wb_ref[...], preferred_element_type=F32)
    yc = jnp.dot(oc_ref[...], wc_ref[...], preferred_element_type=F32)
    m = (jax.nn.sigmoid(ga_ref[...]) * ya + jax.nn.sigmoid(gb_ref[...]) * yb
         + jax.nn.sigmoid(gc_ref[...]) * yc)
    o_ref[...] = m.astype(o_ref.dtype)


def mm_gated3(oa, ob, oc, wa, wb, wc, gates):
    n = oa.shape[0]
    tm = _row_tile(n, 512)
    tn = 512
    nj = D_MODEL // tn
    row = lambda w: pl.BlockSpec((tm, w), lambda i, j: (i, 0))
    col = lambda k: pl.BlockSpec((k, tn), lambda i, j: (0, j))
    gate = lambda s: pl.BlockSpec((tm, tn), lambda i, j: (i, j + s * nj))
    return pl.pallas_call(
        _mm_gated3_kernel,
        grid=(n // tm, nj),
        in_specs=[row(A_W), row(B_W), row(C_W), col(A_W), col(B_W), col(C_W), gate(0), gate(1), gate(2)],
        out_specs=pl.BlockSpec((tm, tn), lambda i, j: (i, j)),
        out_shape=jax.ShapeDtypeStruct((n, D_MODEL), BF16),
        compiler_params=_params("parallel", "parallel"),
        name="mm_gated3",
    )(oa, ob, oc, wa, wb, wc, gates, gates, gates)


def _ffn_kernel(h_ref, g_ref, wg_ref, wu_ref, wo_ref, o_ref, xn_ref, acc_ref):
    f = pl.program_id(1)

    @pl.when(f == 0)
    def _():
        xn_ref[...] = _rms(h_ref[...], g_ref[...]).astype(BF16)
        acc_ref[...] = jnp.zeros_like(acc_ref)

    xn = xn_ref[...]
    gate = jnp.dot(xn, wg_ref[...], preferred_element_type=F32)
    up = jnp.dot(xn, wu_ref[...], preferred_element_type=F32)
    act = (jax.nn.silu(gate) * up).astype(BF16)
    acc_ref[...] += jnp.dot(act, wo_ref[...], preferred_element_type=F32)

    @pl.when(f == pl.num_programs(1) - 1)
    def _():
        o_ref[...] = h_ref[...] + acc_ref[...]


def ffn(h, gain, w_in, w_out):
    n, d = h.shape
    dff = w_out.shape[0]
    tm = _row_tile(n, 512)
    tf = 512
    assert dff % tf == 0
    nf = dff // tf
    return pl.pallas_call(
        _ffn_kernel,
        grid=(n // tm, nf),
        in_specs=[pl.BlockSpec((tm, d), lambda i, f: (i, 0)),
                  pl.BlockSpec((1, d), lambda i, f: (0, 0)),
                  pl.BlockSpec((d, tf), lambda i, f: (0, f)),
                  pl.BlockSpec((d, tf), lambda i, f: (0, f + nf)),
                  pl.BlockSpec((tf, d), lambda i, f: (f, 0))],
        out_specs=pl.BlockSpec((tm, d), lambda i, f: (i, 0)),
        out_shape=jax.ShapeDtypeStruct((n, d), F32),
        scratch_shapes=[pltpu.VMEM((tm, d), BF16), pltpu.VMEM((tm, d), F32)],
        compiler_params=_params("parallel", "arbitrary"),
        name="ffn",
    )(h, gain.reshape(1, d), w_in, w_in, w_out)


def _ple_kernel(h_ref, g_ref, wg_ref, p_ref, wp_ref, hj_ref, o_ref, xn_ref):
    @pl.when(pl.program_id(1) == 0)
    def _():
        xn_ref[...] = _rms(h_ref[...], g_ref[...]).astype(BF16)

    gate = jnp.dot(xn_ref[...], wg_ref[...], preferred_element_type=F32)
    emb = jnp.dot(p_ref[...], wp_ref[...], preferred_element_type=F32)
    o_ref[...] = hj_ref[...] + jax.nn.sigmoid(gate) * emb


def ple(h, gain, w_gate, p, w_proj):
    n, d = h.shape
    pd = p.shape[1]
    tm = _row_tile(n, 1024)
    tn = 512
    return pl.pallas_call(
        _ple_kernel,
        grid=(n // tm, d // tn),
        in_specs=[pl.BlockSpec((tm, d), lambda i, j: (i, 0)),
                  pl.BlockSpec((1, d), lambda i, j: (0, 0)),
                  pl.BlockSpec((d, tn), lambda i, j: (0, j)),
                  pl.BlockSpec((tm, pd), lambda i, j: (i, 0)),
                  pl.BlockSpec((pd, tn), lambda i, j: (0, j)),
                  pl.BlockSpec((tm, tn), lambda i, j: (i, j))],
        out_specs=pl.BlockSpec((tm, tn), lambda i, j: (i, j)),
        out_shape=jax.ShapeDtypeStruct((n, d), F32),
        scratch_shapes=[pltpu.VMEM((tm, d), BF16)],
        compiler_params=_params("parallel", "arbitrary"),
        name="ple",
    )(h, gain.reshape(1, d), w_gate, p, w_proj, h)


def _rope_tables(pos, width, period, rot, scale_lanes=None):
    half = rot // 2
    inv = ROPE_THETA ** (-jnp.arange(half, dtype=F32) / half)
    lane = jnp.arange(width) % period
    first, second = lane < half, (lane >= half) & (lane < rot)
    ang = pos.astype(F32)[:, None] * inv[lane % half][None, :]
    cos, sin = jnp.cos(ang), jnp.sin(ang)
    c = jnp.where((first | second)[None, :], cos, 1.0)
    s_up = jnp.where(first[None, :], -sin, 0.0)
    s_dn = jnp.where(second[None, :], sin, 0.0)
    if scale_lanes is not None:
        only, mult = scale_lanes
        c = jnp.where(only[None, :], c, mult[None, :])
        s_up = jnp.where(only[None, :], s_up, 0.0)
        s_dn = jnp.where(only[None, :], s_dn, 0.0)
    return jnp.stack([c, s_up, s_dn]).astype(F32)


def _rope(x, tab, half):
    return (x * tab[0] + pltpu.roll(x, x.shape[1] - half, 1) * tab[1] + pltpu.roll(x, half, 1) * tab[2])


def _post_in_kernel(z_ref, t128_ref, t64_ref, tki_ref, gaq_ref, gak_ref, gbq_ref, gbk_ref,
                    aq_ref, akf_ref, akb_ref, avf_ref, avb_ref, bq_ref, bkf_ref, bkb_ref,
                    bvf_ref, bvb_ref, qi_ref, kiw_ref, kilo_ref, kihi_ref):
    hd = HEAD_DIM
    t128 = t128_ref[...]
    t64 = t64_ref[...]
    half128 = hd // ROT_FRACTION // 2
    half64 = D_IDX // ROT_FRACTION // 2
    for h in range(H_A):
        sl = slice(h * hd, (h + 1) * hd)
        aq = _rms(z_ref[:, sl], gaq_ref[...]) * (-(hd ** -0.5))
        aq_ref[:, sl] = aq.astype(BF16)
        ak = _rms(z_ref[:, A_W + h * hd:A_W + (h + 1) * hd], gak_ref[...])
        akf_ref[:, h, :] = ak
        akb_ref[:, sl] = ak.astype(BF16)
        av = z_ref[:, 2 * A_W + h * hd:2 * A_W + (h + 1) * hd]
        avf_ref[:, h, :] = av
        avb_ref[:, sl] = av.astype(BF16)
    o = 3 * A_W
    for h in range(H_B):
        sl = slice(h * hd, (h + 1) * hd)
        bq = _rope(_rms(z_ref[:, o + h * hd:o + (h + 1) * hd], gbq_ref[...]), t128, half128) * (hd ** -0.5)
        bq_ref[:, sl] = bq.astype(BF16)
    o += B_W
    bk = _rope(_rms(z_ref[:, o:o + hd], gbk_ref[...]), t128, half128)
    bkf_ref[...] = bk
    bkb_ref[...] = bk.astype(BF16)
    o += hd
    bv = z_ref[:, o:o + hd]
    bvf_ref[...] = bv
    bvb_ref[...] = bv.astype(BF16)
    o += hd
    for c in range(H_IDX * D_IDX // 128):
        qi = _rope(z_ref[:, o + c * 128:o + (c + 1) * 128], t64, half64) * (D_IDX ** -0.5)
        qi_ref[:, c * 128:(c + 1) * 128] = qi.astype(BF16)
    o += H_IDX * D_IDX
    kiw = _rope(z_ref[:, o:o + 128], tki_ref[...], half64)
    kiw_ref[...] = kiw
    lane = lax.broadcasted_iota(jnp.int32, kiw.shape, 1)
    ki = jnp.where(lane < D_IDX, kiw, 0.0)
    kilo_ref[...] = ki.astype(BF16)
    kihi_ref[...] = pltpu.roll(ki, D_IDX, 1).astype(BF16)


def post_in(z, tabs, gains, t_len):
    n = z.shape[0]
    tm = _row_tile(t_len, 512)
    nt = t_len // tm
    rows = lambda w: pl.BlockSpec((tm, w), lambda i: (i, 0))
    tab = pl.BlockSpec((3, tm, 128), lambda i: (0, i % nt, 0))
    gain = pl.BlockSpec((1, HEAD_DIM), lambda i: (0, 0))
    per_head = pl.BlockSpec((tm, H_A, HEAD_DIM), lambda i: (i, 0, 0))
    widths = [(A_W, BF16), (A_W, F32), (A_W, BF16), (A_W, F32), (A_W, BF16), (B_W, BF16),
              (HEAD_DIM, F32), (HEAD_DIM, BF16), (HEAD_DIM, F32), (HEAD_DIM, BF16),
              (H_IDX * D_IDX, BF16), (128, F32), (128, BF16), (128, BF16)]
    return pl.pallas_call(
        _post_in_kernel,
        grid=(n // tm,),
        in_specs=[rows(ATT_W), tab, tab, tab, gain, gain, gain, gain],
        out_specs=[per_head if i in (1, 3) else rows(w) for i, (w, _) in enumerate(widths)],
        out_shape=[jax.ShapeDtypeStruct((n, H_A, HEAD_DIM) if i in (1, 3) else (n, w), dt)
                   for i, (w, dt) in enumerate(widths)],
        compiler_params=_params("parallel"),
        name="post_in",
    )(z, *tabs, *[g.reshape(1, HEAD_DIM) for g in gains])


LOG2E = 1.4426950408889634
ATTN_A_ROW_CHUNK = 256
ATTN_A_BLOCKS_PER_STEP = 4


def _attn_a_kernel(q_ref, k_ref, v_ref, tri_ref, o_ref, *, tq, tk, rc, past):
    i = pl.program_id(2)
    q0 = past + i * tq
    tri = tri_ref[...]
    n_full = q0 // tk
    n_all = (q0 + tq - 1 + tk - 1) // tk
    nrc = tq // rc
    qs = [q_ref[c * rc:(c + 1) * rc, :] for c in range(nrc)]
    rows = [q0 + c * rc + lax.broadcasted_iota(jnp.int32, (rc, tk), 0) for c in range(nrc)]
    col = lax.broadcasted_iota(jnp.int32, (rc, tk), 1)

    def step(j, carry, masked, nsub):
        cs = range(nrc)
        ss = range(nsub)
        offs = [pl.multiple_of((j - s) * tk, tk) for s in ss]
        kb = [k_ref[pl.ds(o, tk), :] for o in offs]
        vb = [v_ref[pl.ds(o, tk), :] for o in offs]
        later = [carry[0][c * rc:(c + 1) * rc] for c in cs]
        acc = [carry[1][c * rc:(c + 1) * rc] for c in cs]
        nz = [[_nt_dot(qs[c], kb[s]) for c in cs] for s in ss]
        loss = [[jnp.log(1.0 + jnp.exp2(jnp.minimum(nz[s][c] * (-LOG2E), 126.0))) for c in cs] for s in ss]
        if masked:
            causal = [[(offs[s] + col) < rows[c] for c in cs] for s in ss]
            loss = [[jnp.where(causal[s][c], loss[s][c], 0.0) for c in cs] for s in ss]
        within = [[jnp.dot(loss[s][c].astype(BF16), tri, preferred_element_type=F32) for c in cs] for s in ss]
        for s in ss:
            w = [jnp.exp2((nz[s][c] + later[c] + within[s][c]) * (-LOG2E)) for c in cs]
            if masked:
                w = [jnp.where(causal[s][c], w[c], 0.0) for c in cs]
            acc = [acc[c] + jnp.dot(w[c].astype(BF16), vb[s], preferred_element_type=F32) for c in cs]
            later = [later[c] + within[s][c][:, 0:1] for c in cs]
        return jnp.concatenate(later, axis=0), jnp.concatenate(acc, axis=0)

    carry = (jnp.zeros((tq, 1), F32), jnp.zeros((tq, HEAD_DIM), F32))
    carry = lax.fori_loop(0, n_all - n_full, lambda t, c: step(n_all - 1 - t, c, True, 1), carry)
    n_multi = n_full // ATTN_A_BLOCKS_PER_STEP
    n_single = n_full - n_multi * ATTN_A_BLOCKS_PER_STEP
    carry = lax.fori_loop(0, n_single, lambda t, c: step(n_full - 1 - t, c, False, 1), carry)
    carry = lax.fori_loop(
        0, n_multi,
        lambda t, c: step(n_full - n_single - 1 - t * ATTN_A_BLOCKS_PER_STEP, c, False, ATTN_A_BLOCKS_PER_STEP),
        carry)
    o_ref[...] = carry[1].astype(o_ref.dtype)


def attn_a(q, k, v, past, tk):
    b, t, _ = q.shape
    lp = k.shape[1]
    tq = _row_tile(t, 512)
    rc = min(tq, ATTN_A_ROW_CHUNK)
    tri = jnp.tri(tk, dtype=BF16)
    return pl.pallas_call(
        functools.partial(_attn_a_kernel, tq=tq, tk=tk, rc=rc, past=past),
        grid=(b, H_A, t // tq),
        in_specs=[pl.BlockSpec((None, tq, HEAD_DIM), lambda bi, h, i: (bi, i, h)),
                  pl.BlockSpec((None, lp, HEAD_DIM), lambda bi, h, i: (bi, 0, h)),
                  pl.BlockSpec((None, lp, HEAD_DIM), lambda bi, h, i: (bi, 0, h)),
                  pl.BlockSpec((tk, tk), lambda bi, h, i: (0, 0))],
        out_specs=pl.BlockSpec((None, tq, HEAD_DIM), lambda bi, h, i: (bi, i, h)),
        out_shape=jax.ShapeDtypeStruct((b, t, A_W), BF16),
        compiler_params=_params("parallel", "parallel", "parallel"),
        name="attn_a",
    )(q, k, v, tri)


DSA_QUERY_ROWS = 256
DSA_ATTEND_ROWS = 128


def _dsa_kernel(qi_ref, kiw_ref, kilo_ref, kihi_ref, q_ref, k_ref, v_ref, triu_ref, o_ref, key_ref,
                *, tq, tk, past, n_keys, topk):
    i = pl.program_id(1)
    q0 = past + i * tq
    pos = q0 + lax.broadcasted_iota(jnp.int32, (tq, 1), 0)
    lim = jnp.minimum((pos // CHUNK + 1) * CHUNK, n_keys)
    n_adm = jnp.minimum(((q0 + tq - 1) // CHUNK + 1) * CHUNK, n_keys)
    nt = (n_adm + tk - 1) // tk
    col = lax.broadcasted_iota(jnp.int32, (tq, tk), 1)
    ngrp = tk // 128

    qi = qi_ref[...]
    kiw = kiw_ref[...]
    wi = [kiw[:, D_IDX + h:D_IDX + h + 1] for h in range(H_IDX)]

    n_pair = (nt + 1) // 2
    pair = range(2)

    def score_pair(t, _):
        offs = [pl.multiple_of((2 * t + u) * tk, tk) for u in pair]
        kis = [(kilo_ref[pl.ds(o, tk), :], kihi_ref[pl.ds(o, tk), :]) for o in offs]
        s_idx = [[_nt_dot(qi[:, (h // 2) * 128:(h // 2 + 1) * 128], kis[u][h % 2]) for h in range(H_IDX)]
                 for u in pair]
        for u in pair:
            s = jnp.zeros((tq, tk), F32)
            for h in range(H_IDX):
                s = s + wi[h] * jnp.maximum(s_idx[u][h], 0.0)
            s = jnp.where(s == 0.0, 0.0, s)
            s = jnp.where(offs[u] + col < lim, s, -jnp.inf)
            bits = lax.bitcast_convert_type(s, jnp.int32)
            key_ref[:, pl.ds(offs[u], tk)] = bits ^ ((bits >> 31) & 0x7FFFFFFF)
        return 0

    lax.fori_loop(0, n_pair, score_pair, 0)

    cr = min(tq, DSA_ATTEND_ROWS)

    def count_ge(cand):
        accs = []
        for r in range(tq // cr):
            cand_b = jnp.broadcast_to(cand[r * cr:(r + 1) * cr], (cr, 128))

            def body(t, acc):
                off = pl.multiple_of(t * tk, tk)
                kt = key_ref[r * cr:(r + 1) * cr, pl.ds(off, tk)]
                for g in range(ngrp):
                    acc = acc + jnp.where(kt[:, g * 128:(g + 1) * 128] >= cand_b, 1, 0)
                return acc

            accs.append(lax.fori_loop(0, nt, body, jnp.zeros((cr, 128), jnp.int32)))
        return jnp.sum(jnp.concatenate(accs, axis=0), axis=1, keepdims=True)

    def bit_step(b, t_u):
        cand_u = t_u | lax.shift_left(jnp.int32(1), 31 - b)
        ok = count_ge(cand_u ^ INT_MIN) >= topk
        return jnp.where(ok, cand_u, t_u)

    t_u = lax.fori_loop(0, 32, bit_step, jnp.zeros((tq, 1), jnp.int32))
    tau = t_u ^ INT_MIN
    tau_next = jnp.where(tau == 2 ** 31 - 1, tau, tau + 1)
    cnt_gt = jnp.where(tau == 2 ** 31 - 1, 0, count_ge(tau_next))
    need = jnp.where(tau == KEY_NEG_INF, 0, topk - cnt_gt).astype(F32)

    triu = triu_ref[...]

    def select_tile(t, seen):
        off = pl.multiple_of(t * tk, tk)
        kt = key_ref[:, pl.ds(off, tk)]
        eq = kt == tau
        prefix = seen + jnp.dot(jnp.where(eq, 1.0, 0.0).astype(BF16), triu, preferred_element_type=F32)
        rank = jnp.where(eq, prefix, jnp.inf)
        bias = jnp.where(kt > tau, 0.0, jnp.where(rank <= need, 0.0, NEG_BIG)).astype(F32)
        key_ref[:, pl.ds(off, tk)] = lax.bitcast_convert_type(bias, jnp.int32)
        return prefix[:, tk - 1:tk]

    lax.fori_loop(0, 2 * n_pair, select_tile, jnp.zeros((tq, 1), F32))

    rg = min(tq, DSA_ATTEND_ROWS)
    groups = range(tq // rg)
    gr = H_B * rg
    qs = [jnp.concatenate([q_ref[r * rg:(r + 1) * rg, h * HEAD_DIM:(h + 1) * HEAD_DIM] for h in range(H_B)], axis=0)
          for r in groups]

    def attend_pair(t, carry):
        offs = [pl.multiple_of((2 * t + u) * tk, tk) for u in pair]
        kb = [k_ref[pl.ds(o, tk), :] for o in offs]
        vb = [v_ref[pl.ds(o, tk), :] for o in offs]
        new = []
        for r in groups:
            m, l, acc = [c[r * gr:(r + 1) * gr] for c in carry]
            bias = [lax.bitcast_convert_type(key_ref[r * rg:(r + 1) * rg, pl.ds(o, tk)], F32) for o in offs]
            logits = [(_nt_dot(qs[r], kb[u]).reshape(H_B, rg, tk) + bias[u][None]).reshape(gr, tk) for u in pair]
            m_new = jnp.maximum(m, jnp.maximum(jnp.max(logits[0], axis=1, keepdims=True),
                                               jnp.max(logits[1], axis=1, keepdims=True)))
            alpha = jnp.exp(m - m_new)
            p = [jnp.exp(logits[u] - m_new) for u in pair]
            l = alpha * l + (jnp.sum(p[0], axis=1, keepdims=True) + jnp.sum(p[1], axis=1, keepdims=True))
            acc = alpha * acc + (jnp.dot(p[0].astype(BF16), vb[0], preferred_element_type=F32)
                                 + jnp.dot(p[1].astype(BF16), vb[1], preferred_element_type=F32))
            new.append((m_new, l, acc))
        return tuple(jnp.concatenate([n[k] for n in new], axis=0) for k in range(3))

    init = (jnp.full((H_B * tq, 1), NEG_BIG, F32), jnp.zeros((H_B * tq, 1), F32),
            jnp.zeros((H_B * tq, HEAD_DIM), F32))
    _, l, acc = lax.fori_loop(0, n_pair, attend_pair, init)
    out = acc / l
    for r in groups:
        for h in range(H_B):
            rows = out[r * gr + h * rg:r * gr + (h + 1) * rg]
            o_ref[r * rg:(r + 1) * rg, h * HEAD_DIM:(h + 1) * HEAD_DIM] = rows.astype(o_ref.dtype)


def dsa(qi, kiw, kilo, kihi, q, k, v, past, n_keys, tk):
    b, t, _ = q.shape
    lp = k.shape[1]
    tq = _row_tile(t, DSA_QUERY_ROWS)
    topk = min(TOPK_MAX, n_keys // 4)
    assert lp % (2 * tk) == 0 and lp >= topk
    triu = jnp.tri(tk, dtype=BF16).T
    qblk = lambda w: pl.BlockSpec((None, tq, w), lambda bi, i: (bi, i, 0))
    kblk = lambda w: pl.BlockSpec((None, lp, w), lambda bi, i: (bi, 0, 0))
    return pl.pallas_call(
        functools.partial(_dsa_kernel, tq=tq, tk=tk, past=past, n_keys=n_keys, topk=topk),
        grid=(b, t // tq),
        in_specs=[qblk(H_IDX * D_IDX), qblk(128), kblk(128), kblk(128), qblk(B_W),
                  kblk(HEAD_DIM), kblk(HEAD_DIM), pl.BlockSpec((tk, tk), lambda bi, i: (0, 0))],
        out_specs=qblk(B_W),
        out_shape=jax.ShapeDtypeStruct((b, t, B_W), BF16),
        scratch_shapes=[pltpu.VMEM((tq, lp), jnp.int32)],
        compiler_params=_params("parallel", "parallel"),
        name="dsa",
    )(qi, kiw, kilo, kihi, q, k, v, triu)


def _softplus(x):
    return jnp.maximum(x, 0.0) + jnp.log(1.0 + jnp.exp(-jnp.abs(x)))


def _c_prep_kernel(*refs, first_layer):
    if first_layer:
        (cz_ref, prev_ref, mu_ref, w0_ref, a0_ref, w2_ref, a2_ref, g2_ref, kk_ref, ka_ref,
         r_ref, kn_ref, k_ref, v_ref, lw_ref, a_ref, g_ref, vtok_ref) = refs
    else:
        (cz_ref, prev_ref, mu_ref, w0_ref, a0_ref, w2_ref, a2_ref, g2_ref, kk_ref, ka_ref,
         vf_ref, v0_ref, v1_ref, v2_ref,
         r_ref, kn_ref, k_ref, v_ref, lw_ref, a_ref, g_ref) = refs

    def put_heads(ref, x):
        for pair in range(H_C // 2):
            tile = x[:, pair * 128:(pair + 1) * 128]
            ref[2 * pair] = tile[:, :N_C]
            ref[2 * pair + 1] = pltpu.roll(tile, N_C, 1)[:, :N_C]

    cz = cz_ref[...]
    rowid = lax.broadcasted_iota(jnp.int32, cz.shape, 0)
    prev = jnp.where(rowid == 0, prev_ref[...], pltpu.roll(cz, 1, 0))
    zz = cz + (prev - cz) * mu_ref[...]
    r = zz[:, 0:C_W]
    k = zz[:, C_W:2 * C_W]
    v = zz[:, 2 * C_W:3 * C_W]
    tail = zz[:, 3 * C_W:C_PAD]
    w = w0_ref[...] + jnp.dot(jnp.tanh(tail).astype(BF16), w2_ref[...], preferred_element_type=F32)
    w = -_softplus(-w) - 0.5
    a = jax.nn.sigmoid(a0_ref[...] + jnp.dot(tail.astype(BF16), a2_ref[...], preferred_element_type=F32))
    g = jnp.dot(jax.nn.sigmoid(tail).astype(BF16), g2_ref[...], preferred_element_type=F32)
    if not first_layer:
        lora = jnp.dot(v.astype(BF16), v1_ref[...], preferred_element_type=F32)
        lora = jnp.dot(lora.astype(BF16), v2_ref[...], preferred_element_type=F32)
        v = v + (vf_ref[...] - v) * jax.nn.sigmoid(v0_ref[...] + lora)
    if first_layer:
        vtok_ref[...] = v
    put_heads(r_ref, r)
    put_heads(kn_ref, k * kk_ref[...])
    put_heads(k_ref, k * (1.0 + (a - 1.0) * ka_ref[...]))
    put_heads(v_ref, v)
    put_heads(lw_ref, -jnp.exp(w))
    put_heads(a_ref, a)
    put_heads(g_ref, g)


def c_prep(cz, prev_rows, vecs, mats, v_first, vres, tm):
    b, t, _ = cz.shape
    first_layer = vres is None
    blk = lambda w: pl.BlockSpec((None, tm, w), lambda bi, i: (bi, i, 0))
    vec = lambda w: pl.BlockSpec((1, w), lambda bi, i: (0, 0))
    mat = lambda m: pl.BlockSpec(m.shape, lambda bi, i: (0, 0))
    mu, w0, a0, k_k, k_a = vecs
    w2, a2, g2 = mats
    ins = [cz, prev_rows, mu, w0, a0, w2, a2, g2, k_k, k_a]
    specs = [blk(C_PAD), pl.BlockSpec((None, None, 1, C_PAD), lambda bi, i: (bi, i, 0, 0)),
             vec(C_PAD), vec(C_W), vec(C_W), mat(w2), mat(a2), mat(g2), vec(C_W), vec(C_W)]
    if not first_layer:
        v0, v1, v2 = vres
        ins += [v_first, v0, v1, v2]
        specs += [blk(C_W), vec(C_W), mat(v1), mat(v2)]
    heads = pl.BlockSpec((None, H_C, tm, N_C), lambda bi, i: (bi, 0, i, 0))
    out_specs = [heads] * 7
    out_shape = [jax.ShapeDtypeStruct((b, H_C, t, N_C), F32)] * 7
    if first_layer:
        out_specs.append(blk(C_W))
        out_shape.append(jax.ShapeDtypeStruct((b, t, C_W), F32))
    return pl.pallas_call(
        functools.partial(_c_prep_kernel, first_layer=first_layer),
        grid=(b, t // tm),
        in_specs=specs,
        out_specs=out_specs,
        out_shape=out_shape,
        compiler_params=_params("parallel", "parallel"),
        name="c_prep",
    )(*ins)


def _bdot(a, b, contract_a, contract_b):
    return lax.dot_general(a.astype(BF16), b.astype(BF16), (((contract_a,), (contract_b,)), ((0,), (0,))),
                           preferred_element_type=F32)


def _bdot3(a, b, contract_a, contract_b):
    ah = a.astype(BF16)
    bh = b.astype(BF16)
    al = (a - ah.astype(F32)).astype(BF16)
    bl = (b - bh.astype(F32)).astype(BF16)
    dn = (((contract_a,), (contract_b,)), ((0,), (0,)))
    dot = lambda x, y: lax.dot_general(x, y, dn, preferred_element_type=F32)
    return dot(ah, bh) + (dot(al, bh) + dot(ah, bl))


C_SCAN_CHUNKS_PER_STEP = 4


def _c_scan_kernel(r_ref, kn_ref, k_ref, v_ref, lw_ref, a_ref, g_ref, s0_ref, rk_ref, lng_ref, lnb_ref,
                   y_ref, st_ref, s_ref, *, nck):
    c_len = SCAN_CHUNK

    @pl.when(pl.program_id(1) == 0)
    def _():
        s_ref[...] = s0_ref[...]

    def step(c0, nc):
        g_n = H_C * nc
        rows = pl.ds(pl.multiple_of(c0 * c_len, c_len), nc * c_len)
        load = lambda ref: ref[:, rows, :].reshape(g_n, c_len, N_C)
        per_head = lambda x: jnp.broadcast_to(x[:, None], (H_C, nc) + x.shape[1:]).reshape((g_n,) + x.shape[1:])
        of_chunk = lambda x, j: x.reshape((H_C, nc) + x.shape[1:])[:, j]
        ti = lax.broadcasted_iota(jnp.int32, (g_n, c_len, c_len), 1)
        si = lax.broadcasted_iota(jnp.int32, (g_n, c_len, c_len), 2)
        incl = si <= ti
        strict = si < ti
        tri = jnp.where(incl, 1.0, 0.0)
        eye = jnp.where(si == ti, 1.0, 0.0)
        r, k, v, lw, a, kn = (load(x) for x in (r_ref, k_ref, v_ref, lw_ref, a_ref, kn_ref))
        kn = kn * lax.rsqrt(jnp.maximum(jnp.sum(kn * kn, axis=-1, keepdims=True), 1e-24))
        aa = -kn
        bb = kn * a
        g = _bdot3(tri, lw, 2, 1)
        g_end = g[:, c_len - 1:c_len, :]
        a_t = aa * jnp.exp(g - lw)
        r_t = r * jnp.exp(g)
        inv = jnp.exp(-g)
        b_h = bb * inv
        k_h = k * inv
        to_end = jnp.exp(g_end - g)
        b_e = bb * to_end
        k_e = k * to_end
        d_end = jnp.exp(g_end)
        a_ab = jnp.where(strict, _bdot(a_t, b_h, 2, 2), 0.0)
        a_ak = jnp.where(strict, _bdot(a_t, k_h, 2, 2), 0.0)
        a_rb = jnp.where(incl, _bdot(r_t, b_h, 2, 2), 0.0)
        a_rk = jnp.where(incl, _bdot(r_t, k_h, 2, 2), 0.0)
        t_inv = eye + a_ab
        pw = a_ab
        for _ in range(4):
            pw = _bdot(pw, pw, 2, 1)
            t_inv = t_inv + _bdot(t_inv, pw, 2, 1)
        p_mat = _bdot(t_inv, a_t, 2, 1)
        q_mat = _bdot(t_inv, _bdot(a_ak, v, 2, 1), 2, 1)
        y_v = _bdot(a_rk, v, 2, 1)
        s = s_ref[...]
        ys = []
        for j in range(nc):
            u = _bdot(of_chunk(p_mat, j), s, 2, 2) + of_chunk(q_mat, j)
            ys.append(_bdot(of_chunk(r_t, j), s, 2, 2) + _bdot(of_chunk(a_rb, j), u, 2, 1) + of_chunk(y_v, j))
            uv = jnp.concatenate([u, of_chunk(v, j)], axis=1)
            bk = jnp.concatenate([of_chunk(b_e, j), of_chunk(k_e, j)], axis=1)
            s = s * of_chunk(d_end, j) + _bdot(uv, bk, 1, 1)
        s_ref[...] = s
        y = jnp.stack(ys, axis=1).reshape(g_n, c_len, N_C)
        mu = jnp.mean(y, axis=-1, keepdims=True)
        yc = y - mu
        yn = yc * lax.rsqrt(jnp.mean(yc * yc, axis=-1, keepdims=True) + LNX_EPS)
        yn = yn * per_head(lng_ref[...]) + per_head(lnb_ref[...])
        bonus = jnp.sum(r * k * per_head(rk_ref[...]), axis=-1, keepdims=True) * v
        o = ((yn + bonus) * load(g_ref)).reshape(H_C, nc * c_len, N_C)
        y_ref[rows, :] = jnp.concatenate([o[h] for h in range(H_C)], axis=-1).astype(y_ref.dtype)

    n_multi = nck // C_SCAN_CHUNKS_PER_STEP
    if n_multi:
        lax.fori_loop(0, n_multi, lambda t, _: step(t * C_SCAN_CHUNKS_PER_STEP, C_SCAN_CHUNKS_PER_STEP), None)
    for c in range(n_multi * C_SCAN_CHUNKS_PER_STEP, nck):
        step(c, 1)

    @pl.when(pl.program_id(1) == pl.num_programs(1) - 1)
    def _():
        st_ref[...] = s_ref[...]


def c_scan(r, kn, k, v, lw, a, g, s0, r_k, ln_g, ln_b, tm):
    b, _, t, _ = r.shape
    blk = pl.BlockSpec((None, H_C, tm, N_C), lambda bi, i: (bi, 0, i, 0))
    st = pl.BlockSpec((None, H_C, N_C, N_C), lambda bi, i: (bi, 0, 0, 0))
    vec = pl.BlockSpec((H_C, 1, N_C), lambda bi, i: (0, 0, 0))
    return pl.pallas_call(
        functools.partial(_c_scan_kernel, nck=tm // SCAN_CHUNK),
        grid=(b, t // tm),
        in_specs=[blk] * 7 + [st, vec, vec, vec],
        out_specs=[pl.BlockSpec((None, tm, C_W), lambda bi, i: (bi, i, 0)), st],
        out_shape=[jax.ShapeDtypeStruct((b, t, C_W), BF16),
                   jax.ShapeDtypeStruct((b, H_C, N_C, N_C), F32)],
        scratch_shapes=[pltpu.VMEM((H_C, N_C, N_C), F32)],
        compiler_params=_params("parallel", "arbitrary"),
        name="c_scan",
    )(r, kn, k, v, lw, a, g, s0, r_k.reshape(H_C, 1, N_C), ln_g.reshape(H_C, 1, N_C), ln_b.reshape(H_C, 1, N_C))


def _pad_rows(x, rows, axis=1):
    pad = rows - x.shape[axis]
    if pad == 0:
        return x
    widths = [(0, 0)] * x.ndim
    widths[axis] = (0, pad)
    return jnp.pad(x, widths)


ATT_COLS = 3 * A_W + B_W + 2 * HEAD_DIM + H_IDX * D_IDX + D_IDX + H_IDX
IN_COLS = ATT_COLS + C_COLS + 3 * D_MODEL


def _regroup_kernel(w_ref, tail_ref, att_ref, c_ref, g_ref):
    lane = lax.broadcasted_iota(jnp.int32, (w_ref.shape[0], 128), 1)
    n_full = IN_COLS // 128

    def tile(j):
        return tail_ref[...] if j == n_full else w_ref[:, j * 128:(j + 1) * 128]

    def shifted(j0, shift, j):
        joined = jnp.where(lane >= shift, tile(j0 + j), tile(j0 + j + 1))
        return pltpu.roll(joined, 128 - shift, 1)

    for j in range(ATT_W // 128):
        x = tile(j)
        if (j + 1) * 128 > ATT_COLS:
            x = jnp.where(lane < ATT_COLS - j * 128, x, 0.0)
        att_ref[:, j * 128:(j + 1) * 128] = x.astype(BF16)
    for j in range(C_PAD // 128):
        x = shifted(ATT_COLS // 128, ATT_COLS % 128, j)
        if (j + 1) * 128 > C_COLS:
            x = jnp.where(lane < C_COLS - j * 128, x, 0.0)
        c_ref[:, j * 128:(j + 1) * 128] = x.astype(BF16)
    g0 = ATT_COLS + C_COLS
    for j in range(3 * D_MODEL // 128):
        g_ref[:, j * 128:(j + 1) * 128] = shifted(g0 // 128, g0 % 128, j).astype(BF16)


def regroup_w_in(w_in):
    depth, d, cols = w_in.shape
    assert cols == IN_COLS and (ATT_COLS + C_COLS) % 128 != 0
    n_full = cols // 128
    tail = jnp.pad(w_in[:, :, n_full * 128:], ((0, 0), (0, 0), (0, (n_full + 1) * 128 - cols)))
    tm = 128
    blk = lambda w: pl.BlockSpec((None, tm, w), lambda l, i: (l, i, 0))
    return pl.pallas_call(
        _regroup_kernel,
        grid=(depth, d // tm),
        in_specs=[blk(cols), blk(128)],
        out_specs=[blk(ATT_W), blk(C_PAD), blk(3 * D_MODEL)],
        out_shape=[jax.ShapeDtypeStruct((depth, d, w), BF16) for w in (ATT_W, C_PAD, 3 * D_MODEL)],
        compiler_params=_params("parallel", "parallel"),
        name="regroup_w_in",
    )(w_in, tail)


def _prep_weights(w):
    out = []
    w_att, w_c, w_g = regroup_w_in(w["w_in"])
    for i in range(DEPTH):
        tail_rows = lambda m, off: jnp.pad(m, ((off, C_TAIL - off - m.shape[0]), (0, 0))).astype(BF16)
        lw = dict(
            w_att=w_att[i], w_c=w_c[i], w_g=w_g[i],
            w2=tail_rows(w["c_w2"][i], 0),
            a2=tail_rows(w["c_a2"][i], D_DECAY_LORA),
            g2=tail_rows(w["c_g2"][i], D_DECAY_LORA + D_AAA_LORA),
            mu=jnp.pad(w["c_mu"][i], (0, C_PAD - C_COLS)).reshape(1, C_PAD),
            w_br_a=w["w_br_a"][i].astype(BF16), w_br_b=w["w_br_b"][i].astype(BF16),
            w_br_c=w["w_br_c"][i].astype(BF16), w_out=w["w_out"][i].astype(BF16),
            w_ffn_in=w["w_ffn_in"][i].astype(BF16), w_ffn_out=w["w_ffn_out"][i].astype(BF16),
            w_ple_gate=w["w_ple_gate"][i].astype(BF16), w_ple_proj=w["w_ple_proj"][i].astype(BF16),
        )
        if i > 0:
            lw["v1"] = w["c_v1"][i - 1].astype(BF16)
            lw["v2"] = w["c_v2"][i - 1].astype(BF16)
        out.append(lw)
    return out


def _run_trunk(x, p, past, caches, w, wl):
    b, t, _ = x.shape
    n = b * t
    n_keys = past + t
    pos = past + jnp.arange(t)
    lane = jnp.arange(128)
    tabs = (_rope_tables(pos, 128, HEAD_DIM, HEAD_DIM // ROT_FRACTION),
            _rope_tables(pos, 128, D_IDX, D_IDX // ROT_FRACTION),
            _rope_tables(pos, 128, 128, D_IDX // ROT_FRACTION,
                         scale_lanes=(lane < D_IDX, jnp.where(lane < D_IDX + H_IDX, H_IDX ** -0.5, 1.0))))
    tk_a = 256
    tk_b = 512
    lp_a = -(-n_keys // tk_a) * tk_a
    lp_b = -(-n_keys // (2 * tk_b)) * (2 * tk_b)
    t_c = -(-t // SCAN_CHUNK) * SCAN_CHUNK
    tm_c = _row_tile(t, 256)
    tm_s = _row_tile(t_c, 256)
    h = x.reshape(n, D_MODEL)
    outs = ([], [], [], [], [], [], [])
    v_first = None
    for i in range(DEPTH):
        lw = wl[i]
        z_att = mm_norm(h, w["norm_mix"][i], lw["w_att"], ATT_W // 3)
        z_c = mm_norm(h, w["norm_mix"][i], lw["w_c"], C_PAD // 4)
        z_g = mm_norm(h, w["norm_mix"][i], lw["w_g"], 1024)
        (aq, ak_f, ak_b, av_f, av_b, bq, bk_f, bk_b, bv_f, bv_b, qi, kiw, kilo, kihi) = post_in(
            z_att, tabs, (w["a_q_norm"][i], w["a_k_norm"][i], w["b_q_norm"][i], w["b_k_norm"][i]), t)
        seq = lambda a: a.reshape(b, t, a.shape[-1])

        def keys(new, old, lp):
            new = seq(new)
            if old is not None:
                new = jnp.concatenate([old.astype(BF16), new], axis=1)
            return _pad_rows(new, lp)

        c = caches
        past_of = lambda name: None if c is None else c[name][i]
        pa_k = None if c is None else c["a_k"][i].reshape(b, past, A_W)
        pa_v = None if c is None else c["a_v"][i].reshape(b, past, A_W)
        o_a = attn_a(seq(aq), keys(ak_b, pa_k, lp_a), keys(av_b, pa_v, lp_a), past, tk_a)
        if c is None:
            p_lo = p_hi = None
        else:
            p_lo = jnp.pad(c["b_ki"][i], ((0, 0), (0, 0), (0, 128 - D_IDX)))
            p_hi = jnp.pad(c["b_ki"][i], ((0, 0), (0, 0), (128 - D_IDX, 0)))
        o_b = dsa(seq(qi), seq(kiw), keys(kilo, p_lo, lp_b), keys(kihi, p_hi, lp_b), seq(bq),
                  keys(bk_b, past_of("b_k"), lp_b), keys(bv_b, past_of("b_v"), lp_b), past, n_keys, tk_b)
        cz = seq(z_c)
        shift0 = jnp.zeros((b, C_COLS), F32) if c is None else c["shift"][i]
        shift0 = jnp.pad(shift0, ((0, 0), (0, C_PAD - C_COLS)))
        prev_rows = jnp.concatenate([shift0[:, None, :], cz[:, tm_c - 1:t - 1:tm_c, :]], axis=1)
        prev_rows = prev_rows.reshape(b, t // tm_c, 1, C_PAD)
        row = lambda a: a.reshape(1, C_W)
        vres = None if i == 0 else (row(w["c_v0"][i - 1]), lw["v1"], lw["v2"])
        prep = c_prep(cz, prev_rows,
                      (lw["mu"], row(w["c_w0"][i]), row(w["c_a0"][i]), row(w["c_k_k"][i]), row(w["c_k_a"][i])),
                      (lw["w2"], lw["a2"], lw["g2"]), v_first, vres, tm_c)
        if i == 0:
            v_first = prep[7]
        s0 = jnp.zeros((b, H_C, N_C, N_C), F32) if c is None else c["wkv"][i]
        y_c, wkv_t = c_scan(*[_pad_rows(a, t_c, axis=2) for a in prep[:7]], s0,
                            w["c_r_k"][i], w["c_ln_g"][i], w["c_ln_b"][i], tm_s)
        o_c = y_c[:, :t].reshape(n, C_W)
        merged = mm_gated3(o_a.reshape(n, A_W), o_b.reshape(n, B_W), o_c,
                           lw["w_br_a"], lw["w_br_b"], lw["w_br_c"], z_g)
        h = mm_res(merged, lw["w_out"], h)
        h = ffn(h, w["norm_ffn"][i], lw["w_ffn_in"], lw["w_ffn_out"])
        h = ple(h, w["norm_ple"][i], lw["w_ple_gate"], p[i].reshape(n, -1).astype(BF16), lw["w_ple_proj"])
        vals = (ak_f.reshape(b, t, H_A, HEAD_DIM), av_f.reshape(b, t, H_A, HEAD_DIM),
                seq(bk_f), seq(bv_f), seq(kiw)[:, :, :D_IDX], wkv_t, cz[:, -1, :C_COLS])
        for lst, val in zip(outs, vals):
            lst.append(val)
    return h.reshape(b, t, D_MODEL), [jnp.stack(l) for l in outs]


def kernel(x_prompt, x_sample, cache_a_k, cache_a_v, cache_b_k, cache_b_v, cache_b_kidx, state_c_wkv, state_c_shift, p_prompt, p_sample, norm_mix, w_in, a_q_norm, a_k_norm, b_q_norm, b_k_norm, c_mu, c_w0, c_w2, c_a0, c_a2, c_g2, c_v0, c_v1, c_v2, c_k_k, c_k_a, c_r_k, c_ln_g, c_ln_b, w_br_a, w_br_b, w_br_c, w_out, norm_ffn, w_ffn_in, w_ffn_out, norm_ple, w_ple_gate, w_ple_proj):
    w = dict(norm_mix=norm_mix, w_in=w_in, a_q_norm=a_q_norm, a_k_norm=a_k_norm, b_q_norm=b_q_norm,
             b_k_norm=b_k_norm, c_mu=c_mu, c_w0=c_w0, c_w2=c_w2, c_a0=c_a0, c_a2=c_a2, c_g2=c_g2,
             c_v0=c_v0, c_v1=c_v1, c_v2=c_v2, c_k_k=c_k_k, c_k_a=c_k_a, c_r_k=c_r_k, c_ln_g=c_ln_g,
             c_ln_b=c_ln_b, w_br_a=w_br_a, w_br_b=w_br_b, w_br_c=w_br_c, w_out=w_out, norm_ffn=norm_ffn,
             w_ffn_in=w_ffn_in, w_ffn_out=w_ffn_out, norm_ple=norm_ple, w_ple_gate=w_ple_gate,
             w_ple_proj=w_ple_proj)
    wl = _prep_weights(w)
    y_p, o_p = _run_trunk(x_prompt, p_prompt, 0, None, w, wl)
    caches = dict(a_k=cache_a_k, a_v=cache_a_v, b_k=cache_b_k, b_v=cache_b_v, b_ki=cache_b_kidx,
                  wkv=state_c_wkv, shift=state_c_shift)
    y_s, o_s = _run_trunk(x_sample, p_sample, cache_a_k.shape[2], caches, w, wl)
    return (y_p, y_s, *o_p, *o_s)
```

```python
import functools
import math

import jax
import jax.numpy as jnp
from jax import lax
from jax.experimental import pallas as pl
from jax.experimental.pallas import tpu as pltpu

F32 = jnp.float32
BF16 = jnp.bfloat16

D_MODEL = 2048
DEPTH = 2
HEAD_DIM = 128
H_A = 4
A_W = H_A * HEAD_DIM
H_B = 4
B_W = H_B * HEAD_DIM
H_IDX = 4
D_IDX = 64
TOPK_MAX = 256
CHUNK = 64
H_C = 16
N_C = 64
C_W = H_C * N_C
SCAN_CHUNK = 32
D_DECAY_LORA = 96
D_AAA_LORA = 96
D_GATE_LORA = 256
C_COLS = 3 * C_W + D_DECAY_LORA + D_AAA_LORA + D_GATE_LORA
C_PAD = 3584
C_TAIL = C_PAD - 3 * C_W
ATT_W = 3 * A_W + B_W + 2 * HEAD_DIM + H_IDX * D_IDX + 128
ROPE_THETA = 500000.0
ROT_FRACTION = 4
NORM_EPS = 1e-6
LNX_EPS = 64e-5
NEG_BIG = -1e30

VMEM_LIMIT = 56 * 1024 * 1024
INT_MIN = -(2 ** 31)
KEY_NEG_INF = (0xFF800000 ^ 0x7FFFFFFF) - 2 ** 32


def _params(*sem):
    return pltpu.CompilerParams(dimension_semantics=sem, vmem_limit_bytes=VMEM_LIMIT)


def _row_tile(n, pref):
    t = min(n, pref)
    assert n % t == 0, (n, t)
    return t


def _nt_dot(a, b):
    return lax.dot_general(a, b, (((1,), (1,)), ((), ())), preferred_element_type=F32)


def _rms(x, gain):
    xn = x * lax.rsqrt(jnp.mean(x * x, axis=-1, keepdims=True) + NORM_EPS)
    return xn * gain


def _mm_norm_kernel(x_ref, g_ref, w_ref, o_ref, xn_ref):
    @pl.when(pl.program_id(1) == 0)
    def _():
        xn_ref[...] = _rms(x_ref[...], g_ref[...]).astype(BF16)

    o_ref[...] = jnp.dot(xn_ref[...], w_ref[...], preferred_element_type=F32)


def mm_norm(x, gain, w, tn):
    n, k = x.shape
    nc = w.shape[1]
    tm = _row_tile(n, 1024)
    assert nc % tn == 0
    return pl.pallas_call(
        _mm_norm_kernel,
        grid=(n // tm, nc // tn),
        in_specs=[pl.BlockSpec((tm, k), lambda i, j: (i, 0)),
                  pl.BlockSpec((1, k), lambda i, j: (0, 0)),
                  pl.BlockSpec((k, tn), lambda i, j: (0, j))],
        out_specs=pl.BlockSpec((tm, tn), lambda i, j: (i, j)),
        out_shape=jax.ShapeDtypeStruct((n, nc), F32),
        scratch_shapes=[pltpu.VMEM((tm, k), BF16)],
        compiler_params=_params("parallel", "arbitrary"),
        name="mm_norm",
    )(x, gain.reshape(1, k), w)


def _mm_res_kernel(x_ref, w_ref, h_ref, o_ref):
    o_ref[...] = h_ref[...] + jnp.dot(x_ref[...], w_ref[...], preferred_element_type=F32)


def mm_res(x, w, h):
    n, k = x.shape
    nc = w.shape[1]
    tm = _row_tile(n, 1024)
    tn = 1024
    return pl.pallas_call(
        _mm_res_kernel,
        grid=(n // tm, nc // tn),
        in_specs=[pl.BlockSpec((tm, k), lambda i, j: (i, 0)),
                  pl.BlockSpec((k, tn), lambda i, j: (0, j)),
                  pl.BlockSpec((tm, tn), lambda i, j: (i, j))],
        out_specs=pl.BlockSpec((tm, tn), lambda i, j: (i, j)),
        out_shape=jax.ShapeDtypeStruct((n, nc), F32),
        compiler_params=_params("parallel", "parallel"),
        name="mm_res",
    )(x, w, h)


def _mm_gated3_kernel(oa_ref, ob_ref, oc_ref, wa_ref, wb_ref, wc_ref, ga_ref, gb_ref, gc_ref, o_ref):
    ya = jnp.dot(oa_ref[...], wa_ref[...], preferred_element_type=F32)
    yb = jnp.dot(ob_ref[...], wb_ref[...], preferred_element_type=F32)
    yc = jnp.dot(oc_ref[...], wc_ref[...], preferred_element_type=F32)
    m = (jax.nn.sigmoid(ga_ref[...]) * ya + jax.nn.sigmoid(gb_ref[...]) * yb
         + jax.nn.sigmoid(gc_ref[...]) * yc)
    o_ref[...] = m.astype(o_ref.dtype)


def mm_gated3(oa, ob, oc, wa, wb, wc, gates):
    n = oa.shape[0]
    tm = _row_tile(n, 1024)
    tn = 512
    nj = D_MODEL // tn
    row = lambda w: pl.BlockSpec((tm, w), lambda i, j: (i, 0))
    col = lambda k: pl.BlockSpec((k, tn), lambda i, j: (0, j))
    gate = lambda s: pl.BlockSpec((tm, tn), lambda i, j: (i, j + s * nj))
    return pl.pallas_call(
        _mm_gated3_kernel,
        grid=(n // tm, nj),
        in_specs=[row(A_W), row(B_W), row(C_W), col(A_W), col(B_W), col(C_W), gate(0), gate(1), gate(2)],
        out_specs=pl.BlockSpec((tm, tn), lambda i, j: (i, j)),
        out_shape=jax.ShapeDtypeStruct((n, D_MODEL), BF16),
        compiler_params=_params("parallel", "parallel"),
        name="mm_gated3",
    )(oa, ob, oc, wa, wb, wc, gates, gates, gates)


def _ffn_kernel(h_ref, g_ref, wg_ref, wu_ref, wo_ref, o_ref, xn_ref):
    @pl.when(pl.program_id(1) == 0)
    def _():
        xn_ref[...] = _rms(h_ref[...], g_ref[...]).astype(BF16)
        o_ref[...] = h_ref[...]

    xn = xn_ref[...]
    gate = jnp.dot(xn, wg_ref[...], preferred_element_type=F32)
    up = jnp.dot(xn, wu_ref[...], preferred_element_type=F32)
    act = (jax.nn.silu(gate) * up).astype(BF16)
    o_ref[...] += jnp.dot(act, wo_ref[...], preferred_element_type=F32)


def ffn(h, gain, w_in, w_out):
    n, d = h.shape
    dff = w_out.shape[0]
    tm = _row_tile(n, 512)
    tf = 512
    assert dff % tf == 0
    nf = dff // tf
    return pl.pallas_call(
        _ffn_kernel,
        grid=(n // tm, nf),
        in_specs=[pl.BlockSpec((tm, d), lambda i, f: (i, 0)),
                  pl.BlockSpec((1, d), lambda i, f: (0, 0)),
                  pl.BlockSpec((d, tf), lambda i, f: (0, f)),
                  pl.BlockSpec((d, tf), lambda i, f: (0, f + nf)),
                  pl.BlockSpec((tf, d), lambda i, f: (f, 0))],
        out_specs=pl.BlockSpec((tm, d), lambda i, f: (i, 0)),
        out_shape=jax.ShapeDtypeStruct((n, d), F32),
        scratch_shapes=[pltpu.VMEM((tm, d), BF16)],
        compiler_params=_params("parallel", "arbitrary"),
        name="ffn",
    )(h, gain.reshape(1, d), w_in, w_in, w_out)


def _ple_kernel(h_ref, g_ref, wg_ref, p_ref, wp_ref, hj_ref, o_ref, xn_ref):
    @pl.when(pl.program_id(1) == 0)
    def _():
        xn_ref[...] = _rms(h_ref[...], g_ref[...]).astype(BF16)

    gate = jnp.dot(xn_ref[...], wg_ref[...], preferred_element_type=F32)
    emb = jnp.dot(p_ref[...], wp_ref[...], preferred_element_type=F32)
    o_ref[...] = hj_ref[...] + jax.nn.sigmoid(gate) * emb


def ple(h, gain, w_gate, p, w_proj):
    n, d = h.shape
    pd = p.shape[1]
    tm = _row_tile(n, 1024)
    tn = 1024
    return pl.pallas_call(
        _ple_kernel,
        grid=(n // tm, d // tn),
        in_specs=[pl.BlockSpec((tm, d), lambda i, j: (i, 0)),
                  pl.BlockSpec((1, d), lambda i, j: (0, 0)),
                  pl.BlockSpec((d, tn), lambda i, j: (0, j)),
                  pl.BlockSpec((tm, pd), lambda i, j: (i, 0)),
                  pl.BlockSpec((pd, tn), lambda i, j: (0, j)),
                  pl.BlockSpec((tm, tn), lambda i, j: (i, j))],
        out_specs=pl.BlockSpec((tm, tn), lambda i, j: (i, j)),
        out_shape=jax.ShapeDtypeStruct((n, d), F32),
        scratch_shapes=[pltpu.VMEM((tm, d), BF16)],
        compiler_params=_params("parallel", "arbitrary"),
        name="ple",
    )(h, gain.reshape(1, d), w_gate, p, w_proj, h)


def _rope_tables(pos, width, period, rot, scale_lanes=None):
    half = rot // 2
    inv = ROPE_THETA ** (-jnp.arange(half, dtype=F32) / half)
    lane = jnp.arange(width) % period
    first, second = lane < half, (lane >= half) & (lane < rot)
    ang = pos.astype(F32)[:, None] * inv[lane % half][None, :]
    cos, sin = jnp.cos(ang), jnp.sin(ang)
    c = jnp.where((first | second)[None, :], cos, 1.0)
    s_up = jnp.where(first[None, :], -sin, 0.0)
    s_dn = jnp.where(second[None, :], sin, 0.0)
    if scale_lanes is not None:
        only, mult = scale_lanes
        c = jnp.where(only[None, :], c, mult[None, :])
        s_up = jnp.where(only[None, :], s_up, 0.0)
        s_dn = jnp.where(only[None, :], s_dn, 0.0)
    return jnp.stack([c, s_up, s_dn]).astype(F32)


def _rope(x, tab, half):
    return (x * tab[0] + pltpu.roll(x, x.shape[1] - half, 1) * tab[1] + pltpu.roll(x, half, 1) * tab[2])


def _post_in_kernel(z_ref, t128_ref, t64_ref, tki_ref, gaq_ref, gak_ref, gbq_ref, gbk_ref,
                    aq_ref, akf_ref, akb_ref, avf_ref, avb_ref, bq_ref, bkf_ref, bkb_ref,
                    bvf_ref, bvb_ref, qi_ref, kiw_ref, kilo_ref, kihi_ref):
    hd = HEAD_DIM
    t128 = t128_ref[...]
    t64 = t64_ref[...]
    half128 = hd // ROT_FRACTION // 2
    half64 = D_IDX // ROT_FRACTION // 2
    for h in range(H_A):
        sl = slice(h * hd, (h + 1) * hd)
        aq = _rms(z_ref[:, sl], gaq_ref[...]) * (-(hd ** -0.5))
        aq_ref[:, sl] = aq.astype(BF16)
        ak = _rms(z_ref[:, A_W + h * hd:A_W + (h + 1) * hd], gak_ref[...])
        akf_ref[:, h, :] = ak
        akb_ref[:, sl] = ak.astype(BF16)
        av = z_ref[:, 2 * A_W + h * hd:2 * A_W + (h + 1) * hd]
        avf_ref[:, h, :] = av
        avb_ref[:, sl] = av.astype(BF16)
    o = 3 * A_W
    for h in range(H_B):
        sl = slice(h * hd, (h + 1) * hd)
        bq = _rope(_rms(z_ref[:, o + h * hd:o + (h + 1) * hd], gbq_ref[...]), t128, half128) * (hd ** -0.5)
        bq_ref[:, sl] = bq.astype(BF16)
    o += B_W
    bk = _rope(_rms(z_ref[:, o:o + hd], gbk_ref[...]), t128, half128)
    bkf_ref[...] = bk
    bkb_ref[...] = bk.astype(BF16)
    o += hd
    bv = z_ref[:, o:o + hd]
    bvf_ref[...] = bv
    bvb_ref[...] = bv.astype(BF16)
    o += hd
    for c in range(H_IDX * D_IDX // 128):
        qi = _rope(z_ref[:, o + c * 128:o + (c + 1) * 128], t64, half64) * (D_IDX ** -0.5)
        qi_ref[:, c * 128:(c + 1) * 128] = qi.astype(BF16)
    o += H_IDX * D_IDX
    kiw = _rope(z_ref[:, o:o + 128], tki_ref[...], half64)
    kiw_ref[...] = kiw
    lane = lax.broadcasted_iota(jnp.int32, kiw.shape, 1)
    ki = jnp.where(lane < D_IDX, kiw, 0.0)
    kilo_ref[...] = ki.astype(BF16)
    kihi_ref[...] = pltpu.roll(ki, D_IDX, 1).astype(BF16)


def post_in(z, tabs, gains, t_len):
    n = z.shape[0]
    tm = _row_tile(t_len, 512)
    nt = t_len // tm
    rows = lambda w: pl.BlockSpec((tm, w), lambda i: (i, 0))
    tab = pl.BlockSpec((3, tm, 128), lambda i: (0, i % nt, 0))
    gain = pl.BlockSpec((1, HEAD_DIM), lambda i: (0, 0))
    per_head = pl.BlockSpec((tm, H_A, HEAD_DIM), lambda i: (i, 0, 0))
    widths = [(A_W, BF16), (A_W, F32), (A_W, BF16), (A_W, F32), (A_W, BF16), (B_W, BF16),
              (HEAD_DIM, F32), (HEAD_DIM, BF16), (HEAD_DIM, F32), (HEAD_DIM, BF16),
              (H_IDX * D_IDX, BF16), (128, F32), (128, BF16), (128, BF16)]
    return pl.pallas_call(
        _post_in_kernel,
        grid=(n // tm,),
        in_specs=[rows(ATT_W), tab, tab, tab, gain, gain, gain, gain],
        out_specs=[per_head if i in (1, 3) else rows(w) for i, (w, _) in enumerate(widths)],
        out_shape=[jax.ShapeDtypeStruct((n, H_A, HEAD_DIM) if i in (1, 3) else (n, w), dt)
                   for i, (w, dt) in enumerate(widths)],
        compiler_params=_params("parallel"),
        name="post_in",
    )(z, *tabs, *[g.reshape(1, HEAD_DIM) for g in gains])


LOG2E = 1.4426950408889634
ATTN_A_ROW_CHUNK = 256
ATTN_A_BLOCKS_PER_STEP = 4


def _attn_a_kernel(q_ref, k_ref, v_ref, tri_ref, o_ref, *, tq, tk, rc, past):
    i = pl.program_id(2)
    q0 = past + i * tq
    tri = tri_ref[...]
    n_full = q0 // tk
    n_all = (q0 + tq - 1 + tk - 1) // tk
    nrc = tq // rc
    qs = [q_ref[c * rc:(c + 1) * rc, :] for c in range(nrc)]
    rows = [q0 + c * rc + lax.broadcasted_iota(jnp.int32, (rc, tk), 0) for c in range(nrc)]
    col = lax.broadcasted_iota(jnp.int32, (rc, tk), 1)

    def step(j, carry, masked, nsub):
        cs = range(nrc)
        ss = range(nsub)
        offs = [pl.multiple_of((j - s) * tk, tk) for s in ss]
        kb = [k_ref[pl.ds(o, tk), :] for o in offs]
        vb = [v_ref[pl.ds(o, tk), :] for o in offs]
        later = [carry[0][c * rc:(c + 1) * rc] for c in cs]
        acc = [carry[1][c * rc:(c + 1) * rc] for c in cs]
        nz = [[_nt_dot(qs[c], kb[s]) for c in cs] for s in ss]
        loss = [[jnp.log(1.0 + jnp.exp2(jnp.minimum(nz[s][c] * (-LOG2E), 126.0))) for c in cs] for s in ss]
        if masked:
            causal = [[(offs[s] + col) < rows[c] for c in cs] for s in ss]
            loss = [[jnp.where(causal[s][c], loss[s][c], 0.0) for c in cs] for s in ss]
        within = [[jnp.dot(loss[s][c].astype(BF16), tri, preferred_element_type=F32) for c in cs] for s in ss]
        for s in ss:
            w = [jnp.exp2((nz[s][c] + later[c] + within[s][c]) * (-LOG2E)) for c in cs]
            if masked:
                w = [jnp.where(causal[s][c], w[c], 0.0) for c in cs]
            acc = [acc[c] + jnp.dot(w[c].astype(BF16), vb[s], preferred_element_type=F32) for c in cs]
            later = [later[c] + within[s][c][:, 0:1] for c in cs]
        return jnp.concatenate(later, axis=0), jnp.concatenate(acc, axis=0)

    carry = (jnp.zeros((tq, 1), F32), jnp.zeros((tq, HEAD_DIM), F32))
    carry = lax.fori_loop(0, n_all - n_full, lambda t, c: step(n_all - 1 - t, c, True, 1), carry)
    n_multi = n_full // ATTN_A_BLOCKS_PER_STEP
    n_single = n_full - n_multi * ATTN_A_BLOCKS_PER_STEP
    carry = lax.fori_loop(0, n_single, lambda t, c: step(n_full - 1 - t, c, False, 1), carry)
    carry = lax.fori_loop(
        0, n_multi,
        lambda t, c: step(n_full - n_single - 1 - t * ATTN_A_BLOCKS_PER_STEP, c, False, ATTN_A_BLOCKS_PER_STEP),
        carry)
    o_ref[...] = carry[1].astype(o_ref.dtype)


def attn_a(q, k, v, past, tk):
    b, t, _ = q.shape
    lp = k.shape[1]
    tq = _row_tile(t, 512)
    rc = min(tq, ATTN_A_ROW_CHUNK)
    tri = jnp.tri(tk, dtype=BF16)
    return pl.pallas_call(
        functools.partial(_attn_a_kernel, tq=tq, tk=tk, rc=rc, past=past),
        grid=(b, H_A, t // tq),
        in_specs=[pl.BlockSpec((None, tq, HEAD_DIM), lambda bi, h, i: (bi, i, h)),
                  pl.BlockSpec((None, lp, HEAD_DIM), lambda bi, h, i: (bi, 0, h)),
                  pl.BlockSpec((None, lp, HEAD_DIM), lambda bi, h, i: (bi, 0, h)),
                  pl.BlockSpec((tk, tk), lambda bi, h, i: (0, 0))],
        out_specs=pl.BlockSpec((None, tq, HEAD_DIM), lambda bi, h, i: (bi, i, h)),
        out_shape=jax.ShapeDtypeStruct((b, t, A_W), BF16),
        compiler_params=_params("parallel", "parallel", "parallel"),
        name="attn_a",
    )(q, k, v, tri)


DSA_QUERY_ROWS = 256
DSA_ATTEND_ROWS = 128


def _dsa_kernel(qi_ref, kiw_ref, kilo_ref, kihi_ref, q_ref, k_ref, v_ref, triu_ref, o_ref, key_ref,
                *, tq, tk, past, n_keys, topk):
    i = pl.program_id(1)
    q0 = past + i * tq
    pos = q0 + lax.broadcasted_iota(jnp.int32, (tq, 1), 0)
    lim = jnp.minimum((pos // CHUNK + 1) * CHUNK, n_keys)
    n_adm = jnp.minimum(((q0 + tq - 1) // CHUNK + 1) * CHUNK, n_keys)
    nt = (n_adm + tk - 1) // tk
    col = lax.broadcasted_iota(jnp.int32, (tq, tk), 1)
    ngrp = tk // 128

    qi = qi_ref[...]
    kiw = kiw_ref[...]
    wi = [kiw[:, D_IDX + h:D_IDX + h + 1] for h in range(H_IDX)]

    n_pair = (nt + 1) // 2
    pair = range(2)

    def score_pair(t, _):
        offs = [pl.multiple_of((2 * t + u) * tk, tk) for u in pair]
        kis = [(kilo_ref[pl.ds(o, tk), :], kihi_ref[pl.ds(o, tk), :]) for o in offs]
        s_idx = [[_nt_dot(qi[:, (h // 2) * 128:(h // 2 + 1) * 128], kis[u][h % 2]) for h in range(H_IDX)]
                 for u in pair]
        for u in pair:
            s = jnp.zeros((tq, tk), F32)
            for h in range(H_IDX):
                s = s + wi[h] * jnp.maximum(s_idx[u][h], 0.0)
            s = jnp.where(s == 0.0, 0.0, s)
            s = jnp.where(offs[u] + col < lim, s, -jnp.inf)
            bits = lax.bitcast_convert_type(s, jnp.int32)
            key_ref[:, pl.ds(offs[u], tk)] = bits ^ ((bits >> 31) & 0x7FFFFFFF)
        return 0

    lax.fori_loop(0, n_pair, score_pair, 0)

    cr = min(tq, DSA_ATTEND_ROWS)

    def count_ge(cand):
        accs = []
        for r in range(tq // cr):
            cand_b = jnp.broadcast_to(cand[r * cr:(r + 1) * cr], (cr, 128))

            def body(t, acc):
                off = pl.multiple_of(t * tk, tk)
                kt = key_ref[r * cr:(r + 1) * cr, pl.ds(off, tk)]
                for g in range(ngrp):
                    acc = acc + jnp.where(kt[:, g * 128:(g + 1) * 128] >= cand_b, 1, 0)
                return acc

            accs.append(lax.fori_loop(0, nt, body, jnp.zeros((cr, 128), jnp.int32)))
        return jnp.sum(jnp.concatenate(accs, axis=0), axis=1, keepdims=True)

    def bit_step(b, t_u):
        cand_u = t_u | lax.shift_left(jnp.int32(1), 31 - b)
        ok = count_ge(cand_u ^ INT_MIN) >= topk
        return jnp.where(ok, cand_u, t_u)

    t_u = lax.fori_loop(0, 32, bit_step, jnp.zeros((tq, 1), jnp.int32))
    tau = t_u ^ INT_MIN
    tau_next = jnp.where(tau == 2 ** 31 - 1, tau, tau + 1)
    cnt_gt = jnp.where(tau == 2 ** 31 - 1, 0, count_ge(tau_next))
    need = jnp.where(tau == KEY_NEG_INF, 0, topk - cnt_gt).astype(F32)

    triu = triu_ref[...]

    def select_tile(t, seen):
        off = pl.multiple_of(t * tk, tk)
        kt = key_ref[:, pl.ds(off, tk)]
        eq = kt == tau
        prefix = seen + jnp.dot(jnp.where(eq, 1.0, 0.0).astype(BF16), triu, preferred_element_type=F32)
        rank = jnp.where(eq, prefix, jnp.inf)
        bias = jnp.where(kt > tau, 0.0, jnp.where(rank <= need, 0.0, NEG_BIG)).astype(F32)
        key_ref[:, pl.ds(off, tk)] = lax.bitcast_convert_type(bias, jnp.int32)
        return prefix[:, tk - 1:tk]

    lax.fori_loop(0, 2 * n_pair, select_tile, jnp.zeros((tq, 1), F32))

    rg = min(tq, DSA_ATTEND_ROWS)
    groups = range(tq // rg)
    gr = H_B * rg
    qs = [jnp.concatenate([q_ref[r * rg:(r + 1) * rg, h * HEAD_DIM:(h + 1) * HEAD_DIM] for h in range(H_B)], axis=0)
          for r in groups]

    def attend_pair(t, carry):
        offs = [pl.multiple_of((2 * t + u) * tk, tk) for u in pair]
        kb = [k_ref[pl.ds(o, tk), :] for o in offs]
        vb = [v_ref[pl.ds(o, tk), :] for o in offs]
        new = []
        for r in groups:
            m, l, acc = [c[r * gr:(r + 1) * gr] for c in carry]
            bias = [lax.bitcast_convert_type(key_ref[r * rg:(r + 1) * rg, pl.ds(o, tk)], F32) for o in offs]
            logits = [(_nt_dot(qs[r], kb[u]).reshape(H_B, rg, tk) + bias[u][None]).reshape(gr, tk) for u in pair]
            m_new = jnp.maximum(m, jnp.maximum(jnp.max(logits[0], axis=1, keepdims=True),
                                               jnp.max(logits[1], axis=1, keepdims=True)))
            alpha = jnp.exp(m - m_new)
            p = [jnp.exp(logits[u] - m_new) for u in pair]
            l = alpha * l + (jnp.sum(p[0], axis=1, keepdims=True) + jnp.sum(p[1], axis=1, keepdims=True))
            acc = alpha * acc + (jnp.dot(p[0].astype(BF16), vb[0], preferred_element_type=F32)
                                 + jnp.dot(p[1].astype(BF16), vb[1], preferred_element_type=F32))
            new.append((m_new, l, acc))
        return tuple(jnp.concatenate([n[k] for n in new], axis=0) for k in range(3))

    init = (jnp.full((H_B * tq, 1), NEG_BIG, F32), jnp.zeros((H_B * tq, 1), F32),
            jnp.zeros((H_B * tq, HEAD_DIM), F32))
    _, l, acc = lax.fori_loop(0, n_pair, attend_pair, init)
    out = acc / l
    for r in groups:
        for h in range(H_B):
            rows = out[r * gr + h * rg:r * gr + (h + 1) * rg]
            o_ref[r * rg:(r + 1) * rg, h * HEAD_DIM:(h + 1) * HEAD_DIM] = rows.astype(o_ref.dtype)


def dsa(qi, kiw, kilo, kihi, q, k, v, past, n_keys, tk):
    b, t, _ = q.shape
    lp = k.shape[1]
    tq = _row_tile(t, DSA_QUERY_ROWS)
    topk = min(TOPK_MAX, n_keys // 4)
    assert lp % (2 * tk) == 0 and lp >= topk
    triu = jnp.tri(tk, dtype=BF16).T
    qblk = lambda w: pl.BlockSpec((None, tq, w), lambda bi, i: (bi, i, 0))
    kblk = lambda w: pl.BlockSpec((None, lp, w), lambda bi, i: (bi, 0, 0))
    return pl.pallas_call(
        functools.partial(_dsa_kernel, tq=tq, tk=tk, past=past, n_keys=n_keys, topk=topk),
        grid=(b, t // tq),
        in_specs=[qblk(H_IDX * D_IDX), qblk(128), kblk(128), kblk(128), qblk(B_W),
                  kblk(HEAD_DIM), kblk(HEAD_DIM), pl.BlockSpec((tk, tk), lambda bi, i: (0, 0))],
        out_specs=qblk(B_W),
        out_shape=jax.ShapeDtypeStruct((b, t, B_W), BF16),
        scratch_shapes=[pltpu.VMEM((tq, lp), jnp.int32)],
        compiler_params=_params("parallel", "parallel"),
        name="dsa",
    )(qi, kiw, kilo, kihi, q, k, v, triu)


def _softplus(x):
    return jnp.maximum(x, 0.0) + jnp.log(1.0 + jnp.exp(-jnp.abs(x)))


def _c_prep_kernel(*refs, first_layer):
    if first_layer:
        (cz_ref, prev_ref, mu_ref, w0_ref, a0_ref, w2_ref, a2_ref, g2_ref, kk_ref, ka_ref,
         r_ref, kn_ref, k_ref, v_ref, lw_ref, a_ref, g_ref, vtok_ref) = refs
    else:
        (cz_ref, prev_ref, mu_ref, w0_ref, a0_ref, w2_ref, a2_ref, g2_ref, kk_ref, ka_ref,
         vf_ref, v0_ref, v1_ref, v2_ref,
         r_ref, kn_ref, k_ref, v_ref, lw_ref, a_ref, g_ref) = refs

    def put_heads(ref, x):
        for pair in range(H_C // 2):
            tile = x[:, pair * 128:(pair + 1) * 128]
            ref[2 * pair] = tile[:, :N_C]
            ref[2 * pair + 1] = pltpu.roll(tile, N_C, 1)[:, :N_C]

    cz = cz_ref[...]
    rowid = lax.broadcasted_iota(jnp.int32, cz.shape, 0)
    prev = jnp.where(rowid == 0, prev_ref[...], pltpu.roll(cz, 1, 0))
    zz = cz + (prev - cz) * mu_ref[...]
    r = zz[:, 0:C_W]
    k = zz[:, C_W:2 * C_W]
    v = zz[:, 2 * C_W:3 * C_W]
    tail = zz[:, 3 * C_W:C_PAD]
    w = w0_ref[...] + jnp.dot(jnp.tanh(tail).astype(BF16), w2_ref[...], preferred_element_type=F32)
    w = -_softplus(-w) - 0.5
    a = jax.nn.sigmoid(a0_ref[...] + jnp.dot(tail.astype(BF16), a2_ref[...], preferred_element_type=F32))
    g = jnp.dot(jax.nn.sigmoid(tail).astype(BF16), g2_ref[...], preferred_element_type=F32)
    if not first_layer:
        lora = jnp.dot(v.astype(BF16), v1_ref[...], preferred_element_type=F32)
        lora = jnp.dot(lora.astype(BF16), v2_ref[...], preferred_element_type=F32)
        v = v + (vf_ref[...] - v) * jax.nn.sigmoid(v0_ref[...] + lora)
    if first_layer:
        vtok_ref[...] = v
    put_heads(r_ref, r)
    put_heads(kn_ref, k * kk_ref[...])
    put_heads(k_ref, k * (1.0 + (a - 1.0) * ka_ref[...]))
    put_heads(v_ref, v)
    put_heads(lw_ref, -jnp.exp(w))
    put_heads(a_ref, a)
    put_heads(g_ref, g)


def c_prep(cz, prev_rows, vecs, mats, v_first, vres, tm):
    b, t, _ = cz.shape
    first_layer = vres is None
    blk = lambda w: pl.BlockSpec((None, tm, w), lambda bi, i: (bi, i, 0))
    vec = lambda w: pl.BlockSpec((1, w), lambda bi, i: (0, 0))
    mat = lambda m: pl.BlockSpec(m.shape, lambda bi, i: (0, 0))
    mu, w0, a0, k_k, k_a = vecs
    w2, a2, g2 = mats
    ins = [cz, prev_rows, mu, w0, a0, w2, a2, g2, k_k, k_a]
    specs = [blk(C_PAD), pl.BlockSpec((None, None, 1, C_PAD), lambda bi, i: (bi, i, 0, 0)),
             vec(C_PAD), vec(C_W), vec(C_W), mat(w2), mat(a2), mat(g2), vec(C_W), vec(C_W)]
    if not first_layer:
        v0, v1, v2 = vres
        ins += [v_first, v0, v1, v2]
        specs += [blk(C_W), vec(C_W), mat(v1), mat(v2)]
    heads = pl.BlockSpec((None, H_C, tm, N_C), lambda bi, i: (bi, 0, i, 0))
    out_specs = [heads] * 7
    out_shape = [jax.ShapeDtypeStruct((b, H_C, t, N_C), F32)] * 7
    if first_layer:
        out_specs.append(blk(C_W))
        out_shape.append(jax.ShapeDtypeStruct((b, t, C_W), F32))
    return pl.pallas_call(
        functools.partial(_c_prep_kernel, first_layer=first_layer),
        grid=(b, t // tm),
        in_specs=specs,
        out_specs=out_specs,
        out_shape=out_shape,
        compiler_params=_params("parallel", "parallel"),
        name="c_prep",
    )(*ins)


def _bdot(a, b, contract_a, contract_b):
    return lax.dot_general(a.astype(BF16), b.astype(BF16), (((contract_a,), (contract_b,)), ((0,), (0,))),
                           preferred_element_type=F32)


def _bdot3(a, b, contract_a, contract_b):
    ah = a.astype(BF16)
    bh = b.astype(BF16)
    al = (a - ah.astype(F32)).astype(BF16)
    bl = (b - bh.astype(F32)).astype(BF16)
    dn = (((contract_a,), (contract_b,)), ((0,), (0,)))
    dot = lambda x, y: lax.dot_general(x, y, dn, preferred_element_type=F32)
    return dot(ah, bh) + (dot(al, bh) + dot(ah, bl))


C_SCAN_CHUNKS_PER_STEP = 4


def _c_scan_kernel(r_ref, kn_ref, k_ref, v_ref, lw_ref, a_ref, g_ref, s0_ref, rk_ref, lng_ref, lnb_ref,
                   y_ref, st_ref, s_ref, *, nck):
    c_len = SCAN_CHUNK

    @pl.when(pl.program_id(1) == 0)
    def _():
        s_ref[...] = s0_ref[...]

    def step(c0, nc):
        g_n = H_C * nc
        rows = pl.ds(pl.multiple_of(c0 * c_len, c_len), nc * c_len)
        load = lambda ref: ref[:, rows, :].reshape(g_n, c_len, N_C)
        per_head = lambda x: jnp.broadcast_to(x[:, None], (H_C, nc) + x.shape[1:]).reshape((g_n,) + x.shape[1:])
        of_chunk = lambda x, j: x.reshape((H_C, nc) + x.shape[1:])[:, j]
        ti = lax.broadcasted_iota(jnp.int32, (g_n, c_len, c_len), 1)
        si = lax.broadcasted_iota(jnp.int32, (g_n, c_len, c_len), 2)
        incl = si <= ti
        strict = si < ti
        tri = jnp.where(incl, 1.0, 0.0)
        eye = jnp.where(si == ti, 1.0, 0.0)
        r, k, v, lw, a, kn = (load(x) for x in (r_ref, k_ref, v_ref, lw_ref, a_ref, kn_ref))
        kn = kn * lax.rsqrt(jnp.maximum(jnp.sum(kn * kn, axis=-1, keepdims=True), 1e-24))
        aa = -kn
        bb = kn * a
        g = _bdot3(tri, lw, 2, 1)
        g_end = g[:, c_len - 1:c_len, :]
        a_t = aa * jnp.exp(g - lw)
        r_t = r * jnp.exp(g)
        inv = jnp.exp(-g)
        b_h = bb * inv
        k_h = k * inv
        to_end = jnp.exp(g_end - g)
        b_e = bb * to_end
        k_e = k * to_end
        d_end = jnp.exp(g_end)
        a_ab = jnp.where(strict, _bdot(a_t, b_h, 2, 2), 0.0)
        a_ak = jnp.where(strict, _bdot(a_t, k_h, 2, 2), 0.0)
        a_rb = jnp.where(incl, _bdot(r_t, b_h, 2, 2), 0.0)
        a_rk = jnp.where(incl, _bdot(r_t, k_h, 2, 2), 0.0)
        t_inv = eye + a_ab
        pw = a_ab
        for _ in range(4):
            pw = _bdot(pw, pw, 2, 1)
            t_inv = t_inv + _bdot(t_inv, pw, 2, 1)
        p_mat = _bdot(t_inv, a_t, 2, 1)
        q_mat = _bdot(t_inv, _bdot(a_ak, v, 2, 1), 2, 1)
        y_v = _bdot(a_rk, v, 2, 1)
        s = s_ref[...]
        ys = []
        for j in range(nc):
            u = _bdot(of_chunk(p_mat, j), s, 2, 2) + of_chunk(q_mat, j)
            ys.append(_bdot(of_chunk(r_t, j), s, 2, 2) + _bdot(of_chunk(a_rb, j), u, 2, 1) + of_chunk(y_v, j))
            uv = jnp.concatenate([u, of_chunk(v, j)], axis=1)
            bk = jnp.concatenate([of_chunk(b_e, j), of_chunk(k_e, j)], axis=1)
            s = s * of_chunk(d_end, j) + _bdot(uv, bk, 1, 1)
        s_ref[...] = s
        y = jnp.stack(ys, axis=1).reshape(g_n, c_len, N_C)
        mu = jnp.mean(y, axis=-1, keepdims=True)
        yc = y - mu
        yn = yc * lax.rsqrt(jnp.mean(yc * yc, axis=-1, keepdims=True) + LNX_EPS)
        yn = yn * per_head(lng_ref[...]) + per_head(lnb_ref[...])
        bonus = jnp.sum(r * k * per_head(rk_ref[...]), axis=-1, keepdims=True) * v
        o = ((yn + bonus) * load(g_ref)).reshape(H_C, nc * c_len, N_C)
        y_ref[rows, :] = jnp.concatenate([o[h] for h in range(H_C)], axis=-1).astype(y_ref.dtype)

    n_multi = nck // C_SCAN_CHUNKS_PER_STEP
    if n_multi:
        lax.fori_loop(0, n_multi, lambda t, _: step(t * C_SCAN_CHUNKS_PER_STEP, C_SCAN_CHUNKS_PER_STEP), None)
    for c in range(n_multi * C_SCAN_CHUNKS_PER_STEP, nck):
        step(c, 1)

    @pl.when(pl.program_id(1) == pl.num_programs(1) - 1)
    def _():
        st_ref[...] = s_ref[...]


def c_scan(r, kn, k, v, lw, a, g, s0, r_k, ln_g, ln_b, tm):
    b, _, t, _ = r.shape
    blk = pl.BlockSpec((None, H_C, tm, N_C), lambda bi, i: (bi, 0, i, 0))
    st = pl.BlockSpec((None, H_C, N_C, N_C), lambda bi, i: (bi, 0, 0, 0))
    vec = pl.BlockSpec((H_C, 1, N_C), lambda bi, i: (0, 0, 0))
    return pl.pallas_call(
        functools.partial(_c_scan_kernel, nck=tm // SCAN_CHUNK),
        grid=(b, t // tm),
        in_specs=[blk] * 7 + [st, vec, vec, vec],
        out_specs=[pl.BlockSpec((None, tm, C_W), lambda bi, i: (bi, i, 0)), st],
        out_shape=[jax.ShapeDtypeStruct((b, t, C_W), BF16),
                   jax.ShapeDtypeStruct((b, H_C, N_C, N_C), F32)],
        scratch_shapes=[pltpu.VMEM((H_C, N_C, N_C), F32)],
        compiler_params=_params("parallel", "arbitrary"),
        name="c_scan",
    )(r, kn, k, v, lw, a, g, s0, r_k.reshape(H_C, 1, N_C), ln_g.reshape(H_C, 1, N_C), ln_b.reshape(H_C, 1, N_C))


def _pad_rows(x, rows, axis=1):
    pad = rows - x.shape[axis]
    if pad == 0:
        return x
    widths = [(0, 0)] * x.ndim
    widths[axis] = (0, pad)
    return jnp.pad(x, widths)


ATT_COLS = 3 * A_W + B_W + 2 * HEAD_DIM + H_IDX * D_IDX + D_IDX + H_IDX
IN_COLS = ATT_COLS + C_COLS + 3 * D_MODEL


def _regroup_kernel(w_ref, tail_ref, att_ref, c_ref, g_ref):
    lane = lax.broadcasted_iota(jnp.int32, (w_ref.shape[0], 128), 1)
    n_full = IN_COLS // 128

    def tile(j):
        return tail_ref[...] if j == n_full else w_ref[:, j * 128:(j + 1) * 128]

    def shifted(j0, shift, j):
        joined = jnp.where(lane >= shift, tile(j0 + j), tile(j0 + j + 1))
        return pltpu.roll(joined, 128 - shift, 1)

    for j in range(ATT_W // 128):
        x = tile(j)
        if (j + 1) * 128 > ATT_COLS:
            x = jnp.where(lane < ATT_COLS - j * 128, x, 0.0)
        att_ref[:, j * 128:(j + 1) * 128] = x.astype(BF16)
    for j in range(C_PAD // 128):
        x = shifted(ATT_COLS // 128, ATT_COLS % 128, j)
        if (j + 1) * 128 > C_COLS:
            x = jnp.where(lane < C_COLS - j * 128, x, 0.0)
        c_ref[:, j * 128:(j + 1) * 128] = x.astype(BF16)
    g0 = ATT_COLS + C_COLS
    for j in range(3 * D_MODEL // 128):
        g_ref[:, j * 128:(j + 1) * 128] = shifted(g0 // 128, g0 % 128, j).astype(BF16)


def regroup_w_in(w_in):
    depth, d, cols = w_in.shape
    assert cols == IN_COLS and (ATT_COLS + C_COLS) % 128 != 0
    n_full = cols // 128
    tail = jnp.pad(w_in[:, :, n_full * 128:], ((0, 0), (0, 0), (0, (n_full + 1) * 128 - cols)))
    tm = 128
    blk = lambda w: pl.BlockSpec((None, tm, w), lambda l, i: (l, i, 0))
    return pl.pallas_call(
        _regroup_kernel,
        grid=(depth, d // tm),
        in_specs=[blk(cols), blk(128)],
        out_specs=[blk(ATT_W), blk(C_PAD), blk(3 * D_MODEL)],
        out_shape=[jax.ShapeDtypeStruct((depth, d, w), BF16) for w in (ATT_W, C_PAD, 3 * D_MODEL)],
        compiler_params=_params("parallel", "parallel"),
        name="regroup_w_in",
    )(w_in, tail)


def _prep_weights(w):
    out = []
    w_att, w_c, w_g = regroup_w_in(w["w_in"])
    for i in range(DEPTH):
        tail_rows = lambda m, off: jnp.pad(m, ((off, C_TAIL - off - m.shape[0]), (0, 0))).astype(BF16)
        lw = dict(
            w_att=w_att[i], w_c=w_c[i], w_g=w_g[i],
            w2=tail_rows(w["c_w2"][i], 0),
            a2=tail_rows(w["c_a2"][i], D_DECAY_LORA),
            g2=tail_rows(w["c_g2"][i], D_DECAY_LORA + D_AAA_LORA),
            mu=jnp.pad(w["c_mu"][i], (0, C_PAD - C_COLS)).reshape(1, C_PAD),
            w_br_a=w["w_br_a"][i].astype(BF16), w_br_b=w["w_br_b"][i].astype(BF16),
            w_br_c=w["w_br_c"][i].astype(BF16), w_out=w["w_out"][i].astype(BF16),
            w_ffn_in=w["w_ffn_in"][i].astype(BF16), w_ffn_out=w["w_ffn_out"][i].astype(BF16),
            w_ple_gate=w["w_ple_gate"][i].astype(BF16), w_ple_proj=w["w_ple_proj"][i].astype(BF16),
        )
        if i > 0:
            lw["v1"] = w["c_v1"][i - 1].astype(BF16)
            lw["v2"] = w["c_v2"][i - 1].astype(BF16)
        out.append(lw)
    return out


def _run_trunk(x, p, past, caches, w, wl):
    b, t, _ = x.shape
    n = b * t
    n_keys = past + t
    pos = past + jnp.arange(t)
    lane = jnp.arange(128)
    tabs = (_rope_tables(pos, 128, HEAD_DIM, HEAD_DIM // ROT_FRACTION),
            _rope_tables(pos, 128, D_IDX, D_IDX // ROT_FRACTION),
            _rope_tables(pos, 128, 128, D_IDX // ROT_FRACTION,
                         scale_lanes=(lane < D_IDX, jnp.where(lane < D_IDX + H_IDX, H_IDX ** -0.5, 1.0))))
    tk_a = 256
    tk_b = 512
    lp_a = -(-n_keys // tk_a) * tk_a
    lp_b = -(-n_keys // (2 * tk_b)) * (2 * tk_b)
    t_c = -(-t // SCAN_CHUNK) * SCAN_CHUNK
    tm_c = _row_tile(t, 256)
    tm_s = _row_tile(t_c, 256)
    h = x.reshape(n, D_MODEL)
    outs = ([], [], [], [], [], [], [])
    v_first = None
    for i in range(DEPTH):
        lw = wl[i]
        z_att = mm_norm(h, w["norm_mix"][i], lw["w_att"], ATT_W // 3)
        z_c = mm_norm(h, w["norm_mix"][i], lw["w_c"], C_PAD // 2)
        z_g = mm_norm(h, w["norm_mix"][i], lw["w_g"], 1536)
        (aq, ak_f, ak_b, av_f, av_b, bq, bk_f, bk_b, bv_f, bv_b, qi, kiw, kilo, kihi) = post_in(
            z_att, tabs, (w["a_q_norm"][i], w["a_k_norm"][i], w["b_q_norm"][i], w["b_k_norm"][i]), t)
        seq = lambda a: a.reshape(b, t, a.shape[-1])

        def keys(new, old, lp):
            new = seq(new)
            if old is not None:
                new = jnp.concatenate([old.astype(BF16), new], axis=1)
            return _pad_rows(new, lp)

        c = caches
        past_of = lambda name: None if c is None else c[name][i]
        pa_k = None if c is None else c["a_k"][i].reshape(b, past, A_W)
        pa_v = None if c is None else c["a_v"][i].reshape(b, past, A_W)
        o_a = attn_a(seq(aq), keys(ak_b, pa_k, lp_a), keys(av_b, pa_v, lp_a), past, tk_a)
        if c is None:
            p_lo = p_hi = None
        else:
            p_lo = jnp.pad(c["b_ki"][i], ((0, 0), (0, 0), (0, 128 - D_IDX)))
            p_hi = jnp.pad(c["b_ki"][i], ((0, 0), (0, 0), (128 - D_IDX, 0)))
        o_b = dsa(seq(qi), seq(kiw), keys(kilo, p_lo, lp_b), keys(kihi, p_hi, lp_b), seq(bq),
                  keys(bk_b, past_of("b_k"), lp_b), keys(bv_b, past_of("b_v"), lp_b), past, n_keys, tk_b)
        cz = seq(z_c)
        shift0 = jnp.zeros((b, C_COLS), F32) if c is None else c["shift"][i]
        shift0 = jnp.pad(shift0, ((0, 0), (0, C_PAD - C_COLS)))
        prev_rows = jnp.concatenate([shift0[:, None, :], cz[:, tm_c - 1:t - 1:tm_c, :]], axis=1)
        prev_rows = prev_rows.reshape(b, t // tm_c, 1, C_PAD)
        row = lambda a: a.reshape(1, C_W)
        vres = None if i == 0 else (row(w["c_v0"][i - 1]), lw["v1"], lw["v2"])
        prep = c_prep(cz, prev_rows,
                      (lw["mu"], row(w["c_w0"][i]), row(w["c_a0"][i]), row(w["c_k_k"][i]), row(w["c_k_a"][i])),
                      (lw["w2"], lw["a2"], lw["g2"]), v_first, vres, tm_c)
        if i == 0:
            v_first = prep[7]
        s0 = jnp.zeros((b, H_C, N_C, N_C), F32) if c is None else c["wkv"][i]
        y_c, wkv_t = c_scan(*[_pad_rows(a, t_c, axis=2) for a in prep[:7]], s0,
                            w["c_r_k"][i], w["c_ln_g"][i], w["c_ln_b"][i], tm_s)
        o_c = y_c[:, :t].reshape(n, C_W)
        merged = mm_gated3(o_a.reshape(n, A_W), o_b.reshape(n, B_W), o_c,
                           lw["w_br_a"], lw["w_br_b"], lw["w_br_c"], z_g)
        h = mm_res(merged, lw["w_out"], h)
        h = ffn(h, w["norm_ffn"][i], lw["w_ffn_in"], lw["w_ffn_out"])
        h = ple(h, w["norm_ple"][i], lw["w_ple_gate"], p[i].reshape(n, -1).astype(BF16), lw["w_ple_proj"])
        vals = (ak_f.reshape(b, t, H_A, HEAD_DIM), av_f.reshape(b, t, H_A, HEAD_DIM),
                seq(bk_f), seq(bv_f), seq(kiw)[:, :, :D_IDX], wkv_t, cz[:, -1, :C_COLS])
        for lst, val in zip(outs, vals):
            lst.append(val)
    return h.reshape(b, t, D_MODEL), [jnp.stack(l) for l in outs]


def kernel(x_prompt, x_sample, cache_a_k, cache_a_v, cache_b_k, cache_b_v, cache_b_kidx, state_c_wkv, state_c_shift, p_prompt, p_sample, norm_mix, w_in, a_q_norm, a_k_norm, b_q_norm, b_k_norm, c_mu, c_w0, c_w2, c_a0, c_a2, c_g2, c_v0, c_v1, c_v2, c_k_k, c_k_a, c_r_k, c_ln_g, c_ln_b, w_br_a, w_br_b, w_br_c, w_out, norm_ffn, w_ffn_in, w_ffn_out, norm_ple, w_ple_gate, w_ple_proj):
    w = dict(norm_mix=norm_mix, w_in=w_in, a_q_norm=a_q_norm, a_k_norm=a_k_norm, b_q_norm=b_q_norm,
             b_k_norm=b_k_norm, c_mu=c_mu, c_w0=c_w0, c_w2=c_w2, c_a0=c_a0, c_a2=c_a2, c_g2=c_g2,
             c_v0=c_v0, c_v1=c_v1, c_v2=c_v2, c_k_k=c_k_k, c_k_a=c_k_a, c_r_k=c_r_k, c_ln_g=c_ln_g,
             c_ln_b=c_ln_b, w_br_a=w_br_a, w_br_b=w_br_b, w_br_c=w_br_c, w_out=w_out, norm_ffn=norm_ffn,
             w_ffn_in=w_ffn_in, w_ffn_out=w_ffn_out, norm_ple=norm_ple, w_ple_gate=w_ple_gate,
             w_ple_proj=w_ple_proj)
    wl = _prep_weights(w)
    y_p, o_p = _run_trunk(x_prompt, p_prompt, 0, None, w, wl)
    caches = dict(a_k=cache_a_k, a_v=cache_a_v, b_k=cache_b_k, b_v=cache_b_v, b_ki=cache_b_kidx,
                  wkv=state_c_wkv, shift=state_c_shift)
    y_s, o_s = _run_trunk(x_sample, p_sample, cache_a_k.shape[2], caches, w, wl)
    return (y_p, y_s, *o_p, *o_s)
```

```python
import functools
import math

import jax
import jax.numpy as jnp
from jax import lax
from jax.experimental import pallas as pl
from jax.experimental.pallas import tpu as pltpu

F32 = jnp.float32
BF16 = jnp.bfloat16

D_MODEL = 2048
DEPTH = 2
HEAD_DIM = 128
H_A = 4
A_W = H_A * HEAD_DIM
H_B = 4
B_W = H_B * HEAD_DIM
H_IDX = 4
D_IDX = 64
TOPK_MAX = 256
CHUNK = 64
H_C = 16
N_C = 64
C_W = H_C * N_C
SCAN_CHUNK = 32
D_DECAY_LORA = 96
D_AAA_LORA = 96
D_GATE_LORA = 256
C_COLS = 3 * C_W + D_DECAY_LORA + D_AAA_LORA + D_GATE_LORA
C_PAD = 3584
C_TAIL = C_PAD - 3 * C_W
ATT_W = 3 * A_W + B_W + 2 * HEAD_DIM + H_IDX * D_IDX + 128
ROPE_THETA = 500000.0
ROT_FRACTION = 4
NORM_EPS = 1e-6
LNX_EPS = 64e-5
NEG_BIG = -1e30

VMEM_LIMIT = 56 * 1024 * 1024
INT_MIN = -(2 ** 31)
KEY_NEG_INF = (0xFF800000 ^ 0x7FFFFFFF) - 2 ** 32


def _params(*sem):
    return pltpu.CompilerParams(dimension_semantics=sem, vmem_limit_bytes=VMEM_LIMIT)


def _row_tile(n, pref):
    t = min(n, pref)
    assert n % t == 0, (n, t)
    return t


def _nt_dot(a, b):
    return lax.dot_general(a, b, (((1,), (1,)), ((), ())), preferred_element_type=F32)


def _rms(x, gain):
    xn = x * lax.rsqrt(jnp.mean(x * x, axis=-1, keepdims=True) + NORM_EPS)
    return xn * gain


def _mm_norm_kernel(x_ref, g_ref, w_ref, o_ref, xn_ref):
    @pl.when(pl.program_id(1) == 0)
    def _():
        xn_ref[...] = _rms(x_ref[...], g_ref[...]).astype(BF16)

    o_ref[...] = jnp.dot(xn_ref[...], w_ref[...], preferred_element_type=F32)


def mm_norm(x, gain, w, tn):
    n, k = x.shape
    nc = w.shape[1]
    tm = _row_tile(n, 1024)
    assert nc % tn == 0
    return pl.pallas_call(
        _mm_norm_kernel,
        grid=(n // tm, nc // tn),
        in_specs=[pl.BlockSpec((tm, k), lambda i, j: (i, 0)),
                  pl.BlockSpec((1, k), lambda i, j: (0, 0)),
                  pl.BlockSpec((k, tn), lambda i, j: (0, j))],
        out_specs=pl.BlockSpec((tm, tn), lambda i, j: (i, j)),
        out_shape=jax.ShapeDtypeStruct((n, nc), F32),
        scratch_shapes=[pltpu.VMEM((tm, k), BF16)],
        compiler_params=_params("parallel", "arbitrary"),
        name="mm_norm",
    )(x, gain.reshape(1, k), w)


def _mm_res_kernel(x_ref, w_ref, h_ref, o_ref):
    o_ref[...] = h_ref[...] + jnp.dot(x_ref[...], w_ref[...], preferred_element_type=F32)


def mm_res(x, w, h):
    n, k = x.shape
    nc = w.shape[1]
    tm = _row_tile(n, 1024)
    tn = 1024
    return pl.pallas_call(
        _mm_res_kernel,
        grid=(n // tm, nc // tn),
        in_specs=[pl.BlockSpec((tm, k), lambda i, j: (i, 0)),
                  pl.BlockSpec((k, tn), lambda i, j: (0, j)),
                  pl.BlockSpec((tm, tn), lambda i, j: (i, j))],
        out_specs=pl.BlockSpec((tm, tn), lambda i, j: (i, j)),
        out_shape=jax.ShapeDtypeStruct((n, nc), F32),
        compiler_params=_params("parallel", "parallel"),
        name="mm_res",
    )(x, w, h)


def _mm_gated3_kernel(oa_ref, ob_ref, oc_ref, wa_ref, wb_ref, wc_ref, ga_ref, gb_ref, gc_ref, o_ref):
    ya = jnp.dot(oa_ref[...], wa_ref[...], preferred_element_type=F32)
    yb = jnp.dot(ob_ref[...], wb_ref[...], preferred_element_type=F32)
    yc = jnp.dot(oc_ref[...], wc_ref[...], preferred_element_type=F32)
    m = (jax.nn.sigmoid(ga_ref[...]) * ya + jax.nn.sigmoid(gb_ref[...]) * yb
         + jax.nn.sigmoid(gc_ref[...]) * yc)
    o_ref[...] = m.astype(o_ref.dtype)


def mm_gated3(oa, ob, oc, wa, wb, wc, gates):
    n = oa.shape[0]
    tm = _row_tile(n, 1024)
    tn = 512
    nj = D_MODEL // tn
    row = lambda w: pl.BlockSpec((tm, w), lambda i, j: (i, 0))
    col = lambda k: pl.BlockSpec((k, tn), lambda i, j: (0, j))
    gate = lambda s: pl.BlockSpec((tm, tn), lambda i, j: (i, j + s * nj))
    return pl.pallas_call(
        _mm_gated3_kernel,
        grid=(n // tm, nj),
        in_specs=[row(A_W), row(B_W), row(C_W), col(A_W), col(B_W), col(C_W), gate(0), gate(1), gate(2)],
        out_specs=pl.BlockSpec((tm, tn), lambda i, j: (i, j)),
        out_shape=jax.ShapeDtypeStruct((n, D_MODEL), BF16),
        compiler_params=_params("parallel", "parallel"),
        name="mm_gated3",
    )(oa, ob, oc, wa, wb, wc, gates, gates, gates)


def _ffn_kernel(h_ref, g_ref, wg_ref, wu_ref, wo_ref, o_ref, xn_ref):
    @pl.when(pl.program_id(1) == 0)
    def _():
        xn_ref[...] = _rms(h_ref[...], g_ref[...]).astype(BF16)
        o_ref[...] = h_ref[...]

    xn = xn_ref[...]
    gate = jnp.dot(xn, wg_ref[...], preferred_element_type=F32)
    up = jnp.dot(xn, wu_ref[...], preferred_element_type=F32)
    act = (jax.nn.silu(gate) * up).astype(BF16)
    o_ref[...] += jnp.dot(act, wo_ref[...], preferred_element_type=F32)


def ffn(h, gain, w_in, w_out):
    n, d = h.shape
    dff = w_out.shape[0]
    tm = _row_tile(n, 512)
    tf = 512
    assert dff % tf == 0
    nf = dff // tf
    return pl.pallas_call(
        _ffn_kernel,
        grid=(n // tm, nf),
        in_specs=[pl.BlockSpec((tm, d), lambda i, f: (i, 0)),
                  pl.BlockSpec((1, d), lambda i, f: (0, 0)),
                  pl.BlockSpec((d, tf), lambda i, f: (0, f)),
                  pl.BlockSpec((d, tf), lambda i, f: (0, f + nf)),
                  pl.BlockSpec((tf, d), lambda i, f: (f, 0))],
        out_specs=pl.BlockSpec((tm, d), lambda i, f: (i, 0)),
        out_shape=jax.ShapeDtypeStruct((n, d), F32),
        scratch_shapes=[pltpu.VMEM((tm, d), BF16)],
        compiler_params=_params("parallel", "arbitrary"),
        name="ffn",
    )(h, gain.reshape(1, d), w_in, w_in, w_out)


def _ple_kernel(h_ref, g_ref, wg_ref, p_ref, wp_ref, hj_ref, o_ref, xn_ref):
    @pl.when(pl.program_id(1) == 0)
    def _():
        xn_ref[...] = _rms(h_ref[...], g_ref[...]).astype(BF16)

    gate = jnp.dot(xn_ref[...], wg_ref[...], preferred_element_type=F32)
    emb = jnp.dot(p_ref[...], wp_ref[...], preferred_element_type=F32)
    o_ref[...] = hj_ref[...] + jax.nn.sigmoid(gate) * emb


def ple(h, gain, w_gate, p, w_proj):
    n, d = h.shape
    pd = p.shape[1]
    tm = _row_tile(n, 1024)
    tn = 1024
    return pl.pallas_call(
        _ple_kernel,
        grid=(n // tm, d // tn),
        in_specs=[pl.BlockSpec((tm, d), lambda i, j: (i, 0)),
                  pl.BlockSpec((1, d), lambda i, j: (0, 0)),
                  pl.BlockSpec((d, tn), lambda i, j: (0, j)),
                  pl.BlockSpec((tm, pd), lambda i, j: (i, 0)),
                  pl.BlockSpec((pd, tn), lambda i, j: (0, j)),
                  pl.BlockSpec((tm, tn), lambda i, j: (i, j))],
        out_specs=pl.BlockSpec((tm, tn), lambda i, j: (i, j)),
        out_shape=jax.ShapeDtypeStruct((n, d), F32),
        scratch_shapes=[pltpu.VMEM((tm, d), BF16)],
        compiler_params=_params("parallel", "arbitrary"),
        name="ple",
    )(h, gain.reshape(1, d), w_gate, p, w_proj, h)


def _rope_tables(pos, width, period, rot, scale_lanes=None):
    half = rot // 2
    inv = ROPE_THETA ** (-jnp.arange(half, dtype=F32) / half)
    lane = jnp.arange(width) % period
    first, second = lane < half, (lane >= half) & (lane < rot)
    ang = pos.astype(F32)[:, None] * inv[lane % half][None, :]
    cos, sin = jnp.cos(ang), jnp.sin(ang)
    c = jnp.where((first | second)[None, :], cos, 1.0)
    s_up = jnp.where(first[None, :], -sin, 0.0)
    s_dn = jnp.where(second[None, :], sin, 0.0)
    if scale_lanes is not None:
        only, mult = scale_lanes
        c = jnp.where(only[None, :], c, mult[None, :])
        s_up = jnp.where(only[None, :], s_up, 0.0)
        s_dn = jnp.where(only[None, :], s_dn, 0.0)
    return jnp.stack([c, s_up, s_dn]).astype(F32)


def _rope(x, tab, half):
    return (x * tab[0] + pltpu.roll(x, x.shape[1] - half, 1) * tab[1] + pltpu.roll(x, half, 1) * tab[2])


def _post_in_kernel(z_ref, t128_ref, t64_ref, tki_ref, gaq_ref, gak_ref, gbq_ref, gbk_ref,
                    aq_ref, akf_ref, akb_ref, avf_ref, avb_ref, bq_ref, bkf_ref, bkb_ref,
                    bvf_ref, bvb_ref, qi_ref, kiw_ref, kilo_ref, kihi_ref):
    hd = HEAD_DIM
    t128 = t128_ref[...]
    t64 = t64_ref[...]
    half128 = hd // ROT_FRACTION // 2
    half64 = D_IDX // ROT_FRACTION // 2
    for h in range(H_A):
        sl = slice(h * hd, (h + 1) * hd)
        aq = _rms(z_ref[:, sl], gaq_ref[...]) * (-(hd ** -0.5))
        aq_ref[:, sl] = aq.astype(BF16)
        ak = _rms(z_ref[:, A_W + h * hd:A_W + (h + 1) * hd], gak_ref[...])
        akf_ref[:, h, :] = ak
        akb_ref[:, sl] = ak.astype(BF16)
        av = z_ref[:, 2 * A_W + h * hd:2 * A_W + (h + 1) * hd]
        avf_ref[:, h, :] = av
        avb_ref[:, sl] = av.astype(BF16)
    o = 3 * A_W
    for h in range(H_B):
        sl = slice(h * hd, (h + 1) * hd)
        bq = _rope(_rms(z_ref[:, o + h * hd:o + (h + 1) * hd], gbq_ref[...]), t128, half128) * (hd ** -0.5)
        bq_ref[:, sl] = bq.astype(BF16)
    o += B_W
    bk = _rope(_rms(z_ref[:, o:o + hd], gbk_ref[...]), t128, half128)
    bkf_ref[...] = bk
    bkb_ref[...] = bk.astype(BF16)
    o += hd
    bv = z_ref[:, o:o + hd]
    bvf_ref[...] = bv
    bvb_ref[...] = bv.astype(BF16)
    o += hd
    for c in range(H_IDX * D_IDX // 128):
        qi = _rope(z_ref[:, o + c * 128:o + (c + 1) * 128], t64, half64) * (D_IDX ** -0.5)
        qi_ref[:, c * 128:(c + 1) * 128] = qi.astype(BF16)
    o += H_IDX * D_IDX
    kiw = _rope(z_ref[:, o:o + 128], tki_ref[...], half64)
    kiw_ref[...] = kiw
    lane = lax.broadcasted_iota(jnp.int32, kiw.shape, 1)
    ki = jnp.where(lane < D_IDX, kiw, 0.0)
    kilo_ref[...] = ki.astype(BF16)
    kihi_ref[...] = pltpu.roll(ki, D_IDX, 1).astype(BF16)


def post_in(z, tabs, gains, t_len):
    n = z.shape[0]
    tm = _row_tile(t_len, 512)
    nt = t_len // tm
    rows = lambda w: pl.BlockSpec((tm, w), lambda i: (i, 0))
    tab = pl.BlockSpec((3, tm, 128), lambda i: (0, i % nt, 0))
    gain = pl.BlockSpec((1, HEAD_DIM), lambda i: (0, 0))
    per_head = pl.BlockSpec((tm, H_A, HEAD_DIM), lambda i: (i, 0, 0))
    widths = [(A_W, BF16), (A_W, F32), (A_W, BF16), (A_W, F32), (A_W, BF16), (B_W, BF16),
              (HEAD_DIM, F32), (HEAD_DIM, BF16), (HEAD_DIM, F32), (HEAD_DIM, BF16),
              (H_IDX * D_IDX, BF16), (128, F32), (128, BF16), (128, BF16)]
    return pl.pallas_call(
        _post_in_kernel,
        grid=(n // tm,),
        in_specs=[rows(ATT_W), tab, tab, tab, gain, gain, gain, gain],
        out_specs=[per_head if i in (1, 3) else rows(w) for i, (w, _) in enumerate(widths)],
        out_shape=[jax.ShapeDtypeStruct((n, H_A, HEAD_DIM) if i in (1, 3) else (n, w), dt)
                   for i, (w, dt) in enumerate(widths)],
        compiler_params=_params("parallel"),
        name="post_in",
    )(z, *tabs, *[g.reshape(1, HEAD_DIM) for g in gains])


LOG2E = 1.4426950408889634
ATTN_A_ROW_CHUNK = 256
ATTN_A_BLOCKS_PER_STEP = 4


def _attn_a_kernel(q_ref, k_ref, v_ref, tri_ref, o_ref, *, tq, tk, rc, past):
    i = pl.program_id(2)
    q0 = past + i * tq
    tri = tri_ref[...]
    n_full = q0 // tk
    n_all = (q0 + tq - 1 + tk - 1) // tk
    nrc = tq // rc
    qs = [q_ref[c * rc:(c + 1) * rc, :] for c in range(nrc)]
    rows = [q0 + c * rc + lax.broadcasted_iota(jnp.int32, (rc, tk), 0) for c in range(nrc)]
    col = lax.broadcasted_iota(jnp.int32, (rc, tk), 1)

    def step(j, carry, masked, nsub):
        cs = range(nrc)
        ss = range(nsub)
        offs = [pl.multiple_of((j - s) * tk, tk) for s in ss]
        kb = [k_ref[pl.ds(o, tk), :] for o in offs]
        vb = [v_ref[pl.ds(o, tk), :] for o in offs]
        later = [carry[0][c * rc:(c + 1) * rc] for c in cs]
        acc = [carry[1][c * rc:(c + 1) * rc] for c in cs]
        nz = [[_nt_dot(qs[c], kb[s]) for c in cs] for s in ss]
        loss = [[jnp.log(1.0 + jnp.exp2(jnp.minimum(nz[s][c] * (-LOG2E), 126.0))) for c in cs] for s in ss]
        if masked:
            causal = [[(offs[s] + col) < rows[c] for c in cs] for s in ss]
            loss = [[jnp.where(causal[s][c], loss[s][c], 0.0) for c in cs] for s in ss]
        within = [[jnp.dot(loss[s][c].astype(BF16), tri, preferred_element_type=F32) for c in cs] for s in ss]
        for s in ss:
            w = [jnp.exp2((nz[s][c] + later[c] + within[s][c]) * (-LOG2E)) for c in cs]
            if masked:
                w = [jnp.where(causal[s][c], w[c], 0.0) for c in cs]
            acc = [acc[c] + jnp.dot(w[c].astype(BF16), vb[s], preferred_element_type=F32) for c in cs]
            later = [later[c] + within[s][c][:, 0:1] for c in cs]
        return jnp.concatenate(later, axis=0), jnp.concatenate(acc, axis=0)

    carry = (jnp.zeros((tq, 1), F32), jnp.zeros((tq, HEAD_DIM), F32))
    carry = lax.fori_loop(0, n_all - n_full, lambda t, c: step(n_all - 1 - t, c, True, 1), carry)
    n_multi = n_full // ATTN_A_BLOCKS_PER_STEP
    n_single = n_full - n_multi * ATTN_A_BLOCKS_PER_STEP
    carry = lax.fori_loop(0, n_single, lambda t, c: step(n_full - 1 - t, c, False, 1), carry)
    carry = lax.fori_loop(
        0, n_multi,
        lambda t, c: step(n_full - n_single - 1 - t * ATTN_A_BLOCKS_PER_STEP, c, False, ATTN_A_BLOCKS_PER_STEP),
        carry)
    o_ref[...] = carry[1].astype(o_ref.dtype)


def attn_a(q, k, v, past, tk):
    b, t, _ = q.shape
    lp = k.shape[1]
    tq = _row_tile(t, 512)
    rc = min(tq, ATTN_A_ROW_CHUNK)
    tri = jnp.tri(tk, dtype=BF16)
    return pl.pallas_call(
        functools.partial(_attn_a_kernel, tq=tq, tk=tk, rc=rc, past=past),
        grid=(b, H_A, t // tq),
        in_specs=[pl.BlockSpec((None, tq, HEAD_DIM), lambda bi, h, i: (bi, i, h)),
                  pl.BlockSpec((None, lp, HEAD_DIM), lambda bi, h, i: (bi, 0, h)),
                  pl.BlockSpec((None, lp, HEAD_DIM), lambda bi, h, i: (bi, 0, h)),
                  pl.BlockSpec((tk, tk), lambda bi, h, i: (0, 0))],
        out_specs=pl.BlockSpec((None, tq, HEAD_DIM), lambda bi, h, i: (bi, i, h)),
        out_shape=jax.ShapeDtypeStruct((b, t, A_W), BF16),
        compiler_params=_params("parallel", "parallel", "parallel"),
        name="attn_a",
    )(q, k, v, tri)


DSA_QUERY_ROWS = 256
DSA_ATTEND_ROWS = 128


def _dsa_kernel(qi_ref, kiw_ref, kilo_ref, kihi_ref, q_ref, k_ref, v_ref, triu_ref, o_ref, key_ref,
                *, tq, tk, past, n_keys, topk):
    i = pl.program_id(1)
    q0 = past + i * tq
    pos = q0 + lax.broadcasted_iota(jnp.int32, (tq, 1), 0)
    lim = jnp.minimum((pos // CHUNK + 1) * CHUNK, n_keys)
    n_adm = jnp.minimum(((q0 + tq - 1) // CHUNK + 1) * CHUNK, n_keys)
    nt = (n_adm + tk - 1) // tk
    col = lax.broadcasted_iota(jnp.int32, (tq, tk), 1)
    ngrp = tk // 128

    qi = qi_ref[...]
    kiw = kiw_ref[...]
    wi = [kiw[:, D_IDX + h:D_IDX + h + 1] for h in range(H_IDX)]

    n_pair = (nt + 1) // 2
    pair = range(2)

    def score_pair(t, _):
        offs = [pl.multiple_of((2 * t + u) * tk, tk) for u in pair]
        kis = [(kilo_ref[pl.ds(o, tk), :], kihi_ref[pl.ds(o, tk), :]) for o in offs]
        s_idx = [[_nt_dot(qi[:, (h // 2) * 128:(h // 2 + 1) * 128], kis[u][h % 2]) for h in range(H_IDX)]
                 for u in pair]
        for u in pair:
            s = jnp.zeros((tq, tk), F32)
            for h in range(H_IDX):
                s = s + wi[h] * jnp.maximum(s_idx[u][h], 0.0)
            s = jnp.where(s == 0.0, 0.0, s)
            s = jnp.where(offs[u] + col < lim, s, -jnp.inf)
            bits = lax.bitcast_convert_type(s, jnp.int32)
            key_ref[:, pl.ds(offs[u], tk)] = bits ^ ((bits >> 31) & 0x7FFFFFFF)
        return 0

    lax.fori_loop(0, n_pair, score_pair, 0)

    cr = min(tq, DSA_ATTEND_ROWS)

    def count_ge(cand):
        accs = []
        for r in range(tq // cr):
            cand_b = jnp.broadcast_to(cand[r * cr:(r + 1) * cr], (cr, 128))

            def body(t, acc):
                off = pl.multiple_of(t * tk, tk)
                kt = key_ref[r * cr:(r + 1) * cr, pl.ds(off, tk)]
                for g in range(ngrp):
                    acc = acc + jnp.where(kt[:, g * 128:(g + 1) * 128] >= cand_b, 1, 0)
                return acc

            accs.append(lax.fori_loop(0, nt, body, jnp.zeros((cr, 128), jnp.int32)))
        return jnp.sum(jnp.concatenate(accs, axis=0), axis=1, keepdims=True)

    def bit_step(b, t_u):
        cand_u = t_u | lax.shift_left(jnp.int32(1), 31 - b)
        ok = count_ge(cand_u ^ INT_MIN) >= topk
        return jnp.where(ok, cand_u, t_u)

    t_u = lax.fori_loop(0, 32, bit_step, jnp.zeros((tq, 1), jnp.int32))
    tau = t_u ^ INT_MIN
    tau_next = jnp.where(tau == 2 ** 31 - 1, tau, tau + 1)
    cnt_gt = jnp.where(tau == 2 ** 31 - 1, 0, count_ge(tau_next))
    need = jnp.where(tau == KEY_NEG_INF, 0, topk - cnt_gt).astype(F32)

    triu = triu_ref[...]

    def select_tile(t, seen):
        off = pl.multiple_of(t * tk, tk)
        kt = key_ref[:, pl.ds(off, tk)]
        eq = kt == tau
        prefix = seen + jnp.dot(jnp.where(eq, 1.0, 0.0).astype(BF16), triu, preferred_element_type=F32)
        rank = jnp.where(eq, prefix, jnp.inf)
        bias = jnp.where(kt > tau, 0.0, jnp.where(rank <= need, 0.0, NEG_BIG)).astype(F32)
        key_ref[:, pl.ds(off, tk)] = lax.bitcast_convert_type(bias, jnp.int32)
        return prefix[:, tk - 1:tk]

    lax.fori_loop(0, 2 * n_pair, select_tile, jnp.zeros((tq, 1), F32))

    rg = min(tq, DSA_ATTEND_ROWS)
    groups = range(tq // rg)
    gr = H_B * rg
    qs = [jnp.concatenate([q_ref[r * rg:(r + 1) * rg, h * HEAD_DIM:(h + 1) * HEAD_DIM] for h in range(H_B)], axis=0)
          for r in groups]

    def attend_pair(t, carry):
        offs = [pl.multiple_of((2 * t + u) * tk, tk) for u in pair]
        kb = [k_ref[pl.ds(o, tk), :] for o in offs]
        vb = [v_ref[pl.ds(o, tk), :] for o in offs]
        new = []
        for r in groups:
            m, l, acc = [c[r * gr:(r + 1) * gr] for c in carry]
            bias = [lax.bitcast_convert_type(key_ref[r * rg:(r + 1) * rg, pl.ds(o, tk)], F32) for o in offs]
            logits = [(_nt_dot(qs[r], kb[u]).reshape(H_B, rg, tk) + bias[u][None]).reshape(gr, tk) for u in pair]
            m_new = jnp.maximum(m, jnp.maximum(jnp.max(logits[0], axis=1, keepdims=True),
                                               jnp.max(logits[1], axis=1, keepdims=True)))
            alpha = jnp.exp(m - m_new)
            p = [jnp.exp(logits[u] - m_new) for u in pair]
            l = alpha * l + (jnp.sum(p[0], axis=1, keepdims=True) + jnp.sum(p[1], axis=1, keepdims=True))
            acc = alpha * acc + (jnp.dot(p[0].astype(BF16), vb[0], preferred_element_type=F32)
                                 + jnp.dot(p[1].astype(BF16), vb[1], preferred_element_type=F32))
            new.append((m_new, l, acc))
        return tuple(jnp.concatenate([n[k] for n in new], axis=0) for k in range(3))

    init = (jnp.full((H_B * tq, 1), NEG_BIG, F32), jnp.zeros((H_B * tq, 1), F32),
            jnp.zeros((H_B * tq, HEAD_DIM), F32))
    _, l, acc = lax.fori_loop(0, n_pair, attend_pair, init)
    out = acc / l
    for r in groups:
        for h in range(H_B):
            rows = out[r * gr + h * rg:r * gr + (h + 1) * rg]
            o_ref[r * rg:(r + 1) * rg, h * HEAD_DIM:(h + 1) * HEAD_DIM] = rows.astype(o_ref.dtype)


def dsa(qi, kiw, kilo, kihi, q, k, v, past, n_keys, tk):
    b, t, _ = q.shape
    lp = k.shape[1]
    tq = _row_tile(t, DSA_QUERY_ROWS)
    topk = min(TOPK_MAX, n_keys // 4)
    assert lp % (2 * tk) == 0 and lp >= topk
    triu = jnp.tri(tk, dtype=BF16).T
    qblk = lambda w: pl.BlockSpec((None, tq, w), lambda bi, i: (bi, i, 0))
    kblk = lambda w: pl.BlockSpec((None, lp, w), lambda bi, i: (bi, 0, 0))
    return pl.pallas_call(
        functools.partial(_dsa_kernel, tq=tq, tk=tk, past=past, n_keys=n_keys, topk=topk),
        grid=(b, t // tq),
        in_specs=[qblk(H_IDX * D_IDX), qblk(128), kblk(128), kblk(128), qblk(B_W),
                  kblk(HEAD_DIM), kblk(HEAD_DIM), pl.BlockSpec((tk, tk), lambda bi, i: (0, 0))],
        out_specs=qblk(B_W),
        out_shape=jax.ShapeDtypeStruct((b, t, B_W), BF16),
        scratch_shapes=[pltpu.VMEM((tq, lp), jnp.int32)],
        compiler_params=_params("parallel", "parallel"),
        name="dsa",
    )(qi, kiw, kilo, kihi, q, k, v, triu)


def _softplus(x):
    return jnp.maximum(x, 0.0) + jnp.log(1.0 + jnp.exp(-jnp.abs(x)))


PREV_TILE = 8


def _c_prep_kernel(*refs, first_layer):
    if first_layer:
        (cz_ref, before_ref, shift_ref, mu_ref, w0_ref, a0_ref, w2_ref, a2_ref, g2_ref, kk_ref, ka_ref,
         r_ref, kn_ref, k_ref, v_ref, lw_ref, a_ref, g_ref, vtok_ref) = refs
    else:
        (cz_ref, before_ref, shift_ref, mu_ref, w0_ref, a0_ref, w2_ref, a2_ref, g2_ref, kk_ref, ka_ref,
         vf_ref, v0_ref, v1_ref, v2_ref,
         r_ref, kn_ref, k_ref, v_ref, lw_ref, a_ref, g_ref) = refs

    def put_heads(ref, x):
        for pair in range(H_C // 2):
            tile = x[:, pair * 128:(pair + 1) * 128]
            ref[2 * pair] = tile[:, :N_C]
            ref[2 * pair + 1] = pltpu.roll(tile, N_C, 1)[:, :N_C]

    cz = cz_ref[...]
    rowid = lax.broadcasted_iota(jnp.int32, cz.shape, 0)
    before = jnp.where(pl.program_id(1) == 0, shift_ref[...], before_ref[PREV_TILE - 1:PREV_TILE, :])
    prev = jnp.where(rowid == 0, before, pltpu.roll(cz, 1, 0))
    zz = cz + (prev - cz) * mu_ref[...]
    r = zz[:, 0:C_W]
    k = zz[:, C_W:2 * C_W]
    v = zz[:, 2 * C_W:3 * C_W]
    tail = zz[:, 3 * C_W:C_PAD]
    w = w0_ref[...] + jnp.dot(jnp.tanh(tail).astype(BF16), w2_ref[...], preferred_element_type=F32)
    w = -_softplus(-w) - 0.5
    a = jax.nn.sigmoid(a0_ref[...] + jnp.dot(tail.astype(BF16), a2_ref[...], preferred_element_type=F32))
    g = jnp.dot(jax.nn.sigmoid(tail).astype(BF16), g2_ref[...], preferred_element_type=F32)
    if not first_layer:
        lora = jnp.dot(v.astype(BF16), v1_ref[...], preferred_element_type=F32)
        lora = jnp.dot(lora.astype(BF16), v2_ref[...], preferred_element_type=F32)
        v = v + (vf_ref[...] - v) * jax.nn.sigmoid(v0_ref[...] + lora)
    if first_layer:
        vtok_ref[...] = v
    put_heads(r_ref, r)
    put_heads(kn_ref, k * kk_ref[...])
    put_heads(k_ref, k * (1.0 + (a - 1.0) * ka_ref[...]))
    put_heads(v_ref, v)
    put_heads(lw_ref, -jnp.exp(w))
    put_heads(a_ref, a)
    put_heads(g_ref, g)


def c_prep(cz, shift0, vecs, mats, v_first, vres, tm):
    b, t, _ = cz.shape
    first_layer = vres is None
    blk = lambda w: pl.BlockSpec((None, tm, w), lambda bi, i: (bi, i, 0))
    vec = lambda w: pl.BlockSpec((1, w), lambda bi, i: (0, 0))
    mat = lambda m: pl.BlockSpec(m.shape, lambda bi, i: (0, 0))
    mu, w0, a0, k_k, k_a = vecs
    w2, a2, g2 = mats
    assert tm % PREV_TILE == 0
    before = pl.BlockSpec((None, PREV_TILE, C_PAD), lambda bi, i: (bi, jnp.maximum(i * (tm // PREV_TILE) - 1, 0), 0))
    ins = [cz, cz, shift0, mu, w0, a0, w2, a2, g2, k_k, k_a]
    specs = [blk(C_PAD), before, pl.BlockSpec((None, 1, C_PAD), lambda bi, i: (bi, 0, 0)),
             vec(C_PAD), vec(C_W), vec(C_W), mat(w2), mat(a2), mat(g2), vec(C_W), vec(C_W)]
    if not first_layer:
        v0, v1, v2 = vres
        ins += [v_first, v0, v1, v2]
        specs += [blk(C_W), vec(C_W), mat(v1), mat(v2)]
    heads = pl.BlockSpec((None, H_C, tm, N_C), lambda bi, i: (bi, 0, i, 0))
    out_specs = [heads] * 7
    out_shape = [jax.ShapeDtypeStruct((b, H_C, t, N_C), F32)] * 7
    if first_layer:
        out_specs.append(blk(C_W))
        out_shape.append(jax.ShapeDtypeStruct((b, t, C_W), F32))
    return pl.pallas_call(
        functools.partial(_c_prep_kernel, first_layer=first_layer),
        grid=(b, t // tm),
        in_specs=specs,
        out_specs=out_specs,
        out_shape=out_shape,
        compiler_params=_params("parallel", "parallel"),
        name="c_prep",
    )(*ins)


def _bdot(a, b, contract_a, contract_b):
    return lax.dot_general(a.astype(BF16), b.astype(BF16), (((contract_a,), (contract_b,)), ((0,), (0,))),
                           preferred_element_type=F32)


def _bdot3(a, b, contract_a, contract_b):
    ah = a.astype(BF16)
    bh = b.astype(BF16)
    al = (a - ah.astype(F32)).astype(BF16)
    bl = (b - bh.astype(F32)).astype(BF16)
    dn = (((contract_a,), (contract_b,)), ((0,), (0,)))
    dot = lambda x, y: lax.dot_general(x, y, dn, preferred_element_type=F32)
    return dot(ah, bh) + (dot(al, bh) + dot(ah, bl))


C_SCAN_CHUNKS_PER_STEP = 4


def _c_scan_kernel(r_ref, kn_ref, k_ref, v_ref, lw_ref, a_ref, g_ref, s0_ref, rk_ref, lng_ref, lnb_ref,
                   y_ref, st_ref, s_ref, *, nck):
    c_len = SCAN_CHUNK

    @pl.when(pl.program_id(1) == 0)
    def _():
        s_ref[...] = s0_ref[...]

    def step(c0, nc):
        g_n = H_C * nc
        rows = pl.ds(pl.multiple_of(c0 * c_len, c_len), nc * c_len)
        load = lambda ref: ref[:, rows, :].reshape(g_n, c_len, N_C)
        per_head = lambda x: jnp.broadcast_to(x[:, None], (H_C, nc) + x.shape[1:]).reshape((g_n,) + x.shape[1:])
        of_chunk = lambda x, j: x.reshape((H_C, nc) + x.shape[1:])[:, j]
        ti = lax.broadcasted_iota(jnp.int32, (g_n, c_len, c_len), 1)
        si = lax.broadcasted_iota(jnp.int32, (g_n, c_len, c_len), 2)
        incl = si <= ti
        strict = si < ti
        tri = jnp.where(incl, 1.0, 0.0)
        eye = jnp.where(si == ti, 1.0, 0.0)
        r, k, v, lw, a, kn = (load(x) for x in (r_ref, k_ref, v_ref, lw_ref, a_ref, kn_ref))
        kn = kn * lax.rsqrt(jnp.maximum(jnp.sum(kn * kn, axis=-1, keepdims=True), 1e-24))
        aa = -kn
        bb = kn * a
        g = _bdot3(tri, lw, 2, 1)
        g_end = g[:, c_len - 1:c_len, :]
        a_t = aa * jnp.exp(g - lw)
        r_t = r * jnp.exp(g)
        inv = jnp.exp(-g)
        b_h = bb * inv
        k_h = k * inv
        to_end = jnp.exp(g_end - g)
        b_e = bb * to_end
        k_e = k * to_end
        d_end = jnp.exp(g_end)
        a_ab = jnp.where(strict, _bdot(a_t, b_h, 2, 2), 0.0)
        a_ak = jnp.where(strict, _bdot(a_t, k_h, 2, 2), 0.0)
        a_rb = jnp.where(incl, _bdot(r_t, b_h, 2, 2), 0.0)
        a_rk = jnp.where(incl, _bdot(r_t, k_h, 2, 2), 0.0)
        t_inv = eye + a_ab
        pw = a_ab
        for _ in range(4):
            pw = _bdot(pw, pw, 2, 1)
            t_inv = t_inv + _bdot(t_inv, pw, 2, 1)
        p_mat = _bdot(t_inv, a_t, 2, 1)
        q_mat = _bdot(t_inv, _bdot(a_ak, v, 2, 1), 2, 1)
        y_v = _bdot(a_rk, v, 2, 1)
        s = s_ref[...]
        ys = []
        for j in range(nc):
            u = _bdot(of_chunk(p_mat, j), s, 2, 2) + of_chunk(q_mat, j)
            ys.append(_bdot(of_chunk(r_t, j), s, 2, 2) + _bdot(of_chunk(a_rb, j), u, 2, 1) + of_chunk(y_v, j))
            uv = jnp.concatenate([u, of_chunk(v, j)], axis=1)
            bk = jnp.concatenate([of_chunk(b_e, j), of_chunk(k_e, j)], axis=1)
            s = s * of_chunk(d_end, j) + _bdot(uv, bk, 1, 1)
        s_ref[...] = s
        y = jnp.stack(ys, axis=1).reshape(g_n, c_len, N_C)
        mu = jnp.mean(y, axis=-1, keepdims=True)
        yc = y - mu
        yn = yc * lax.rsqrt(jnp.mean(yc * yc, axis=-1, keepdims=True) + LNX_EPS)
        yn = yn * per_head(lng_ref[...]) + per_head(lnb_ref[...])
        bonus = jnp.sum(r * k * per_head(rk_ref[...]), axis=-1, keepdims=True) * v
        o = ((yn + bonus) * load(g_ref)).reshape(H_C, nc * c_len, N_C)
        y_ref[rows, :] = jnp.concatenate([o[h] for h in range(H_C)], axis=-1).astype(y_ref.dtype)

    n_multi = nck // C_SCAN_CHUNKS_PER_STEP
    if n_multi:
        lax.fori_loop(0, n_multi, lambda t, _: step(t * C_SCAN_CHUNKS_PER_STEP, C_SCAN_CHUNKS_PER_STEP), None)
    for c in range(n_multi * C_SCAN_CHUNKS_PER_STEP, nck):
        step(c, 1)

    @pl.when(pl.program_id(1) == pl.num_programs(1) - 1)
    def _():
        st_ref[...] = s_ref[...]


def c_scan(r, kn, k, v, lw, a, g, s0, r_k, ln_g, ln_b, tm):
    b, _, t, _ = r.shape
    blk = pl.BlockSpec((None, H_C, tm, N_C), lambda bi, i: (bi, 0, i, 0))
    st = pl.BlockSpec((None, H_C, N_C, N_C), lambda bi, i: (bi, 0, 0, 0))
    vec = pl.BlockSpec((H_C, 1, N_C), lambda bi, i: (0, 0, 0))
    return pl.pallas_call(
        functools.partial(_c_scan_kernel, nck=tm // SCAN_CHUNK),
        grid=(b, t // tm),
        in_specs=[blk] * 7 + [st, vec, vec, vec],
        out_specs=[pl.BlockSpec((None, tm, C_W), lambda bi, i: (bi, i, 0)), st],
        out_shape=[jax.ShapeDtypeStruct((b, t, C_W), BF16),
                   jax.ShapeDtypeStruct((b, H_C, N_C, N_C), F32)],
        scratch_shapes=[pltpu.VMEM((H_C, N_C, N_C), F32)],
        compiler_params=_params("parallel", "arbitrary"),
        name="c_scan",
    )(r, kn, k, v, lw, a, g, s0, r_k.reshape(H_C, 1, N_C), ln_g.reshape(H_C, 1, N_C), ln_b.reshape(H_C, 1, N_C))


def _pad_rows(x, rows, axis=1):
    pad = rows - x.shape[axis]
    if pad == 0:
        return x
    widths = [(0, 0)] * x.ndim
    widths[axis] = (0, pad)
    return jnp.pad(x, widths)


ATT_COLS = 3 * A_W + B_W + 2 * HEAD_DIM + H_IDX * D_IDX + D_IDX + H_IDX
IN_COLS = ATT_COLS + C_COLS + 3 * D_MODEL


def _regroup_kernel(w_ref, tail_ref, att_ref, c_ref, g_ref):
    lane = lax.broadcasted_iota(jnp.int32, (w_ref.shape[0], 128), 1)
    n_full = IN_COLS // 128

    def tile(j):
        return tail_ref[...] if j == n_full else w_ref[:, j * 128:(j + 1) * 128]

    def shifted(j0, shift, j):
        joined = jnp.where(lane >= shift, tile(j0 + j), tile(j0 + j + 1))
        return pltpu.roll(joined, 128 - shift, 1)

    for j in range(ATT_W // 128):
        x = tile(j)
        if (j + 1) * 128 > ATT_COLS:
            x = jnp.where(lane < ATT_COLS - j * 128, x, 0.0)
        att_ref[:, j * 128:(j + 1) * 128] = x.astype(BF16)
    for j in range(C_PAD // 128):
        x = shifted(ATT_COLS // 128, ATT_COLS % 128, j)
        if (j + 1) * 128 > C_COLS:
            x = jnp.where(lane < C_COLS - j * 128, x, 0.0)
        c_ref[:, j * 128:(j + 1) * 128] = x.astype(BF16)
    g0 = ATT_COLS + C_COLS
    for j in range(3 * D_MODEL // 128):
        g_ref[:, j * 128:(j + 1) * 128] = shifted(g0 // 128, g0 % 128, j).astype(BF16)


def regroup_w_in(w_in):
    depth, d, cols = w_in.shape
    assert cols == IN_COLS and (ATT_COLS + C_COLS) % 128 != 0
    n_full = cols // 128
    tail = jnp.pad(w_in[:, :, n_full * 128:], ((0, 0), (0, 0), (0, (n_full + 1) * 128 - cols)))
    tm = 128
    blk = lambda w: pl.BlockSpec((None, tm, w), lambda l, i: (l, i, 0))
    return pl.pallas_call(
        _regroup_kernel,
        grid=(depth, d // tm),
        in_specs=[blk(cols), blk(128)],
        out_specs=[blk(ATT_W), blk(C_PAD), blk(3 * D_MODEL)],
        out_shape=[jax.ShapeDtypeStruct((depth, d, w), BF16) for w in (ATT_W, C_PAD, 3 * D_MODEL)],
        compiler_params=_params("parallel", "parallel"),
        name="regroup_w_in",
    )(w_in, tail)


def _prep_weights(w):
    out = []
    w_att, w_c, w_g = regroup_w_in(w["w_in"])
    for i in range(DEPTH):
        tail_rows = lambda m, off: jnp.pad(m, ((off, C_TAIL - off - m.shape[0]), (0, 0))).astype(BF16)
        lw = dict(
            w_att=w_att[i], w_c=w_c[i], w_g=w_g[i],
            w2=tail_rows(w["c_w2"][i], 0),
            a2=tail_rows(w["c_a2"][i], D_DECAY_LORA),
            g2=tail_rows(w["c_g2"][i], D_DECAY_LORA + D_AAA_LORA),
            mu=jnp.pad(w["c_mu"][i], (0, C_PAD - C_COLS)).reshape(1, C_PAD),
            w_br_a=w["w_br_a"][i].astype(BF16), w_br_b=w["w_br_b"][i].astype(BF16),
            w_br_c=w["w_br_c"][i].astype(BF16), w_out=w["w_out"][i].astype(BF16),
            w_ffn_in=w["w_ffn_in"][i].astype(BF16), w_ffn_out=w["w_ffn_out"][i].astype(BF16),
            w_ple_gate=w["w_ple_gate"][i].astype(BF16), w_ple_proj=w["w_ple_proj"][i].astype(BF16),
        )
        if i > 0:
            lw["v1"] = w["c_v1"][i - 1].astype(BF16)
            lw["v2"] = w["c_v2"][i - 1].astype(BF16)
        out.append(lw)
    return out


def _run_trunk(x, p, past, caches, w, wl):
    b, t, _ = x.shape
    n = b * t
    n_keys = past + t
    pos = past + jnp.arange(t)
    lane = jnp.arange(128)
    tabs = (_rope_tables(pos, 128, HEAD_DIM, HEAD_DIM // ROT_FRACTION),
            _rope_tables(pos, 128, D_IDX, D_IDX // ROT_FRACTION),
            _rope_tables(pos, 128, 128, D_IDX // ROT_FRACTION,
                         scale_lanes=(lane < D_IDX, jnp.where(lane < D_IDX + H_IDX, H_IDX ** -0.5, 1.0))))
    tk_a = 256
    tk_b = 512
    lp_a = -(-n_keys // tk_a) * tk_a
    lp_b = -(-n_keys // (2 * tk_b)) * (2 * tk_b)
    t_c = -(-t // SCAN_CHUNK) * SCAN_CHUNK
    tm_c = _row_tile(t, 256)
    tm_s = _row_tile(t_c, 256)
    h = x.reshape(n, D_MODEL)
    outs = ([], [], [], [], [], [], [])
    v_first = None
    for i in range(DEPTH):
        lw = wl[i]
        z_att = mm_norm(h, w["norm_mix"][i], lw["w_att"], ATT_W // 3)
        z_c = mm_norm(h, w["norm_mix"][i], lw["w_c"], C_PAD // 2)
        z_g = mm_norm(h, w["norm_mix"][i], lw["w_g"], 1536)
        (aq, ak_f, ak_b, av_f, av_b, bq, bk_f, bk_b, bv_f, bv_b, qi, kiw, kilo, kihi) = post_in(
            z_att, tabs, (w["a_q_norm"][i], w["a_k_norm"][i], w["b_q_norm"][i], w["b_k_norm"][i]), t)
        seq = lambda a: a.reshape(b, t, a.shape[-1])

        def keys(new, old, lp):
            new = seq(new)
            if old is not None:
                new = jnp.concatenate([old.astype(BF16), new], axis=1)
            return _pad_rows(new, lp)

        c = caches
        past_of = lambda name: None if c is None else c[name][i]
        pa_k = None if c is None else c["a_k"][i].reshape(b, past, A_W)
        pa_v = None if c is None else c["a_v"][i].reshape(b, past, A_W)
        o_a = attn_a(seq(aq), keys(ak_b, pa_k, lp_a), keys(av_b, pa_v, lp_a), past, tk_a)
        if c is None:
            p_lo = p_hi = None
        else:
            p_lo = jnp.pad(c["b_ki"][i], ((0, 0), (0, 0), (0, 128 - D_IDX)))
            p_hi = jnp.pad(c["b_ki"][i], ((0, 0), (0, 0), (128 - D_IDX, 0)))
        o_b = dsa(seq(qi), seq(kiw), keys(kilo, p_lo, lp_b), keys(kihi, p_hi, lp_b), seq(bq),
                  keys(bk_b, past_of("b_k"), lp_b), keys(bv_b, past_of("b_v"), lp_b), past, n_keys, tk_b)
        cz = seq(z_c)
        shift0 = jnp.zeros((b, C_COLS), F32) if c is None else c["shift"][i]
        shift0 = jnp.pad(shift0, ((0, 0), (0, C_PAD - C_COLS)))
        row = lambda a: a.reshape(1, C_W)
        vres = None if i == 0 else (row(w["c_v0"][i - 1]), lw["v1"], lw["v2"])
        prep = c_prep(cz, shift0[:, None, :],
                      (lw["mu"], row(w["c_w0"][i]), row(w["c_a0"][i]), row(w["c_k_k"][i]), row(w["c_k_a"][i])),
                      (lw["w2"], lw["a2"], lw["g2"]), v_first, vres, tm_c)
        if i == 0:
            v_first = prep[7]
        s0 = jnp.zeros((b, H_C, N_C, N_C), F32) if c is None else c["wkv"][i]
        y_c, wkv_t = c_scan(*[_pad_rows(a, t_c, axis=2) for a in prep[:7]], s0,
                            w["c_r_k"][i], w["c_ln_g"][i], w["c_ln_b"][i], tm_s)
        o_c = y_c[:, :t].reshape(n, C_W)
        merged = mm_gated3(o_a.reshape(n, A_W), o_b.reshape(n, B_W), o_c,
                           lw["w_br_a"], lw["w_br_b"], lw["w_br_c"], z_g)
        h = mm_res(merged, lw["w_out"], h)
        h = ffn(h, w["norm_ffn"][i], lw["w_ffn_in"], lw["w_ffn_out"])
        h = ple(h, w["norm_ple"][i], lw["w_ple_gate"], p[i].reshape(n, -1).astype(BF16), lw["w_ple_proj"])
        vals = (ak_f.reshape(b, t, H_A, HEAD_DIM), av_f.reshape(b, t, H_A, HEAD_DIM),
                seq(bk_f), seq(bv_f), seq(kiw)[:, :, :D_IDX], wkv_t, cz[:, -1, :C_COLS])
        for lst, val in zip(outs, vals):
            lst.append(val)
    return h.reshape(b, t, D_MODEL), [jnp.stack(l) for l in outs]


def kernel(x_prompt, x_sample, cache_a_k, cache_a_v, cache_b_k, cache_b_v, cache_b_kidx, state_c_wkv, state_c_shift, p_prompt, p_sample, norm_mix, w_in, a_q_norm, a_k_norm, b_q_norm, b_k_norm, c_mu, c_w0, c_w2, c_a0, c_a2, c_g2, c_v0, c_v1, c_v2, c_k_k, c_k_a, c_r_k, c_ln_g, c_ln_b, w_br_a, w_br_b, w_br_c, w_out, norm_ffn, w_ffn_in, w_ffn_out, norm_ple, w_ple_gate, w_ple_proj):
    w = dict(norm_mix=norm_mix, w_in=w_in, a_q_norm=a_q_norm, a_k_norm=a_k_norm, b_q_norm=b_q_norm,
             b_k_norm=b_k_norm, c_mu=c_mu, c_w0=c_w0, c_w2=c_w2, c_a0=c_a0, c_a2=c_a2, c_g2=c_g2,
             c_v0=c_v0, c_v1=c_v1, c_v2=c_v2, c_k_k=c_k_k, c_k_a=c_k_a, c_r_k=c_r_k, c_ln_g=c_ln_g,
             c_ln_b=c_ln_b, w_br_a=w_br_a, w_br_b=w_br_b, w_br_c=w_br_c, w_out=w_out, norm_ffn=norm_ffn,
             w_ffn_in=w_ffn_in, w_ffn_out=w_ffn_out, norm_ple=norm_ple, w_ple_gate=w_ple_gate,
             w_ple_proj=w_ple_proj)
    wl = _prep_weights(w)
    y_p, o_p = _run_trunk(x_prompt, p_prompt, 0, None, w, wl)
    caches = dict(a_k=cache_a_k, a_v=cache_a_v, b_k=cache_b_k, b_v=cache_b_v, b_ki=cache_b_kidx,
                  wkv=state_c_wkv, shift=state_c_shift)
    y_s, o_s = _run_trunk(x_sample, p_sample, cache_a_k.shape[2], caches, w, wl)
    return (y_p, y_s, *o_p, *o_s)
```

```python
import functools
import math

import jax
import jax.numpy as jnp
from jax import lax
from jax.experimental import pallas as pl
from jax.experimental.pallas import tpu as pltpu

F32 = jnp.float32
BF16 = jnp.bfloat16

D_MODEL = 2048
DEPTH = 2
HEAD_DIM = 128
H_A = 4
A_W = H_A * HEAD_DIM
H_B = 4
B_W = H_B * HEAD_DIM
H_IDX = 4
D_IDX = 64
TOPK_MAX = 256
CHUNK = 64
H_C = 16
N_C = 64
C_W = H_C * N_C
SCAN_CHUNK = 32
D_DECAY_LORA = 96
D_AAA_LORA = 96
D_GATE_LORA = 256
C_COLS = 3 * C_W + D_DECAY_LORA + D_AAA_LORA + D_GATE_LORA
C_PAD = 3584
C_TAIL = C_PAD - 3 * C_W
ATT_W = 3 * A_W + B_W + 2 * HEAD_DIM + H_IDX * D_IDX + 128
ROPE_THETA = 500000.0
ROT_FRACTION = 4
NORM_EPS = 1e-6
LNX_EPS = 64e-5
NEG_BIG = -1e30

VMEM_LIMIT = 56 * 1024 * 1024
INT_MIN = -(2 ** 31)
KEY_NEG_INF = (0xFF800000 ^ 0x7FFFFFFF) - 2 ** 32


def _params(*sem):
    return pltpu.CompilerParams(dimension_semantics=sem, vmem_limit_bytes=VMEM_LIMIT)


def _row_tile(n, pref):
    t = min(n, pref)
    assert n % t == 0, (n, t)
    return t


def _nt_dot(a, b):
    return lax.dot_general(a, b, (((1,), (1,)), ((), ())), preferred_element_type=F32)


def _rms(x, gain):
    xn = x * lax.rsqrt(jnp.mean(x * x, axis=-1, keepdims=True) + NORM_EPS)
    return xn * gain


def _mm_norm_kernel(x_ref, g_ref, w_ref, o_ref, xn_ref):
    @pl.when(pl.program_id(1) == 0)
    def _():
        xn_ref[...] = _rms(x_ref[...], g_ref[...]).astype(BF16)

    o_ref[...] = jnp.dot(xn_ref[...], w_ref[...], preferred_element_type=F32)


def mm_norm(x, gain, w, tn):
    n, k = x.shape
    nc = w.shape[1]
    tm = _row_tile(n, 1024)
    assert nc % tn == 0
    return pl.pallas_call(
        _mm_norm_kernel,
        grid=(n // tm, nc // tn),
        in_specs=[pl.BlockSpec((tm, k), lambda i, j: (i, 0)),
                  pl.BlockSpec((1, k), lambda i, j: (0, 0)),
                  pl.BlockSpec((k, tn), lambda i, j: (0, j))],
        out_specs=pl.BlockSpec((tm, tn), lambda i, j: (i, j)),
        out_shape=jax.ShapeDtypeStruct((n, nc), F32),
        scratch_shapes=[pltpu.VMEM((tm, k), BF16)],
        compiler_params=_params("parallel", "arbitrary"),
        name="mm_norm",
    )(x, gain.reshape(1, k), w)


def _mm_res_kernel(x_ref, w_ref, h_ref, o_ref):
    o_ref[...] = h_ref[...] + jnp.dot(x_ref[...], w_ref[...], preferred_element_type=F32)


def mm_res(x, w, h):
    n, k = x.shape
    nc = w.shape[1]
    tm = _row_tile(n, 1024)
    tn = 1024
    return pl.pallas_call(
        _mm_res_kernel,
        grid=(n // tm, nc // tn),
        in_specs=[pl.BlockSpec((tm, k), lambda i, j: (i, 0)),
                  pl.BlockSpec((k, tn), lambda i, j: (0, j)),
                  pl.BlockSpec((tm, tn), lambda i, j: (i, j))],
        out_specs=pl.BlockSpec((tm, tn), lambda i, j: (i, j)),
        out_shape=jax.ShapeDtypeStruct((n, nc), F32),
        compiler_params=_params("parallel", "parallel"),
        name="mm_res",
    )(x, w, h)


def _mm_gated3_kernel(oa_ref, ob_ref, oc_ref, wa_ref, wb_ref, wc_ref, ga_ref, gb_ref, gc_ref, o_ref):
    ya = jnp.dot(oa_ref[...], wa_ref[...], preferred_element_type=F32)
    yb = jnp.dot(ob_ref[...], wb_ref[...], preferred_element_type=F32)
    yc = jnp.dot(oc_ref[...], wc_ref[...], preferred_element_type=F32)
    m = (jax.nn.sigmoid(ga_ref[...]) * ya + jax.nn.sigmoid(gb_ref[...]) * yb
         + jax.nn.sigmoid(gc_ref[...]) * yc)
    o_ref[...] = m.astype(o_ref.dtype)


def mm_gated3(oa, ob, oc, wa, wb, wc, gates):
    n = oa.shape[0]
    tm = _row_tile(n, 1024)
    tn = 512
    nj = D_MODEL // tn
    row = lambda w: pl.BlockSpec((tm, w), lambda i, j: (i, 0))
    col = lambda k: pl.BlockSpec((k, tn), lambda i, j: (0, j))
    gate = lambda s: pl.BlockSpec((tm, tn), lambda i, j: (i, j + s * nj))
    return pl.pallas_call(
        _mm_gated3_kernel,
        grid=(n // tm, nj),
        in_specs=[row(A_W), row(B_W), row(C_W), col(A_W), col(B_W), col(C_W), gate(0), gate(1), gate(2)],
        out_specs=pl.BlockSpec((tm, tn), lambda i, j: (i, j)),
        out_shape=jax.ShapeDtypeStruct((n, D_MODEL), BF16),
        compiler_params=_params("parallel", "parallel"),
        name="mm_gated3",
    )(oa, ob, oc, wa, wb, wc, gates, gates, gates)


def _ffn_kernel(h_ref, g_ref, wg_ref, wu_ref, wo_ref, o_ref, xn_ref):
    @pl.when(pl.program_id(1) == 0)
    def _():
        xn_ref[...] = _rms(h_ref[...], g_ref[...]).astype(BF16)
        o_ref[...] = h_ref[...]

    xn = xn_ref[...]
    gate = jnp.dot(xn, wg_ref[...], preferred_element_type=F32)
    up = jnp.dot(xn, wu_ref[...], preferred_element_type=F32)
    act = (jax.nn.silu(gate) * up).astype(BF16)
    o_ref[...] += jnp.dot(act, wo_ref[...], preferred_element_type=F32)


def ffn(h, gain, w_in, w_out):
    n, d = h.shape
    dff = w_out.shape[0]
    tm = _row_tile(n, 512)
    tf = 512
    assert dff % tf == 0
    nf = dff // tf
    return pl.pallas_call(
        _ffn_kernel,
        grid=(n // tm, nf),
        in_specs=[pl.BlockSpec((tm, d), lambda i, f: (i, 0)),
                  pl.BlockSpec((1, d), lambda i, f: (0, 0)),
                  pl.BlockSpec((d, tf), lambda i, f: (0, f)),
                  pl.BlockSpec((d, tf), lambda i, f: (0, f + nf)),
                  pl.BlockSpec((tf, d), lambda i, f: (f, 0))],
        out_specs=pl.BlockSpec((tm, d), lambda i, f: (i, 0)),
        out_shape=jax.ShapeDtypeStruct((n, d), F32),
        scratch_shapes=[pltpu.VMEM((tm, d), BF16)],
        compiler_params=_params("parallel", "arbitrary"),
        name="ffn",
    )(h, gain.reshape(1, d), w_in, w_in, w_out)


def _ple_kernel(h_ref, g_ref, wg_ref, p_ref, wp_ref, hj_ref, o_ref, xn_ref):
    @pl.when(pl.program_id(1) == 0)
    def _():
        xn_ref[...] = _rms(h_ref[...], g_ref[...]).astype(BF16)

    gate = jnp.dot(xn_ref[...], wg_ref[...], preferred_element_type=F32)
    emb = jnp.dot(p_ref[...], wp_ref[...], preferred_element_type=F32)
    o_ref[...] = hj_ref[...] + jax.nn.sigmoid(gate) * emb


def ple(h, gain, w_gate, p, w_proj):
    n, d = h.shape
    pd = p.shape[1]
    tm = _row_tile(n, 1024)
    tn = 1024
    return pl.pallas_call(
        _ple_kernel,
        grid=(n // tm, d // tn),
        in_specs=[pl.BlockSpec((tm, d), lambda i, j: (i, 0)),
                  pl.BlockSpec((1, d), lambda i, j: (0, 0)),
                  pl.BlockSpec((d, tn), lambda i, j: (0, j)),
                  pl.BlockSpec((tm, pd), lambda i, j: (i, 0)),
                  pl.BlockSpec((pd, tn), lambda i, j: (0, j)),
                  pl.BlockSpec((tm, tn), lambda i, j: (i, j))],
        out_specs=pl.BlockSpec((tm, tn), lambda i, j: (i, j)),
        out_shape=jax.ShapeDtypeStruct((n, d), F32),
        scratch_shapes=[pltpu.VMEM((tm, d), BF16)],
        compiler_params=_params("parallel", "arbitrary"),
        name="ple",
    )(h, gain.reshape(1, d), w_gate, p, w_proj, h)


def _rope_tables(pos, width, period, rot, scale_lanes=None):
    half = rot // 2
    inv = ROPE_THETA ** (-jnp.arange(half, dtype=F32) / half)
    lane = jnp.arange(width) % period
    first, second = lane < half, (lane >= half) & (lane < rot)
    ang = pos.astype(F32)[:, None] * inv[lane % half][None, :]
    cos, sin = jnp.cos(ang), jnp.sin(ang)
    c = jnp.where((first | second)[None, :], cos, 1.0)
    s_up = jnp.where(first[None, :], -sin, 0.0)
    s_dn = jnp.where(second[None, :], sin, 0.0)
    if scale_lanes is not None:
        only, mult = scale_lanes
        c = jnp.where(only[None, :], c, mult[None, :])
        s_up = jnp.where(only[None, :], s_up, 0.0)
        s_dn = jnp.where(only[None, :], s_dn, 0.0)
    return jnp.stack([c, s_up, s_dn]).astype(F32)


def _rope(x, tab, half):
    return (x * tab[0] + pltpu.roll(x, x.shape[1] - half, 1) * tab[1] + pltpu.roll(x, half, 1) * tab[2])


def _post_in_kernel(z_ref, t128_ref, t64_ref, tki_ref, gaq_ref, gak_ref, gbq_ref, gbk_ref,
                    aq_ref, akf_ref, akb_ref, avf_ref, avb_ref, bq_ref, bkf_ref, bkb_ref,
                    bvf_ref, bvb_ref, qi_ref, kiw_ref, kilo_ref, kihi_ref):
    hd = HEAD_DIM
    t128 = t128_ref[...]
    t64 = t64_ref[...]
    half128 = hd // ROT_FRACTION // 2
    half64 = D_IDX // ROT_FRACTION // 2
    for h in range(H_A):
        sl = slice(h * hd, (h + 1) * hd)
        aq = _rms(z_ref[:, sl], gaq_ref[...]) * (-(hd ** -0.5))
        aq_ref[:, sl] = aq.astype(BF16)
        ak = _rms(z_ref[:, A_W + h * hd:A_W + (h + 1) * hd], gak_ref[...])
        akf_ref[:, h, :] = ak
        akb_ref[:, sl] = ak.astype(BF16)
        av = z_ref[:, 2 * A_W + h * hd:2 * A_W + (h + 1) * hd]
        avf_ref[:, h, :] = av
        avb_ref[:, sl] = av.astype(BF16)
    o = 3 * A_W
    for h in range(H_B):
        sl = slice(h * hd, (h + 1) * hd)
        bq = _rope(_rms(z_ref[:, o + h * hd:o + (h + 1) * hd], gbq_ref[...]), t128, half128) * (hd ** -0.5)
        bq_ref[:, sl] = bq.astype(BF16)
    o += B_W
    bk = _rope(_rms(z_ref[:, o:o + hd], gbk_ref[...]), t128, half128)
    bkf_ref[...] = bk
    bkb_ref[...] = bk.astype(BF16)
    o += hd
    bv = z_ref[:, o:o + hd]
    bvf_ref[...] = bv
    bvb_ref[...] = bv.astype(BF16)
    o += hd
    for c in range(H_IDX * D_IDX // 128):
        qi = _rope(z_ref[:, o + c * 128:o + (c + 1) * 128], t64, half64) * (D_IDX ** -0.5)
        qi_ref[:, c * 128:(c + 1) * 128] = qi.astype(BF16)
    o += H_IDX * D_IDX
    kiw = _rope(z_ref[:, o:o + 128], tki_ref[...], half64)
    kiw_ref[...] = kiw
    lane = lax.broadcasted_iota(jnp.int32, kiw.shape, 1)
    ki = jnp.where(lane < D_IDX, kiw, 0.0)
    kilo_ref[...] = ki.astype(BF16)
    kihi_ref[...] = pltpu.roll(ki, D_IDX, 1).astype(BF16)


def post_in(z, tabs, gains, t_len):
    n = z.shape[0]
    tm = _row_tile(t_len, 512)
    nt = t_len // tm
    rows = lambda w: pl.BlockSpec((tm, w), lambda i: (i, 0))
    tab = pl.BlockSpec((3, tm, 128), lambda i: (0, i % nt, 0))
    gain = pl.BlockSpec((1, HEAD_DIM), lambda i: (0, 0))
    per_head = pl.BlockSpec((tm, H_A, HEAD_DIM), lambda i: (i, 0, 0))
    widths = [(A_W, BF16), (A_W, F32), (A_W, BF16), (A_W, F32), (A_W, BF16), (B_W, BF16),
              (HEAD_DIM, F32), (HEAD_DIM, BF16), (HEAD_DIM, F32), (HEAD_DIM, BF16),
              (H_IDX * D_IDX, BF16), (128, F32), (128, BF16), (128, BF16)]
    return pl.pallas_call(
        _post_in_kernel,
        grid=(n // tm,),
        in_specs=[rows(ATT_W), tab, tab, tab, gain, gain, gain, gain],
        out_specs=[per_head if i in (1, 3) else rows(w) for i, (w, _) in enumerate(widths)],
        out_shape=[jax.ShapeDtypeStruct((n, H_A, HEAD_DIM) if i in (1, 3) else (n, w), dt)
                   for i, (w, dt) in enumerate(widths)],
        compiler_params=_params("parallel"),
        name="post_in",
    )(z, *tabs, *[g.reshape(1, HEAD_DIM) for g in gains])


LOG2E = 1.4426950408889634
ATTN_A_ROW_CHUNK = 256
ATTN_A_BLOCKS_PER_STEP = 4


def _attn_a_kernel(q_ref, k_ref, v_ref, tri_ref, o_ref, *, tq, tk, rc, past):
    i = pl.program_id(2)
    q0 = past + i * tq
    tri = tri_ref[...]
    n_full = q0 // tk
    n_all = (q0 + tq - 1 + tk - 1) // tk
    nrc = tq // rc
    qs = [q_ref[c * rc:(c + 1) * rc, :] for c in range(nrc)]
    rows = [q0 + c * rc + lax.broadcasted_iota(jnp.int32, (rc, tk), 0) for c in range(nrc)]
    col = lax.broadcasted_iota(jnp.int32, (rc, tk), 1)

    def step(j, carry, masked, nsub):
        cs = range(nrc)
        ss = range(nsub)
        offs = [pl.multiple_of((j - s) * tk, tk) for s in ss]
        kb = [k_ref[pl.ds(o, tk), :] for o in offs]
        vb = [v_ref[pl.ds(o, tk), :] for o in offs]
        later = [carry[0][c * rc:(c + 1) * rc] for c in cs]
        acc = [carry[1][c * rc:(c + 1) * rc] for c in cs]
        nz = [[_nt_dot(qs[c], kb[s]) for c in cs] for s in ss]
        loss = [[jnp.log(1.0 + jnp.exp2(jnp.minimum(nz[s][c] * (-LOG2E), 126.0))) for c in cs] for s in ss]
        if masked:
            causal = [[(offs[s] + col) < rows[c] for c in cs] for s in ss]
            loss = [[jnp.where(causal[s][c], loss[s][c], 0.0) for c in cs] for s in ss]
        within = [[jnp.dot(loss[s][c].astype(BF16), tri, preferred_element_type=F32) for c in cs] for s in ss]
        for s in ss:
            w = [jnp.exp2((nz[s][c] + later[c] + within[s][c]) * (-LOG2E)) for c in cs]
            if masked:
                w = [jnp.where(causal[s][c], w[c], 0.0) for c in cs]
            acc = [acc[c] + jnp.dot(w[c].astype(BF16), vb[s], preferred_element_type=F32) for c in cs]
            later = [later[c] + within[s][c][:, 0:1] for c in cs]
        return jnp.concatenate(later, axis=0), jnp.concatenate(acc, axis=0)

    carry = (jnp.zeros((tq, 1), F32), jnp.zeros((tq, HEAD_DIM), F32))
    carry = lax.fori_loop(0, n_all - n_full, lambda t, c: step(n_all - 1 - t, c, True, 1), carry)
    n_multi = n_full // ATTN_A_BLOCKS_PER_STEP
    n_single = n_full - n_multi * ATTN_A_BLOCKS_PER_STEP
    carry = lax.fori_loop(0, n_single, lambda t, c: step(n_full - 1 - t, c, False, 1), carry)
    carry = lax.fori_loop(
        0, n_multi,
        lambda t, c: step(n_full - n_single - 1 - t * ATTN_A_BLOCKS_PER_STEP, c, False, ATTN_A_BLOCKS_PER_STEP),
        carry)
    o_ref[...] = carry[1].astype(o_ref.dtype)


def attn_a(q, k, v, past, tk):
    b, t, _ = q.shape
    lp = k.shape[1]
    tq = _row_tile(t, 512)
    rc = min(tq, ATTN_A_ROW_CHUNK)
    tri = jnp.tri(tk, dtype=BF16)
    return pl.pallas_call(
        functools.partial(_attn_a_kernel, tq=tq, tk=tk, rc=rc, past=past),
        grid=(b, H_A, t // tq),
        in_specs=[pl.BlockSpec((None, tq, HEAD_DIM), lambda bi, h, i: (bi, i, h)),
                  pl.BlockSpec((None, lp, HEAD_DIM), lambda bi, h, i: (bi, 0, h)),
                  pl.BlockSpec((None, lp, HEAD_DIM), lambda bi, h, i: (bi, 0, h)),
                  pl.BlockSpec((tk, tk), lambda bi, h, i: (0, 0))],
        out_specs=pl.BlockSpec((None, tq, HEAD_DIM), lambda bi, h, i: (bi, i, h)),
        out_shape=jax.ShapeDtypeStruct((b, t, A_W), BF16),
        compiler_params=_params("parallel", "parallel", "parallel"),
        name="attn_a",
    )(q, k, v, tri)


DSA_QUERY_ROWS = 256
DSA_ATTEND_ROWS = 128


def _dsa_kernel(qi_ref, kiw_ref, kilo_ref, kihi_ref, q_ref, k_ref, v_ref, triu_ref, o_ref, key_ref,
                *, tq, tk, past, n_keys, topk):
    i = pl.program_id(1)
    q0 = past + i * tq
    pos = q0 + lax.broadcasted_iota(jnp.int32, (tq, 1), 0)
    lim = jnp.minimum((pos // CHUNK + 1) * CHUNK, n_keys)
    n_adm = jnp.minimum(((q0 + tq - 1) // CHUNK + 1) * CHUNK, n_keys)
    nt = (n_adm + tk - 1) // tk
    col = lax.broadcasted_iota(jnp.int32, (tq, tk), 1)
    ngrp = tk // 128

    qi = qi_ref[...]
    kiw = kiw_ref[...]
    wi = [kiw[:, D_IDX + h:D_IDX + h + 1] for h in range(H_IDX)]

    n_pair = (nt + 1) // 2
    pair = range(2)

    def score_pair(t, _):
        offs = [pl.multiple_of((2 * t + u) * tk, tk) for u in pair]
        kis = [(kilo_ref[pl.ds(o, tk), :], kihi_ref[pl.ds(o, tk), :]) for o in offs]
        s_idx = [[_nt_dot(qi[:, (h // 2) * 128:(h // 2 + 1) * 128], kis[u][h % 2]) for h in range(H_IDX)]
                 for u in pair]
        for u in pair:
            s = jnp.zeros((tq, tk), F32)
            for h in range(H_IDX):
                s = s + wi[h] * jnp.maximum(s_idx[u][h], 0.0)
            s = jnp.where(s == 0.0, 0.0, s)
            s = jnp.where(offs[u] + col < lim, s, -jnp.inf)
            bits = lax.bitcast_convert_type(s, jnp.int32)
            key_ref[:, pl.ds(offs[u], tk)] = bits ^ ((bits >> 31) & 0x7FFFFFFF)
        return 0

    lax.fori_loop(0, n_pair, score_pair, 0)

    cr = min(tq, DSA_ATTEND_ROWS)

    def count_ge(cand):
        accs = []
        for r in range(tq // cr):
            cand_b = jnp.broadcast_to(cand[r * cr:(r + 1) * cr], (cr, 128))

            def body(t, acc):
                for u in pair:
                    off = pl.multiple_of((2 * t + u) * tk, tk)
                    kt = key_ref[r * cr:(r + 1) * cr, pl.ds(off, tk)]
                    for g in range(ngrp):
                        acc = acc + jnp.where(kt[:, g * 128:(g + 1) * 128] >= cand_b, 1, 0)
                return acc

            accs.append(lax.fori_loop(0, n_pair, body, jnp.zeros((cr, 128), jnp.int32)))
        return jnp.sum(jnp.concatenate(accs, axis=0), axis=1, keepdims=True)

    def bit_step(b, t_u):
        cand_u = t_u | lax.shift_left(jnp.int32(1), 31 - b)
        ok = count_ge(cand_u ^ INT_MIN) >= topk
        return jnp.where(ok, cand_u, t_u)

    t_u = lax.fori_loop(0, 32, bit_step, jnp.zeros((tq, 1), jnp.int32))
    tau = t_u ^ INT_MIN
    tau_next = jnp.where(tau == 2 ** 31 - 1, tau, tau + 1)
    cnt_gt = jnp.where(tau == 2 ** 31 - 1, 0, count_ge(tau_next))
    need = jnp.where(tau == KEY_NEG_INF, 0, topk - cnt_gt).astype(F32)

    triu = triu_ref[...]

    def select_tile(t, seen):
        off = pl.multiple_of(t * tk, tk)
        kt = key_ref[:, pl.ds(off, tk)]
        eq = kt == tau
        prefix = seen + jnp.dot(jnp.where(eq, 1.0, 0.0).astype(BF16), triu, preferred_element_type=F32)
        rank = jnp.where(eq, prefix, jnp.inf)
        bias = jnp.where(kt > tau, 0.0, jnp.where(rank <= need, 0.0, NEG_BIG)).astype(F32)
        key_ref[:, pl.ds(off, tk)] = lax.bitcast_convert_type(bias, jnp.int32)
        return prefix[:, tk - 1:tk]

    lax.fori_loop(0, 2 * n_pair, select_tile, jnp.zeros((tq, 1), F32))

    rg = min(tq, DSA_ATTEND_ROWS)
    groups = range(tq // rg)
    gr = H_B * rg
    qs = [jnp.concatenate([q_ref[r * rg:(r + 1) * rg, h * HEAD_DIM:(h + 1) * HEAD_DIM] for h in range(H_B)], axis=0)
          for r in groups]

    def attend_pair(t, carry):
        offs = [pl.multiple_of((2 * t + u) * tk, tk) for u in pair]
        kb = [k_ref[pl.ds(o, tk), :] for o in offs]
        vb = [v_ref[pl.ds(o, tk), :] for o in offs]
        new = []
        for r in groups:
            m, l, acc = [c[r * gr:(r + 1) * gr] for c in carry]
            bias = [lax.bitcast_convert_type(key_ref[r * rg:(r + 1) * rg, pl.ds(o, tk)], F32) for o in offs]
            logits = [(_nt_dot(qs[r], kb[u]).reshape(H_B, rg, tk) + bias[u][None]).reshape(gr, tk) for u in pair]
            m_new = jnp.maximum(m, jnp.maximum(jnp.max(logits[0], axis=1, keepdims=True),
                                               jnp.max(logits[1], axis=1, keepdims=True)))
            alpha = jnp.exp(m - m_new)
            p = [jnp.exp(logits[u] - m_new) for u in pair]
            l = alpha * l + (jnp.sum(p[0], axis=1, keepdims=True) + jnp.sum(p[1], axis=1, keepdims=True))
            acc = alpha * acc + (jnp.dot(p[0].astype(BF16), vb[0], preferred_element_type=F32)
                                 + jnp.dot(p[1].astype(BF16), vb[1], preferred_element_type=F32))
            new.append((m_new, l, acc))
        return tuple(jnp.concatenate([n[k] for n in new], axis=0) for k in range(3))

    init = (jnp.full((H_B * tq, 1), NEG_BIG, F32), jnp.zeros((H_B * tq, 1), F32),
            jnp.zeros((H_B * tq, HEAD_DIM), F32))
    _, l, acc = lax.fori_loop(0, n_pair, attend_pair, init)
    out = acc / l
    for r in groups:
        for h in range(H_B):
            rows = out[r * gr + h * rg:r * gr + (h + 1) * rg]
            o_ref[r * rg:(r + 1) * rg, h * HEAD_DIM:(h + 1) * HEAD_DIM] = rows.astype(o_ref.dtype)


def dsa(qi, kiw, kilo, kihi, q, k, v, past, n_keys, tk):
    b, t, _ = q.shape
    lp = k.shape[1]
    tq = _row_tile(t, DSA_QUERY_ROWS)
    topk = min(TOPK_MAX, n_keys // 4)
    assert lp % (2 * tk) == 0 and lp >= topk
    triu = jnp.tri(tk, dtype=BF16).T
    qblk = lambda w: pl.BlockSpec((None, tq, w), lambda bi, i: (bi, i, 0))
    kblk = lambda w: pl.BlockSpec((None, lp, w), lambda bi, i: (bi, 0, 0))
    return pl.pallas_call(
        functools.partial(_dsa_kernel, tq=tq, tk=tk, past=past, n_keys=n_keys, topk=topk),
        grid=(b, t // tq),
        in_specs=[qblk(H_IDX * D_IDX), qblk(128), kblk(128), kblk(128), qblk(B_W),
                  kblk(HEAD_DIM), kblk(HEAD_DIM), pl.BlockSpec((tk, tk), lambda bi, i: (0, 0))],
        out_specs=qblk(B_W),
        out_shape=jax.ShapeDtypeStruct((b, t, B_W), BF16),
        scratch_shapes=[pltpu.VMEM((tq, lp), jnp.int32)],
        compiler_params=_params("parallel", "parallel"),
        name="dsa",
    )(qi, kiw, kilo, kihi, q, k, v, triu)


def _softplus(x):
    return jnp.maximum(x, 0.0) + jnp.log(1.0 + jnp.exp(-jnp.abs(x)))


PREV_TILE = 8


def _c_prep_kernel(*refs, first_layer):
    if first_layer:
        (cz_ref, before_ref, shift_ref, mu_ref, w0_ref, a0_ref, w2_ref, a2_ref, g2_ref, kk_ref, ka_ref,
         r_ref, kn_ref, k_ref, v_ref, lw_ref, a_ref, g_ref, vtok_ref) = refs
    else:
        (cz_ref, before_ref, shift_ref, mu_ref, w0_ref, a0_ref, w2_ref, a2_ref, g2_ref, kk_ref, ka_ref,
         vf_ref, v0_ref, v1_ref, v2_ref,
         r_ref, kn_ref, k_ref, v_ref, lw_ref, a_ref, g_ref) = refs

    def put_heads(ref, x):
        for pair in range(H_C // 2):
            tile = x[:, pair * 128:(pair + 1) * 128]
            ref[2 * pair] = tile[:, :N_C]
            ref[2 * pair + 1] = pltpu.roll(tile, N_C, 1)[:, :N_C]

    cz = cz_ref[...]
    rowid = lax.broadcasted_iota(jnp.int32, cz.shape, 0)
    before = jnp.where(pl.program_id(1) == 0, shift_ref[...], before_ref[PREV_TILE - 1:PREV_TILE, :])
    prev = jnp.where(rowid == 0, before, pltpu.roll(cz, 1, 0))
    zz = cz + (prev - cz) * mu_ref[...]
    r = zz[:, 0:C_W]
    k = zz[:, C_W:2 * C_W]
    v = zz[:, 2 * C_W:3 * C_W]
    tail = zz[:, 3 * C_W:C_PAD]
    w = w0_ref[...] + jnp.dot(jnp.tanh(tail).astype(BF16), w2_ref[...], preferred_element_type=F32)
    w = -_softplus(-w) - 0.5
    a = jax.nn.sigmoid(a0_ref[...] + jnp.dot(tail.astype(BF16), a2_ref[...], preferred_element_type=F32))
    g = jnp.dot(jax.nn.sigmoid(tail).astype(BF16), g2_ref[...], preferred_element_type=F32)
    if not first_layer:
        lora = jnp.dot(v.astype(BF16), v1_ref[...], preferred_element_type=F32)
        lora = jnp.dot(lora.astype(BF16), v2_ref[...], preferred_element_type=F32)
        v = v + (vf_ref[...] - v) * jax.nn.sigmoid(v0_ref[...] + lora)
    if first_layer:
        vtok_ref[...] = v
    put_heads(r_ref, r)
    put_heads(kn_ref, k * kk_ref[...])
    put_heads(k_ref, k * (1.0 + (a - 1.0) * ka_ref[...]))
    put_heads(v_ref, v)
    put_heads(lw_ref, -jnp.exp(w))
    put_heads(a_ref, a)
    put_heads(g_ref, g)


def c_prep(cz, shift0, vecs, mats, v_first, vres, tm):
    b, t, _ = cz.shape
    first_layer = vres is None
    blk = lambda w: pl.BlockSpec((None, tm, w), lambda bi, i: (bi, i, 0))
    vec = lambda w: pl.BlockSpec((1, w), lambda bi, i: (0, 0))
    mat = lambda m: pl.BlockSpec(m.shape, lambda bi, i: (0, 0))
    mu, w0, a0, k_k, k_a = vecs
    w2, a2, g2 = mats
    assert tm % PREV_TILE == 0
    before = pl.BlockSpec((None, PREV_TILE, C_PAD), lambda bi, i: (bi, jnp.maximum(i * (tm // PREV_TILE) - 1, 0), 0))
    ins = [cz, cz, shift0, mu, w0, a0, w2, a2, g2, k_k, k_a]
    specs = [blk(C_PAD), before, pl.BlockSpec((None, 1, C_PAD), lambda bi, i: (bi, 0, 0)),
             vec(C_PAD), vec(C_W), vec(C_W), mat(w2), mat(a2), mat(g2), vec(C_W), vec(C_W)]
    if not first_layer:
        v0, v1, v2 = vres
        ins += [v_first, v0, v1, v2]
        specs += [blk(C_W), vec(C_W), mat(v1), mat(v2)]
    heads = pl.BlockSpec((None, H_C, tm, N_C), lambda bi, i: (bi, 0, i, 0))
    out_specs = [heads] * 7
    out_shape = [jax.ShapeDtypeStruct((b, H_C, t, N_C), F32)] * 7
    if first_layer:
        out_specs.append(blk(C_W))
        out_shape.append(jax.ShapeDtypeStruct((b, t, C_W), F32))
    return pl.pallas_call(
        functools.partial(_c_prep_kernel, first_layer=first_layer),
        grid=(b, t // tm),
        in_specs=specs,
        out_specs=out_specs,
        out_shape=out_shape,
        compiler_params=_params("parallel", "parallel"),
        name="c_prep",
    )(*ins)


def _bdot(a, b, contract_a, contract_b):
    return lax.dot_general(a.astype(BF16), b.astype(BF16), (((contract_a,), (contract_b,)), ((0,), (0,))),
                           preferred_element_type=F32)


def _bdot3(a, b, contract_a, contract_b):
    ah = a.astype(BF16)
    bh = b.astype(BF16)
    al = (a - ah.astype(F32)).astype(BF16)
    bl = (b - bh.astype(F32)).astype(BF16)
    dn = (((contract_a,), (contract_b,)), ((0,), (0,)))
    dot = lambda x, y: lax.dot_general(x, y, dn, preferred_element_type=F32)
    return dot(ah, bh) + (dot(al, bh) + dot(ah, bl))


C_SCAN_CHUNKS_PER_STEP = 4


def _c_scan_kernel(r_ref, kn_ref, k_ref, v_ref, lw_ref, a_ref, g_ref, s0_ref, rk_ref, lng_ref, lnb_ref,
                   y_ref, st_ref, s_ref, *, nck):
    c_len = SCAN_CHUNK

    @pl.when(pl.program_id(1) == 0)
    def _():
        s_ref[...] = s0_ref[...]

    def step(c0, nc):
        g_n = H_C * nc
        rows = pl.ds(pl.multiple_of(c0 * c_len, c_len), nc * c_len)
        load = lambda ref: ref[:, rows, :].reshape(g_n, c_len, N_C)
        per_head = lambda x: jnp.broadcast_to(x[:, None], (H_C, nc) + x.shape[1:]).reshape((g_n,) + x.shape[1:])
        of_chunk = lambda x, j: x.reshape((H_C, nc) + x.shape[1:])[:, j]
        ti = lax.broadcasted_iota(jnp.int32, (g_n, c_len, c_len), 1)
        si = lax.broadcasted_iota(jnp.int32, (g_n, c_len, c_len), 2)
        incl = si <= ti
        strict = si < ti
        tri = jnp.where(incl, 1.0, 0.0)
        eye = jnp.where(si == ti, 1.0, 0.0)
        r, k, v, lw, a, kn = (load(x) for x in (r_ref, k_ref, v_ref, lw_ref, a_ref, kn_ref))
        kn = kn * lax.rsqrt(jnp.maximum(jnp.sum(kn * kn, axis=-1, keepdims=True), 1e-24))
        aa = -kn
        bb = kn * a
        g = _bdot3(tri, lw, 2, 1)
        g_end = g[:, c_len - 1:c_len, :]
        a_t = aa * jnp.exp(g - lw)
        r_t = r * jnp.exp(g)
        inv = jnp.exp(-g)
        b_h = bb * inv
        k_h = k * inv
        to_end = jnp.exp(g_end - g)
        b_e = bb * to_end
        k_e = k * to_end
        d_end = jnp.exp(g_end)
        a_ab = jnp.where(strict, _bdot(a_t, b_h, 2, 2), 0.0)
        a_ak = jnp.where(strict, _bdot(a_t, k_h, 2, 2), 0.0)
        a_rb = jnp.where(incl, _bdot(r_t, b_h, 2, 2), 0.0)
        a_rk = jnp.where(incl, _bdot(r_t, k_h, 2, 2), 0.0)
        t_inv = eye + a_ab
        pw = a_ab
        for _ in range(4):
            pw = _bdot(pw, pw, 2, 1)
            t_inv = t_inv + _bdot(t_inv, pw, 2, 1)
        p_mat = _bdot(t_inv, a_t, 2, 1)
        q_mat = _bdot(t_inv, _bdot(a_ak, v, 2, 1), 2, 1)
        y_v = _bdot(a_rk, v, 2, 1)
        s = s_ref[...]
        ys = []
        for j in range(nc):
            u = _bdot(of_chunk(p_mat, j), s, 2, 2) + of_chunk(q_mat, j)
            ys.append(_bdot(of_chunk(r_t, j), s, 2, 2) + _bdot(of_chunk(a_rb, j), u, 2, 1) + of_chunk(y_v, j))
            uv = jnp.concatenate([u, of_chunk(v, j)], axis=1)
            bk = jnp.concatenate([of_chunk(b_e, j), of_chunk(k_e, j)], axis=1)
            s = s * of_chunk(d_end, j) + _bdot(uv, bk, 1, 1)
        s_ref[...] = s
        y = jnp.stack(ys, axis=1).reshape(g_n, c_len, N_C)
        mu = jnp.mean(y, axis=-1, keepdims=True)
        yc = y - mu
        yn = yc * lax.rsqrt(jnp.mean(yc * yc, axis=-1, keepdims=True) + LNX_EPS)
        yn = yn * per_head(lng_ref[...]) + per_head(lnb_ref[...])
        bonus = jnp.sum(r * k * per_head(rk_ref[...]), axis=-1, keepdims=True) * v
        o = ((yn + bonus) * load(g_ref)).reshape(H_C, nc * c_len, N_C)
        y_ref[rows, :] = jnp.concatenate([o[h] for h in range(H_C)], axis=-1).astype(y_ref.dtype)

    n_multi = nck // C_SCAN_CHUNKS_PER_STEP
    if n_multi:
        lax.fori_loop(0, n_multi, lambda t, _: step(t * C_SCAN_CHUNKS_PER_STEP, C_SCAN_CHUNKS_PER_STEP), None)
    for c in range(n_multi * C_SCAN_CHUNKS_PER_STEP, nck):
        step(c, 1)

    @pl.when(pl.program_id(1) == pl.num_programs(1) - 1)
    def _():
        st_ref[...] = s_ref[...]


def c_scan(r, kn, k, v, lw, a, g, s0, r_k, ln_g, ln_b, tm):
    b, _, t, _ = r.shape
    blk = pl.BlockSpec((None, H_C, tm, N_C), lambda bi, i: (bi, 0, i, 0))
    st = pl.BlockSpec((None, H_C, N_C, N_C), lambda bi, i: (bi, 0, 0, 0))
    vec = pl.BlockSpec((H_C, 1, N_C), lambda bi, i: (0, 0, 0))
    return pl.pallas_call(
        functools.partial(_c_scan_kernel, nck=tm // SCAN_CHUNK),
        grid=(b, t // tm),
        in_specs=[blk] * 7 + [st, vec, vec, vec],
        out_specs=[pl.BlockSpec((None, tm, C_W), lambda bi, i: (bi, i, 0)), st],
        out_shape=[jax.ShapeDtypeStruct((b, t, C_W), BF16),
                   jax.ShapeDtypeStruct((b, H_C, N_C, N_C), F32)],
        scratch_shapes=[pltpu.VMEM((H_C, N_C, N_C), F32)],
        compiler_params=_params("parallel", "arbitrary"),
        name="c_scan",
    )(r, kn, k, v, lw, a, g, s0, r_k.reshape(H_C, 1, N_C), ln_g.reshape(H_C, 1, N_C), ln_b.reshape(H_C, 1, N_C))


def _pad_rows(x, rows, axis=1):
    pad = rows - x.shape[axis]
    if pad == 0:
        return x
    widths = [(0, 0)] * x.ndim
    widths[axis] = (0, pad)
    return jnp.pad(x, widths)


ATT_COLS = 3 * A_W + B_W + 2 * HEAD_DIM + H_IDX * D_IDX + D_IDX + H_IDX
IN_COLS = ATT_COLS + C_COLS + 3 * D_MODEL


def _regroup_kernel(w_ref, tail_ref, att_ref, c_ref, g_ref):
    lane = lax.broadcasted_iota(jnp.int32, (w_ref.shape[0], 128), 1)
    n_full = IN_COLS // 128

    def tile(j):
        return tail_ref[...] if j == n_full else w_ref[:, j * 128:(j + 1) * 128]

    def shifted(j0, shift, j):
        joined = jnp.where(lane >= shift, tile(j0 + j), tile(j0 + j + 1))
        return pltpu.roll(joined, 128 - shift, 1)

    for j in range(ATT_W // 128):
        x = tile(j)
        if (j + 1) * 128 > ATT_COLS:
            x = jnp.where(lane < ATT_COLS - j * 128, x, 0.0)
        att_ref[:, j * 128:(j + 1) * 128] = x.astype(BF16)
    for j in range(C_PAD // 128):
        x = shifted(ATT_COLS // 128, ATT_COLS % 128, j)
        if (j + 1) * 128 > C_COLS:
            x = jnp.where(lane < C_COLS - j * 128, x, 0.0)
        c_ref[:, j * 128:(j + 1) * 128] = x.astype(BF16)
    g0 = ATT_COLS + C_COLS
    for j in range(3 * D_MODEL // 128):
        g_ref[:, j * 128:(j + 1) * 128] = shifted(g0 // 128, g0 % 128, j).astype(BF16)


def regroup_w_in(w_in):
    depth, d, cols = w_in.shape
    assert cols == IN_COLS and (ATT_COLS + C_COLS) % 128 != 0
    n_full = cols // 128
    tail = jnp.pad(w_in[:, :, n_full * 128:], ((0, 0), (0, 0), (0, (n_full + 1) * 128 - cols)))
    tm = 128
    blk = lambda w: pl.BlockSpec((None, tm, w), lambda l, i: (l, i, 0))
    return pl.pallas_call(
        _regroup_kernel,
        grid=(depth, d // tm),
        in_specs=[blk(cols), blk(128)],
        out_specs=[blk(ATT_W), blk(C_PAD), blk(3 * D_MODEL)],
        out_shape=[jax.ShapeDtypeStruct((depth, d, w), BF16) for w in (ATT_W, C_PAD, 3 * D_MODEL)],
        compiler_params=_params("parallel", "parallel"),
        name="regroup_w_in",
    )(w_in, tail)


def _prep_weights(w):
    out = []
    w_att, w_c, w_g = regroup_w_in(w["w_in"])
    for i in range(DEPTH):
        tail_rows = lambda m, off: jnp.pad(m, ((off, C_TAIL - off - m.shape[0]), (0, 0))).astype(BF16)
        lw = dict(
            w_att=w_att[i], w_c=w_c[i], w_g=w_g[i],
            w2=tail_rows(w["c_w2"][i], 0),
            a2=tail_rows(w["c_a2"][i], D_DECAY_LORA),
            g2=tail_rows(w["c_g2"][i], D_DECAY_LORA + D_AAA_LORA),
            mu=jnp.pad(w["c_mu"][i], (0, C_PAD - C_COLS)).reshape(1, C_PAD),
            w_br_a=w["w_br_a"][i].astype(BF16), w_br_b=w["w_br_b"][i].astype(BF16),
            w_br_c=w["w_br_c"][i].astype(BF16), w_out=w["w_out"][i].astype(BF16),
            w_ffn_in=w["w_ffn_in"][i].astype(BF16), w_ffn_out=w["w_ffn_out"][i].astype(BF16),
            w_ple_gate=w["w_ple_gate"][i].astype(BF16), w_ple_proj=w["w_ple_proj"][i].astype(BF16),
        )
        if i > 0:
            lw["v1"] = w["c_v1"][i - 1].astype(BF16)
            lw["v2"] = w["c_v2"][i - 1].astype(BF16)
        out.append(lw)
    return out


def _run_trunk(x, p, past, caches, w, wl):
    b, t, _ = x.shape
    n = b * t
    n_keys = past + t
    pos = past + jnp.arange(t)
    lane = jnp.arange(128)
    tabs = (_rope_tables(pos, 128, HEAD_DIM, HEAD_DIM // ROT_FRACTION),
            _rope_tables(pos, 128, D_IDX, D_IDX // ROT_FRACTION),
            _rope_tables(pos, 128, 128, D_IDX // ROT_FRACTION,
                         scale_lanes=(lane < D_IDX, jnp.where(lane < D_IDX + H_IDX, H_IDX ** -0.5, 1.0))))
    tk_a = 256
    tk_b = 512
    lp_a = -(-n_keys // tk_a) * tk_a
    lp_b = -(-n_keys // (2 * tk_b)) * (2 * tk_b)
    t_c = -(-t // SCAN_CHUNK) * SCAN_CHUNK
    tm_c = _row_tile(t, 256)
    tm_s = _row_tile(t_c, 256)
    h = x.reshape(n, D_MODEL)
    outs = ([], [], [], [], [], [], [])
    v_first = None
    for i in range(DEPTH):
        lw = wl[i]
        z_att = mm_norm(h, w["norm_mix"][i], lw["w_att"], ATT_W // 3)
        z_c = mm_norm(h, w["norm_mix"][i], lw["w_c"], C_PAD // 2)
        z_g = mm_norm(h, w["norm_mix"][i], lw["w_g"], 1536)
        (aq, ak_f, ak_b, av_f, av_b, bq, bk_f, bk_b, bv_f, bv_b, qi, kiw, kilo, kihi) = post_in(
            z_att, tabs, (w["a_q_norm"][i], w["a_k_norm"][i], w["b_q_norm"][i], w["b_k_norm"][i]), t)
        seq = lambda a: a.reshape(b, t, a.shape[-1])

        def keys(new, old, lp):
            new = seq(new)
            if old is not None:
                new = jnp.concatenate([old.astype(BF16), new], axis=1)
            return _pad_rows(new, lp)

        c = caches
        past_of = lambda name: None if c is None else c[name][i]
        pa_k = None if c is None else c["a_k"][i].reshape(b, past, A_W)
        pa_v = None if c is None else c["a_v"][i].reshape(b, past, A_W)
        o_a = attn_a(seq(aq), keys(ak_b, pa_k, lp_a), keys(av_b, pa_v, lp_a), past, tk_a)
        if c is None:
            p_lo = p_hi = None
        else:
            p_lo = jnp.pad(c["b_ki"][i], ((0, 0), (0, 0), (0, 128 - D_IDX)))
            p_hi = jnp.pad(c["b_ki"][i], ((0, 0), (0, 0), (128 - D_IDX, 0)))
        o_b = dsa(seq(qi), seq(kiw), keys(kilo, p_lo, lp_b), keys(kihi, p_hi, lp_b), seq(bq),
                  keys(bk_b, past_of("b_k"), lp_b), keys(bv_b, past_of("b_v"), lp_b), past, n_keys, tk_b)
        cz = seq(z_c)
        shift0 = jnp.zeros((b, C_COLS), F32) if c is None else c["shift"][i]
        shift0 = jnp.pad(shift0, ((0, 0), (0, C_PAD - C_COLS)))
        row = lambda a: a.reshape(1, C_W)
        vres = None if i == 0 else (row(w["c_v0"][i - 1]), lw["v1"], lw["v2"])
        prep = c_prep(cz, shift0[:, None, :],
                      (lw["mu"], row(w["c_w0"][i]), row(w["c_a0"][i]), row(w["c_k_k"][i]), row(w["c_k_a"][i])),
                      (lw["w2"], lw["a2"], lw["g2"]), v_first, vres, tm_c)
        if i == 0:
            v_first = prep[7]
        s0 = jnp.zeros((b, H_C, N_C, N_C), F32) if c is None else c["wkv"][i]
        y_c, wkv_t = c_scan(*[_pad_rows(a, t_c, axis=2) for a in prep[:7]], s0,
                            w["c_r_k"][i], w["c_ln_g"][i], w["c_ln_b"][i], tm_s)
        o_c = y_c[:, :t].reshape(n, C_W)
        merged = mm_gated3(o_a.reshape(n, A_W), o_b.reshape(n, B_W), o_c,
                           lw["w_br_a"], lw["w_br_b"], lw["w_br_c"], z_g)
        h = mm_res(merged, lw["w_out"], h)
        h = ffn(h, w["norm_ffn"][i], lw["w_ffn_in"], lw["w_ffn_out"])
        h = ple(h, w["norm_ple"][i], lw["w_ple_gate"], p[i].reshape(n, -1).astype(BF16), lw["w_ple_proj"])
        vals = (ak_f.reshape(b, t, H_A, HEAD_DIM), av_f.reshape(b, t, H_A, HEAD_DIM),
                seq(bk_f), seq(bv_f), seq(kiw)[:, :, :D_IDX], wkv_t, cz[:, -1, :C_COLS])
        for lst, val in zip(outs, vals):
            lst.append(val)
    return h.reshape(b, t, D_MODEL), [jnp.stack(l) for l in outs]


def kernel(x_prompt, x_sample, cache_a_k, cache_a_v, cache_b_k, cache_b_v, cache_b_kidx, state_c_wkv, state_c_shift, p_prompt, p_sample, norm_mix, w_in, a_q_norm, a_k_norm, b_q_norm, b_k_norm, c_mu, c_w0, c_w2, c_a0, c_a2, c_g2, c_v0, c_v1, c_v2, c_k_k, c_k_a, c_r_k, c_ln_g, c_ln_b, w_br_a, w_br_b, w_br_c, w_out, norm_ffn, w_ffn_in, w_ffn_out, norm_ple, w_ple_gate, w_ple_proj):
    w = dict(norm_mix=norm_mix, w_in=w_in, a_q_norm=a_q_norm, a_k_norm=a_k_norm, b_q_norm=b_q_norm,
             b_k_norm=b_k_norm, c_mu=c_mu, c_w0=c_w0, c_w2=c_w2, c_a0=c_a0, c_a2=c_a2, c_g2=c_g2,
             c_v0=c_v0, c_v1=c_v1, c_v2=c_v2, c_k_k=c_k_k, c_k_a=c_k_a, c_r_k=c_r_k, c_ln_g=c_ln_g,
             c_ln_b=c_ln_b, w_br_a=w_br_a, w_br_b=w_br_b, w_br_c=w_br_c, w_out=w_out, norm_ffn=norm_ffn,
             w_ffn_in=w_ffn_in, w_ffn_out=w_ffn_out, norm_ple=norm_ple, w_ple_gate=w_ple_gate,
             w_ple_proj=w_ple_proj)
    wl = _prep_weights(w)
    y_p, o_p = _run_trunk(x_prompt, p_prompt, 0, None, w, wl)
    caches = dict(a_k=cache_a_k, a_v=cache_a_v, b_k=cache_b_k, b_v=cache_b_v, b_ki=cache_b_kidx,
                  wkv=state_c_wkv, shift=state_c_shift)
    y_s, o_s = _run_trunk(x_sample, p_sample, cache_a_k.shape[2], caches, w, wl)
    return (y_p, y_s, *o_p, *o_s)
```

```python
import functools
import math

import jax
import jax.numpy as jnp
from jax import lax
from jax.experimental import pallas as pl
from jax.experimental.pallas import tpu as pltpu

F32 = jnp.float32
BF16 = jnp.bfloat16

D_MODEL = 2048
DEPTH = 2
HEAD_DIM = 128
H_A = 4
A_W = H_A * HEAD_DIM
H_B = 4
B_W = H_B * HEAD_DIM
H_IDX = 4
D_IDX = 64
TOPK_MAX = 256
CHUNK = 64
H_C = 16
N_C = 64
C_W = H_C * N_C
SCAN_CHUNK = 32
D_DECAY_LORA = 96
D_AAA_LORA = 96
D_GATE_LORA = 256
C_COLS = 3 * C_W + D_DECAY_LORA + D_AAA_LORA + D_GATE_LORA
C_PAD = 3584
C_TAIL = C_PAD - 3 * C_W
ATT_W = 3 * A_W + B_W + 2 * HEAD_DIM + H_IDX * D_IDX + 128
ROPE_THETA = 500000.0
ROT_FRACTION = 4
NORM_EPS = 1e-6
LNX_EPS = 64e-5
NEG_BIG = -1e30

VMEM_LIMIT = 56 * 1024 * 1024
INT_MIN = -(2 ** 31)
KEY_NEG_INF = (0xFF800000 ^ 0x7FFFFFFF) - 2 ** 32


def _params(*sem):
    return pltpu.CompilerParams(dimension_semantics=sem, vmem_limit_bytes=VMEM_LIMIT)


def _row_tile(n, pref):
    t = min(n, pref)
    assert n % t == 0, (n, t)
    return t


def _nt_dot(a, b):
    return lax.dot_general(a, b, (((1,), (1,)), ((), ())), preferred_element_type=F32)


def _rms(x, gain):
    xn = x * lax.rsqrt(jnp.mean(x * x, axis=-1, keepdims=True) + NORM_EPS)
    return xn * gain


def _mm_norm_kernel(x_ref, g_ref, w_ref, o_ref, xn_ref):
    @pl.when(pl.program_id(1) == 0)
    def _():
        xn_ref[...] = _rms(x_ref[...], g_ref[...]).astype(BF16)

    o_ref[...] = jnp.dot(xn_ref[...], w_ref[...], preferred_element_type=F32)


def mm_norm(x, gain, w, tn):
    n, k = x.shape
    nc = w.shape[1]
    tm = _row_tile(n, 1024)
    assert nc % tn == 0
    return pl.pallas_call(
        _mm_norm_kernel,
        grid=(n // tm, nc // tn),
        in_specs=[pl.BlockSpec((tm, k), lambda i, j: (i, 0)),
                  pl.BlockSpec((1, k), lambda i, j: (0, 0)),
                  pl.BlockSpec((k, tn), lambda i, j: (0, j))],
        out_specs=pl.BlockSpec((tm, tn), lambda i, j: (i, j)),
        out_shape=jax.ShapeDtypeStruct((n, nc), F32),
        scratch_shapes=[pltpu.VMEM((tm, k), BF16)],
        compiler_params=_params("parallel", "arbitrary"),
        name="mm_norm",
    )(x, gain.reshape(1, k), w)


def _mm_res_kernel(x_ref, w_ref, h_ref, o_ref):
    o_ref[...] = h_ref[...] + jnp.dot(x_ref[...], w_ref[...], preferred_element_type=F32)


def mm_res(x, w, h):
    n, k = x.shape
    nc = w.shape[1]
    tm = _row_tile(n, 1024)
    tn = 1024
    return pl.pallas_call(
        _mm_res_kernel,
        grid=(n // tm, nc // tn),
        in_specs=[pl.BlockSpec((tm, k), lambda i, j: (i, 0)),
                  pl.BlockSpec((k, tn), lambda i, j: (0, j)),
                  pl.BlockSpec((tm, tn), lambda i, j: (i, j))],
        out_specs=pl.BlockSpec((tm, tn), lambda i, j: (i, j)),
        out_shape=jax.ShapeDtypeStruct((n, nc), F32),
        compiler_params=_params("parallel", "parallel"),
        name="mm_res",
    )(x, w, h)


def _mm_gated3_kernel(oa_ref, ob_ref, oc_ref, wa_ref, wb_ref, wc_ref, ga_ref, gb_ref, gc_ref, o_ref):
    ya = jnp.dot(oa_ref[...], wa_ref[...], preferred_element_type=F32)
    yb = jnp.dot(ob_ref[...], wb_ref[...], preferred_element_type=F32)
    yc = jnp.dot(oc_ref[...], wc_ref[...], preferred_element_type=F32)
    m = (jax.nn.sigmoid(ga_ref[...]) * ya + jax.nn.sigmoid(gb_ref[...]) * yb
         + jax.nn.sigmoid(gc_ref[...]) * yc)
    o_ref[...] = m.astype(o_ref.dtype)


def mm_gated3(oa, ob, oc, wa, wb, wc, gates):
    n = oa.shape[0]
    tm = _row_tile(n, 1024)
    tn = 512
    nj = D_MODEL // tn
    row = lambda w: pl.BlockSpec((tm, w), lambda i, j: (i, 0))
    col = lambda k: pl.BlockSpec((k, tn), lambda i, j: (0, j))
    gate = lambda s: pl.BlockSpec((tm, tn), lambda i, j: (i, j + s * nj))
    return pl.pallas_call(
        _mm_gated3_kernel,
        grid=(n // tm, nj),
        in_specs=[row(A_W), row(B_W), row(C_W), col(A_W), col(B_W), col(C_W), gate(0), gate(1), gate(2)],
        out_specs=pl.BlockSpec((tm, tn), lambda i, j: (i, j)),
        out_shape=jax.ShapeDtypeStruct((n, D_MODEL), BF16),
        compiler_params=_params("parallel", "parallel"),
        name="mm_gated3",
    )(oa, ob, oc, wa, wb, wc, gates, gates, gates)


def _ffn_kernel(h_ref, g_ref, wg_ref, wu_ref, wo_ref, o_ref, xn_ref):
    @pl.when(pl.program_id(1) == 0)
    def _():
        xn_ref[...] = _rms(h_ref[...], g_ref[...]).astype(BF16)
        o_ref[...] = h_ref[...]

    xn = xn_ref[...]
    gate = jnp.dot(xn, wg_ref[...], preferred_element_type=F32)
    up = jnp.dot(xn, wu_ref[...], preferred_element_type=F32)
    act = (jax.nn.silu(gate) * up).astype(BF16)
    o_ref[...] += jnp.dot(act, wo_ref[...], preferred_element_type=F32)


def ffn(h, gain, w_in, w_out):
    n, d = h.shape
    dff = w_out.shape[0]
    tm = _row_tile(n, 512)
    tf = 512
    assert dff % tf == 0
    nf = dff // tf
    return pl.pallas_call(
        _ffn_kernel,
        grid=(n // tm, nf),
        in_specs=[pl.BlockSpec((tm, d), lambda i, f: (i, 0)),
                  pl.BlockSpec((1, d), lambda i, f: (0, 0)),
                  pl.BlockSpec((d, tf), lambda i, f: (0, f)),
                  pl.BlockSpec((d, tf), lambda i, f: (0, f + nf)),
                  pl.BlockSpec((tf, d), lambda i, f: (f, 0))],
        out_specs=pl.BlockSpec((tm, d), lambda i, f: (i, 0)),
        out_shape=jax.ShapeDtypeStruct((n, d), F32),
        scratch_shapes=[pltpu.VMEM((tm, d), BF16)],
        compiler_params=_params("parallel", "arbitrary"),
        name="ffn",
    )(h, gain.reshape(1, d), w_in, w_in, w_out)


def _ple_kernel(h_ref, g_ref, wg_ref, p_ref, wp_ref, hj_ref, o_ref, xn_ref):
    @pl.when(pl.program_id(1) == 0)
    def _():
        xn_ref[...] = _rms(h_ref[...], g_ref[...]).astype(BF16)

    gate = jnp.dot(xn_ref[...], wg_ref[...], preferred_element_type=F32)
    emb = jnp.dot(p_ref[...], wp_ref[...], preferred_element_type=F32)
    o_ref[...] = hj_ref[...] + jax.nn.sigmoid(gate) * emb


def ple(h, gain, w_gate, p, w_proj):
    n, d = h.shape
    pd = p.shape[1]
    tm = _row_tile(n, 1024)
    tn = 1024
    return pl.pallas_call(
        _ple_kernel,
        grid=(n // tm, d // tn),
        in_specs=[pl.BlockSpec((tm, d), lambda i, j: (i, 0)),
                  pl.BlockSpec((1, d), lambda i, j: (0, 0)),
                  pl.BlockSpec((d, tn), lambda i, j: (0, j)),
                  pl.BlockSpec((tm, pd), lambda i, j: (i, 0)),
                  pl.BlockSpec((pd, tn), lambda i, j: (0, j)),
                  pl.BlockSpec((tm, tn), lambda i, j: (i, j))],
        out_specs=pl.BlockSpec((tm, tn), lambda i, j: (i, j)),
        out_shape=jax.ShapeDtypeStruct((n, d), F32),
        scratch_shapes=[pltpu.VMEM((tm, d), BF16)],
        compiler_params=_params("parallel", "arbitrary"),
        name="ple",
    )(h, gain.reshape(1, d), w_gate, p, w_proj, h)


def _rope_tables(pos, width, period, rot, scale_lanes=None):
    half = rot // 2
    inv = ROPE_THETA ** (-jnp.arange(half, dtype=F32) / half)
    lane = jnp.arange(width) % period
    first, second = lane < half, (lane >= half) & (lane < rot)
    ang = pos.astype(F32)[:, None] * inv[lane % half][None, :]
    cos, sin = jnp.cos(ang), jnp.sin(ang)
    c = jnp.where((first | second)[None, :], cos, 1.0)
    s_up = jnp.where(first[None, :], -sin, 0.0)
    s_dn = jnp.where(second[None, :], sin, 0.0)
    if scale_lanes is not None:
        only, mult = scale_lanes
        c = jnp.where(only[None, :], c, mult[None, :])
        s_up = jnp.where(only[None, :], s_up, 0.0)
        s_dn = jnp.where(only[None, :], s_dn, 0.0)
    return jnp.stack([c, s_up, s_dn]).astype(F32)


def _rope(x, tab, half):
    return (x * tab[0] + pltpu.roll(x, x.shape[1] - half, 1) * tab[1] + pltpu.roll(x, half, 1) * tab[2])


def _post_in_kernel(z_ref, t128_ref, t64_ref, tki_ref, gaq_ref, gak_ref, gbq_ref, gbk_ref,
                    aq_ref, akf_ref, akb_ref, avf_ref, avb_ref, bq_ref, bkf_ref, bkb_ref,
                    bvf_ref, bvb_ref, qi_ref, kiw_ref, kilo_ref, kihi_ref):
    hd = HEAD_DIM
    t128 = t128_ref[...]
    t64 = t64_ref[...]
    half128 = hd // ROT_FRACTION // 2
    half64 = D_IDX // ROT_FRACTION // 2
    for h in range(H_A):
        sl = slice(h * hd, (h + 1) * hd)
        aq = _rms(z_ref[:, sl], gaq_ref[...]) * (-(hd ** -0.5))
        aq_ref[:, sl] = aq.astype(BF16)
        ak = _rms(z_ref[:, A_W + h * hd:A_W + (h + 1) * hd], gak_ref[...])
        akf_ref[:, h, :] = ak
        akb_ref[:, sl] = ak.astype(BF16)
        av = z_ref[:, 2 * A_W + h * hd:2 * A_W + (h + 1) * hd]
        avf_ref[:, h, :] = av
        avb_ref[:, sl] = av.astype(BF16)
    o = 3 * A_W
    for h in range(H_B):
        sl = slice(h * hd, (h + 1) * hd)
        bq = _rope(_rms(z_ref[:, o + h * hd:o + (h + 1) * hd], gbq_ref[...]), t128, half128) * (hd ** -0.5)
        bq_ref[:, sl] = bq.astype(BF16)
    o += B_W
    bk = _rope(_rms(z_ref[:, o:o + hd], gbk_ref[...]), t128, half128)
    bkf_ref[...] = bk
    bkb_ref[...] = bk.astype(BF16)
    o += hd
    bv = z_ref[:, o:o + hd]
    bvf_ref[...] = bv
    bvb_ref[...] = bv.astype(BF16)
    o += hd
    for c in range(H_IDX * D_IDX // 128):
        qi = _rope(z_ref[:, o + c * 128:o + (c + 1) * 128], t64, half64) * (D_IDX ** -0.5)
        qi_ref[:, c * 128:(c + 1) * 128] = qi.astype(BF16)
    o += H_IDX * D_IDX
    kiw = _rope(z_ref[:, o:o + 128], tki_ref[...], half64)
    kiw_ref[...] = kiw
    lane = lax.broadcasted_iota(jnp.int32, kiw.shape, 1)
    ki = jnp.where(lane < D_IDX, kiw, 0.0)
    kilo_ref[...] = ki.astype(BF16)
    kihi_ref[...] = pltpu.roll(ki, D_IDX, 1).astype(BF16)


def post_in(z, tabs, gains, t_len):
    n = z.shape[0]
    tm = _row_tile(t_len, 512)
    nt = t_len // tm
    rows = lambda w: pl.BlockSpec((tm, w), lambda i: (i, 0))
    tab = pl.BlockSpec((3, tm, 128), lambda i: (0, i % nt, 0))
    gain = pl.BlockSpec((1, HEAD_DIM), lambda i: (0, 0))
    per_head = pl.BlockSpec((tm, H_A, HEAD_DIM), lambda i: (i, 0, 0))
    widths = [(A_W, BF16), (A_W, F32), (A_W, BF16), (A_W, F32), (A_W, BF16), (B_W, BF16),
              (HEAD_DIM, F32), (HEAD_DIM, BF16), (HEAD_DIM, F32), (HEAD_DIM, BF16),
              (H_IDX * D_IDX, BF16), (128, F32), (128, BF16), (128, BF16)]
    return pl.pallas_call(
        _post_in_kernel,
        grid=(n // tm,),
        in_specs=[rows(ATT_W), tab, tab, tab, gain, gain, gain, gain],
        out_specs=[per_head if i in (1, 3) else rows(w) for i, (w, _) in enumerate(widths)],
        out_shape=[jax.ShapeDtypeStruct((n, H_A, HEAD_DIM) if i in (1, 3) else (n, w), dt)
                   for i, (w, dt) in enumerate(widths)],
        compiler_params=_params("parallel"),
        name="post_in",
    )(z, *tabs, *[g.reshape(1, HEAD_DIM) for g in gains])


LOG2E = 1.4426950408889634
ATTN_A_ROW_CHUNK = 256
ATTN_A_BLOCKS_PER_STEP = 4


def _attn_a_kernel(q_ref, k_ref, v_ref, tri_ref, o_ref, *, tq, tk, rc, past):
    i = pl.program_id(2)
    q0 = past + i * tq
    tri = tri_ref[...]
    n_full = q0 // tk
    n_all = (q0 + tq - 1 + tk - 1) // tk
    nrc = tq // rc
    qs = [q_ref[c * rc:(c + 1) * rc, :] for c in range(nrc)]
    rows = [q0 + c * rc + lax.broadcasted_iota(jnp.int32, (rc, tk), 0) for c in range(nrc)]
    col = lax.broadcasted_iota(jnp.int32, (rc, tk), 1)

    def step(j, carry, masked, nsub):
        cs = range(nrc)
        ss = range(nsub)
        offs = [pl.multiple_of((j - s) * tk, tk) for s in ss]
        kb = [k_ref[pl.ds(o, tk), :] for o in offs]
        vb = [v_ref[pl.ds(o, tk), :] for o in offs]
        later = [carry[0][c * rc:(c + 1) * rc] for c in cs]
        acc = [carry[1][c * rc:(c + 1) * rc] for c in cs]
        nz = [[_nt_dot(qs[c], kb[s]) for c in cs] for s in ss]
        loss = [[jnp.log(1.0 + jnp.exp2(jnp.minimum(nz[s][c] * (-LOG2E), 126.0))) for c in cs] for s in ss]
        if masked:
            causal = [[(offs[s] + col) < rows[c] for c in cs] for s in ss]
            loss = [[jnp.where(causal[s][c], loss[s][c], 0.0) for c in cs] for s in ss]
        within = [[jnp.dot(loss[s][c].astype(BF16), tri, preferred_element_type=F32) for c in cs] for s in ss]
        for s in ss:
            w = [jnp.exp2((nz[s][c] + later[c] + within[s][c]) * (-LOG2E)) for c in cs]
            if masked:
                w = [jnp.where(causal[s][c], w[c], 0.0) for c in cs]
            acc = [acc[c] + jnp.dot(w[c].astype(BF16), vb[s], preferred_element_type=F32) for c in cs]
            later = [later[c] + within[s][c][:, 0:1] for c in cs]
        return jnp.concatenate(later, axis=0), jnp.concatenate(acc, axis=0)

    carry = (jnp.zeros((tq, 1), F32), jnp.zeros((tq, HEAD_DIM), F32))
    carry = lax.fori_loop(0, n_all - n_full, lambda t, c: step(n_all - 1 - t, c, True, 1), carry)
    n_multi = n_full // ATTN_A_BLOCKS_PER_STEP
    n_single = n_full - n_multi * ATTN_A_BLOCKS_PER_STEP
    carry = lax.fori_loop(0, n_single, lambda t, c: step(n_full - 1 - t, c, False, 1), carry)
    carry = lax.fori_loop(
        0, n_multi,
        lambda t, c: step(n_full - n_single - 1 - t * ATTN_A_BLOCKS_PER_STEP, c, False, ATTN_A_BLOCKS_PER_STEP),
        carry)
    o_ref[...] = carry[1].astype(o_ref.dtype)


def attn_a(q, k, v, past, tk):
    b, t, _ = q.shape
    lp = k.shape[1]
    tq = _row_tile(t, 512)
    rc = min(tq, ATTN_A_ROW_CHUNK)
    tri = jnp.tri(tk, dtype=BF16)
    return pl.pallas_call(
        functools.partial(_attn_a_kernel, tq=tq, tk=tk, rc=rc, past=past),
        grid=(b, H_A, t // tq),
        in_specs=[pl.BlockSpec((None, tq, HEAD_DIM), lambda bi, h, i: (bi, i, h)),
                  pl.BlockSpec((None, lp, HEAD_DIM), lambda bi, h, i: (bi, 0, h)),
                  pl.BlockSpec((None, lp, HEAD_DIM), lambda bi, h, i: (bi, 0, h)),
                  pl.BlockSpec((tk, tk), lambda bi, h, i: (0, 0))],
        out_specs=pl.BlockSpec((None, tq, HEAD_DIM), lambda bi, h, i: (bi, i, h)),
        out_shape=jax.ShapeDtypeStruct((b, t, A_W), BF16),
        compiler_params=_params("parallel", "parallel", "parallel"),
        name="attn_a",
    )(q, k, v, tri)


DSA_QUERY_ROWS = 256
DSA_ATTEND_ROWS = 128


def _dsa_kernel(qi_ref, kiw_ref, kilo_ref, kihi_ref, q_ref, k_ref, v_ref, triu_ref, o_ref, key_ref,
                *, tq, tk, past, n_keys, topk):
    i = pl.program_id(1)
    q0 = past + i * tq
    pos = q0 + lax.broadcasted_iota(jnp.int32, (tq, 1), 0)
    lim = jnp.minimum((pos // CHUNK + 1) * CHUNK, n_keys)
    n_adm = jnp.minimum(((q0 + tq - 1) // CHUNK + 1) * CHUNK, n_keys)
    nt = (n_adm + tk - 1) // tk
    col = lax.broadcasted_iota(jnp.int32, (tq, tk), 1)
    ngrp = tk // 128

    qi = qi_ref[...]
    kiw = kiw_ref[...]
    wi = [kiw[:, D_IDX + h:D_IDX + h + 1] for h in range(H_IDX)]

    n_pair = (nt + 1) // 2
    pair = range(2)

    def score_pair(t, _):
        offs = [pl.multiple_of((2 * t + u) * tk, tk) for u in pair]
        kis = [(kilo_ref[pl.ds(o, tk), :], kihi_ref[pl.ds(o, tk), :]) for o in offs]
        s_idx = [[_nt_dot(qi[:, (h // 2) * 128:(h // 2 + 1) * 128], kis[u][h % 2]) for h in range(H_IDX)]
                 for u in pair]
        for u in pair:
            s = jnp.zeros((tq, tk), F32)
            for h in range(H_IDX):
                s = s + wi[h] * jnp.maximum(s_idx[u][h], 0.0)
            s = jnp.where(s == 0.0, 0.0, s)
            s = jnp.where(offs[u] + col < lim, s, -jnp.inf)
            bits = lax.bitcast_convert_type(s, jnp.int32)
            key_ref[:, pl.ds(offs[u], tk)] = bits ^ ((bits >> 31) & 0x7FFFFFFF)
        return 0

    lax.fori_loop(0, n_pair, score_pair, 0)

    cr = min(tq, DSA_ATTEND_ROWS)

    def count_ge(cand):
        accs = []
        for r in range(tq // cr):
            cand_b = jnp.broadcast_to(cand[r * cr:(r + 1) * cr], (cr, 128))

            def body(t, acc):
                for u in pair:
                    off = pl.multiple_of((2 * t + u) * tk, tk)
                    kt = key_ref[r * cr:(r + 1) * cr, pl.ds(off, tk)]
                    for g in range(ngrp):
                        acc = acc + jnp.where(kt[:, g * 128:(g + 1) * 128] >= cand_b, 1, 0)
                return acc

            accs.append(lax.fori_loop(0, n_pair, body, jnp.zeros((cr, 128), jnp.int32)))
        return jnp.sum(jnp.concatenate(accs, axis=0), axis=1, keepdims=True)

    def bit_step(b, t_u):
        cand_u = t_u | lax.shift_left(jnp.int32(1), 31 - b)
        ok = count_ge(cand_u ^ INT_MIN) >= topk
        return jnp.where(ok, cand_u, t_u)

    t_u = lax.fori_loop(0, 32, bit_step, jnp.zeros((tq, 1), jnp.int32))
    tau = t_u ^ INT_MIN
    tau_next = jnp.where(tau == 2 ** 31 - 1, tau, tau + 1)
    cnt_gt = jnp.where(tau == 2 ** 31 - 1, 0, count_ge(tau_next))
    need = jnp.where(tau == KEY_NEG_INF, 0, topk - cnt_gt).astype(F32)

    triu = triu_ref[...]

    def select_pair(t, seen):
        offs = [pl.multiple_of((2 * t + u) * tk, tk) for u in pair]
        kts = [key_ref[:, pl.ds(o, tk)] for o in offs]
        eqs = [kt == tau for kt in kts]
        ranks = [jnp.dot(jnp.where(eq, 1.0, 0.0).astype(BF16), triu, preferred_element_type=F32) for eq in eqs]
        for u in pair:
            prefix = seen + ranks[u]
            rank = jnp.where(eqs[u], prefix, jnp.inf)
            bias = jnp.where(kts[u] > tau, 0.0, jnp.where(rank <= need, 0.0, NEG_BIG)).astype(F32)
            key_ref[:, pl.ds(offs[u], tk)] = lax.bitcast_convert_type(bias, jnp.int32)
            seen = prefix[:, tk - 1:tk]
        return seen

    lax.fori_loop(0, n_pair, select_pair, jnp.zeros((tq, 1), F32))

    rg = min(tq, DSA_ATTEND_ROWS)
    groups = range(tq // rg)
    gr = H_B * rg
    qs = [jnp.concatenate([q_ref[r * rg:(r + 1) * rg, h * HEAD_DIM:(h + 1) * HEAD_DIM] for h in range(H_B)], axis=0)
          for r in groups]

    def attend_pair(t, carry):
        offs = [pl.multiple_of((2 * t + u) * tk, tk) for u in pair]
        kb = [k_ref[pl.ds(o, tk), :] for o in offs]
        vb = [v_ref[pl.ds(o, tk), :] for o in offs]
        new = []
        for r in groups:
            m, l, acc = [c[r * gr:(r + 1) * gr] for c in carry]
            bias = [lax.bitcast_convert_type(key_ref[r * rg:(r + 1) * rg, pl.ds(o, tk)], F32) for o in offs]
            logits = [(_nt_dot(qs[r], kb[u]).reshape(H_B, rg, tk) + bias[u][None]).reshape(gr, tk) for u in pair]
            m_new = jnp.maximum(m, jnp.maximum(jnp.max(logits[0], axis=1, keepdims=True),
                                               jnp.max(logits[1], axis=1, keepdims=True)))
            alpha = jnp.exp(m - m_new)
            p = [jnp.exp(logits[u] - m_new) for u in pair]
            l = alpha * l + (jnp.sum(p[0], axis=1, keepdims=True) + jnp.sum(p[1], axis=1, keepdims=True))
            acc = alpha * acc + (jnp.dot(p[0].astype(BF16), vb[0], preferred_element_type=F32)
                                 + jnp.dot(p[1].astype(BF16), vb[1], preferred_element_type=F32))
            new.append((m_new, l, acc))
        return tuple(jnp.concatenate([n[k] for n in new], axis=0) for k in range(3))

    init = (jnp.full((H_B * tq, 1), NEG_BIG, F32), jnp.zeros((H_B * tq, 1), F32),
            jnp.zeros((H_B * tq, HEAD_DIM), F32))
    _, l, acc = lax.fori_loop(0, n_pair, attend_pair, init)
    out = acc / l
    for r in groups:
        for h in range(H_B):
            rows = out[r * gr + h * rg:r * gr + (h + 1) * rg]
            o_ref[r * rg:(r + 1) * rg, h * HEAD_DIM:(h + 1) * HEAD_DIM] = rows.astype(o_ref.dtype)


def dsa(qi, kiw, kilo, kihi, q, k, v, past, n_keys, tk):
    b, t, _ = q.shape
    lp = k.shape[1]
    tq = _row_tile(t, DSA_QUERY_ROWS)
    topk = min(TOPK_MAX, n_keys // 4)
    assert lp % (2 * tk) == 0 and lp >= topk
    triu = jnp.tri(tk, dtype=BF16).T
    qblk = lambda w: pl.BlockSpec((None, tq, w), lambda bi, i: (bi, i, 0))
    kblk = lambda w: pl.BlockSpec((None, lp, w), lambda bi, i: (bi, 0, 0))
    return pl.pallas_call(
        functools.partial(_dsa_kernel, tq=tq, tk=tk, past=past, n_keys=n_keys, topk=topk),
        grid=(b, t // tq),
        in_specs=[qblk(H_IDX * D_IDX), qblk(128), kblk(128), kblk(128), qblk(B_W),
                  kblk(HEAD_DIM), kblk(HEAD_DIM), pl.BlockSpec((tk, tk), lambda bi, i: (0, 0))],
        out_specs=qblk(B_W),
        out_shape=jax.ShapeDtypeStruct((b, t, B_W), BF16),
        scratch_shapes=[pltpu.VMEM((tq, lp), jnp.int32)],
        compiler_params=_params("parallel", "parallel"),
        name="dsa",
    )(qi, kiw, kilo, kihi, q, k, v, triu)


def _softplus(x):
    return jnp.maximum(x, 0.0) + jnp.log(1.0 + jnp.exp(-jnp.abs(x)))


PREV_TILE = 8


def _c_prep_kernel(*refs, first_layer):
    if first_layer:
        (cz_ref, before_ref, shift_ref, mu_ref, w0_ref, a0_ref, w2_ref, a2_ref, g2_ref, kk_ref, ka_ref,
         r_ref, kn_ref, k_ref, v_ref, lw_ref, a_ref, g_ref, vtok_ref) = refs
    else:
        (cz_ref, before_ref, shift_ref, mu_ref, w0_ref, a0_ref, w2_ref, a2_ref, g2_ref, kk_ref, ka_ref,
         vf_ref, v0_ref, v1_ref, v2_ref,
         r_ref, kn_ref, k_ref, v_ref, lw_ref, a_ref, g_ref) = refs

    def put_heads(ref, x):
        for pair in range(H_C // 2):
            tile = x[:, pair * 128:(pair + 1) * 128]
            ref[2 * pair] = tile[:, :N_C]
            ref[2 * pair + 1] = pltpu.roll(tile, N_C, 1)[:, :N_C]

    cz = cz_ref[...]
    rowid = lax.broadcasted_iota(jnp.int32, cz.shape, 0)
    before = jnp.where(pl.program_id(1) == 0, shift_ref[...], before_ref[PREV_TILE - 1:PREV_TILE, :])
    prev = jnp.where(rowid == 0, before, pltpu.roll(cz, 1, 0))
    zz = cz + (prev - cz) * mu_ref[...]
    r = zz[:, 0:C_W]
    k = zz[:, C_W:2 * C_W]
    v = zz[:, 2 * C_W:3 * C_W]
    tail = zz[:, 3 * C_W:C_PAD]
    w = w0_ref[...] + jnp.dot(jnp.tanh(tail).astype(BF16), w2_ref[...], preferred_element_type=F32)
    w = -_softplus(-w) - 0.5
    a = jax.nn.sigmoid(a0_ref[...] + jnp.dot(tail.astype(BF16), a2_ref[...], preferred_element_type=F32))
    g = jnp.dot(jax.nn.sigmoid(tail).astype(BF16), g2_ref[...], preferred_element_type=F32)
    if not first_layer:
        lora = jnp.dot(v.astype(BF16), v1_ref[...], preferred_element_type=F32)
        lora = jnp.dot(lora.astype(BF16), v2_ref[...], preferred_element_type=F32)
        v = v + (vf_ref[...] - v) * jax.nn.sigmoid(v0_ref[...] + lora)
    if first_layer:
        vtok_ref[...] = v
    put_heads(r_ref, r)
    put_heads(kn_ref, k * kk_ref[...])
    put_heads(k_ref, k * (1.0 + (a - 1.0) * ka_ref[...]))
    put_heads(v_ref, v)
    put_heads(lw_ref, -jnp.exp(w))
    put_heads(a_ref, a)
    put_heads(g_ref, g)


def c_prep(cz, shift0, vecs, mats, v_first, vres, tm):
    b, t, _ = cz.shape
    first_layer = vres is None
    blk = lambda w: pl.BlockSpec((None, tm, w), lambda bi, i: (bi, i, 0))
    vec = lambda w: pl.BlockSpec((1, w), lambda bi, i: (0, 0))
    mat = lambda m: pl.BlockSpec(m.shape, lambda bi, i: (0, 0))
    mu, w0, a0, k_k, k_a = vecs
    w2, a2, g2 = mats
    assert tm % PREV_TILE == 0
    before = pl.BlockSpec((None, PREV_TILE, C_PAD), lambda bi, i: (bi, jnp.maximum(i * (tm // PREV_TILE) - 1, 0), 0))
    ins = [cz, cz, shift0, mu, w0, a0, w2, a2, g2, k_k, k_a]
    specs = [blk(C_PAD), before, pl.BlockSpec((None, 1, C_PAD), lambda bi, i: (bi, 0, 0)),
             vec(C_PAD), vec(C_W), vec(C_W), mat(w2), mat(a2), mat(g2), vec(C_W), vec(C_W)]
    if not first_layer:
        v0, v1, v2 = vres
        ins += [v_first, v0, v1, v2]
        specs += [blk(C_W), vec(C_W), mat(v1), mat(v2)]
    heads = pl.BlockSpec((None, H_C, tm, N_C), lambda bi, i: (bi, 0, i, 0))
    out_specs = [heads] * 7
    out_shape = [jax.ShapeDtypeStruct((b, H_C, t, N_C), F32)] * 7
    if first_layer:
        out_specs.append(blk(C_W))
        out_shape.append(jax.ShapeDtypeStruct((b, t, C_W), F32))
    return pl.pallas_call(
        functools.partial(_c_prep_kernel, first_layer=first_layer),
        grid=(b, t // tm),
        in_specs=specs,
        out_specs=out_specs,
        out_shape=out_shape,
        compiler_params=_params("parallel", "parallel"),
        name="c_prep",
    )(*ins)


def _bdot(a, b, contract_a, contract_b):
    return lax.dot_general(a.astype(BF16), b.astype(BF16), (((contract_a,), (contract_b,)), ((0,), (0,))),
                           preferred_element_type=F32)


def _bdot3(a, b, contract_a, contract_b):
    ah = a.astype(BF16)
    bh = b.astype(BF16)
    al = (a - ah.astype(F32)).astype(BF16)
    bl = (b - bh.astype(F32)).astype(BF16)
    dn = (((contract_a,), (contract_b,)), ((0,), (0,)))
    dot = lambda x, y: lax.dot_general(x, y, dn, preferred_element_type=F32)
    return dot(ah, bh) + (dot(al, bh) + dot(ah, bl))


C_SCAN_CHUNKS_PER_STEP = 4


def _c_scan_kernel(r_ref, kn_ref, k_ref, v_ref, lw_ref, a_ref, g_ref, s0_ref, rk_ref, lng_ref, lnb_ref,
                   y_ref, st_ref, s_ref, *, nck):
    c_len = SCAN_CHUNK

    @pl.when(pl.program_id(1) == 0)
    def _():
        s_ref[...] = s0_ref[...]

    def step(c0, nc):
        g_n = H_C * nc
        rows = pl.ds(pl.multiple_of(c0 * c_len, c_len), nc * c_len)
        load = lambda ref: ref[:, rows, :].reshape(g_n, c_len, N_C)
        per_head = lambda x: jnp.broadcast_to(x[:, None], (H_C, nc) + x.shape[1:]).reshape((g_n,) + x.shape[1:])
        of_chunk = lambda x, j: x.reshape((H_C, nc) + x.shape[1:])[:, j]
        ti = lax.broadcasted_iota(jnp.int32, (g_n, c_len, c_len), 1)
        si = lax.broadcasted_iota(jnp.int32, (g_n, c_len, c_len), 2)
        incl = si <= ti
        strict = si < ti
        tri = jnp.where(incl, 1.0, 0.0)
        eye = jnp.where(si == ti, 1.0, 0.0)
        r, k, v, lw, a, kn = (load(x) for x in (r_ref, k_ref, v_ref, lw_ref, a_ref, kn_ref))
        kn = kn * lax.rsqrt(jnp.maximum(jnp.sum(kn * kn, axis=-1, keepdims=True), 1e-24))
        aa = -kn
        bb = kn * a
        g = _bdot3(tri, lw, 2, 1)
        g_end = g[:, c_len - 1:c_len, :]
        a_t = aa * jnp.exp(g - lw)
        r_t = r * jnp.exp(g)
        inv = jnp.exp(-g)
        b_h = bb * inv
        k_h = k * inv
        to_end = jnp.exp(g_end - g)
        b_e = bb * to_end
        k_e = k * to_end
        d_end = jnp.exp(g_end)
        a_ab = jnp.where(strict, _bdot(a_t, b_h, 2, 2), 0.0)
        a_ak = jnp.where(strict, _bdot(a_t, k_h, 2, 2), 0.0)
        a_rb = jnp.where(incl, _bdot(r_t, b_h, 2, 2), 0.0)
        a_rk = jnp.where(incl, _bdot(r_t, k_h, 2, 2), 0.0)
        t_inv = eye + a_ab
        pw = a_ab
        for _ in range(4):
            pw = _bdot(pw, pw, 2, 1)
            t_inv = t_inv + _bdot(t_inv, pw, 2, 1)
        p_mat = _bdot(t_inv, a_t, 2, 1)
        q_mat = _bdot(t_inv, _bdot(a_ak, v, 2, 1), 2, 1)
        y_v = _bdot(a_rk, v, 2, 1)
        s = s_ref[...]
        ys = []
        for j in range(nc):
            u = _bdot(of_chunk(p_mat, j), s, 2, 2) + of_chunk(q_mat, j)
            ys.append(_bdot(of_chunk(r_t, j), s, 2, 2) + _bdot(of_chunk(a_rb, j), u, 2, 1) + of_chunk(y_v, j))
            uv = jnp.concatenate([u, of_chunk(v, j)], axis=1)
            bk = jnp.concatenate([of_chunk(b_e, j), of_chunk(k_e, j)], axis=1)
            s = s * of_chunk(d_end, j) + _bdot(uv, bk, 1, 1)
        s_ref[...] = s
        y = jnp.stack(ys, axis=1).reshape(g_n, c_len, N_C)
        mu = jnp.mean(y, axis=-1, keepdims=True)
        yc = y - mu
        yn = yc * lax.rsqrt(jnp.mean(yc * yc, axis=-1, keepdims=True) + LNX_EPS)
        yn = yn * per_head(lng_ref[...]) + per_head(lnb_ref[...])
        bonus = jnp.sum(r * k * per_head(rk_ref[...]), axis=-1, keepdims=True) * v
        o = ((yn + bonus) * load(g_ref)).reshape(H_C, nc * c_len, N_C)
        y_ref[rows, :] = jnp.concatenate([o[h] for h in range(H_C)], axis=-1).astype(y_ref.dtype)

    n_multi = nck // C_SCAN_CHUNKS_PER_STEP
    if n_multi:
        lax.fori_loop(0, n_multi, lambda t, _: step(t * C_SCAN_CHUNKS_PER_STEP, C_SCAN_CHUNKS_PER_STEP), None)
    for c in range(n_multi * C_SCAN_CHUNKS_PER_STEP, nck):
        step(c, 1)

    @pl.when(pl.program_id(1) == pl.num_programs(1) - 1)
    def _():
        st_ref[...] = s_ref[...]


def c_scan(r, kn, k, v, lw, a, g, s0, r_k, ln_g, ln_b, tm):
    b, _, t, _ = r.shape
    blk = pl.BlockSpec((None, H_C, tm, N_C), lambda bi, i: (bi, 0, i, 0))
    st = pl.BlockSpec((None, H_C, N_C, N_C), lambda bi, i: (bi, 0, 0, 0))
    vec = pl.BlockSpec((H_C, 1, N_C), lambda bi, i: (0, 0, 0))
    return pl.pallas_call(
        functools.partial(_c_scan_kernel, nck=tm // SCAN_CHUNK),
        grid=(b, t // tm),
        in_specs=[blk] * 7 + [st, vec, vec, vec],
        out_specs=[pl.BlockSpec((None, tm, C_W), lambda bi, i: (bi, i, 0)), st],
        out_shape=[jax.ShapeDtypeStruct((b, t, C_W), BF16),
                   jax.ShapeDtypeStruct((b, H_C, N_C, N_C), F32)],
        scratch_shapes=[pltpu.VMEM((H_C, N_C, N_C), F32)],
        compiler_params=_params("parallel", "arbitrary"),
        name="c_scan",
    )(r, kn, k, v, lw, a, g, s0, r_k.reshape(H_C, 1, N_C), ln_g.reshape(H_C, 1, N_C), ln_b.reshape(H_C, 1, N_C))


def _pad_rows(x, rows, axis=1):
    pad = rows - x.shape[axis]
    if pad == 0:
        return x
    widths = [(0, 0)] * x.ndim
    widths[axis] = (0, pad)
    return jnp.pad(x, widths)


ATT_COLS = 3 * A_W + B_W + 2 * HEAD_DIM + H_IDX * D_IDX + D_IDX + H_IDX
IN_COLS = ATT_COLS + C_COLS + 3 * D_MODEL


def _regroup_kernel(w_ref, tail_ref, att_ref, c_ref, g_ref):
    lane = lax.broadcasted_iota(jnp.int32, (w_ref.shape[0], 128), 1)
    n_full = IN_COLS // 128

    def tile(j):
        return tail_ref[...] if j == n_full else w_ref[:, j * 128:(j + 1) * 128]

    def shifted(j0, shift, j):
        joined = jnp.where(lane >= shift, tile(j0 + j), tile(j0 + j + 1))
        return pltpu.roll(joined, 128 - shift, 1)

    for j in range(ATT_W // 128):
        x = tile(j)
        if (j + 1) * 128 > ATT_COLS:
            x = jnp.where(lane < ATT_COLS - j * 128, x, 0.0)
        att_ref[:, j * 128:(j + 1) * 128] = x.astype(BF16)
    for j in range(C_PAD // 128):
        x = shifted(ATT_COLS // 128, ATT_COLS % 128, j)
        if (j + 1) * 128 > C_COLS:
            x = jnp.where(lane < C_COLS - j * 128, x, 0.0)
        c_ref[:, j * 128:(j + 1) * 128] = x.astype(BF16)
    g0 = ATT_COLS + C_COLS
    for j in range(3 * D_MODEL // 128):
        g_ref[:, j * 128:(j + 1) * 128] = shifted(g0 // 128, g0 % 128, j).astype(BF16)


def regroup_w_in(w_in):
    depth, d, cols = w_in.shape
    assert cols == IN_COLS and (ATT_COLS + C_COLS) % 128 != 0
    n_full = cols // 128
    tail = jnp.pad(w_in[:, :, n_full * 128:], ((0, 0), (0, 0), (0, (n_full + 1) * 128 - cols)))
    tm = 128
    blk = lambda w: pl.BlockSpec((None, tm, w), lambda l, i: (l, i, 0))
    return pl.pallas_call(
        _regroup_kernel,
        grid=(depth, d // tm),
        in_specs=[blk(cols), blk(128)],
        out_specs=[blk(ATT_W), blk(C_PAD), blk(3 * D_MODEL)],
        out_shape=[jax.ShapeDtypeStruct((depth, d, w), BF16) for w in (ATT_W, C_PAD, 3 * D_MODEL)],
        compiler_params=_params("parallel", "parallel"),
        name="regroup_w_in",
    )(w_in, tail)


def _prep_weights(w):
    out = []
    w_att, w_c, w_g = regroup_w_in(w["w_in"])
    for i in range(DEPTH):
        tail_rows = lambda m, off: jnp.pad(m, ((off, C_TAIL - off - m.shape[0]), (0, 0))).astype(BF16)
        lw = dict(
            w_att=w_att[i], w_c=w_c[i], w_g=w_g[i],
            w2=tail_rows(w["c_w2"][i], 0),
            a2=tail_rows(w["c_a2"][i], D_DECAY_LORA),
            g2=tail_rows(w["c_g2"][i], D_DECAY_LORA + D_AAA_LORA),
            mu=jnp.pad(w["c_mu"][i], (0, C_PAD - C_COLS)).reshape(1, C_PAD),
            w_br_a=w["w_br_a"][i].astype(BF16), w_br_b=w["w_br_b"][i].astype(BF16),
            w_br_c=w["w_br_c"][i].astype(BF16), w_out=w["w_out"][i].astype(BF16),
            w_ffn_in=w["w_ffn_in"][i].astype(BF16), w_ffn_out=w["w_ffn_out"][i].astype(BF16),
            w_ple_gate=w["w_ple_gate"][i].astype(BF16), w_ple_proj=w["w_ple_proj"][i].astype(BF16),
        )
        if i > 0:
            lw["v1"] = w["c_v1"][i - 1].astype(BF16)
            lw["v2"] = w["c_v2"][i - 1].astype(BF16)
        out.append(lw)
    return out


def _run_trunk(x, p, past, caches, w, wl):
    b, t, _ = x.shape
    n = b * t
    n_keys = past + t
    pos = past + jnp.arange(t)
    lane = jnp.arange(128)
    tabs = (_rope_tables(pos, 128, HEAD_DIM, HEAD_DIM // ROT_FRACTION),
            _rope_tables(pos, 128, D_IDX, D_IDX // ROT_FRACTION),
            _rope_tables(pos, 128, 128, D_IDX // ROT_FRACTION,
                         scale_lanes=(lane < D_IDX, jnp.where(lane < D_IDX + H_IDX, H_IDX ** -0.5, 1.0))))
    tk_a = 256
    tk_b = 512
    lp_a = -(-n_keys // tk_a) * tk_a
    lp_b = -(-n_keys // (2 * tk_b)) * (2 * tk_b)
    t_c = -(-t // SCAN_CHUNK) * SCAN_CHUNK
    tm_c = _row_tile(t, 256)
    tm_s = _row_tile(t_c, 256)
    h = x.reshape(n, D_MODEL)
    outs = ([], [], [], [], [], [], [])
    v_first = None
    for i in range(DEPTH):
        lw = wl[i]
        z_att = mm_norm(h, w["norm_mix"][i], lw["w_att"], ATT_W // 3)
        z_c = mm_norm(h, w["norm_mix"][i], lw["w_c"], C_PAD // 2)
        z_g = mm_norm(h, w["norm_mix"][i], lw["w_g"], 1536)
        (aq, ak_f, ak_b, av_f, av_b, bq, bk_f, bk_b, bv_f, bv_b, qi, kiw, kilo, kihi) = post_in(
            z_att, tabs, (w["a_q_norm"][i], w["a_k_norm"][i], w["b_q_norm"][i], w["b_k_norm"][i]), t)
        seq = lambda a: a.reshape(b, t, a.shape[-1])

        def keys(new, old, lp):
            new = seq(new)
            if old is not None:
                new = jnp.concatenate([old.astype(BF16), new], axis=1)
            return _pad_rows(new, lp)

        c = caches
        past_of = lambda name: None if c is None else c[name][i]
        pa_k = None if c is None else c["a_k"][i].reshape(b, past, A_W)
        pa_v = None if c is None else c["a_v"][i].reshape(b, past, A_W)
        o_a = attn_a(seq(aq), keys(ak_b, pa_k, lp_a), keys(av_b, pa_v, lp_a), past, tk_a)
        if c is None:
            p_lo = p_hi = None
        else:
            p_lo = jnp.pad(c["b_ki"][i], ((0, 0), (0, 0), (0, 128 - D_IDX)))
            p_hi = jnp.pad(c["b_ki"][i], ((0, 0), (0, 0), (128 - D_IDX, 0)))
        o_b = dsa(seq(qi), seq(kiw), keys(kilo, p_lo, lp_b), keys(kihi, p_hi, lp_b), seq(bq),
                  keys(bk_b, past_of("b_k"), lp_b), keys(bv_b, past_of("b_v"), lp_b), past, n_keys, tk_b)
        cz = seq(z_c)
        shift0 = jnp.zeros((b, C_COLS), F32) if c is None else c["shift"][i]
        shift0 = jnp.pad(shift0, ((0, 0), (0, C_PAD - C_COLS)))
        row = lambda a: a.reshape(1, C_W)
        vres = None if i == 0 else (row(w["c_v0"][i - 1]), lw["v1"], lw["v2"])
        prep = c_prep(cz, shift0[:, None, :],
                      (lw["mu"], row(w["c_w0"][i]), row(w["c_a0"][i]), row(w["c_k_k"][i]), row(w["c_k_a"][i])),
                      (lw["w2"], lw["a2"], lw["g2"]), v_first, vres, tm_c)
        if i == 0:
            v_first = prep[7]
        s0 = jnp.zeros((b, H_C, N_C, N_C), F32) if c is None else c["wkv"][i]
        y_c, wkv_t = c_scan(*[_pad_rows(a, t_c, axis=2) for a in prep[:7]], s0,
                            w["c_r_k"][i], w["c_ln_g"][i], w["c_ln_b"][i], tm_s)
        o_c = y_c[:, :t].reshape(n, C_W)
        merged = mm_gated3(o_a.reshape(n, A_W), o_b.reshape(n, B_W), o_c,
                           lw["w_br_a"], lw["w_br_b"], lw["w_br_c"], z_g)
        h = mm_res(merged, lw["w_out"], h)
        h = ffn(h, w["norm_ffn"][i], lw["w_ffn_in"], lw["w_ffn_out"])
        h = ple(h, w["norm_ple"][i], lw["w_ple_gate"], p[i].reshape(n, -1).astype(BF16), lw["w_ple_proj"])
        vals = (ak_f.reshape(b, t, H_A, HEAD_DIM), av_f.reshape(b, t, H_A, HEAD_DIM),
                seq(bk_f), seq(bv_f), seq(kiw)[:, :, :D_IDX], wkv_t, cz[:, -1, :C_COLS])
        for lst, val in zip(outs, vals):
            lst.append(val)
    return h.reshape(b, t, D_MODEL), [jnp.stack(l) for l in outs]


def kernel(x_prompt, x_sample, cache_a_k, cache_a_v, cache_b_k, cache_b_v, cache_b_kidx, state_c_wkv, state_c_shift, p_prompt, p_sample, norm_mix, w_in, a_q_norm, a_k_norm, b_q_norm, b_k_norm, c_mu, c_w0, c_w2, c_a0, c_a2, c_g2, c_v0, c_v1, c_v2, c_k_k, c_k_a, c_r_k, c_ln_g, c_ln_b, w_br_a, w_br_b, w_br_c, w_out, norm_ffn, w_ffn_in, w_ffn_out, norm_ple, w_ple_gate, w_ple_proj):
    w = dict(norm_mix=norm_mix, w_in=w_in, a_q_norm=a_q_norm, a_k_norm=a_k_norm, b_q_norm=b_q_norm,
             b_k_norm=b_k_norm, c_mu=c_mu, c_w0=c_w0, c_w2=c_w2, c_a0=c_a0, c_a2=c_a2, c_g2=c_g2,
             c_v0=c_v0, c_v1=c_v1, c_v2=c_v2, c_k_k=c_k_k, c_k_a=c_k_a, c_r_k=c_r_k, c_ln_g=c_ln_g,
             c_ln_b=c_ln_b, w_br_a=w_br_a, w_br_b=w_br_b, w_br_c=w_br_c, w_out=w_out, norm_ffn=norm_ffn,
             w_ffn_in=w_ffn_in, w_ffn_out=w_ffn_out, norm_ple=norm_ple, w_ple_gate=w_ple_gate,
             w_ple_proj=w_ple_proj)
    wl = _prep_weights(w)
    y_p, o_p = _run_trunk(x_prompt, p_prompt, 0, None, w, wl)
    caches = dict(a_k=cache_a_k, a_v=cache_a_v, b_k=cache_b_k, b_v=cache_b_v, b_ki=cache_b_kidx,
                  wkv=state_c_wkv, shift=state_c_shift)
    y_s, o_s = _run_trunk(x_sample, p_sample, cache_a_k.shape[2], caches, w, wl)
    return (y_p, y_s, *o_p, *o_s)
```
